```python
import functools
import jax
import jax.numpy as jnp
from jax import lax
import numpy as np

D_MODEL = 1024
BATCH = 16
SEQ = 4096
DEPTH = 1
DEC_BATCH = 128
DEC_SEQ = 4
PAST_LEN = 8192
PAGE_SIZE = 128

HEAD_DIM = 64
D_MIX = D_MODEL
C_RWKV = D_MIX // 2
C_ATT = D_MIX - C_RWKV
H_RWKV = C_RWKV // HEAD_DIM
H_ATT = C_ATT // HEAD_DIM
R_DECAY = 64
R_ICLR = 64
R_GATE = 128
GN_EPS = 64e-5
NORM_EPS = 1e-6
DILATED_CONFIGS = ((128, 1), (512, 4), (2048, 16))
WINDOW_MAX = max(w for w, _ in DILATED_CONFIGS)
N_GROUPS = 4
EXPERTS_PER_GROUP = 8
N_EXPERTS = N_GROUPS * EXPERTS_PER_GROUP
TOP_K_INNER = 2
D_EXPERT = 512
EXPERT_BLOCK = 128
NEG_INF = -1e30

kernel_name = 'hybrid_rwkv7_dilated_attn_hmoe_step'


def _rms_norm(x, g):
    xf = x.astype(jnp.float32)
    xf = xf * lax.rsqrt(jnp.mean(xf * xf, axis=-1, keepdims=True) + NORM_EPS)
    return xf.astype(x.dtype) * g


def _rwkv7_mix(xn, dx, cur, prev, s0, lp):
    b, t, _ = xn.shape
    f32 = jnp.float32
    heads = lambda z: z.reshape(b, t, H_RWKV, HEAD_DIM)
    mu = lp['rwkv_mu_rkv']
    r_c, k_c, v_c = jnp.split(cur, 3, axis=-1)
    r_p, k_p, v_p = jnp.split(prev, 3, axis=-1)
    r = r_c + mu[0] * (r_p - r_c)
    k = k_c + mu[1] * (k_p - k_c)
    v = v_c + mu[2] * (v_p - v_c)
    mx = lp['rwkv_mu_wag']
    xw, xa, xg = xn + mx[0] * dx, xn + mx[1] * dx, xn + mx[2] * dx
    w_log = -jax.nn.softplus(-(lp['rwkv_w0'] + jnp.tanh(xw @ lp['rwkv_w1']) @ lp['rwkv_w2']).astype(f32)) - 0.5
    decay = jnp.exp(-jnp.exp(w_log))
    a = jax.nn.sigmoid((lp['rwkv_a0'] + (xa @ lp['rwkv_a1']) @ lp['rwkv_a2']).astype(f32))
    g = jax.nn.sigmoid(xg @ lp['rwkv_g1']) @ lp['rwkv_g2']
    kk = heads((k * lp['rwkv_k_k']).astype(f32))
    kk = kk * lax.rsqrt(jnp.maximum(jnp.sum(kk * kk, axis=-1, keepdims=True), 1e-24))
    a_h = heads(a)
    k_h = heads(k.astype(f32)) * (1.0 + (a_h - 1.0) * lp['rwkv_k_a'].astype(f32).reshape(H_RWKV, HEAD_DIM))
    r_h, v_h, w_h = heads(r.astype(f32)), heads(v.astype(f32)), heads(decay)

    def step(s, inp):
        r_t, w_t, k_t, v_t, kk_t, a_t = inp
        sa = jnp.einsum('bhvk,bhk->bhv', s, -kk_t)
        s = (s * w_t[:, :, None, :] + sa[..., None] * (kk_t * a_t)[:, :, None, :]
             + v_t[..., None] * k_t[:, :, None, :])
        return s, jnp.einsum('bhvk,bhk->bhv', s, r_t)

    tm = lambda z: jnp.swapaxes(z, 0, 1)
    s_fin, ys = lax.scan(step, s0.astype(f32), (tm(r_h), tm(w_h), tm(k_h), tm(v_h), tm(kk), tm(a_h)))
    y = tm(ys)
    mean = jnp.mean(y, axis=-1, keepdims=True)
    var = jnp.mean(jnp.square(y - mean), axis=-1, keepdims=True)
    y = ((y - mean) * lax.rsqrt(var + GN_EPS)).reshape(b, t, C_RWKV)
    y = y * lp['rwkv_ln_w'].astype(f32) + lp['rwkv_ln_b'].astype(f32)
    bonus = jnp.sum(r_h * k_h * lp['rwkv_r_k'].astype(f32), axis=-1, keepdims=True) * v_h
    y = (y + bonus.reshape(b, t, C_RWKV)) * g.astype(f32)
    return y.astype(xn.dtype), s_fin.astype(s0.dtype)


def _strided_band_attn(q, k, v, window, dilation):
    b, s, h, dh = q.shape
    band = window // dilation
    n_res = s // dilation
    n_blk = -(-n_res // band)
    l_pad = n_blk * band
    bd = b * dilation

    def to_res(z):
        z = z.reshape(b, n_res, dilation, h, dh).transpose(0, 2, 1, 3, 4).reshape(bd, n_res, h, dh)
        z = jnp.pad(z, ((0, 0), (0, l_pad - n_res), (0, 0), (0, 0)))
        return z.reshape(bd, n_blk, band, h, dh)

    qb, kb, vb = to_res(q), to_res(k), to_res(v)
    pad_prev = ((0, 0), (1, 0), (0, 0), (0, 0), (0, 0))
    kw = jnp.concatenate([jnp.pad(kb, pad_prev)[:, :-1], kb], axis=2)
    vw = jnp.concatenate([jnp.pad(vb, pad_prev)[:, :-1], vb], axis=2)
    sc = jnp.einsum('bnqhd,bnkhd->bnhqk', qb, kw).astype(jnp.float32) * dh ** -0.5
    qi = jnp.arange(band)[:, None]
    kj = jnp.arange(2 * band)[None, :]
    in_band = (kj >= qi) & (kj <= qi + band)
    has_prev = (jnp.arange(n_blk)[:, None, None] > 0) | (kj[None] >= band)
    mask = in_band[None] & has_prev
    sc = jnp.where(mask[None, :, None], sc, NEG_INF)
    m = jnp.max(sc, axis=-1, keepdims=True)
    p = jnp.exp(sc - m)
    l = jnp.sum(p, axis=-1, keepdims=True)
    o = jnp.einsum('bnhqk,bnkhd->bnqhd', (p / l).astype(v.dtype), vw)
    lse = (m + jnp.log(l))[..., 0].transpose(0, 1, 3, 2)
    o = o.reshape(bd, l_pad, h, dh)[:, :n_res].reshape(b, dilation, n_res, h, dh)
    lse = lse.reshape(bd, l_pad, h)[:, :n_res].reshape(b, dilation, n_res, h)
    return (o.transpose(0, 2, 1, 3, 4).reshape(b, s, h, dh),
            lse.transpose(0, 2, 1, 3).reshape(b, s, h))


def _merge_dilations(outs, lses):
    wts = jax.nn.softmax(jnp.stack(lses, axis=0), axis=0)
    return jnp.sum(wts[..., None] * jnp.stack(outs, axis=0).astype(jnp.float32), axis=0)


def _dilated_attn_prompt(q, k, v):
    outs, lses = [], []
    for window, dilation in DILATED_CONFIGS:
        o, lse = _strided_band_attn(q, k, v, window, dilation)
        outs.append(o)
        lses.append(lse)
    return _merge_dilations(outs, lses).astype(q.dtype)


def _dilated_attn_sample(q, k, v, k_buf, v_buf):
    b, t, h, dh = q.shape
    n_buf = k_buf.shape[1]
    k_all = jnp.concatenate([k_buf.astype(k.dtype), k], axis=1)
    v_all = jnp.concatenate([v_buf.astype(v.dtype), v], axis=1)
    outs, lses = [], []
    for window, dilation in DILATED_CONFIGS:
        n_keys = window // dilation + 1
        idx = n_buf + jnp.arange(t)[:, None] - dilation * jnp.arange(n_keys)[None, :]
        valid = idx >= 0
        idx = jnp.maximum(idx, 0)
        kg, vg = k_all[:, idx], v_all[:, idx]
        sc = jnp.einsum('bthd,btjhd->bthj', q, kg).astype(jnp.float32) * dh ** -0.5
        sc = jnp.where(valid[None, :, None, :], sc, NEG_INF)
        lse = jax.nn.logsumexp(sc, axis=-1)
        p = jnp.exp(sc - lse[..., None])
        outs.append(jnp.einsum('bthj,btjhd->bthd', p.astype(v.dtype), vg))
        lses.append(lse)
    return _merge_dilations(outs, lses).astype(q.dtype)


def _hier_moe(x, lp):
    n, d = x.shape
    f32 = jnp.float32
    p_group = jax.nn.softmax((x @ lp['router_group_w']).astype(f32) + lp['router_group_b'].astype(f32), axis=-1)
    g_w, g_idx = lax.top_k(p_group, 1)
    logit_e = ((x @ lp['router_expert_w']).astype(f32) + lp['router_expert_b'].astype(f32)).reshape(n, N_GROUPS, EXPERTS_PER_GROUP)
    logit_e = jnp.take_along_axis(logit_e, jnp.broadcast_to(g_idx[:, :, None], (n, 1, EXPERTS_PER_GROUP)), axis=1)[:, 0]
    e_w, e_idx = lax.top_k(jax.nn.softmax(logit_e, axis=-1), TOP_K_INNER)
    gates = g_w * e_w / jnp.sum(e_w, axis=-1, keepdims=True)
    eid = (g_idx * EXPERTS_PER_GROUP + e_idx).reshape(-1)
    m = n * TOP_K_INNER
    tok = jnp.repeat(jnp.arange(n, dtype=jnp.int32), TOP_K_INNER)
    gate = gates.reshape(-1)
    order = jnp.argsort(eid)
    e_sorted = eid[order]
    counts = jnp.bincount(eid, length=N_EXPERTS)
    starts = jnp.cumsum(counts) - counts
    padded = (counts + EXPERT_BLOCK - 1) // EXPERT_BLOCK * EXPERT_BLOCK
    p_ends = jnp.cumsum(padded)
    p_starts = p_ends - padded
    dest = p_starts[e_sorted] + jnp.arange(m) - starts[e_sorted]
    n_blocks = -(-m // EXPERT_BLOCK) + N_EXPERTS
    n_slots = n_blocks * EXPERT_BLOCK
    slot_tok = jnp.full((n_slots,), n, jnp.int32).at[dest].set(tok[order])
    slot_gate = jnp.zeros((n_slots,), f32).at[dest].set(gate[order])
    blk_exp = jnp.minimum(jnp.searchsorted(p_ends, jnp.arange(n_blocks) * EXPERT_BLOCK, side='right'), N_EXPERTS - 1)
    x_pad = jnp.concatenate([x, jnp.zeros((1, d), x.dtype)], axis=0)
    xb = x_pad[slot_tok].reshape(n_blocks, EXPERT_BLOCK, d)
    w1, w3, w2 = lp['expert_w1'], lp['expert_w3'], lp['expert_w2']

    def expert_block(args):
        xe, e = args
        return (jax.nn.silu(xe @ w1[e]) * (xe @ w3[e])) @ w2[e]

    yb = lax.map(expert_block, (xb, blk_exp)).reshape(n_slots, d)
    y = jax.ops.segment_sum(yb.astype(f32) * slot_gate[:, None], slot_tok, num_segments=n + 1)
    return y[:n].astype(x.dtype)


def _layer(x, x_prev, s_rwkv, attn_fn, lp):
    b, t, d = x.shape
    xn = _rms_norm(x, lp['norm_mix_g'])
    x_cat = jnp.concatenate([x_prev[:, None].astype(xn.dtype), xn], axis=1)
    proj = x_cat @ lp['w_in']
    cur, prev = proj[:, 1:], proj[:, :-1]
    y_rwkv, s_new = _rwkv7_mix(xn, x_cat[:, :-1] - xn, cur[..., :3 * C_RWKV], prev[..., :3 * C_RWKV], s_rwkv, lp)
    q, k, v = (c.reshape(b, t, H_ATT, HEAD_DIM) for c in jnp.split(cur[..., 3 * C_RWKV:], 3, axis=-1))
    y_att = attn_fn(q, k, v).reshape(b, t, C_ATT)
    h = x + jnp.concatenate([y_rwkv, y_att], axis=-1) @ lp['w_out']
    hn = _rms_norm(h, lp['norm_ffn_g']).reshape(b * t, d)
    h = h + _hier_moe(hn, lp).reshape(b, t, d)
    return h, s_new, xn[:, -1], k, v


def setup_inputs(seed: int = 0) -> dict:
    key = jax.random.key(seed)
    ks = iter(jax.random.split(key, 40))
    nrm = lambda shape, scale: scale * jax.random.normal(next(ks), shape, jnp.float32)
    uni = lambda shape, lo, hi: jax.random.uniform(next(ks), shape, jnp.float32, lo, hi)
    att_buf = min(WINDOW_MAX, PAST_LEN)
    L = DEPTH
    return {
        'x_prompt': nrm((BATCH, SEQ, D_MODEL), 1.0),
        'x_sample': nrm((DEC_BATCH, DEC_SEQ, D_MODEL), 1.0),
        'state_rwkv': nrm((L, DEC_BATCH, H_RWKV, HEAD_DIM, HEAD_DIM), 0.1),
        'state_shift': nrm((L, DEC_BATCH, D_MODEL), 1.0),
        'cache_att_k': nrm((L, DEC_BATCH, att_buf, H_ATT, HEAD_DIM), 1.0),
        'cache_att_v': nrm((L, DEC_BATCH, att_buf, H_ATT, HEAD_DIM), 1.0),
        'norm_mix_g': 1.0 + nrm((L, D_MODEL), 0.02),
        'w_in': nrm((L, D_MODEL, 3 * C_RWKV + 3 * C_ATT), D_MODEL ** -0.5),
        'rwkv_mu_rkv': uni((L, 3, C_RWKV), 0.0, 1.0),
        'rwkv_mu_wag': uni((L, 3, D_MODEL), 0.0, 1.0),
        'rwkv_w0': uni((L, C_RWKV), -6.0, 1.0),
        'rwkv_w1': nrm((L, D_MODEL, R_DECAY), D_MODEL ** -0.5),
        'rwkv_w2': nrm((L, R_DECAY, C_RWKV), 0.5 * R_DECAY ** -0.5),
        'rwkv_a0': nrm((L, C_RWKV), 0.5),
        'rwkv_a1': nrm((L, D_MODEL, R_ICLR), D_MODEL ** -0.5),
        'rwkv_a2': nrm((L, R_ICLR, C_RWKV), R_ICLR ** -0.5),
        'rwkv_g1': nrm((L, D_MODEL, R_GATE), D_MODEL ** -0.5),
        'rwkv_g2': nrm((L, R_GATE, C_RWKV), R_GATE ** -0.5),
        'rwkv_k_k': 0.85 + nrm((L, C_RWKV), 0.05),
        'rwkv_k_a': 1.0 + nrm((L, C_RWKV), 0.05),
        'rwkv_r_k': nrm((L, H_RWKV, HEAD_DIM), 0.1),
        'rwkv_ln_w': 1.0 + nrm((L, C_RWKV), 0.02),
        'rwkv_ln_b': nrm((L, C_RWKV), 0.02),
        'w_out': nrm((L, D_MIX, D_MODEL), D_MIX ** -0.5),
        'norm_ffn_g': 1.0 + nrm((L, D_MODEL), 0.02),
        'router_group_w': nrm((L, D_MODEL, N_GROUPS), D_MODEL ** -0.5),
        'router_group_b': nrm((L, N_GROUPS), 0.01),
        'router_expert_w': nrm((L, D_MODEL, N_EXPERTS), D_MODEL ** -0.5),
        'router_expert_b': nrm((L, N_EXPERTS), 0.01),
        'expert_w1': nrm((L, N_EXPERTS, D_MODEL, D_EXPERT), D_MODEL ** -0.5),
        'expert_w3': nrm((L, N_EXPERTS, D_MODEL, D_EXPERT), D_MODEL ** -0.5),
        'expert_w2': nrm((L, N_EXPERTS, D_EXPERT, D_MODEL), D_EXPERT ** -0.5),
        'norm_final_g': 1.0 + nrm((D_MODEL,), 0.02),
    }


def reference(x_prompt, x_sample, state_rwkv, state_shift, cache_att_k, cache_att_v,
              norm_mix_g, w_in, rwkv_mu_rkv, rwkv_mu_wag, rwkv_w0, rwkv_w1, rwkv_w2,
              rwkv_a0, rwkv_a1, rwkv_a2, rwkv_g1, rwkv_g2, rwkv_k_k, rwkv_k_a, rwkv_r_k,
              rwkv_ln_w, rwkv_ln_b, w_out, norm_ffn_g, router_group_w, router_group_b,
              router_expert_w, router_expert_b, expert_w1, expert_w3, expert_w2, norm_final_g):
    b_p, s_p = x_prompt.shape[0], x_prompt.shape[1]
    keep = min(WINDOW_MAX, s_p)
    h_p, h_s = x_prompt, x_sample
    rw_p, sh_p, kc_p, vc_p = [], [], [], []
    rw_s, sh_s, kc_s, vc_s = [], [], [], []
    for layer in range(DEPTH):
        lp = {
            'norm_mix_g': norm_mix_g[layer], 'w_in': w_in[layer],
            'rwkv_mu_rkv': rwkv_mu_rkv[layer], 'rwkv_mu_wag': rwkv_mu_wag[layer],
            'rwkv_w0': rwkv_w0[layer], 'rwkv_w1': rwkv_w1[layer], 'rwkv_w2': rwkv_w2[layer],
            'rwkv_a0': rwkv_a0[layer], 'rwkv_a1': rwkv_a1[layer], 'rwkv_a2': rwkv_a2[layer],
            'rwkv_g1': rwkv_g1[layer], 'rwkv_g2': rwkv_g2[layer],
            'rwkv_k_k': rwkv_k_k[layer], 'rwkv_k_a': rwkv_k_a[layer], 'rwkv_r_k': rwkv_r_k[layer],
            'rwkv_ln_w': rwkv_ln_w[layer], 'rwkv_ln_b': rwkv_ln_b[layer],
            'w_out': w_out[layer], 'norm_ffn_g': norm_ffn_g[layer],
            'router_group_w': router_group_w[layer], 'router_group_b': router_group_b[layer],
            'router_expert_w': router_expert_w[layer], 'router_expert_b': router_expert_b[layer],
            'expert_w1': expert_w1[layer], 'expert_w3': expert_w3[layer], 'expert_w2': expert_w2[layer],
        }
        h_p, st, last, k_new, v_new = _layer(
            h_p, jnp.zeros((b_p, D_MODEL), x_prompt.dtype),
            jnp.zeros((b_p, H_RWKV, HEAD_DIM, HEAD_DIM), state_rwkv.dtype), _dilated_attn_prompt, lp)
        rw_p.append(st)
        sh_p.append(last)
        kc_p.append(k_new[:, s_p - keep:])
        vc_p.append(v_new[:, s_p - keep:])
        attn_sample = functools.partial(_dilated_attn_sample, k_buf=cache_att_k[layer], v_buf=cache_att_v[layer])
        h_s, st, last, k_new, v_new = _layer(h_s, state_shift[layer], state_rwkv[layer], attn_sample, lp)
        rw_s.append(st)
        sh_s.append(last)
        kc_s.append(k_new)
        vc_s.append(v_new)
    y_prompt = _rms_norm(h_p, norm_final_g)
    y_sample = _rms_norm(h_s, norm_final_g)
    new_state_rwkv_prompt = jnp.stack(rw_p, axis=0)
    new_state_shift_prompt = jnp.stack(sh_p, axis=0)
    new_cache_att_k_prompt = jnp.stack(kc_p, axis=0)
    new_cache_att_v_prompt = jnp.stack(vc_p, axis=0)
    new_state_rwkv_sample = jnp.stack(rw_s, axis=0)
    new_state_shift_sample = jnp.stack(sh_s, axis=0)
    new_cache_att_k_sample = jnp.stack(kc_s, axis=0)
    new_cache_att_v_sample = jnp.stack(vc_s, axis=0)
    return (y_prompt, y_sample,
            new_state_rwkv_prompt, new_state_shift_prompt, new_cache_att_k_prompt, new_cache_att_v_prompt,
            new_state_rwkv_sample, new_state_shift_sample, new_cache_att_k_sample, new_cache_att_v_sample)
```

```python
import functools
import math

import jax
import jax.numpy as jnp
from jax import lax
from jax.experimental import pallas as pl
from jax.experimental.pallas import tpu as pltpu

F32 = jnp.float32
BF16 = jnp.bfloat16

HEAD_DIM = 64
GN_EPS = 64e-5
NORM_EPS = 1e-6
DILATED_CONFIGS = ((128, 1), (512, 4), (2048, 16))
N_GROUPS = 4
EXPERTS_PER_GROUP = 8
N_EXPERTS = N_GROUPS * EXPERTS_PER_GROUP
NEG_INF = -1e30

V7X_VMEM_LIMIT = 56 * 1024 * 1024
LANES = 128

PROJ_TILE = 256
RWKV_CHUNK = 64
ATT_BAND = 128
EXPERT_TILE = 256

HIGHEST = lax.Precision.HIGHEST
NT = (((1,), (1,)), ((), ()))
TN = (((0,), (0,)), ((), ()))


def _dot(a, b, precision=None):
    return jnp.dot(a, b, preferred_element_type=F32, precision=precision)


def _dot_split(a, b_bf16):
    hi = a.astype(BF16)
    lo = (a - hi.astype(F32)).astype(BF16)
    return _dot(hi, b_bf16) + _dot(lo, b_bf16)


def _sigmoid(z):
    return 1.0 / (1.0 + jnp.exp(-z))


def _proj_kernel(x_ref, flag_ref, gmix_ref, win_ref, wdx_ref, mu_ref, w0a0_ref, w2a2_ref, g2_ref,
                 kk_ref, ka_ref, seg_ref,
                 r_o, lw_o, k_o, v_o, kkn_o, b_o, g_o, qa_o, kat_o, vat_o, xl_o,
                 xn_carry, pj_carry, *, tiles_per_seq, c_rwkv):
    i = pl.program_id(0)

    @pl.when(i % tiles_per_seq == 0)
    def _():
        xn_carry[...] = jnp.zeros_like(xn_carry)
        pj_carry[...] = jnp.zeros_like(pj_carry)

    c = c_rwkv
    x = x_ref[...]
    tm = x.shape[0]
    ms = jnp.mean(x * x, axis=-1, keepdims=True)
    xn = (x * lax.rsqrt(ms + NORM_EPS)) * gmix_ref[...]
    xn = jnp.where(flag_ref[...] > 0.0, x, xn)
    row = lax.broadcasted_iota(jnp.int32, (tm, 1), 0)
    xn_prev = jnp.where(row == 0, xn_carry[7:8, :], pltpu.roll(xn, 1, axis=0))
    dx = xn_prev - xn

    proj = _dot(xn.astype(BF16), win_ref[...])
    cur = proj[:, :3 * c]
    prev = jnp.where(row == 0, pj_carry[7:8, :], pltpu.roll(cur, 1, axis=0))
    xn_carry[...] = xn[tm - 8:, :]
    pj_carry[...] = cur[tm - 8:, :]
    xl_rows = xl_o.shape[1]
    xl_o[0] = xn[tm - xl_rows:, :]

    mu = mu_ref[...]
    r = cur[:, :c] + mu[0:1] * (prev[:, :c] - cur[:, :c])
    k = cur[:, c:2 * c] + mu[1:2] * (prev[:, c:2 * c] - cur[:, c:2 * c])
    v = cur[:, 2 * c:3 * c] + mu[2:3] * (prev[:, 2 * c:3 * c] - cur[:, 2 * c:3 * c])

    lr = proj[:, 6 * c:] + _dot(dx.astype(BF16), wdx_ref[...])
    lane = lax.broadcasted_iota(jnp.int32, (1, LANES), 1)
    wa_in = jnp.where(lane < 64, jnp.tanh(lr[:, :LANES]), lr[:, :LANES])
    wa = _dot(wa_in.astype(BF16), w2a2_ref[...]) + w0a0_ref[...]
    z = -wa[:, :c]
    softplus = jnp.maximum(z, 0.0) + jnp.log1p(jnp.exp(-jnp.abs(z)))
    lw = -jnp.exp(-softplus - 0.5)
    a = _sigmoid(wa[:, c:])
    g = _dot(_sigmoid(lr[:, LANES:]).astype(BF16), g2_ref[...])

    kk = k * kk_ref[...]
    ss = _dot_split(kk * kk, seg_ref[...])
    kk = kk * lax.rsqrt(jnp.maximum(ss, 1e-24))

    r_o[...] = r
    lw_o[...] = lw
    k_o[...] = k * (1.0 + (a - 1.0) * ka_ref[...])
    v_o[...] = v
    kkn_o[...] = kk
    b_o[...] = kk * a
    g_o[...] = g
    qa_o[...] = proj[:, 3 * c:4 * c]
    kat_o[...] = proj[:, 4 * c:5 * c]
    vat_o[...] = proj[:, 5 * c:6 * c]


def _proj_call(x2, flag, p, tiles_per_seq, tm, xl_rows=8):
    t, d = x2.shape
    c = p['c_rwkv']
    n_tiles = t // tm
    full = lambda a: pl.BlockSpec(a.shape, lambda i: (0,) * a.ndim)
    tok = lambda w: pl.BlockSpec((tm, w), lambda i: (i, 0))
    weights = [p['gmix'], p['win'], p['wdx'], p['mu_rkv'], p['w0a0'], p['w2a2'], p['g2'], p['k_k'], p['k_a'], p['seg']]
    outs = pl.pallas_call(
        functools.partial(_proj_kernel, tiles_per_seq=tiles_per_seq, c_rwkv=c),
        grid=(n_tiles,),
        in_specs=[tok(d), tok(1)] + [full(w) for w in weights],
        out_specs=[tok(c)] * 10 + [pl.BlockSpec((1, xl_rows, d), lambda i: (i, 0, 0))],
        out_shape=[jax.ShapeDtypeStruct((t, c), F32)] * 10 + [jax.ShapeDtypeStruct((n_tiles, xl_rows, d), F32)],
        scratch_shapes=[pltpu.VMEM((8, d), F32), pltpu.VMEM((8, 3 * c), F32)],
        compiler_params=pltpu.CompilerParams(dimension_semantics=("arbitrary",), vmem_limit_bytes=V7X_VMEM_LIMIT),
        name="proj",
    )(x2, flag, *weights)
    return outs


def _rwkv_kernel(r_ref, lw_ref, k_ref, v_ref, kk_ref, b_ref, g_ref, s0_ref, rk_ref, lnw_ref, lnb_ref,
                 y_ref, sout_ref, s_scr, *, n_heads):
    ci = pl.program_id(1)

    @pl.when(ci == 0)
    def _():
        s_scr[...] = s0_ref[0]

    hd = HEAD_DIM
    lw = lw_ref[...]
    L = lw.shape[0]
    ti = lax.broadcasted_iota(jnp.int32, (L, L), 0)
    tj = lax.broadcasted_iota(jnp.int32, (L, L), 1)
    incl = ti >= tj
    strict = ti > tj
    cs = _dot(incl.astype(F32), lw, HIGHEST)
    cp = cs - lw
    cm = cs[L // 2 - 1:L // 2, :]
    c_last = cs[L - 1:L, :]
    r, k, v, kk, b = r_ref[...], k_ref[...], v_ref[...], kk_ref[...], b_ref[...]
    e_dn = jnp.exp(cm - cs)
    rt = r * jnp.exp(cs - cm)
    kkt = kk * jnp.exp(cp - cm)
    bt = b * e_dn
    kt = k * e_dn
    kg = kk * jnp.exp(cp)
    rg = r * jnp.exp(cs)
    e_l = jnp.exp(c_last - cs)
    bh = b * e_l
    kh = k * e_l
    g_last = jnp.exp(c_last)
    n_apply = max(1, int(math.log2(L)))

    for h in range(n_heads):
        sl = slice(h * hd, (h + 1) * hd)
        a_all = lax.dot_general(jnp.concatenate([kkt[:, sl], rt[:, sl]], axis=0),
                                jnp.concatenate([bt[:, sl], kt[:, sl]], axis=0),
                                NT, precision=HIGHEST, preferred_element_type=F32)
        a_ab = jnp.where(strict, a_all[:L, :L], 0.0)
        a_ak = jnp.where(strict, a_all[:L, L:], 0.0)
        a_rb = jnp.where(incl, a_all[L:, :L], 0.0)
        a_rk = jnp.where(incl, a_all[L:, L:], 0.0)
        vh = v[:, sl]
        xw = kg[:, sl]
        xu = _dot(a_ak, vh, HIGHEST)
        nm = -a_ab
        xw = xw + _dot(nm, xw, HIGHEST)
        xu = xu + _dot(nm, xu, HIGHEST)
        for _ in range(n_apply - 1):
            nm = _dot(nm, nm, HIGHEST)
            xw = xw + _dot(nm, xw, HIGHEST)
            xu = xu + _dot(nm, xu, HIGHEST)
        w_m = xw
        u0 = -xu
        r_hat = rg[:, sl] - _dot(a_rb, w_m, HIGHEST)
        y0 = _dot(a_rb, u0, HIGHEST) + _dot(a_rk, vh, HIGHEST)
        s_old = s_scr[h]
        ws = lax.dot_general(jnp.concatenate([w_m, r_hat], axis=0), s_old, NT,
                             precision=HIGHEST, preferred_element_type=F32)
        u = u0 - ws[:L]
        y = ws[L:] + y0
        s_new = s_old * g_last[:, sl] + lax.dot_general(
            jnp.concatenate([u, vh], axis=0), jnp.concatenate([bh[:, sl], kh[:, sl]], axis=0), TN,
            precision=HIGHEST, preferred_element_type=F32)
        s_scr[h] = s_new

        mean = jnp.mean(y, axis=-1, keepdims=True)
        yc = y - mean
        var = jnp.mean(yc * yc, axis=-1, keepdims=True)
        yn = yc * lax.rsqrt(var + GN_EPS) * lnw_ref[:, sl] + lnb_ref[:, sl]
        bonus = jnp.sum(r[:, sl] * k[:, sl] * rk_ref[:, sl], axis=-1, keepdims=True) * vh
        y_ref[:, sl] = (yn + bonus) * g_ref[:, sl]

    @pl.when(ci == pl.num_programs(1) - 1)
    def _():
        sout_ref[0] = s_scr[...]


def _rwkv_call(vecs, s0, p, n_seq, seq_len, chunk):
    c = p['c_rwkv']
    n_heads = c // HEAD_DIM
    n_chunks = seq_len // chunk
    tok = pl.BlockSpec((chunk, c), lambda bi, ci: (bi * n_chunks + ci, 0))
    st = pl.BlockSpec((1, n_heads, HEAD_DIM, HEAD_DIM), lambda bi, ci: (bi, 0, 0, 0))
    rowvec = pl.BlockSpec((1, c), lambda bi, ci: (0, 0))
    y, s_out = pl.pallas_call(
        functools.partial(_rwkv_kernel, n_heads=n_heads),
        grid=(n_seq, n_chunks),
        in_specs=[tok] * 7 + [st, rowvec, rowvec, rowvec],
        out_specs=[tok, st],
        out_shape=[jax.ShapeDtypeStruct((n_seq * seq_len, c), F32),
                   jax.ShapeDtypeStruct((n_seq, n_heads, HEAD_DIM, HEAD_DIM), F32)],
        scratch_shapes=[pltpu.VMEM((n_heads, HEAD_DIM, HEAD_DIM), F32)],
        compiler_params=pltpu.CompilerParams(dimension_semantics=("arbitrary", "arbitrary"),
                                             vmem_limit_bytes=V7X_VMEM_LIMIT),
        name="rwkv",
    )(*vecs, s0, p['r_k'], p['ln_w'], p['ln_b'])
    return y, s_out


def _attn_prompt_kernel(q_ref, k_ref, v_ref, o_ref, m_scr, l_scr, acc_scr):
    s_len = q_ref.shape[0]
    band = ATT_BAND
    n_blk = s_len // band
    m_scr[...] = jnp.full_like(m_scr, NEG_INF)
    l_scr[...] = jnp.zeros_like(l_scr)
    acc_scr[...] = jnp.zeros_like(acc_scr)

    lane = lax.broadcasted_iota(jnp.int32, (1, LANES), 1)
    head0 = lane < HEAD_DIM
    qi = lax.broadcasted_iota(jnp.int32, (band, 2 * band), 0)
    kj = lax.broadcasted_iota(jnp.int32, (band, 2 * band), 1)
    in_band = (kj >= qi) & (kj <= qi + band)
    scale = HEAD_DIM ** -0.5

    for window, dil in DILATED_CONFIGS:
        assert window // dil == band
        per_res = n_blk // dil

        def body(i, carry, dil=dil, per_res=per_res):
            res = i // per_res
            blk = i % per_res
            start = res + blk * (band * dil)
            prev = jnp.maximum(start - band * dil, 0)
            rows = pl.ds(start, band, stride=dil) if dil > 1 else pl.ds(start, band)
            prows = pl.ds(prev, band, stride=dil) if dil > 1 else pl.ds(prev, band)
            q = q_ref[rows, :] * scale
            kc = jnp.concatenate([k_ref[prows, :], k_ref[rows, :]], axis=0).astype(BF16)
            vc = jnp.concatenate([v_ref[prows, :], v_ref[rows, :]], axis=0).astype(BF16)
            mask = in_band & (kj >= jnp.where(blk > 0, 0, band))
            m_old = m_scr[rows, :]
            l_old = l_scr[rows, :]
            acc_old = acc_scr[rows, :]
            q0 = jnp.where(head0, q, 0.0).astype(BF16)
            q1 = jnp.where(head0, 0.0, q).astype(BF16)
            s0 = jnp.where(mask, lax.dot_general(q0, kc, NT, preferred_element_type=F32), NEG_INF)
            s1 = jnp.where(mask, lax.dot_general(q1, kc, NT, preferred_element_type=F32), NEG_INF)
            rm = jnp.where(head0, jnp.max(s0, axis=-1, keepdims=True), jnp.max(s1, axis=-1, keepdims=True))
            m_new = jnp.maximum(m_old, rm)
            alpha = jnp.exp(m_old - m_new)
            p0 = jnp.exp(s0 - m_new[:, 0:1])
            p1 = jnp.exp(s1 - m_new[:, LANES - 1:LANES])
            rs = jnp.where(head0, jnp.sum(p0, axis=-1, keepdims=True), jnp.sum(p1, axis=-1, keepdims=True))
            pv = jnp.where(head0, _dot(p0.astype(BF16), vc), _dot(p1.astype(BF16), vc))
            m_scr[rows, :] = m_new
            l_scr[rows, :] = alpha * l_old + rs
            acc_scr[rows, :] = alpha * acc_old + pv
            return carry

        lax.fori_loop(0, n_blk, body, 0)

    o_ref[...] = acc_scr[...] / l_scr[...]


def _attn_prompt_call(q, k, v, n_seq, seq_len):
    c = q.shape[1]
    n_pairs = c // LANES
    blk = pl.BlockSpec((seq_len, LANES), lambda bi, hi: (bi, hi))
    return pl.pallas_call(
        _attn_prompt_kernel,
        grid=(n_seq, n_pairs),
        in_specs=[blk, blk, blk],
        out_specs=blk,
        out_shape=jax.ShapeDtypeStruct((n_seq * seq_len, c), F32),
        scratch_shapes=[pltpu.VMEM((seq_len, LANES), F32)] * 3,
        compiler_params=pltpu.CompilerParams(dimension_semantics=("arbitrary", "arbitrary"),
                                             vmem_limit_bytes=V7X_VMEM_LIMIT),
        name="attn_prompt",
    )(q, k, v)


def _attn_sample_kernel(q_ref, kn_ref, vn_ref, kc_ref, vc_ref, o_ref, *, n_new, n_heads):
    n_buf = kc_ref.shape[1]
    t_pad = kn_ref.shape[1]
    c = q_ref.shape[2]
    q = q_ref[0] * (HEAD_DIM ** -0.5)
    lane_head = lax.broadcasted_iota(jnp.int32, (1, c), 1) // HEAD_DIM
    qs = jnp.concatenate([jnp.where(lane_head == h, q, 0.0) for h in range(n_heads)], axis=0).astype(BF16)
    n_rows = n_heads * t_pad
    t_idx = lax.broadcasted_iota(jnp.int32, (n_rows, 1), 0) % t_pad

    def multiplicity(dist):
        mult = jnp.zeros(dist.shape, F32)
        for window, dil in DILATED_CONFIGS:
            hit = (dist >= 0) & (dist <= window) & (dist % dil == 0)
            mult = mult + jnp.where(hit, 1.0, 0.0)
        return mult

    jc = lax.broadcasted_iota(jnp.int32, (1, n_buf), 1)
    mult_c = multiplicity(n_buf + t_idx - jc)
    jn = lax.broadcasted_iota(jnp.int32, (1, t_pad), 1)
    mult_n = jnp.where(jn < n_new, multiplicity(t_idx - jn), 0.0)

    sc = lax.dot_general(qs, kc_ref[0].astype(BF16), NT, preferred_element_type=F32)
    sn = lax.dot_general(qs, kn_ref[0].astype(BF16), NT, preferred_element_type=F32)
    sc = jnp.where(mult_c > 0.0, sc, NEG_INF)
    sn = jnp.where(mult_n > 0.0, sn, NEG_INF)
    m = jnp.maximum(jnp.max(sc, axis=-1, keepdims=True), jnp.max(sn, axis=-1, keepdims=True))
    pc = mult_c * jnp.exp(sc - m)
    pn = mult_n * jnp.exp(sn - m)
    l = jnp.sum(pc, axis=-1, keepdims=True) + jnp.sum(pn, axis=-1, keepdims=True)
    o = (_dot(pc.astype(BF16), vc_ref[0].astype(BF16)) + _dot(pn.astype(BF16), vn_ref[0].astype(BF16))) / l
    out = jnp.zeros((t_pad, c), F32)
    for h in range(n_heads):
        out = out + jnp.where(lane_head == h, o[h * t_pad:(h + 1) * t_pad, :], 0.0)
    o_ref[0] = out


def _attn_sample_call(q, kn, vn, k_buf, v_buf, n_new):
    b, t_pad, c = q.shape
    n_buf = k_buf.shape[1]
    new = pl.BlockSpec((1, t_pad, c), lambda bi: (bi, 0, 0))
    buf = pl.BlockSpec((1, n_buf, c), lambda bi: (bi, 0, 0))
    return pl.pallas_call(
        functools.partial(_attn_sample_kernel, n_new=n_new, n_heads=c // HEAD_DIM),
        grid=(b,),
        in_specs=[new, new, new, buf, buf],
        out_specs=new,
        out_shape=jax.ShapeDtypeStruct((b, t_pad, c), F32),
        compiler_params=pltpu.CompilerParams(dimension_semantics=("arbitrary",), vmem_limit_bytes=V7X_VMEM_LIMIT),
        name="attn_sample",
    )(q, kn, vn, k_buf, v_buf)


def _post_kernel(x_ref, yr_ref, ya_ref, wo_ref, gffn_ref, rw_hi_ref, rw_lo_ref, rb_ref,
                 h_o, hn_o, lg_o):
    c = yr_ref.shape[1]
    h = (x_ref[...] + _dot(yr_ref[...].astype(BF16), wo_ref[:c, :]) + _dot(ya_ref[...].astype(BF16), wo_ref[c:, :]))
    ms = jnp.mean(h * h, axis=-1, keepdims=True)
    hn = (h * lax.rsqrt(ms + NORM_EPS)) * gffn_ref[...]
    h_o[...] = h
    hn_o[...] = hn.astype(BF16)
    hi = hn.astype(BF16)
    lo = (hn - hi.astype(F32)).astype(BF16)
    lg_o[...] = (_dot(hi, rw_hi_ref[...]) + _dot(hi, rw_lo_ref[...]) + _dot(lo, rw_hi_ref[...])) + rb_ref[...]


def _post_call(x2, yr, ya, p, tm):
    t, d = x2.shape
    c = yr.shape[1]
    full = lambda a: pl.BlockSpec(a.shape, lambda i: (0,) * a.ndim)
    tok = lambda w: pl.BlockSpec((tm, w), lambda i: (i, 0))
    weights = [p['wout'], p['gffn'], p['rw_hi'], p['rw_lo'], p['rb']]
    return pl.pallas_call(
        _post_kernel,
        grid=(t // tm,),
        in_specs=[tok(d), tok(c), tok(c)] + [full(w) for w in weights],
        out_specs=[tok(d), tok(d), tok(LANES)],
        out_shape=[jax.ShapeDtypeStruct((t, d), F32), jax.ShapeDtypeStruct((t, d), BF16),
                   jax.ShapeDtypeStruct((t, LANES), F32)],
        compiler_params=pltpu.CompilerParams(dimension_semantics=("arbitrary",), vmem_limit_bytes=V7X_VMEM_LIMIT),
        name="post",
    )(x2, yr, ya, *weights)


def _expert_kernel(be_ref, nb_ref, xs_ref, gate_ref, w1_ref, w3_ref, w2_ref, y_ref):
    i = pl.program_id(0)

    @pl.when(i < nb_ref[0])
    def _():
        xs = xs_ref[...]
        h1 = _dot(xs, w1_ref[...])
        h3 = _dot(xs, w3_ref[...])
        act = (h1 * _sigmoid(h1)) * h3
        y_ref[...] = _dot(act.astype(BF16), w2_ref[...]) * gate_ref[...]

    @pl.when(i >= nb_ref[0])
    def _():
        y_ref[...] = jnp.zeros_like(y_ref)


def _expert_call(blk_exp, n_used, xs, slot_gate, p, bm):
    n_slots, d = xs.shape
    de = p['w1'].shape[2]
    grid_spec = pltpu.PrefetchScalarGridSpec(
        num_scalar_prefetch=2,
        grid=(n_slots // bm,),
        in_specs=[pl.BlockSpec((bm, d), lambda i, be, nb: (i, 0)),
                  pl.BlockSpec((bm, 1), lambda i, be, nb: (i, 0)),
                  pl.BlockSpec((None, d, de), lambda i, be, nb: (be[i], 0, 0)),
                  pl.BlockSpec((None, d, de), lambda i, be, nb: (be[i], 0, 0)),
                  pl.BlockSpec((None, de, d), lambda i, be, nb: (be[i], 0, 0))],
        out_specs=pl.BlockSpec((bm, d), lambda i, be, nb: (i, 0)),
    )
    return pl.pallas_call(
        _expert_kernel,
        grid_spec=grid_spec,
        out_shape=jax.ShapeDtypeStruct((n_slots, d), F32),
        compiler_params=pltpu.CompilerParams(dimension_semantics=("arbitrary",), vmem_limit_bytes=V7X_VMEM_LIMIT),
        name="experts",
    )(blk_exp, n_used, xs, slot_gate, p['w1'], p['w3'], p['w2'])


def _final_kernel(h_ref, y1_ref, y2_ref, gfin_ref, o_ref):
    h = h_ref[...] + (y1_ref[...] + y2_ref[...])
    ms = jnp.mean(h * h, axis=-1, keepdims=True)
    o_ref[...] = (h * lax.rsqrt(ms + NORM_EPS)) * gfin_ref[...]


def _final_call(h, y1, y2, gfin, tm):
    t, d = h.shape
    tok = pl.BlockSpec((tm, d), lambda i: (i, 0))
    return pl.pallas_call(
        _final_kernel,
        grid=(t // tm,),
        in_specs=[tok, tok, tok, pl.BlockSpec((1, d), lambda i: (0, 0))],
        out_specs=tok,
        out_shape=jax.ShapeDtypeStruct((t, d), F32),
        compiler_params=pltpu.CompilerParams(dimension_semantics=("arbitrary",), vmem_limit_bytes=V7X_VMEM_LIMIT),
        name="final",
    )(h, y1, y2, gfin)


def _route(logits, bm):
    n = logits.shape[0]
    p_group = jax.nn.softmax(logits[:, :N_GROUPS], axis=-1)
    g_w, g_idx = lax.top_k(p_group, 1)
    logit_e = logits[:, N_GROUPS:N_GROUPS + N_EXPERTS].reshape(n, N_GROUPS, EXPERTS_PER_GROUP)
    logit_e = jnp.take_along_axis(logit_e, jnp.broadcast_to(g_idx[:, :, None], (n, 1, EXPERTS_PER_GROUP)), axis=1)[:, 0]
    e_w, e_idx = lax.top_k(jax.nn.softmax(logit_e, axis=-1), 2)
    gates = (g_w * e_w / jnp.sum(e_w, axis=-1, keepdims=True)).reshape(-1)
    eid = (g_idx * EXPERTS_PER_GROUP + e_idx).reshape(-1).astype(jnp.int32)
    m = eid.shape[0]
    tok = jnp.repeat(jnp.arange(n, dtype=jnp.int32), 2)
    order = jnp.argsort(eid)
    e_sorted = eid[order]
    counts = jnp.bincount(eid, length=N_EXPERTS).astype(jnp.int32)
    starts = jnp.cumsum(counts) - counts
    padded = (counts + bm - 1) // bm * bm
    p_ends = jnp.cumsum(padded)
    p_starts = p_ends - padded
    dest_sorted = p_starts[e_sorted] + jnp.arange(m, dtype=jnp.int32) - starts[e_sorted]
    n_blocks = -(-m // bm) + N_EXPERTS
    n_slots = n_blocks * bm
    slot_tok = jnp.zeros((n_slots,), jnp.int32).at[dest_sorted].set(tok[order])
    slot_gate = jnp.zeros((n_slots,), F32).at[dest_sorted].set(gates[order])
    dest = jnp.zeros((m,), jnp.int32).at[order].set(dest_sorted).reshape(n, 2)
    blk_exp = jnp.minimum(jnp.searchsorted(p_ends, jnp.arange(n_blocks, dtype=jnp.int32) * bm, side='right'),
                          N_EXPERTS - 1).astype(jnp.int32)
    n_used = (p_ends[-1] // bm).astype(jnp.int32).reshape(1)
    return slot_tok, slot_gate.reshape(n_slots, 1), dest, blk_exp, n_used


def _moe_and_final(x2, yr, ya, p, tm):
    h, hn, logits = _post_call(x2, yr, ya, p, tm)
    slot_tok, slot_gate, dest, blk_exp, n_used = _route(logits, EXPERT_TILE)
    xs = jnp.take(hn, slot_tok, axis=0)
    yb = _expert_call(blk_exp, n_used, xs, slot_gate, p, EXPERT_TILE)
    y1 = jnp.take(yb, dest[:, 0], axis=0)
    y2 = jnp.take(yb, dest[:, 1], axis=0)
    return _final_call(h, y1, y2, p['gfin'], tm)


def _prep_params(layer, norm_mix_g, w_in, rwkv_mu_rkv, rwkv_mu_wag, rwkv_w0, rwkv_w1, rwkv_w2, rwkv_a0, rwkv_a1,
                 rwkv_a2, rwkv_g1, rwkv_g2, rwkv_k_k, rwkv_k_a, rwkv_r_k, rwkv_ln_w, rwkv_ln_b, w_out, norm_ffn_g,
                 router_group_w, router_group_b, router_expert_w, router_expert_b, expert_w1, expert_w3, expert_w2,
                 norm_final_g):
    d = w_in.shape[1]
    c = rwkv_w0.shape[1]
    row = lambda a: a.reshape(1, -1).astype(F32)
    lowrank = jnp.concatenate([rwkv_w1[layer], rwkv_a1[layer], rwkv_g1[layer]], axis=1)
    mx = rwkv_mu_wag[layer]
    r_w = rwkv_w1.shape[2]
    r_a = rwkv_a1.shape[2]
    r_g = rwkv_g1.shape[2]
    assert r_w + r_a == LANES and r_g == LANES
    mx_cols = jnp.concatenate([jnp.broadcast_to(mx[0][:, None], (d, r_w)), jnp.broadcast_to(mx[1][:, None], (d, r_a)),
                               jnp.broadcast_to(mx[2][:, None], (d, r_g))], axis=1)
    w2a2 = jnp.zeros((LANES, 2 * c), F32)
    w2a2 = w2a2.at[:r_w, :c].set(rwkv_w2[layer]).at[r_w:, c:].set(rwkv_a2[layer])
    head = jnp.arange(c) // HEAD_DIM
    rw = jnp.zeros((d, LANES), F32)
    rw = rw.at[:, :N_GROUPS].set(router_group_w[layer]).at[:, N_GROUPS:N_GROUPS + N_EXPERTS].set(router_expert_w[layer])
    rw_hi = rw.astype(BF16)
    rb = jnp.zeros((1, LANES), F32)
    rb = rb.at[0, :N_GROUPS].set(router_group_b[layer]).at[0, N_GROUPS:N_GROUPS + N_EXPERTS].set(router_expert_b[layer])
    return {
        'c_rwkv': c,
        'gmix': row(norm_mix_g[layer]),
        'win': jnp.concatenate([w_in[layer], lowrank], axis=1).astype(BF16),
        'wdx': (mx_cols * lowrank).astype(BF16),
        'mu_rkv': rwkv_mu_rkv[layer],
        'w0a0': jnp.concatenate([row(rwkv_w0[layer]), row(rwkv_a0[layer])], axis=1),
        'w2a2': w2a2.astype(BF16),
        'g2': rwkv_g2[layer].astype(BF16),
        'k_k': row(rwkv_k_k[layer]),
        'k_a': row(rwkv_k_a[layer]),
        'seg': (head[:, None] == head[None, :]).astype(BF16),
        'r_k': row(rwkv_r_k[layer]),
        'ln_w': row(rwkv_ln_w[layer]),
        'ln_b': row(rwkv_ln_b[layer]),
        'wout': w_out[layer].astype(BF16),
        'gffn': row(norm_ffn_g[layer]),
        'rw_hi': rw_hi,
        'rw_lo': (rw - rw_hi.astype(F32)).astype(BF16),
        'rb': rb,
        'w1': expert_w1[layer].astype(BF16),
        'w3': expert_w3[layer].astype(BF16),
        'w2': expert_w2[layer].astype(BF16),
        'gfin': row(norm_final_g),
    }


def _prompt_group(x, p):
    b, s, d = x.shape
    c = p['c_rwkv']
    x2 = x.reshape(b * s, d)
    tm = PROJ_TILE
    flag = jnp.zeros((b * s, 1), F32)
    outs = _proj_call(x2, flag, p, s // tm, tm)
    r, lw, k, v, kk, bb, g, qa, ka, va, xl = outs
    s0 = jnp.zeros((b, c // HEAD_DIM, HEAD_DIM, HEAD_DIM), F32)
    yr, s_new = _rwkv_call((r, lw, k, v, kk, bb, g), s0, p, b, s, RWKV_CHUNK)
    ya = _attn_prompt_call(qa, ka, va, b, s)
    y = _moe_and_final(x2, yr, ya, p, tm)
    shift = xl.reshape(b, s // tm, 8, d)[:, -1, 7, :]
    keep = min(max(w for w, _ in DILATED_CONFIGS), s)
    k_keep = ka.reshape(b, s, c // HEAD_DIM, HEAD_DIM)[:, s - keep:]
    v_keep = va.reshape(b, s, c // HEAD_DIM, HEAD_DIM)[:, s - keep:]
    return y.reshape(b, s, d), s_new, shift, k_keep, v_keep


def _sample_group(x, shift0, s0, k_buf, v_buf, p):
    b, t, d = x.shape
    c = p['c_rwkv']
    n_heads = c // HEAD_DIM
    t_pad = 8
    xc = jnp.concatenate([shift0[:, None, :], x, jnp.zeros((b, t_pad - 1 - t, d), x.dtype)], axis=1)
    flag = jnp.zeros((b, t_pad, 1), F32).at[:, 0].set(1.0)
    outs = _proj_call(xc.reshape(b * t_pad, d), flag.reshape(b * t_pad, 1), p, 1, b * t_pad, xl_rows=b * t_pad)
    xl = outs[10]
    live = (jnp.arange(t_pad) < t)[None, :, None]
    shifted = [jnp.where(live, jnp.roll(o.reshape(b, t_pad, c), -1, axis=1), 0.0) for o in outs[:10]]
    r, lw, k, v, kk, bb, g, qa, ka, va = shifted
    flat = lambda z: z.reshape(b * t_pad, c)
    yr, s_new = _rwkv_call(tuple(flat(z) for z in (r, lw, k, v, kk, bb, g)), s0, p, b, t_pad, t_pad)
    n_buf = k_buf.shape[1]
    ya = _attn_sample_call(qa, ka, va, k_buf.reshape(b, n_buf, c), v_buf.reshape(b, n_buf, c), t)
    x_pad = jnp.concatenate([x, jnp.zeros((b, t_pad - t, d), x.dtype)], axis=1).reshape(b * t_pad, d)
    y = _moe_and_final(x_pad, yr, flat(ya), p, b * t_pad // 2)
    y = y.reshape(b, t_pad, d)[:, :t]
    shift = xl.reshape(b, t_pad, d)[:, t]
    return (y, s_new, shift, ka[:, :t].reshape(b, t, n_heads, HEAD_DIM), va[:, :t].reshape(b, t, n_heads, HEAD_DIM))


def kernel(x_prompt, x_sample, state_rwkv, state_shift, cache_att_k, cache_att_v, norm_mix_g, w_in, rwkv_mu_rkv, rwkv_mu_wag, rwkv_w0, rwkv_w1, rwkv_w2, rwkv_a0, rwkv_a1, rwkv_a2, rwkv_g1, rwkv_g2, rwkv_k_k, rwkv_k_a, rwkv_r_k, rwkv_ln_w, rwkv_ln_b, w_out, norm_ffn_g, router_group_w, router_group_b, router_expert_w, router_expert_b, expert_w1, expert_w3, expert_w2, norm_final_g):
    assert w_in.shape[0] == 1, "single-layer trunk"
    p = _prep_params(0, norm_mix_g, w_in, rwkv_mu_rkv, rwkv_mu_wag, rwkv_w0, rwkv_w1, rwkv_w2, rwkv_a0, rwkv_a1,
                     rwkv_a2, rwkv_g1, rwkv_g2, rwkv_k_k, rwkv_k_a, rwkv_r_k, rwkv_ln_w, rwkv_ln_b, w_out,
                     norm_ffn_g, router_group_w, router_group_b, router_expert_w, router_expert_b, expert_w1,
                     expert_w3, expert_w2, norm_final_g)
    y_p, rw_p, sh_p, kc_p, vc_p = _prompt_group(x_prompt, p)
    y_s, rw_s, sh_s, kc_s, vc_s = _sample_group(x_sample, state_shift[0], state_rwkv[0], cache_att_k[0],
                                                cache_att_v[0], p)
    return (y_p, y_s, rw_p[None], sh_p[None], kc_p[None], vc_p[None], rw_s[None], sh_s[None], kc_s[None], vc_s[None])
```

```python
import functools
import math

import jax
import jax.numpy as jnp
from jax import lax
from jax.experimental import pallas as pl
from jax.experimental.pallas import tpu as pltpu

F32 = jnp.float32
BF16 = jnp.bfloat16

HEAD_DIM = 64
GN_EPS = 64e-5
NORM_EPS = 1e-6
DILATED_CONFIGS = ((128, 1), (512, 4), (2048, 16))
N_GROUPS = 4
EXPERTS_PER_GROUP = 8
N_EXPERTS = N_GROUPS * EXPERTS_PER_GROUP
NEG_INF = -1e30

V7X_VMEM_LIMIT = 56 * 1024 * 1024
LANES = 128

PROJ_TILE = 256
RWKV_CHUNK = 64
ATT_BAND = 128
EXPERT_TILE = 256
ATT_UNROLL = 4

HIGHEST = lax.Precision.HIGHEST
NN = (((1,), (0,)), ((), ()))
NT = (((1,), (1,)), ((), ()))
TN = (((0,), (0,)), ((), ()))


def _dot(a, b, precision=None):
    return jnp.dot(a, b, preferred_element_type=F32, precision=precision)


def _dot_split(a, b_bf16):
    hi = a.astype(BF16)
    lo = (a - hi.astype(F32)).astype(BF16)
    return _dot(hi, b_bf16) + _dot(lo, b_bf16)


def _mm(a, b, dims, mode):
    if mode == 'f32':
        return lax.dot_general(a, b, dims, precision=HIGHEST, preferred_element_type=F32)
    a_hi = a.astype(BF16)
    b_hi = b.astype(BF16)
    out = lax.dot_general(a_hi, b_hi, dims, preferred_element_type=F32)
    if mode == 'x3':
        a_lo = (a - a_hi.astype(F32)).astype(BF16)
        b_lo = (b - b_hi.astype(F32)).astype(BF16)
        out = out + lax.dot_general(a_hi, b_lo, dims, preferred_element_type=F32)
        out = out + lax.dot_general(a_lo, b_hi, dims, preferred_element_type=F32)
    return out


RWKV_MODES = dict(A='bf16', AV='bf16', SQ='bf16', AP='bf16', RB='bf16', RK='bf16', WS='bf16', UP='bf16')


def _pre(x, mode):
    return x.astype(BF16) if mode == 'bf16' else x


def _sigmoid(z):
    return 1.0 / (1.0 + jnp.exp(-z))


def _proj_kernel(x_ref, flag_ref, gmix_ref, win_ref, wdx_ref, mu_ref, w0a0_ref, w2a2_ref, g2_ref,
                 kk_ref, ka_ref, seg_ref,
                 r_o, lw_o, k_o, v_o, kkn_o, b_o, g_o, qa_o, kat_o, vat_o, xl_o,
                 xn_carry, pj_carry, *, tiles_per_seq, c_rwkv):
    i = pl.program_id(0)

    @pl.when(i % tiles_per_seq == 0)
    def _():
        xn_carry[...] = jnp.zeros_like(xn_carry)
        pj_carry[...] = jnp.zeros_like(pj_carry)

    c = c_rwkv
    x = x_ref[...]
    tm = x.shape[0]
    ms = jnp.mean(x * x, axis=-1, keepdims=True)
    xn = (x * lax.rsqrt(ms + NORM_EPS)) * gmix_ref[...]
    xn = jnp.where(flag_ref[...] > 0.0, x, xn)
    row = lax.broadcasted_iota(jnp.int32, (tm, 1), 0)
    xn_prev = jnp.where(row == 0, xn_carry[7:8, :], pltpu.roll(xn, 1, axis=0))
    dx = xn_prev - xn

    proj = _dot(xn.astype(BF16), win_ref[...])
    cur = proj[:, :3 * c]
    prev = jnp.where(row == 0, pj_carry[7:8, :], pltpu.roll(cur, 1, axis=0))
    xn_carry[...] = xn[tm - 8:, :]
    pj_carry[...] = cur[tm - 8:, :]
    xl_rows = xl_o.shape[1]
    xl_o[0] = xn[tm - xl_rows:, :]

    mu = mu_ref[...]
    r = cur[:, :c] + mu[0:1] * (prev[:, :c] - cur[:, :c])
    k = cur[:, c:2 * c] + mu[1:2] * (prev[:, c:2 * c] - cur[:, c:2 * c])
    v = cur[:, 2 * c:3 * c] + mu[2:3] * (prev[:, 2 * c:3 * c] - cur[:, 2 * c:3 * c])

    lr = proj[:, 6 * c:] + _dot(dx.astype(BF16), wdx_ref[...])
    lane = lax.broadcasted_iota(jnp.int32, (1, LANES), 1)
    wa_in = jnp.where(lane < 64, jnp.tanh(lr[:, :LANES]), lr[:, :LANES])
    wa = _dot(wa_in.astype(BF16), w2a2_ref[...]) + w0a0_ref[...]
    z = -wa[:, :c]
    softplus = jnp.maximum(z, 0.0) + jnp.log1p(jnp.exp(-jnp.abs(z)))
    lw = -jnp.exp(-softplus - 0.5)
    a = _sigmoid(wa[:, c:])
    g = _dot(_sigmoid(lr[:, LANES:]).astype(BF16), g2_ref[...])

    kk = k * kk_ref[...]
    ss = _dot_split(kk * kk, seg_ref[...])
    kk = kk * lax.rsqrt(jnp.maximum(ss, 1e-24))

    r_o[...] = r
    lw_o[...] = lw
    k_o[...] = k * (1.0 + (a - 1.0) * ka_ref[...])
    v_o[...] = v
    kkn_o[...] = kk
    b_o[...] = kk * a
    g_o[...] = g
    qa_o[...] = proj[:, 3 * c:4 * c]
    kat_o[...] = proj[:, 4 * c:5 * c]
    vat_o[...] = proj[:, 5 * c:6 * c]


def _proj_call(x2, flag, p, tiles_per_seq, tm, xl_rows=8):
    t, d = x2.shape
    c = p['c_rwkv']
    n_tiles = t // tm
    full = lambda a: pl.BlockSpec(a.shape, lambda i: (0,) * a.ndim)
    tok = lambda w: pl.BlockSpec((tm, w), lambda i: (i, 0))
    weights = [p['gmix'], p['win'], p['wdx'], p['mu_rkv'], p['w0a0'], p['w2a2'], p['g2'], p['k_k'], p['k_a'], p['seg']]
    outs = pl.pallas_call(
        functools.partial(_proj_kernel, tiles_per_seq=tiles_per_seq, c_rwkv=c),
        grid=(n_tiles,),
        in_specs=[tok(d), tok(1)] + [full(w) for w in weights],
        out_specs=[tok(c)] * 10 + [pl.BlockSpec((1, xl_rows, d), lambda i: (i, 0, 0))],
        out_shape=[jax.ShapeDtypeStruct((t, c), F32)] * 10 + [jax.ShapeDtypeStruct((n_tiles, xl_rows, d), F32)],
        scratch_shapes=[pltpu.VMEM((8, d), F32), pltpu.VMEM((8, 3 * c), F32)],
        compiler_params=pltpu.CompilerParams(dimension_semantics=("arbitrary",), vmem_limit_bytes=V7X_VMEM_LIMIT),
        name="proj",
    )(x2, flag, *weights)
    return outs


GROUP_LANES = 256
GROUP_HEADS = GROUP_LANES // HEAD_DIM


def _rwkv_kernel(r_ref, lw_ref, k_ref, v_ref, kk_ref, b_ref, g_ref, s0_ref, rk_ref, lnw_ref, lnb_ref, seg_ref,
                 y_ref, sout_ref, s_scr):
    ci = pl.program_id(1)
    nb, L, c = r_ref.shape
    gw, gh, hd = GROUP_LANES, GROUP_HEADS, HEAD_DIM
    n_groups = c // gw
    md = RWKV_MODES

    lane_head = lax.broadcasted_iota(jnp.int32, (1, gw), 1) // hd
    head_masks = [lane_head == j for j in range(gh)]
    bd_state = (lax.broadcasted_iota(jnp.int32, (gw, gw), 0) // hd) == (lax.broadcasted_iota(jnp.int32, (gw, gw), 1) // hd)
    bd_time = (lax.broadcasted_iota(jnp.int32, (gh * L, gh * L), 0) // L) == (lax.broadcasted_iota(jnp.int32, (gh * L, gh * L), 1) // L)
    t_row = lax.broadcasted_iota(jnp.int32, (L, gh * L), 0)
    t_col = lax.broadcasted_iota(jnp.int32, (L, gh * L), 1) % L
    strict4 = t_row > t_col
    incl4 = t_row >= t_col
    incl = lax.broadcasted_iota(jnp.int32, (L, L), 0) >= lax.broadcasted_iota(jnp.int32, (L, L), 1)

    def stack(x):
        return jnp.concatenate([jnp.where(m, x, jnp.zeros_like(x)) for m in head_masks], axis=0)

    def block_diag(n):
        tiled = jnp.concatenate([n] * gh, axis=0)
        return jnp.where(bd_time, tiled, jnp.zeros_like(tiled))

    @pl.when(ci == 0)
    def _():
        for bi in range(nb):
            for gi in range(n_groups):
                s_in = s0_ref[bi, gi * gh:(gi + 1) * gh].reshape(gw, hd)
                s_scr[bi, gi] = jnp.where(bd_state, jnp.concatenate([s_in] * gh, axis=1), 0.0)

    n_apply = max(1, int(math.log2(L)))
    seg = seg_ref[...]
    for bi in range(nb):
        lw = lw_ref[bi]
        cs = _dot(incl.astype(F32), lw, HIGHEST)
        cp = cs - lw
        cm = cs[L // 2 - 1:L // 2, :]
        c_last = cs[L - 1:L, :]
        r, k, v, kk, b = r_ref[bi], k_ref[bi], v_ref[bi], kk_ref[bi], b_ref[bi]
        e_dn = jnp.exp(cm - cs)
        rt = r * jnp.exp(cs - cm)
        kkt = kk * jnp.exp(cp - cm)
        bt = b * e_dn
        kt = k * e_dn
        kg = kk * jnp.exp(cp)
        rg = r * jnp.exp(cs)
        e_l = jnp.exp(c_last - cs)
        bh = b * e_l
        kh = k * e_l
        g_last = jnp.exp(c_last)
        rkk = r * k * rk_ref[...]
        for gi in range(n_groups):
            sl = slice(gi * gw, (gi + 1) * gw)
            vg = v[:, sl]
            v_st = stack(_pre(vg, md['AV']))
            a_all = _mm(jnp.concatenate([kkt[:, sl], rt[:, sl]], axis=0),
                        jnp.concatenate([stack(_pre(bt[:, sl], md['A'])), stack(_pre(kt[:, sl], md['A']))], axis=0),
                        NT, md['A'])
            p_ab = jnp.where(strict4, a_all[:L, :gh * L], 0.0)
            p_ak = jnp.where(strict4, a_all[:L, gh * L:], 0.0)
            p_rb = jnp.where(incl4, a_all[L:, :gh * L], 0.0)
            p_rk = jnp.where(incl4, a_all[L:, gh * L:], 0.0)
            x = jnp.concatenate([kg[:, sl], _mm(p_ak, v_st, NN, md['AV'])], axis=1)
            nm = -p_ab
            for it in range(n_apply):
                if it > 0:
                    nm = _mm(nm, block_diag(_pre(nm, md['SQ'])), NN, md['SQ'])
                xs = _pre(x, md['AP'])
                x = x + _mm(nm, jnp.concatenate([stack(xs[:, :gw]), stack(xs[:, gw:])], axis=1), NN, md['AP'])
            w_m = x[:, :gw]
            u0 = -x[:, gw:]
            rbw = _mm(p_rb, jnp.concatenate([stack(_pre(w_m, md['RB'])), stack(_pre(u0, md['RB']))], axis=1),
                      NN, md['RB'])
            r_hat = rg[:, sl] - rbw[:, :gw]
            y0 = rbw[:, gw:] + _mm(p_rk, v_st, NN, md['RK'])
            s_old = s_scr[bi, gi]
            ws = _mm(jnp.concatenate([w_m, r_hat], axis=0), s_old, NT, md['WS'])
            u = u0 - ws[:L]
            y = ws[L:] + y0
            upd = _mm(jnp.concatenate([u, vg], axis=0), jnp.concatenate([bh[:, sl], kh[:, sl]], axis=0), TN, md['UP'])
            s_scr[bi, gi] = s_old * g_last[:, sl] + jnp.where(bd_state, upd, 0.0)

            inv = 1.0 / hd
            mean = _dot_split(y, seg) * inv
            yc = y - mean
            var = _dot_split(yc * yc, seg) * inv
            yn = yc * lax.rsqrt(var + GN_EPS) * lnw_ref[:, sl] + lnb_ref[:, sl]
            bonus = _dot_split(rkk[:, sl], seg) * vg
            y_ref[bi, :, sl] = (yn + bonus) * g_ref[bi, :, sl]

    @pl.when(ci == pl.num_programs(1) - 1)
    def _():
        for bi in range(nb):
            for gi in range(n_groups):
                bd = s_scr[bi, gi]
                folded = bd[:, 0:hd]
                for j in range(1, gh):
                    folded = folded + bd[:, j * hd:(j + 1) * hd]
                sout_ref[bi, gi * gh:(gi + 1) * gh] = folded.reshape(gh, hd, hd)


def _rwkv_call(vecs, s0, p, n_seq, seq_len, chunk, nb):
    c = p['c_rwkv']
    n_heads = c // HEAD_DIM
    n_chunks = seq_len // chunk
    vecs = [z.reshape(n_seq, seq_len, c) for z in vecs]
    tok = pl.BlockSpec((nb, chunk, c), lambda bi, ci: (bi, ci, 0))
    st = pl.BlockSpec((nb, n_heads, HEAD_DIM, HEAD_DIM), lambda bi, ci: (bi, 0, 0, 0))
    rowvec = pl.BlockSpec((1, c), lambda bi, ci: (0, 0))
    seg = p['seg'][:GROUP_LANES, :GROUP_LANES]
    y, s_out = pl.pallas_call(
        _rwkv_kernel,
        grid=(n_seq // nb, n_chunks),
        in_specs=[tok] * 7 + [st, rowvec, rowvec, rowvec, pl.BlockSpec(seg.shape, lambda bi, ci: (0, 0))],
        out_specs=[tok, st],
        out_shape=[jax.ShapeDtypeStruct((n_seq, seq_len, c), F32),
                   jax.ShapeDtypeStruct((n_seq, n_heads, HEAD_DIM, HEAD_DIM), F32)],
        scratch_shapes=[pltpu.VMEM((nb, c // GROUP_LANES, GROUP_LANES, GROUP_LANES), F32)],
        compiler_params=pltpu.CompilerParams(dimension_semantics=("arbitrary", "arbitrary"),
                                             vmem_limit_bytes=V7X_VMEM_LIMIT),
        name="rwkv",
    )(*vecs, s0, p['r_k'], p['ln_w'], p['ln_b'], seg)
    return y.reshape(n_seq * seq_len, c), s_out


def _attn_prompt_kernel(q_ref, k_ref, v_ref, o_ref, m_scr, l_scr, acc_scr):
    s_len = q_ref.shape[0]
    band = ATT_BAND
    n_blk = s_len // band

    lane = lax.broadcasted_iota(jnp.int32, (1, LANES), 1)
    head0 = lane < HEAD_DIM
    qi = lax.broadcasted_iota(jnp.int32, (band, 2 * band), 0)
    kj = lax.broadcasted_iota(jnp.int32, (band, 2 * band), 1)
    in_band = (kj >= qi) & (kj <= qi + band)
    scale = HEAD_DIM ** -0.5
    ones = jnp.ones((2 * band, LANES), BF16)

    for ci, (window, dil) in enumerate(DILATED_CONFIGS):
        assert window // dil == band
        per_res = n_blk // dil

        def body(i, carry, ci=ci, dil=dil, per_res=per_res):
            res = i // per_res
            blk = i % per_res
            start = res + blk * (band * dil)
            prev = jnp.maximum(start - band * dil, 0)
            rows = pl.ds(start, band, stride=dil) if dil > 1 else pl.ds(start, band)
            prows = pl.ds(prev, band, stride=dil) if dil > 1 else pl.ds(prev, band)
            q = q_ref[rows, :] * scale
            kc = jnp.concatenate([k_ref[prows, :], k_ref[rows, :]], axis=0).astype(BF16)
            vc = jnp.concatenate([v_ref[prows, :], v_ref[rows, :]], axis=0).astype(BF16)
            mask = in_band & (kj >= jnp.where(blk > 0, 0, band))
            q0 = jnp.where(head0, q, 0.0).astype(BF16)
            q1 = jnp.where(head0, 0.0, q).astype(BF16)
            s0 = jnp.where(mask, lax.dot_general(q0, kc, NT, preferred_element_type=F32), NEG_INF)
            s1 = jnp.where(mask, lax.dot_general(q1, kc, NT, preferred_element_type=F32), NEG_INF)
            m0 = jnp.max(s0, axis=-1, keepdims=True)
            m1 = jnp.max(s1, axis=-1, keepdims=True)
            p0 = jnp.exp(s0 - m0).astype(BF16)
            p1 = jnp.exp(s1 - m1).astype(BF16)
            pv0 = _dot(p0, jnp.where(head0, vc, ones))
            pv1 = _dot(p1, jnp.where(head0, ones, vc))
            m_scr[ci, rows, :] = jnp.where(head0, m0, m1)
            acc_scr[ci, rows, :] = jnp.where(head0, pv0, pv1)
            l_scr[ci, rows, :] = jnp.where(head0, pv1, pv0)
            return carry

        lax.fori_loop(0, n_blk, body, 0, unroll=ATT_UNROLL)

    rows_per = 128

    def merge(i, carry):
        rows = pl.ds(pl.multiple_of(i * rows_per, rows_per), rows_per)
        ms = [m_scr[ci, rows, :] for ci in range(len(DILATED_CONFIGS))]
        m_all = functools.reduce(jnp.maximum, ms)
        num = jnp.zeros((rows_per, LANES), F32)
        den = jnp.zeros((rows_per, LANES), F32)
        for ci, m_c in enumerate(ms):
            w_c = jnp.exp(m_c - m_all)
            num = num + w_c * acc_scr[ci, rows, :]
            den = den + pltpu.roll(w_c, HEAD_DIM, axis=1) * l_scr[ci, rows, :]
        o_ref[rows, :] = num / pltpu.roll(den, HEAD_DIM, axis=1)
        return carry

    lax.fori_loop(0, s_len // rows_per, merge, 0)


def _attn_prompt_call(q, k, v, n_seq, seq_len):
    c = q.shape[1]
    n_pairs = c // LANES
    blk = pl.BlockSpec((seq_len, LANES), lambda bi, hi: (bi, hi))
    return pl.pallas_call(
        _attn_prompt_kernel,
        grid=(n_seq, n_pairs),
        in_specs=[blk, blk, blk],
        out_specs=blk,
        out_shape=jax.ShapeDtypeStruct((n_seq * seq_len, c), F32),
        scratch_shapes=[pltpu.VMEM((len(DILATED_CONFIGS), seq_len, LANES), F32)] * 3,
        compiler_params=pltpu.CompilerParams(dimension_semantics=("arbitrary", "arbitrary"),
                                             vmem_limit_bytes=V7X_VMEM_LIMIT),
        name="attn_prompt",
    )(q, k, v)


def _attn_sample_kernel(q_ref, kn_ref, vn_ref, kc_ref, vc_ref, o_ref, *, n_new, n_heads):
    n_buf = kc_ref.shape[1]
    t_pad = kn_ref.shape[1]
    c = q_ref.shape[2]
    q = q_ref[0] * (HEAD_DIM ** -0.5)
    lane_head = lax.broadcasted_iota(jnp.int32, (1, c), 1) // HEAD_DIM
    qs = jnp.concatenate([jnp.where(lane_head == h, q, 0.0) for h in range(n_heads)], axis=0).astype(BF16)
    n_rows = n_heads * t_pad
    t_idx = lax.broadcasted_iota(jnp.int32, (n_rows, 1), 0) % t_pad

    def multiplicity(dist):
        mult = jnp.zeros(dist.shape, F32)
        for window, dil in DILATED_CONFIGS:
            hit = (dist >= 0) & (dist <= window) & (dist % dil == 0)
            mult = mult + jnp.where(hit, 1.0, 0.0)
        return mult

    jc = lax.broadcasted_iota(jnp.int32, (1, n_buf), 1)
    mult_c = multiplicity(n_buf + t_idx - jc)
    jn = lax.broadcasted_iota(jnp.int32, (1, t_pad), 1)
    mult_n = jnp.where(jn < n_new, multiplicity(t_idx - jn), 0.0)

    sc = lax.dot_general(qs, kc_ref[0].astype(BF16), NT, preferred_element_type=F32)
    sn = lax.dot_general(qs, kn_ref[0].astype(BF16), NT, preferred_element_type=F32)
    sc = jnp.where(mult_c > 0.0, sc, NEG_INF)
    sn = jnp.where(mult_n > 0.0, sn, NEG_INF)
    m = jnp.maximum(jnp.max(sc, axis=-1, keepdims=True), jnp.max(sn, axis=-1, keepdims=True))
    pc = mult_c * jnp.exp(sc - m)
    pn = mult_n * jnp.exp(sn - m)
    l = jnp.sum(pc, axis=-1, keepdims=True) + jnp.sum(pn, axis=-1, keepdims=True)
    o = (_dot(pc.astype(BF16), vc_ref[0].astype(BF16)) + _dot(pn.astype(BF16), vn_ref[0].astype(BF16))) / l
    out = jnp.zeros((t_pad, c), F32)
    for h in range(n_heads):
        out = out + jnp.where(lane_head == h, o[h * t_pad:(h + 1) * t_pad, :], 0.0)
    o_ref[0] = out


def _attn_sample_call(q, kn, vn, k_buf, v_buf, n_new):
    b, t_pad, c = q.shape
    n_buf = k_buf.shape[1]
    new = pl.BlockSpec((1, t_pad, c), lambda bi: (bi, 0, 0))
    buf = pl.BlockSpec((1, n_buf, c), lambda bi: (bi, 0, 0))
    return pl.pallas_call(
        functools.partial(_attn_sample_kernel, n_new=n_new, n_heads=c // HEAD_DIM),
        grid=(b,),
        in_specs=[new, new, new, buf, buf],
        out_specs=new,
        out_shape=jax.ShapeDtypeStruct((b, t_pad, c), F32),
        compiler_params=pltpu.CompilerParams(dimension_semantics=("arbitrary",), vmem_limit_bytes=V7X_VMEM_LIMIT),
        name="attn_sample",
    )(q, kn, vn, k_buf, v_buf)


def _post_kernel(x_ref, yr_ref, ya_ref, wo_ref, gffn_ref, rw_hi_ref, rw_lo_ref, rb_ref,
                 h_o, hn_o, lg_o):
    c = yr_ref.shape[1]
    h = (x_ref[...] + _dot(yr_ref[...].astype(BF16), wo_ref[:c, :]) + _dot(ya_ref[...].astype(BF16), wo_ref[c:, :]))
    ms = jnp.mean(h * h, axis=-1, keepdims=True)
    hn = (h * lax.rsqrt(ms + NORM_EPS)) * gffn_ref[...]
    h_o[...] = h
    hn_o[...] = hn
    hi = hn.astype(BF16)
    lo = (hn - hi.astype(F32)).astype(BF16)
    lg_o[...] = (_dot(hi, rw_hi_ref[...]) + _dot(hi, rw_lo_ref[...]) + _dot(lo, rw_hi_ref[...])) + rb_ref[...]


def _post_call(x2, yr, ya, p, tm):
    t, d = x2.shape
    c = yr.shape[1]
    full = lambda a: pl.BlockSpec(a.shape, lambda i: (0,) * a.ndim)
    tok = lambda w: pl.BlockSpec((tm, w), lambda i: (i, 0))
    weights = [p['wout'], p['gffn'], p['rw_hi'], p['rw_lo'], p['rb']]
    return pl.pallas_call(
        _post_kernel,
        grid=(t // tm,),
        in_specs=[tok(d), tok(c), tok(c)] + [full(w) for w in weights],
        out_specs=[tok(d), tok(d), tok(LANES)],
        out_shape=[jax.ShapeDtypeStruct((t, d), F32), jax.ShapeDtypeStruct((t, d), F32),
                   jax.ShapeDtypeStruct((t, LANES), F32)],
        compiler_params=pltpu.CompilerParams(dimension_semantics=("arbitrary",), vmem_limit_bytes=V7X_VMEM_LIMIT),
        name="post",
    )(x2, yr, ya, *weights)


def _expert_kernel(be_ref, nb_ref, xs_ref, gate_ref, w1_ref, w3_ref, w2_ref, y_ref):
    i = pl.program_id(0)

    @pl.when(i < nb_ref[0])
    def _():
        xs = xs_ref[...].astype(BF16)
        h1 = _dot(xs, w1_ref[...])
        h3 = _dot(xs, w3_ref[...])
        act = (h1 * _sigmoid(h1)) * h3
        y_ref[...] = _dot(act.astype(BF16), w2_ref[...]) * gate_ref[...]

    @pl.when(i >= nb_ref[0])
    def _():
        y_ref[...] = jnp.zeros_like(y_ref)


def _expert_call(blk_exp, n_used, xs, slot_gate, p, bm):
    n_slots, d = xs.shape
    de = p['w1'].shape[2]
    grid_spec = pltpu.PrefetchScalarGridSpec(
        num_scalar_prefetch=2,
        grid=(n_slots // bm,),
        in_specs=[pl.BlockSpec((bm, d), lambda i, be, nb: (i, 0)),
                  pl.BlockSpec((bm, 1), lambda i, be, nb: (i, 0)),
                  pl.BlockSpec((None, d, de), lambda i, be, nb: (be[i], 0, 0)),
                  pl.BlockSpec((None, d, de), lambda i, be, nb: (be[i], 0, 0)),
                  pl.BlockSpec((None, de, d), lambda i, be, nb: (be[i], 0, 0))],
        out_specs=pl.BlockSpec((bm, d), lambda i, be, nb: (i, 0)),
    )
    return pl.pallas_call(
        _expert_kernel,
        grid_spec=grid_spec,
        out_shape=jax.ShapeDtypeStruct((n_slots, d), F32),
        compiler_params=pltpu.CompilerParams(dimension_semantics=("arbitrary",), vmem_limit_bytes=V7X_VMEM_LIMIT),
        name="experts",
    )(blk_exp, n_used, xs, slot_gate, p['w1'], p['w3'], p['w2'])


def _final_kernel(h_ref, y1_ref, y2_ref, gfin_ref, o_ref):
    h = h_ref[...] + (y1_ref[...] + y2_ref[...])
    ms = jnp.mean(h * h, axis=-1, keepdims=True)
    o_ref[...] = (h * lax.rsqrt(ms + NORM_EPS)) * gfin_ref[...]


def _final_call(h, y1, y2, gfin, tm):
    t, d = h.shape
    tok = pl.BlockSpec((tm, d), lambda i: (i, 0))
    return pl.pallas_call(
        _final_kernel,
        grid=(t // tm,),
        in_specs=[tok, tok, tok, pl.BlockSpec((1, d), lambda i: (0, 0))],
        out_specs=tok,
        out_shape=jax.ShapeDtypeStruct((t, d), F32),
        compiler_params=pltpu.CompilerParams(dimension_semantics=("arbitrary",), vmem_limit_bytes=V7X_VMEM_LIMIT),
        name="final",
    )(h, y1, y2, gfin)


def _route(logits, bm):
    n = logits.shape[0]
    p_group = jax.nn.softmax(logits[:, :N_GROUPS], axis=-1)
    g_w, g_idx = lax.top_k(p_group, 1)
    logit_e = logits[:, N_GROUPS:N_GROUPS + N_EXPERTS].reshape(n, N_GROUPS, EXPERTS_PER_GROUP)
    logit_e = jnp.take_along_axis(logit_e, jnp.broadcast_to(g_idx[:, :, None], (n, 1, EXPERTS_PER_GROUP)), axis=1)[:, 0]
    e_w, e_idx = lax.top_k(jax.nn.softmax(logit_e, axis=-1), 2)
    gates = (g_w * e_w / jnp.sum(e_w, axis=-1, keepdims=True)).reshape(-1)
    eid = (g_idx * EXPERTS_PER_GROUP + e_idx).reshape(-1).astype(jnp.int32)
    m = eid.shape[0]
    tok = jnp.repeat(jnp.arange(n, dtype=jnp.int32), 2)
    order = jnp.argsort(eid)
    e_sorted = eid[order]
    counts = jnp.bincount(eid, length=N_EXPERTS).astype(jnp.int32)
    starts = jnp.cumsum(counts) - counts
    padded = (counts + bm - 1) // bm * bm
    p_ends = jnp.cumsum(padded)
    p_starts = p_ends - padded
    dest_sorted = p_starts[e_sorted] + jnp.arange(m, dtype=jnp.int32) - starts[e_sorted]
    n_blocks = -(-m // bm) + N_EXPERTS
    n_slots = n_blocks * bm
    slot_tok = jnp.zeros((n_slots,), jnp.int32).at[dest_sorted].set(tok[order])
    slot_gate = jnp.zeros((n_slots,), F32).at[dest_sorted].set(gates[order])
    dest = jnp.zeros((m,), jnp.int32).at[order].set(dest_sorted).reshape(n, 2)
    blk_exp = jnp.minimum(jnp.searchsorted(p_ends, jnp.arange(n_blocks, dtype=jnp.int32) * bm, side='right'),
                          N_EXPERTS - 1).astype(jnp.int32)
    n_used = (p_ends[-1] // bm).astype(jnp.int32).reshape(1)
    return slot_tok, slot_gate.reshape(n_slots, 1), dest, blk_exp, n_used


def _moe_and_final(x2, yr, ya, p, tm):
    h, hn, logits = _post_call(x2, yr, ya, p, tm)
    slot_tok, slot_gate, dest, blk_exp, n_used = _route(logits, EXPERT_TILE)
    xs = jnp.take(hn, slot_tok, axis=0)
    yb = _expert_call(blk_exp, n_used, xs, slot_gate, p, EXPERT_TILE)
    y1 = jnp.take(yb, dest[:, 0], axis=0)
    y2 = jnp.take(yb, dest[:, 1], axis=0)
    return _final_call(h, y1, y2, p['gfin'], tm)


def _prep_params(layer, norm_mix_g, w_in, rwkv_mu_rkv, rwkv_mu_wag, rwkv_w0, rwkv_w1, rwkv_w2, rwkv_a0, rwkv_a1,
                 rwkv_a2, rwkv_g1, rwkv_g2, rwkv_k_k, rwkv_k_a, rwkv_r_k, rwkv_ln_w, rwkv_ln_b, w_out, norm_ffn_g,
                 router_group_w, router_group_b, router_expert_w, router_expert_b, expert_w1, expert_w3, expert_w2,
                 norm_final_g):
    d = w_in.shape[1]
    c = rwkv_w0.shape[1]
    row = lambda a: a.reshape(1, -1).astype(F32)
    lowrank = jnp.concatenate([rwkv_w1[layer], rwkv_a1[layer], rwkv_g1[layer]], axis=1)
    mx = rwkv_mu_wag[layer]
    r_w = rwkv_w1.shape[2]
    r_a = rwkv_a1.shape[2]
    r_g = rwkv_g1.shape[2]
    assert r_w + r_a == LANES and r_g == LANES
    mx_cols = jnp.concatenate([jnp.broadcast_to(mx[0][:, None], (d, r_w)), jnp.broadcast_to(mx[1][:, None], (d, r_a)),
                               jnp.broadcast_to(mx[2][:, None], (d, r_g))], axis=1)
    w2a2 = jnp.zeros((LANES, 2 * c), F32)
    w2a2 = w2a2.at[:r_w, :c].set(rwkv_w2[layer]).at[r_w:, c:].set(rwkv_a2[layer])
    head = jnp.arange(c) // HEAD_DIM
    rw = jnp.zeros((d, LANES), F32)
    rw = rw.at[:, :N_GROUPS].set(router_group_w[layer]).at[:, N_GROUPS:N_GROUPS + N_EXPERTS].set(router_expert_w[layer])
    rw_hi = rw.astype(BF16)
    rb = jnp.zeros((1, LANES), F32)
    rb = rb.at[0, :N_GROUPS].set(router_group_b[layer]).at[0, N_GROUPS:N_GROUPS + N_EXPERTS].set(router_expert_b[layer])
    return {
        'c_rwkv': c,
        'gmix': row(norm_mix_g[layer]),
        'win': jnp.concatenate([w_in[layer], lowrank], axis=1).astype(BF16),
        'wdx': (mx_cols * lowrank).astype(BF16),
        'mu_rkv': rwkv_mu_rkv[layer],
        'w0a0': jnp.concatenate([row(rwkv_w0[layer]), row(rwkv_a0[layer])], axis=1),
        'w2a2': w2a2.astype(BF16),
        'g2': rwkv_g2[layer].astype(BF16),
        'k_k': row(rwkv_k_k[layer]),
        'k_a': row(rwkv_k_a[layer]),
        'seg': (head[:, None] == head[None, :]).astype(BF16),
        'r_k': row(rwkv_r_k[layer]),
        'ln_w': row(rwkv_ln_w[layer]),
        'ln_b': row(rwkv_ln_b[layer]),
        'wout': w_out[layer].astype(BF16),
        'gffn': row(norm_ffn_g[layer]),
        'rw_hi': rw_hi,
        'rw_lo': (rw - rw_hi.astype(F32)).astype(BF16),
        'rb': rb,
        'w1': expert_w1[layer].astype(BF16),
        'w3': expert_w3[layer].astype(BF16),
        'w2': expert_w2[layer].astype(BF16),
        'gfin': row(norm_final_g),
    }


def _prompt_group(x, p):
    b, s, d = x.shape
    c = p['c_rwkv']
    x2 = x.reshape(b * s, d)
    tm = PROJ_TILE
    flag = jnp.zeros((b * s, 1), F32)
    outs = _proj_call(x2, flag, p, s // tm, tm)
    r, lw, k, v, kk, bb, g, qa, ka, va, xl = outs
    s0 = jnp.zeros((b, c // HEAD_DIM, HEAD_DIM, HEAD_DIM), F32)
    yr, s_new = _rwkv_call((r, lw, k, v, kk, bb, g), s0, p, b, s, RWKV_CHUNK, 2)
    ya = _attn_prompt_call(qa, ka, va, b, s)
    y = _moe_and_final(x2, yr, ya, p, tm)
    shift = xl.reshape(b, s // tm, 8, d)[:, -1, 7, :]
    keep = min(max(w for w, _ in DILATED_CONFIGS), s)
    k_keep = ka.reshape(b, s, c // HEAD_DIM, HEAD_DIM)[:, s - keep:]
    v_keep = va.reshape(b, s, c // HEAD_DIM, HEAD_DIM)[:, s - keep:]
    return y.reshape(b, s, d), s_new, shift, k_keep, v_keep


def _sample_group(x, shift0, s0, k_buf, v_buf, p):
    b, t, d = x.shape
    c = p['c_rwkv']
    n_heads = c // HEAD_DIM
    t_pad = 8
    xc = jnp.concatenate([shift0[:, None, :], x, jnp.zeros((b, t_pad - 1 - t, d), x.dtype)], axis=1)
    flag = jnp.zeros((b, t_pad, 1), F32).at[:, 0].set(1.0)
    outs = _proj_call(xc.reshape(b * t_pad, d), flag.reshape(b * t_pad, 1), p, 1, b * t_pad, xl_rows=b * t_pad)
    xl = outs[10]
    live = (jnp.arange(t_pad) < t)[None, :, None]
    shifted = [jnp.where(live, jnp.roll(o.reshape(b, t_pad, c), -1, axis=1), 0.0) for o in outs[:10]]
    r, lw, k, v, kk, bb, g, qa, ka, va = shifted
    flat = lambda z: z.reshape(b * t_pad, c)
    yr, s_new = _rwkv_call(tuple(flat(z) for z in (r, lw, k, v, kk, bb, g)), s0, p, b, t_pad, t_pad, 4)
    n_buf = k_buf.shape[1]
    ya = _attn_sample_call(qa, ka, va, k_buf.reshape(b, n_buf, c), v_buf.reshape(b, n_buf, c), t)
    x_pad = jnp.concatenate([x, jnp.zeros((b, t_pad - t, d), x.dtype)], axis=1).reshape(b * t_pad, d)
    y = _moe_and_final(x_pad, yr, flat(ya), p, b * t_pad // 2)
    y = y.reshape(b, t_pad, d)[:, :t]
    shift = xl.reshape(b, t_pad, d)[:, t]
    return (y, s_new, shift, ka[:, :t].reshape(b, t, n_heads, HEAD_DIM), va[:, :t].reshape(b, t, n_heads, HEAD_DIM))


def kernel(x_prompt, x_sample, state_rwkv, state_shift, cache_att_k, cache_att_v, norm_mix_g, w_in, rwkv_mu_rkv, rwkv_mu_wag, rwkv_w0, rwkv_w1, rwkv_w2, rwkv_a0, rwkv_a1, rwkv_a2, rwkv_g1, rwkv_g2, rwkv_k_k, rwkv_k_a, rwkv_r_k, rwkv_ln_w, rwkv_ln_b, w_out, norm_ffn_g, router_group_w, router_group_b, router_expert_w, router_expert_b, expert_w1, expert_w3, expert_w2, norm_final_g):
    assert w_in.shape[0] == 1, "single-layer trunk"
    p = _prep_params(0, norm_mix_g, w_in, rwkv_mu_rkv, rwkv_mu_wag, rwkv_w0, rwkv_w1, rwkv_w2, rwkv_a0, rwkv_a1,
                     rwkv_a2, rwkv_g1, rwkv_g2, rwkv_k_k, rwkv_k_a, rwkv_r_k, rwkv_ln_w, rwkv_ln_b, w_out,
                     norm_ffn_g, router_group_w, router_group_b, router_expert_w, router_expert_b, expert_w1,
                     expert_w3, expert_w2, norm_final_g)
    y_p, rw_p, sh_p, kc_p, vc_p = _prompt_group(x_prompt, p)
    y_s, rw_s, sh_s, kc_s, vc_s = _sample_group(x_sample, state_shift[0], state_rwkv[0], cache_att_k[0],
                                                cache_att_v[0], p)
    return (y_p, y_s, rw_p[None], sh_p[None], kc_p[None], vc_p[None], rw_s[None], sh_s[None], kc_s[None], vc_s[None])
```

```python
import functools
import math

import jax
import jax.numpy as jnp
from jax import lax
from jax.experimental import pallas as pl
from jax.experimental.pallas import tpu as pltpu

F32 = jnp.float32
BF16 = jnp.bfloat16

HEAD_DIM = 64
GN_EPS = 64e-5
NORM_EPS = 1e-6
DILATED_CONFIGS = ((128, 1), (512, 4), (2048, 16))
N_GROUPS = 4
EXPERTS_PER_GROUP = 8
N_EXPERTS = N_GROUPS * EXPERTS_PER_GROUP
NEG_INF = -1e30

V7X_VMEM_LIMIT = 56 * 1024 * 1024
LANES = 128

PROJ_TILE = 256
RWKV_CHUNK = 64
ATT_BAND = 128
EXPERT_TILE = 256
ATT_UNROLL = 4

HIGHEST = lax.Precision.HIGHEST
NN = (((1,), (0,)), ((), ()))
NT = (((1,), (1,)), ((), ()))
TN = (((0,), (0,)), ((), ()))


def _dot(a, b, precision=None):
    return jnp.dot(a, b, preferred_element_type=F32, precision=precision)


def _dot_split(a, b_bf16):
    hi = a.astype(BF16)
    lo = (a - hi.astype(F32)).astype(BF16)
    return _dot(hi, b_bf16) + _dot(lo, b_bf16)


def _mm(a, b, dims, mode):
    if mode == 'f32':
        return lax.dot_general(a, b, dims, precision=HIGHEST, preferred_element_type=F32)
    a_hi = a.astype(BF16)
    b_hi = b.astype(BF16)
    out = lax.dot_general(a_hi, b_hi, dims, preferred_element_type=F32)
    if mode == 'x3':
        a_lo = (a - a_hi.astype(F32)).astype(BF16)
        b_lo = (b - b_hi.astype(F32)).astype(BF16)
        out = out + lax.dot_general(a_hi, b_lo, dims, preferred_element_type=F32)
        out = out + lax.dot_general(a_lo, b_hi, dims, preferred_element_type=F32)
    return out


RWKV_MODES = dict(A='bf16', AV='bf16', SQ='bf16', AP='bf16', RB='bf16', RK='bf16', WS='bf16', UP='bf16')


def _pre(x, mode):
    return x.astype(BF16) if mode == 'bf16' else x


def _sigmoid(z):
    return 1.0 / (1.0 + jnp.exp(-z))


def _proj_kernel(x_ref, flag_ref, gmix_ref, win_ref, wdx_ref, mu_ref, w0a0_ref, w2a2_ref, g2_ref,
                 kk_ref, ka_ref, seg_ref,
                 r_o, lw_o, k_o, v_o, kkn_o, b_o, g_o, qa_o, kat_o, vat_o, xl_o,
                 xn_carry, pj_carry, *, tiles_per_seq, c_rwkv):
    i = pl.program_id(0)

    @pl.when(i % tiles_per_seq == 0)
    def _():
        xn_carry[...] = jnp.zeros_like(xn_carry)
        pj_carry[...] = jnp.zeros_like(pj_carry)

    c = c_rwkv
    x = x_ref[...]
    tm = x.shape[0]
    ms = jnp.mean(x * x, axis=-1, keepdims=True)
    xn = (x * lax.rsqrt(ms + NORM_EPS)) * gmix_ref[...]
    xn = jnp.where(flag_ref[...] > 0.0, x, xn)
    row = lax.broadcasted_iota(jnp.int32, (tm, 1), 0)
    xn_prev = jnp.where(row == 0, xn_carry[7:8, :], pltpu.roll(xn, 1, axis=0))
    dx = xn_prev - xn

    proj = _dot(xn.astype(BF16), win_ref[...])
    cur = proj[:, :3 * c]
    prev = jnp.where(row == 0, pj_carry[7:8, :], pltpu.roll(cur, 1, axis=0))
    xn_carry[...] = xn[tm - 8:, :]
    pj_carry[...] = cur[tm - 8:, :]
    xl_rows = xl_o.shape[1]
    xl_o[0] = xn[tm - xl_rows:, :]

    mu = mu_ref[...]
    r = cur[:, :c] + mu[0:1] * (prev[:, :c] - cur[:, :c])
    k = cur[:, c:2 * c] + mu[1:2] * (prev[:, c:2 * c] - cur[:, c:2 * c])
    v = cur[:, 2 * c:3 * c] + mu[2:3] * (prev[:, 2 * c:3 * c] - cur[:, 2 * c:3 * c])

    lr = proj[:, 6 * c:] + _dot(dx.astype(BF16), wdx_ref[...])
    lane = lax.broadcasted_iota(jnp.int32, (1, LANES), 1)
    wa_in = jnp.where(lane < 64, jnp.tanh(lr[:, :LANES]), lr[:, :LANES])
    wa = _dot(wa_in.astype(BF16), w2a2_ref[...]) + w0a0_ref[...]
    z = -wa[:, :c]
    softplus = jnp.maximum(z, 0.0) + jnp.log1p(jnp.exp(-jnp.abs(z)))
    lw = -jnp.exp(-softplus - 0.5)
    a = _sigmoid(wa[:, c:])
    g = _dot(_sigmoid(lr[:, LANES:]).astype(BF16), g2_ref[...])

    kk = k * kk_ref[...]
    ss = _dot_split(kk * kk, seg_ref[...])
    kk = kk * lax.rsqrt(jnp.maximum(ss, 1e-24))

    r_o[...] = r
    lw_o[...] = lw
    k_o[...] = k * (1.0 + (a - 1.0) * ka_ref[...])
    v_o[...] = v
    kkn_o[...] = kk
    b_o[...] = kk * a
    g_o[...] = g
    qa_o[...] = proj[:, 3 * c:4 * c]
    kat_o[...] = proj[:, 4 * c:5 * c]
    vat_o[...] = proj[:, 5 * c:6 * c]


def _proj_call(x2, flag, p, tiles_per_seq, tm, xl_rows=8):
    t, d = x2.shape
    c = p['c_rwkv']
    n_tiles = t // tm
    full = lambda a: pl.BlockSpec(a.shape, lambda i: (0,) * a.ndim)
    tok = lambda w: pl.BlockSpec((tm, w), lambda i: (i, 0))
    weights = [p['gmix'], p['win'], p['wdx'], p['mu_rkv'], p['w0a0'], p['w2a2'], p['g2'], p['k_k'], p['k_a'], p['seg']]
    outs = pl.pallas_call(
        functools.partial(_proj_kernel, tiles_per_seq=tiles_per_seq, c_rwkv=c),
        grid=(n_tiles,),
        in_specs=[tok(d), tok(1)] + [full(w) for w in weights],
        out_specs=[tok(c)] * 10 + [pl.BlockSpec((1, xl_rows, d), lambda i: (i, 0, 0))],
        out_shape=[jax.ShapeDtypeStruct((t, c), F32)] * 10 + [jax.ShapeDtypeStruct((n_tiles, xl_rows, d), F32)],
        scratch_shapes=[pltpu.VMEM((8, d), F32), pltpu.VMEM((8, 3 * c), F32)],
        compiler_params=pltpu.CompilerParams(dimension_semantics=("arbitrary",), vmem_limit_bytes=V7X_VMEM_LIMIT),
        name="proj",
    )(x2, flag, *weights)
    return outs


GROUP_LANES = 256
GROUP_HEADS = GROUP_LANES // HEAD_DIM


def _rwkv_kernel(r_ref, lw_ref, k_ref, v_ref, kk_ref, b_ref, g_ref, s0_ref, rk_ref, lnw_ref, lnb_ref, seg_ref,
                 y_ref, sout_ref, s_scr):
    ci = pl.program_id(1)
    nb, L, c = r_ref.shape
    gw, gh, hd = GROUP_LANES, GROUP_HEADS, HEAD_DIM
    n_groups = c // gw
    md = RWKV_MODES

    lane_head = lax.broadcasted_iota(jnp.int32, (1, gw), 1) // hd
    head_masks = [lane_head == j for j in range(gh)]
    bd_state = (lax.broadcasted_iota(jnp.int32, (gw, gw), 0) // hd) == (lax.broadcasted_iota(jnp.int32, (gw, gw), 1) // hd)
    bd_time = (lax.broadcasted_iota(jnp.int32, (gh * L, gh * L), 0) // L) == (lax.broadcasted_iota(jnp.int32, (gh * L, gh * L), 1) // L)
    t_row = lax.broadcasted_iota(jnp.int32, (L, gh * L), 0)
    t_col = lax.broadcasted_iota(jnp.int32, (L, gh * L), 1) % L
    strict4 = t_row > t_col
    incl4 = t_row >= t_col
    incl = lax.broadcasted_iota(jnp.int32, (L, L), 0) >= lax.broadcasted_iota(jnp.int32, (L, L), 1)

    def stack(x):
        return jnp.concatenate([jnp.where(m, x, jnp.zeros_like(x)) for m in head_masks], axis=0)

    def block_diag(n):
        tiled = jnp.concatenate([n] * gh, axis=0)
        return jnp.where(bd_time, tiled, jnp.zeros_like(tiled))

    @pl.when(ci == 0)
    def _():
        for bi in range(nb):
            for gi in range(n_groups):
                s_in = s0_ref[bi, gi * gh:(gi + 1) * gh].reshape(gw, hd)
                s_scr[bi, gi] = jnp.where(bd_state, jnp.concatenate([s_in] * gh, axis=1), 0.0)

    n_apply = max(1, int(math.log2(L)))
    seg = seg_ref[...]
    pre = []
    for bi in range(nb):
        lw = lw_ref[bi]
        cs = _dot(incl.astype(F32), lw, HIGHEST)
        cp = cs - lw
        cm = cs[L // 2 - 1:L // 2, :]
        c_last = cs[L - 1:L, :]
        r, k, v, kk, b = r_ref[bi], k_ref[bi], v_ref[bi], kk_ref[bi], b_ref[bi]
        e_dn = jnp.exp(cm - cs)
        e_l = jnp.exp(c_last - cs)
        pre.append(dict(v=v, rt=r * jnp.exp(cs - cm), kkt=kk * jnp.exp(cp - cm), bt=b * e_dn, kt=k * e_dn,
                        kg=kk * jnp.exp(cp), rg=r * jnp.exp(cs), bh=b * e_l, kh=k * e_l, g_last=jnp.exp(c_last),
                        rkk=r * k * rk_ref[...]))

    chains = [(bi, gi) for bi in range(nb) for gi in range(n_groups)]
    col = lambda bi, gi, name: pre[bi][name][:, gi * gw:(gi + 1) * gw]
    each = lambda fn: [fn(i, bi, gi) for i, (bi, gi) in enumerate(chains)]

    vg = each(lambda i, bi, gi: col(bi, gi, 'v'))
    v_st = each(lambda i, bi, gi: stack(_pre(vg[i], md['AV'])))
    a_all = each(lambda i, bi, gi: _mm(
        jnp.concatenate([col(bi, gi, 'kkt'), col(bi, gi, 'rt')], axis=0),
        jnp.concatenate([stack(_pre(col(bi, gi, 'bt'), md['A'])), stack(_pre(col(bi, gi, 'kt'), md['A']))], axis=0),
        NT, md['A']))
    p_ak = each(lambda i, bi, gi: jnp.where(strict4, a_all[i][:L, gh * L:], 0.0))
    p_rb = each(lambda i, bi, gi: jnp.where(incl4, a_all[i][L:, :gh * L], 0.0))
    p_rk = each(lambda i, bi, gi: jnp.where(incl4, a_all[i][L:, gh * L:], 0.0))
    nm = each(lambda i, bi, gi: -jnp.where(strict4, a_all[i][:L, :gh * L], 0.0))
    x = each(lambda i, bi, gi: jnp.concatenate([col(bi, gi, 'kg'), _mm(p_ak[i], v_st[i], NN, md['AV'])], axis=1))
    for it in range(n_apply):
        if it > 0:
            nm = each(lambda i, bi, gi: _mm(nm[i], block_diag(_pre(nm[i], md['SQ'])), NN, md['SQ']))

        def apply(i, bi, gi):
            xs = _pre(x[i], md['AP'])
            return x[i] + _mm(nm[i], jnp.concatenate([stack(xs[:, :gw]), stack(xs[:, gw:])], axis=1), NN, md['AP'])

        x = each(apply)
    w_m = [xi[:, :gw] for xi in x]
    u0 = [-xi[:, gw:] for xi in x]
    rbw = each(lambda i, bi, gi: _mm(
        p_rb[i], jnp.concatenate([stack(_pre(w_m[i], md['RB'])), stack(_pre(u0[i], md['RB']))], axis=1),
        NN, md['RB']))
    rkv = each(lambda i, bi, gi: _mm(p_rk[i], v_st[i], NN, md['RK']))
    s_old = each(lambda i, bi, gi: s_scr[bi, gi])
    ws = each(lambda i, bi, gi: _mm(
        jnp.concatenate([w_m[i], col(bi, gi, 'rg') - rbw[i][:, :gw]], axis=0), s_old[i], NT, md['WS']))
    u = each(lambda i, bi, gi: u0[i] - ws[i][:L])
    y = each(lambda i, bi, gi: ws[i][L:] + rbw[i][:, gw:] + rkv[i])
    upd = each(lambda i, bi, gi: _mm(
        jnp.concatenate([u[i], vg[i]], axis=0),
        jnp.concatenate([col(bi, gi, 'bh'), col(bi, gi, 'kh')], axis=0), TN, md['UP']))
    for i, (bi, gi) in enumerate(chains):
        s_scr[bi, gi] = s_old[i] * col(bi, gi, 'g_last') + jnp.where(bd_state, upd[i], 0.0)

    inv = 1.0 / hd
    mean = each(lambda i, bi, gi: _dot_split(y[i], seg) * inv)
    bonus = each(lambda i, bi, gi: _dot_split(col(bi, gi, 'rkk'), seg) * vg[i])
    yc = each(lambda i, bi, gi: y[i] - mean[i])
    var = each(lambda i, bi, gi: _dot_split(yc[i] * yc[i], seg) * inv)
    for i, (bi, gi) in enumerate(chains):
        sl = slice(gi * gw, (gi + 1) * gw)
        yn = yc[i] * lax.rsqrt(var[i] + GN_EPS) * lnw_ref[:, sl] + lnb_ref[:, sl]
        y_ref[bi, :, sl] = (yn + bonus[i]) * g_ref[bi, :, sl]

    @pl.when(ci == pl.num_programs(1) - 1)
    def _():
        for bi in range(nb):
            for gi in range(n_groups):
                bd = s_scr[bi, gi]
                folded = bd[:, 0:hd]
                for j in range(1, gh):
                    folded = folded + bd[:, j * hd:(j + 1) * hd]
                sout_ref[bi, gi * gh:(gi + 1) * gh] = folded.reshape(gh, hd, hd)


def _rwkv_call(vecs, s0, p, n_seq, seq_len, chunk, nb):
    c = p['c_rwkv']
    n_heads = c // HEAD_DIM
    n_chunks = seq_len // chunk
    vecs = [z.reshape(n_seq, seq_len, c) for z in vecs]
    tok = pl.BlockSpec((nb, chunk, c), lambda bi, ci: (bi, ci, 0))
    st = pl.BlockSpec((nb, n_heads, HEAD_DIM, HEAD_DIM), lambda bi, ci: (bi, 0, 0, 0))
    rowvec = pl.BlockSpec((1, c), lambda bi, ci: (0, 0))
    seg = p['seg'][:GROUP_LANES, :GROUP_LANES]
    y, s_out = pl.pallas_call(
        _rwkv_kernel,
        grid=(n_seq // nb, n_chunks),
        in_specs=[tok] * 7 + [st, rowvec, rowvec, rowvec, pl.BlockSpec(seg.shape, lambda bi, ci: (0, 0))],
        out_specs=[tok, st],
        out_shape=[jax.ShapeDtypeStruct((n_seq, seq_len, c), F32),
                   jax.ShapeDtypeStruct((n_seq, n_heads, HEAD_DIM, HEAD_DIM), F32)],
        scratch_shapes=[pltpu.VMEM((nb, c // GROUP_LANES, GROUP_LANES, GROUP_LANES), F32)],
        compiler_params=pltpu.CompilerParams(dimension_semantics=("arbitrary", "arbitrary"),
                                             vmem_limit_bytes=V7X_VMEM_LIMIT),
        name="rwkv",
    )(*vecs, s0, p['r_k'], p['ln_w'], p['ln_b'], seg)
    return y.reshape(n_seq * seq_len, c), s_out


def _attn_prompt_kernel(q_ref, k_ref, v_ref, o_ref, m_scr, l_scr, acc_scr):
    s_len = q_ref.shape[0]
    band = ATT_BAND
    n_blk = s_len // band

    lane = lax.broadcasted_iota(jnp.int32, (1, LANES), 1)
    head0 = lane < HEAD_DIM
    qi = lax.broadcasted_iota(jnp.int32, (band, 2 * band), 0)
    kj = lax.broadcasted_iota(jnp.int32, (band, 2 * band), 1)
    in_band = (kj >= qi) & (kj <= qi + band)
    scale = HEAD_DIM ** -0.5
    ones = jnp.ones((2 * band, LANES), BF16)

    for ci, (window, dil) in enumerate(DILATED_CONFIGS):
        assert window // dil == band
        per_res = n_blk // dil

        def body(it, carry, ci=ci, dil=dil, per_res=per_res):
            blocks = []
            for j in range(ATT_UNROLL):
                i = it * ATT_UNROLL + j
                res = i // per_res
                blk = i % per_res
                start = res + blk * (band * dil)
                prev = jnp.maximum(start - band * dil, 0)
                rows = pl.ds(start, band, stride=dil) if dil > 1 else pl.ds(start, band)
                prows = pl.ds(prev, band, stride=dil) if dil > 1 else pl.ds(prev, band)
                blocks.append((blk, rows, prows))
            each = lambda fn: [fn(j, *blocks[j]) for j in range(ATT_UNROLL)]
            q = each(lambda j, blk, rows, prows: q_ref[rows, :] * scale)
            kc = each(lambda j, blk, rows, prows:
                      jnp.concatenate([k_ref[prows, :], k_ref[rows, :]], axis=0).astype(BF16))
            vc = each(lambda j, blk, rows, prows:
                      jnp.concatenate([v_ref[prows, :], v_ref[rows, :]], axis=0).astype(BF16))
            mask = each(lambda j, blk, rows, prows: in_band & (kj >= jnp.where(blk > 0, 0, band)))
            s0 = each(lambda j, blk, rows, prows: jnp.where(mask[j], lax.dot_general(
                jnp.where(head0, q[j], 0.0).astype(BF16), kc[j], NT, preferred_element_type=F32), NEG_INF))
            s1 = each(lambda j, blk, rows, prows: jnp.where(mask[j], lax.dot_general(
                jnp.where(head0, 0.0, q[j]).astype(BF16), kc[j], NT, preferred_element_type=F32), NEG_INF))
            m0 = [jnp.max(z, axis=-1, keepdims=True) for z in s0]
            m1 = [jnp.max(z, axis=-1, keepdims=True) for z in s1]
            p0 = [jnp.exp(z - m).astype(BF16) for z, m in zip(s0, m0)]
            p1 = [jnp.exp(z - m).astype(BF16) for z, m in zip(s1, m1)]
            pv0 = [_dot(p, jnp.where(head0, v, ones)) for p, v in zip(p0, vc)]
            pv1 = [_dot(p, jnp.where(head0, ones, v)) for p, v in zip(p1, vc)]
            for j, (blk, rows, prows) in enumerate(blocks):
                m_scr[ci, rows, :] = jnp.where(head0, m0[j], m1[j])
                acc_scr[ci, rows, :] = jnp.where(head0, pv0[j], pv1[j])
                l_scr[ci, rows, :] = pltpu.roll(jnp.where(head0, pv1[j], pv0[j]), HEAD_DIM, axis=1)
            return carry

        lax.fori_loop(0, n_blk // ATT_UNROLL, body, 0)

    rows_per = 256

    def merge(i, carry):
        rows = pl.ds(pl.multiple_of(i * rows_per, rows_per), rows_per)
        ms = [m_scr[ci, rows, :] for ci in range(len(DILATED_CONFIGS))]
        m_all = functools.reduce(jnp.maximum, ms)
        num = jnp.zeros((rows_per, LANES), F32)
        den = jnp.zeros((rows_per, LANES), F32)
        for ci, m_c in enumerate(ms):
            w_c = jnp.exp(m_c - m_all)
            num = num + w_c * acc_scr[ci, rows, :]
            den = den + w_c * l_scr[ci, rows, :]
        o_ref[rows, :] = num / den
        return carry

    lax.fori_loop(0, s_len // rows_per, merge, 0)


def _attn_prompt_call(q, k, v, n_seq, seq_len):
    c = q.shape[1]
    n_pairs = c // LANES
    blk = pl.BlockSpec((seq_len, LANES), lambda bi, hi: (bi, hi))
    return pl.pallas_call(
        _attn_prompt_kernel,
        grid=(n_seq, n_pairs),
        in_specs=[blk, blk, blk],
        out_specs=blk,
        out_shape=jax.ShapeDtypeStruct((n_seq * seq_len, c), F32),
        scratch_shapes=[pltpu.VMEM((len(DILATED_CONFIGS), seq_len, LANES), F32)] * 3,
        compiler_params=pltpu.CompilerParams(dimension_semantics=("arbitrary", "arbitrary"),
                                             vmem_limit_bytes=V7X_VMEM_LIMIT),
        name="attn_prompt",
    )(q, k, v)


def _attn_sample_kernel(q_ref, kn_ref, vn_ref, kc_ref, vc_ref, o_ref, *, n_new, n_heads):
    hd = HEAD_DIM
    n_buf = kc_ref.shape[1] // n_heads
    t_pad = kn_ref.shape[1]
    c = q_ref.shape[2]
    q = q_ref[0] * (hd ** -0.5)
    lane_head = lax.broadcasted_iota(jnp.int32, (1, c), 1) // hd
    qs = jnp.concatenate([jnp.where(lane_head == h, q, 0.0) for h in range(n_heads)], axis=0).astype(BF16)
    n_rows = n_heads * t_pad
    t_idx = lax.broadcasted_iota(jnp.int32, (n_rows, 1), 0) % t_pad

    def multiplicity(dist):
        mult = jnp.zeros(dist.shape, F32)
        for window, dil in DILATED_CONFIGS:
            hit = (dist >= 0) & (dist <= window) & (dist % dil == 0)
            mult = mult + jnp.where(hit, 1.0, 0.0)
        return mult

    jc = lax.broadcasted_iota(jnp.int32, (1, n_buf), 1)
    mult_c = multiplicity(n_buf + t_idx - jc)
    jn = lax.broadcasted_iota(jnp.int32, (1, t_pad), 1)
    mult_n = jnp.where(jn < n_new, multiplicity(t_idx - jn), 0.0)

    head_rows = lambda ref, h: ref[0, pl.ds(h, n_buf, stride=n_heads), :].astype(BF16)
    sc = jnp.concatenate(
        [lax.dot_general(qs[h * t_pad:(h + 1) * t_pad, h * hd:(h + 1) * hd], head_rows(kc_ref, h), NT,
                         preferred_element_type=F32) for h in range(n_heads)], axis=0)
    sn = lax.dot_general(qs, kn_ref[0].astype(BF16), NT, preferred_element_type=F32)
    sc = jnp.where(mult_c > 0.0, sc, NEG_INF)
    sn = jnp.where(mult_n > 0.0, sn, NEG_INF)
    m = jnp.maximum(jnp.max(sc, axis=-1, keepdims=True), jnp.max(sn, axis=-1, keepdims=True))
    pc = (mult_c * jnp.exp(sc - m)).astype(BF16)
    pn = mult_n * jnp.exp(sn - m)
    inv_l = 1.0 / (jnp.sum(pc.astype(F32), axis=-1, keepdims=True) + jnp.sum(pn, axis=-1, keepdims=True))
    o_new = _dot(pn.astype(BF16), vn_ref[0].astype(BF16)) * inv_l
    out = jnp.zeros((t_pad, c), F32)
    for h in range(n_heads):
        out = out + jnp.where(lane_head == h, o_new[h * t_pad:(h + 1) * t_pad, :], 0.0)
    o_buf = [_dot(pc[h * t_pad:(h + 1) * t_pad, :], head_rows(vc_ref, h)) * inv_l[h * t_pad:(h + 1) * t_pad, :]
             for h in range(n_heads)]
    o_ref[0] = out + jnp.concatenate(o_buf, axis=1)


def _attn_sample_call(q, kn, vn, k_buf, v_buf, n_new):
    b, t_pad, c = q.shape
    _, n_buf, n_heads, hd = k_buf.shape
    k_buf = k_buf.reshape(b, n_buf * n_heads, hd)
    v_buf = v_buf.reshape(b, n_buf * n_heads, hd)
    new = pl.BlockSpec((1, t_pad, c), lambda bi: (bi, 0, 0))
    buf = pl.BlockSpec((1, n_buf * n_heads, hd), lambda bi: (bi, 0, 0))
    return pl.pallas_call(
        functools.partial(_attn_sample_kernel, n_new=n_new, n_heads=n_heads),
        grid=(b,),
        in_specs=[new, new, new, buf, buf],
        out_specs=new,
        out_shape=jax.ShapeDtypeStruct((b, t_pad, c), F32),
        compiler_params=pltpu.CompilerParams(dimension_semantics=("arbitrary",), vmem_limit_bytes=V7X_VMEM_LIMIT),
        name="attn_sample",
    )(q, kn, vn, k_buf, v_buf)


def _route_rows(logits):
    lane = lax.broadcasted_iota(jnp.int32, logits.shape, 1)
    lane_f = lane.astype(F32)
    first = lambda hit: jnp.min(jnp.where(hit, lane_f, float(LANES)), axis=-1, keepdims=True)
    is_g = lane < N_GROUPS
    lg = jnp.where(is_g, logits, NEG_INF)
    g_max = jnp.max(lg, axis=-1, keepdims=True)
    g_idx = first(lg == g_max)
    g_w = 1.0 / jnp.sum(jnp.where(is_g, jnp.exp(lg - g_max), 0.0), axis=-1, keepdims=True)
    lo = N_GROUPS + EXPERTS_PER_GROUP * g_idx
    le = jnp.where((lane_f >= lo) & (lane_f < lo + EXPERTS_PER_GROUP), logits, NEG_INF)
    e1 = jnp.max(le, axis=-1, keepdims=True)
    i1 = first(le == e1)
    le2 = jnp.where(lane_f == i1, NEG_INF, le)
    e2 = jnp.max(le2, axis=-1, keepdims=True)
    i2 = first(le2 == e2)
    ex = jnp.exp(e2 - e1)
    gate1 = g_w / (1.0 + ex)
    gate2 = g_w * ex / (1.0 + ex)
    out = jnp.where(lane == 0, gate1, jnp.where(lane == 1, gate2, 0.0))
    out = jnp.where(lane == 2, i1 - N_GROUPS, jnp.where(lane == 3, i2 - N_GROUPS, out))
    return out


def _post_kernel(x_ref, yr_ref, ya_ref, wo_ref, gffn_ref, rw_hi_ref, rw_lo_ref, rb_ref,
                 h_o, hn_o, lg_o):
    c = yr_ref.shape[1]
    h = (x_ref[...] + _dot(yr_ref[...].astype(BF16), wo_ref[:c, :]) + _dot(ya_ref[...].astype(BF16), wo_ref[c:, :]))
    ms = jnp.mean(h * h, axis=-1, keepdims=True)
    hn = (h * lax.rsqrt(ms + NORM_EPS)) * gffn_ref[...]
    h_o[...] = h
    hn_o[...] = hn
    hi = hn.astype(BF16)
    lo = (hn - hi.astype(F32)).astype(BF16)
    logits = (_dot(hi, rw_hi_ref[...]) + _dot(hi, rw_lo_ref[...]) + _dot(lo, rw_hi_ref[...])) + rb_ref[...]
    lg_o[...] = _route_rows(logits)


def _post_call(x2, yr, ya, p, tm):
    t, d = x2.shape
    c = yr.shape[1]
    full = lambda a: pl.BlockSpec(a.shape, lambda i: (0,) * a.ndim)
    tok = lambda w: pl.BlockSpec((tm, w), lambda i: (i, 0))
    weights = [p['wout'], p['gffn'], p['rw_hi'], p['rw_lo'], p['rb']]
    return pl.pallas_call(
        _post_kernel,
        grid=(t // tm,),
        in_specs=[tok(d), tok(c), tok(c)] + [full(w) for w in weights],
        out_specs=[tok(d), tok(d), tok(LANES)],
        out_shape=[jax.ShapeDtypeStruct((t, d), F32), jax.ShapeDtypeStruct((t, d), F32),
                   jax.ShapeDtypeStruct((t, LANES), F32)],
        compiler_params=pltpu.CompilerParams(dimension_semantics=("arbitrary",), vmem_limit_bytes=V7X_VMEM_LIMIT),
        name="post",
    )(x2, yr, ya, *weights)


def _expert_kernel(be_ref, nb_ref, xs_ref, gate_ref, w1_ref, w3_ref, w2_ref, y_ref):
    i = pl.program_id(0)

    @pl.when(i < nb_ref[0])
    def _():
        xs = xs_ref[...].astype(BF16)
        h1 = _dot(xs, w1_ref[...])
        h3 = _dot(xs, w3_ref[...])
        act = (h1 * _sigmoid(h1)) * h3
        y_ref[...] = _dot(act.astype(BF16), w2_ref[...]) * gate_ref[...]

    @pl.when(i >= nb_ref[0])
    def _():
        y_ref[...] = jnp.zeros_like(y_ref)


def _expert_call(blk_exp, n_used, xs, slot_gate, p, bm):
    n_slots, d = xs.shape
    de = p['w1'].shape[2]
    grid_spec = pltpu.PrefetchScalarGridSpec(
        num_scalar_prefetch=2,
        grid=(n_slots // bm,),
        in_specs=[pl.BlockSpec((bm, d), lambda i, be, nb: (i, 0)),
                  pl.BlockSpec((bm, 1), lambda i, be, nb: (i, 0)),
                  pl.BlockSpec((None, d, de), lambda i, be, nb: (be[i], 0, 0)),
                  pl.BlockSpec((None, d, de), lambda i, be, nb: (be[i], 0, 0)),
                  pl.BlockSpec((None, de, d), lambda i, be, nb: (be[i], 0, 0))],
        out_specs=pl.BlockSpec((bm, d), lambda i, be, nb: (i, 0)),
    )
    return pl.pallas_call(
        _expert_kernel,
        grid_spec=grid_spec,
        out_shape=jax.ShapeDtypeStruct((n_slots, d), F32),
        compiler_params=pltpu.CompilerParams(dimension_semantics=("arbitrary",), vmem_limit_bytes=V7X_VMEM_LIMIT),
        name="experts",
    )(blk_exp, n_used, xs, slot_gate, p['w1'], p['w3'], p['w2'])


def _final_kernel(h_ref, y1_ref, y2_ref, gfin_ref, o_ref):
    h = h_ref[...] + (y1_ref[...] + y2_ref[...])
    ms = jnp.mean(h * h, axis=-1, keepdims=True)
    o_ref[...] = (h * lax.rsqrt(ms + NORM_EPS)) * gfin_ref[...]


def _final_call(h, y1, y2, gfin, tm):
    t, d = h.shape
    tok = pl.BlockSpec((tm, d), lambda i: (i, 0))
    return pl.pallas_call(
        _final_kernel,
        grid=(t // tm,),
        in_specs=[tok, tok, tok, pl.BlockSpec((1, d), lambda i: (0, 0))],
        out_specs=tok,
        out_shape=jax.ShapeDtypeStruct((t, d), F32),
        compiler_params=pltpu.CompilerParams(dimension_semantics=("arbitrary",), vmem_limit_bytes=V7X_VMEM_LIMIT),
        name="final",
    )(h, y1, y2, gfin)


def _route(route, bm):
    n = route.shape[0]
    gates = route[:, 0:2].reshape(-1)
    eid = route[:, 2:4].astype(jnp.int32).reshape(-1)
    m = eid.shape[0]
    experts = jnp.arange(N_EXPERTS + 1, dtype=jnp.int32)
    e_sorted, order = lax.sort_key_val(eid, jnp.arange(m, dtype=jnp.int32))
    below = jnp.sum((eid[:, None] < experts[None, :]).astype(jnp.int32), axis=0)
    starts, counts = below[:-1], below[1:] - below[:-1]
    padded = (counts + bm - 1) // bm * bm
    p_ends = jnp.cumsum(padded)
    p_starts = p_ends - padded
    shift = p_starts - starts
    dest_sorted = jnp.arange(m, dtype=jnp.int32) + jnp.sum(
        jnp.where(e_sorted[:, None] == experts[None, :-1], shift[None, :], 0), axis=1)
    _, dest = lax.sort_key_val(order, dest_sorted)
    n_blocks = -(-m // bm) + N_EXPERTS
    blk_start = jnp.arange(n_blocks, dtype=jnp.int32) * bm
    blk_exp = jnp.minimum(jnp.sum((p_ends[None, :] <= blk_start[:, None]).astype(jnp.int32), axis=1), N_EXPERTS - 1)
    pick = lambda tbl: jnp.sum(jnp.where(blk_exp[:, None] == experts[None, :-1], tbl[None, :], 0), axis=1)
    pos = (blk_start - pick(p_starts))[:, None] + jnp.arange(bm, dtype=jnp.int32)[None, :]
    valid = pos < pick(counts)[:, None]
    src = jnp.where(valid, pick(starts)[:, None] + pos, 0).reshape(-1)
    src_assign = order.at[src].get(mode='promise_in_bounds')
    valid = valid.reshape(-1)
    slot_tok = jnp.where(valid, src_assign // 2, 0)
    slot_gate = jnp.where(valid, gates.at[src_assign].get(mode='promise_in_bounds'), 0.0)
    n_used = (p_ends[-1] // bm).astype(jnp.int32).reshape(1)
    return slot_tok, slot_gate.reshape(-1, 1), dest.reshape(n, 2), blk_exp.astype(jnp.int32), n_used


def _moe_and_final(x2, yr, ya, p, tm):
    h, hn, route = _post_call(x2, yr, ya, p, tm)
    slot_tok, slot_gate, dest, blk_exp, n_used = _route(route, EXPERT_TILE)
    xs = hn.at[slot_tok].get(mode='promise_in_bounds')
    yb = _expert_call(blk_exp, n_used, xs, slot_gate, p, EXPERT_TILE)
    y1 = yb.at[dest[:, 0]].get(mode='promise_in_bounds')
    y2 = yb.at[dest[:, 1]].get(mode='promise_in_bounds')
    return _final_call(h, y1, y2, p['gfin'], tm)


def _prep_params(layer, norm_mix_g, w_in, rwkv_mu_rkv, rwkv_mu_wag, rwkv_w0, rwkv_w1, rwkv_w2, rwkv_a0, rwkv_a1,
                 rwkv_a2, rwkv_g1, rwkv_g2, rwkv_k_k, rwkv_k_a, rwkv_r_k, rwkv_ln_w, rwkv_ln_b, w_out, norm_ffn_g,
                 router_group_w, router_group_b, router_expert_w, router_expert_b, expert_w1, expert_w3, expert_w2,
                 norm_final_g):
    d = w_in.shape[1]
    c = rwkv_w0.shape[1]
    row = lambda a: a.reshape(1, -1).astype(F32)
    lowrank = jnp.concatenate([rwkv_w1[layer], rwkv_a1[layer], rwkv_g1[layer]], axis=1)
    mx = rwkv_mu_wag[layer]
    r_w = rwkv_w1.shape[2]
    r_a = rwkv_a1.shape[2]
    r_g = rwkv_g1.shape[2]
    assert r_w + r_a == LANES and r_g == LANES
    mx_cols = jnp.concatenate([jnp.broadcast_to(mx[0][:, None], (d, r_w)), jnp.broadcast_to(mx[1][:, None], (d, r_a)),
                               jnp.broadcast_to(mx[2][:, None], (d, r_g))], axis=1)
    w2a2 = jnp.zeros((LANES, 2 * c), F32)
    w2a2 = w2a2.at[:r_w, :c].set(rwkv_w2[layer]).at[r_w:, c:].set(rwkv_a2[layer])
    head = jnp.arange(c) // HEAD_DIM
    rw = jnp.zeros((d, LANES), F32)
    rw = rw.at[:, :N_GROUPS].set(router_group_w[layer]).at[:, N_GROUPS:N_GROUPS + N_EXPERTS].set(router_expert_w[layer])
    rw_hi = rw.astype(BF16)
    rb = jnp.zeros((1, LANES), F32)
    rb = rb.at[0, :N_GROUPS].set(router_group_b[layer]).at[0, N_GROUPS:N_GROUPS + N_EXPERTS].set(router_expert_b[layer])
    return {
        'c_rwkv': c,
        'gmix': row(norm_mix_g[layer]),
        'win': jnp.concatenate([w_in[layer], lowrank], axis=1).astype(BF16),
        'wdx': (mx_cols * lowrank).astype(BF16),
        'mu_rkv': rwkv_mu_rkv[layer],
        'w0a0': jnp.concatenate([row(rwkv_w0[layer]), row(rwkv_a0[layer])], axis=1),
        'w2a2': w2a2.astype(BF16),
        'g2': rwkv_g2[layer].astype(BF16),
        'k_k': row(rwkv_k_k[layer]),
        'k_a': row(rwkv_k_a[layer]),
        'seg': (head[:, None] == head[None, :]).astype(BF16),
        'r_k': row(rwkv_r_k[layer]),
        'ln_w': row(rwkv_ln_w[layer]),
        'ln_b': row(rwkv_ln_b[layer]),
        'wout': w_out[layer].astype(BF16),
        'gffn': row(norm_ffn_g[layer]),
        'rw_hi': rw_hi,
        'rw_lo': (rw - rw_hi.astype(F32)).astype(BF16),
        'rb': rb,
        'w1': expert_w1[layer].astype(BF16),
        'w3': expert_w3[layer].astype(BF16),
        'w2': expert_w2[layer].astype(BF16),
        'gfin': row(norm_final_g),
    }


def _prompt_group(x, p):
    b, s, d = x.shape
    c = p['c_rwkv']
    x2 = x.reshape(b * s, d)
    tm = PROJ_TILE
    flag = jnp.zeros((b * s, 1), F32)
    outs = _proj_call(x2, flag, p, s // tm, tm)
    r, lw, k, v, kk, bb, g, qa, ka, va, xl = outs
    s0 = jnp.zeros((b, c // HEAD_DIM, HEAD_DIM, HEAD_DIM), F32)
    yr, s_new = _rwkv_call((r, lw, k, v, kk, bb, g), s0, p, b, s, RWKV_CHUNK, 4)
    ya = _attn_prompt_call(qa, ka, va, b, s)
    y = _moe_and_final(x2, yr, ya, p, tm)
    shift = xl.reshape(b, s // tm, 8, d)[:, -1, 7, :]
    keep = min(max(w for w, _ in DILATED_CONFIGS), s)
    k_keep = ka.reshape(b, s, c // HEAD_DIM, HEAD_DIM)[:, s - keep:]
    v_keep = va.reshape(b, s, c // HEAD_DIM, HEAD_DIM)[:, s - keep:]
    return y.reshape(b, s, d), s_new, shift, k_keep, v_keep


def _sample_group(x, shift0, s0, k_buf, v_buf, p):
    b, t, d = x.shape
    c = p['c_rwkv']
    n_heads = c // HEAD_DIM
    t_pad = 8
    xc = jnp.concatenate([shift0[:, None, :], x, jnp.zeros((b, t_pad - 1 - t, d), x.dtype)], axis=1)
    flag = jnp.zeros((b, t_pad, 1), F32).at[:, 0].set(1.0)
    outs = _proj_call(xc.reshape(b * t_pad, d), flag.reshape(b * t_pad, 1), p, 1, b * t_pad, xl_rows=b * t_pad)
    xl = outs[10]
    live = (jnp.arange(t_pad) < t)[None, :, None]
    shifted = [jnp.where(live, jnp.roll(o.reshape(b, t_pad, c), -1, axis=1), 0.0) for o in outs[:10]]
    r, lw, k, v, kk, bb, g, qa, ka, va = shifted
    flat = lambda z: z.reshape(b * t_pad, c)
    yr, s_new = _rwkv_call(tuple(flat(z) for z in (r, lw, k, v, kk, bb, g)), s0, p, b, t_pad, t_pad, 8)
    ya = _attn_sample_call(qa, ka, va, k_buf, v_buf, t)
    x_pad = jnp.concatenate([x, jnp.zeros((b, t_pad - t, d), x.dtype)], axis=1).reshape(b * t_pad, d)
    y = _moe_and_final(x_pad, yr, flat(ya), p, b * t_pad // 2)
    y = y.reshape(b, t_pad, d)[:, :t]
    shift = xl.reshape(b, t_pad, d)[:, t]
    return (y, s_new, shift, ka[:, :t].reshape(b, t, n_heads, HEAD_DIM), va[:, :t].reshape(b, t, n_heads, HEAD_DIM))


def kernel(x_prompt, x_sample, state_rwkv, state_shift, cache_att_k, cache_att_v, norm_mix_g, w_in, rwkv_mu_rkv, rwkv_mu_wag, rwkv_w0, rwkv_w1, rwkv_w2, rwkv_a0, rwkv_a1, rwkv_a2, rwkv_g1, rwkv_g2, rwkv_k_k, rwkv_k_a, rwkv_r_k, rwkv_ln_w, rwkv_ln_b, w_out, norm_ffn_g, router_group_w, router_group_b, router_expert_w, router_expert_b, expert_w1, expert_w3, expert_w2, norm_final_g):
    assert w_in.shape[0] == 1, "single-layer trunk"
    p = _prep_params(0, norm_mix_g, w_in, rwkv_mu_rkv, rwkv_mu_wag, rwkv_w0, rwkv_w1, rwkv_w2, rwkv_a0, rwkv_a1,
                     rwkv_a2, rwkv_g1, rwkv_g2, rwkv_k_k, rwkv_k_a, rwkv_r_k, rwkv_ln_w, rwkv_ln_b, w_out,
                     norm_ffn_g, router_group_w, router_group_b, router_expert_w, router_expert_b, expert_w1,
                     expert_w3, expert_w2, norm_final_g)
    y_p, rw_p, sh_p, kc_p, vc_p = _prompt_group(x_prompt, p)
    y_s, rw_s, sh_s, kc_s, vc_s = _sample_group(x_sample, state_shift[0], state_rwkv[0], cache_att_k[0],
                                                cache_att_v[0], p)
    return (y_p, y_s, rw_p[None], sh_p[None], kc_p[None], vc_p[None], rw_s[None], sh_s[None], kc_s[None], vc_s[None])
```

```python
import functools
import math

import jax
import jax.numpy as jnp
from jax import lax
from jax.experimental import pallas as pl
from jax.experimental.pallas import tpu as pltpu

F32 = jnp.float32
BF16 = jnp.bfloat16

HEAD_DIM = 64
GN_EPS = 64e-5
NORM_EPS = 1e-6
DILATED_CONFIGS = ((128, 1), (512, 4), (2048, 16))
N_GROUPS = 4
EXPERTS_PER_GROUP = 8
N_EXPERTS = N_GROUPS * EXPERTS_PER_GROUP
NEG_INF = -1e30

V7X_VMEM_LIMIT = 56 * 1024 * 1024
LANES = 128

PROJ_TILE = 512
RWKV_CHUNK = 64
ATT_BAND = 128
EXPERT_TILE = 512
ATT_UNROLL = 4

HIGHEST = lax.Precision.HIGHEST
NN = (((1,), (0,)), ((), ()))
NT = (((1,), (1,)), ((), ()))
TN = (((0,), (0,)), ((), ()))


def _dot(a, b, precision=None):
    return jnp.dot(a, b, preferred_element_type=F32, precision=precision)


def _dot_split(a, b_bf16):
    hi = a.astype(BF16)
    lo = (a - hi.astype(F32)).astype(BF16)
    return _dot(hi, b_bf16) + _dot(lo, b_bf16)


def _mm(a, b, dims, mode):
    if mode == 'f32':
        return lax.dot_general(a, b, dims, precision=HIGHEST, preferred_element_type=F32)
    a_hi = a.astype(BF16)
    b_hi = b.astype(BF16)
    out = lax.dot_general(a_hi, b_hi, dims, preferred_element_type=F32)
    if mode == 'x3':
        a_lo = (a - a_hi.astype(F32)).astype(BF16)
        b_lo = (b - b_hi.astype(F32)).astype(BF16)
        out = out + lax.dot_general(a_hi, b_lo, dims, preferred_element_type=F32)
        out = out + lax.dot_general(a_lo, b_hi, dims, preferred_element_type=F32)
    return out


RWKV_MODES = dict(A='bf16', AV='bf16', SQ='bf16', AP='bf16', RB='bf16', RK='bf16', WS='bf16', UP='bf16')


def _pre(x, mode):
    return x.astype(BF16) if mode == 'bf16' else x


def _sigmoid(z):
    return 1.0 / (1.0 + jnp.exp(-z))


def _proj_kernel(x_ref, flag_ref, gmix_ref, win_ref, wdx_ref, mu_ref, w0a0_ref, w2a2_ref, g2_ref,
                 kk_ref, ka_ref, seg_ref,
                 r_o, lw_o, k_o, v_o, kkn_o, b_o, g_o, qa_o, kat_o, vat_o, xl_o,
                 xn_carry, pj_carry, *, tiles_per_seq, c_rwkv):
    i = pl.program_id(0)

    @pl.when(i % tiles_per_seq == 0)
    def _():
        xn_carry[...] = jnp.zeros_like(xn_carry)
        pj_carry[...] = jnp.zeros_like(pj_carry)

    c = c_rwkv
    x = x_ref[...]
    tm = x.shape[0]
    ms = jnp.mean(x * x, axis=-1, keepdims=True)
    xn = (x * lax.rsqrt(ms + NORM_EPS)) * gmix_ref[...]
    xn = jnp.where(flag_ref[...] > 0.0, x, xn)
    row = lax.broadcasted_iota(jnp.int32, (tm, 1), 0)
    xn_prev = jnp.where(row == 0, xn_carry[7:8, :], pltpu.roll(xn, 1, axis=0))
    dx = xn_prev - xn

    proj = _dot(xn.astype(BF16), win_ref[...])
    cur = proj[:, :3 * c]
    prev = jnp.where(row == 0, pj_carry[7:8, :], pltpu.roll(cur, 1, axis=0))
    xn_carry[...] = xn[tm - 8:, :]
    pj_carry[...] = cur[tm - 8:, :]
    xl_rows = xl_o.shape[1]
    xl_o[0] = xn[tm - xl_rows:, :]

    mu = mu_ref[...]
    r = cur[:, :c] + mu[0:1] * (prev[:, :c] - cur[:, :c])
    k = cur[:, c:2 * c] + mu[1:2] * (prev[:, c:2 * c] - cur[:, c:2 * c])
    v = cur[:, 2 * c:3 * c] + mu[2:3] * (prev[:, 2 * c:3 * c] - cur[:, 2 * c:3 * c])

    lr = proj[:, 6 * c:] + _dot(dx.astype(BF16), wdx_ref[...])
    lane = lax.broadcasted_iota(jnp.int32, (1, LANES), 1)
    wa_in = jnp.where(lane < 64, jnp.tanh(lr[:, :LANES]), lr[:, :LANES])
    wa = _dot(wa_in.astype(BF16), w2a2_ref[...]) + w0a0_ref[...]
    z = -wa[:, :c]
    softplus = jnp.maximum(z, 0.0) + jnp.log1p(jnp.exp(-jnp.abs(z)))
    lw = -jnp.exp(-softplus - 0.5)
    a = _sigmoid(wa[:, c:])
    g = _dot(_sigmoid(lr[:, LANES:]).astype(BF16), g2_ref[...])

    kk = k * kk_ref[...]
    ss = _dot_split(kk * kk, seg_ref[...])
    kk = kk * lax.rsqrt(jnp.maximum(ss, 1e-24))

    r_o[...] = r
    lw_o[...] = lw
    k_o[...] = k * (1.0 + (a - 1.0) * ka_ref[...])
    v_o[...] = v
    kkn_o[...] = kk
    b_o[...] = kk * a
    g_o[...] = g
    qa_o[...] = proj[:, 3 * c:4 * c]
    kat_o[...] = proj[:, 4 * c:5 * c]
    vat_o[...] = proj[:, 5 * c:6 * c]


def _proj_call(x2, flag, p, tiles_per_seq, tm, xl_rows=8):
    t, d = x2.shape
    c = p['c_rwkv']
    n_tiles = t // tm
    full = lambda a: pl.BlockSpec(a.shape, lambda i: (0,) * a.ndim, pipeline_mode=pl.Buffered(1))
    tok = lambda w: pl.BlockSpec((tm, w), lambda i: (i, 0))
    weights = [p['gmix'], p['win'], p['wdx'], p['mu_rkv'], p['w0a0'], p['w2a2'], p['g2'], p['k_k'], p['k_a'], p['seg']]
    outs = pl.pallas_call(
        functools.partial(_proj_kernel, tiles_per_seq=tiles_per_seq, c_rwkv=c),
        grid=(n_tiles,),
        in_specs=[tok(d), tok(1)] + [full(w) for w in weights],
        out_specs=[tok(c)] * 10 + [pl.BlockSpec((1, xl_rows, d), lambda i: (i, 0, 0))],
        out_shape=[jax.ShapeDtypeStruct((t, c), F32)] * 10 + [jax.ShapeDtypeStruct((n_tiles, xl_rows, d), F32)],
        scratch_shapes=[pltpu.VMEM((8, d), F32), pltpu.VMEM((8, 3 * c), F32)],
        compiler_params=pltpu.CompilerParams(dimension_semantics=("arbitrary",), vmem_limit_bytes=V7X_VMEM_LIMIT),
        name="proj",
    )(x2, flag, *weights)
    return outs


GROUP_LANES = 256
GROUP_HEADS = GROUP_LANES // HEAD_DIM


def _rwkv_kernel(r_ref, lw_ref, k_ref, v_ref, kk_ref, b_ref, g_ref, s0_ref, rk_ref, lnw_ref, lnb_ref, seg_ref,
                 y_ref, sout_ref, s_scr):
    ci = pl.program_id(1)
    nb, L, c = r_ref.shape
    gw, gh, hd = GROUP_LANES, GROUP_HEADS, HEAD_DIM
    n_groups = c // gw
    md = RWKV_MODES

    lane_head = lax.broadcasted_iota(jnp.int32, (1, gw), 1) // hd
    head_masks = [lane_head == j for j in range(gh)]
    bd_state = (lax.broadcasted_iota(jnp.int32, (gw, gw), 0) // hd) == (lax.broadcasted_iota(jnp.int32, (gw, gw), 1) // hd)
    bd_time = (lax.broadcasted_iota(jnp.int32, (gh * L, gh * L), 0) // L) == (lax.broadcasted_iota(jnp.int32, (gh * L, gh * L), 1) // L)
    t_row = lax.broadcasted_iota(jnp.int32, (L, gh * L), 0)
    t_col = lax.broadcasted_iota(jnp.int32, (L, gh * L), 1) % L
    strict4 = t_row > t_col
    incl4 = t_row >= t_col
    incl = lax.broadcasted_iota(jnp.int32, (L, L), 0) >= lax.broadcasted_iota(jnp.int32, (L, L), 1)

    def stack(x):
        return jnp.concatenate([jnp.where(m, x, jnp.zeros_like(x)) for m in head_masks], axis=0)

    def block_diag(n):
        tiled = jnp.concatenate([n] * gh, axis=0)
        return jnp.where(bd_time, tiled, jnp.zeros_like(tiled))

    @pl.when(ci == 0)
    def _():
        for bi in range(nb):
            for gi in range(n_groups):
                s_in = s0_ref[bi, gi * gh:(gi + 1) * gh].reshape(gw, hd)
                s_scr[bi, gi] = jnp.where(bd_state, jnp.concatenate([s_in] * gh, axis=1), 0.0)

    n_apply = max(1, int(math.log2(L)))
    seg = seg_ref[...]
    pre = []
    for bi in range(nb):
        lw = lw_ref[bi]
        cs = _dot(incl.astype(F32), lw, HIGHEST)
        cp = cs - lw
        cm = cs[L // 2 - 1:L // 2, :]
        c_last = cs[L - 1:L, :]
        r, k, v, kk, b = r_ref[bi], k_ref[bi], v_ref[bi], kk_ref[bi], b_ref[bi]
        e_dn = jnp.exp(cm - cs)
        e_l = jnp.exp(c_last - cs)
        pre.append(dict(v=v, rt=r * jnp.exp(cs - cm), kkt=kk * jnp.exp(cp - cm), bt=b * e_dn, kt=k * e_dn,
                        kg=kk * jnp.exp(cp), rg=r * jnp.exp(cs), bh=b * e_l, kh=k * e_l, g_last=jnp.exp(c_last),
                        rkk=r * k * rk_ref[...]))

    chains = [(bi, gi) for bi in range(nb) for gi in range(n_groups)]
    col = lambda bi, gi, name: pre[bi][name][:, gi * gw:(gi + 1) * gw]
    each = lambda fn: [fn(i, bi, gi) for i, (bi, gi) in enumerate(chains)]

    vg = each(lambda i, bi, gi: col(bi, gi, 'v'))
    v_st = each(lambda i, bi, gi: stack(_pre(vg[i], md['AV'])))
    a_all = each(lambda i, bi, gi: _mm(
        jnp.concatenate([col(bi, gi, 'kkt'), col(bi, gi, 'rt')], axis=0),
        jnp.concatenate([stack(_pre(col(bi, gi, 'bt'), md['A'])), stack(_pre(col(bi, gi, 'kt'), md['A']))], axis=0),
        NT, md['A']))
    p_ak = each(lambda i, bi, gi: jnp.where(strict4, a_all[i][:L, gh * L:], 0.0))
    p_rb = each(lambda i, bi, gi: jnp.where(incl4, a_all[i][L:, :gh * L], 0.0))
    p_rk = each(lambda i, bi, gi: jnp.where(incl4, a_all[i][L:, gh * L:], 0.0))
    eye4 = (t_row == t_col).astype(F32)
    nm = each(lambda i, bi, gi: -jnp.where(strict4, a_all[i][:L, :gh * L], 0.0))
    t_inv = [eye4 + n for n in nm]
    for it in range(n_apply - 1):
        lhs = nm if it == 0 else [jnp.concatenate([n, t], axis=0) for n, t in zip(nm, t_inv)]
        both = each(lambda i, bi, gi: _mm(lhs[i], block_diag(_pre(nm[i], md['SQ'])), NN, md['SQ']))
        if it > 0:
            t_inv = [t + bo[L:] for t, bo in zip(t_inv, both)]
        nm = [bo[:L] for bo in both]
    t_inv = each(lambda i, bi, gi: t_inv[i] + _mm(t_inv[i], block_diag(_pre(nm[i], md['SQ'])), NN, md['SQ']))
    av = each(lambda i, bi, gi: _mm(p_ak[i], v_st[i], NN, md['AV']))
    x = each(lambda i, bi, gi: _mm(
        t_inv[i], jnp.concatenate([stack(_pre(col(bi, gi, 'kg'), md['AP'])), stack(_pre(av[i], md['AP']))], axis=1),
        NN, md['AP']))
    w_m = [xi[:, :gw] for xi in x]
    u0 = [-xi[:, gw:] for xi in x]
    rbw = each(lambda i, bi, gi: _mm(
        p_rb[i], jnp.concatenate([stack(_pre(w_m[i], md['RB'])), stack(_pre(u0[i], md['RB']))], axis=1),
        NN, md['RB']))
    rkv = each(lambda i, bi, gi: _mm(p_rk[i], v_st[i], NN, md['RK']))
    s_old = each(lambda i, bi, gi: s_scr[bi, gi])
    ws = each(lambda i, bi, gi: _mm(
        jnp.concatenate([w_m[i], col(bi, gi, 'rg') - rbw[i][:, :gw]], axis=0), s_old[i], NT, md['WS']))
    u = each(lambda i, bi, gi: u0[i] - ws[i][:L])
    y = each(lambda i, bi, gi: ws[i][L:] + rbw[i][:, gw:] + rkv[i])
    upd = each(lambda i, bi, gi: _mm(
        jnp.concatenate([u[i], vg[i]], axis=0),
        jnp.concatenate([col(bi, gi, 'bh'), col(bi, gi, 'kh')], axis=0), TN, md['UP']))
    for i, (bi, gi) in enumerate(chains):
        s_scr[bi, gi] = s_old[i] * col(bi, gi, 'g_last') + jnp.where(bd_state, upd[i], 0.0)

    inv = 1.0 / hd
    n_ch = len(chains)
    sums = _dot_split(jnp.concatenate(y + each(lambda i, bi, gi: col(bi, gi, 'rkk')), axis=0), seg)
    mean = [sums[i * L:(i + 1) * L] * inv for i in range(n_ch)]
    bonus = [sums[(n_ch + i) * L:(n_ch + i + 1) * L] * vg[i] for i in range(n_ch)]
    yc = [y[i] - mean[i] for i in range(n_ch)]
    sq = _dot_split(jnp.concatenate([z * z for z in yc], axis=0), seg)
    var = [sq[i * L:(i + 1) * L] * inv for i in range(n_ch)]
    for i, (bi, gi) in enumerate(chains):
        sl = slice(gi * gw, (gi + 1) * gw)
        yn = yc[i] * lax.rsqrt(var[i] + GN_EPS) * lnw_ref[:, sl] + lnb_ref[:, sl]
        y_ref[bi, :, sl] = (yn + bonus[i]) * g_ref[bi, :, sl]

    @pl.when(ci == pl.num_programs(1) - 1)
    def _():
        for bi in range(nb):
            for gi in range(n_groups):
                bd = s_scr[bi, gi]
                folded = bd[:, 0:hd]
                for j in range(1, gh):
                    folded = folded + bd[:, j * hd:(j + 1) * hd]
                sout_ref[bi, gi * gh:(gi + 1) * gh] = folded.reshape(gh, hd, hd)


def _rwkv_call(vecs, s0, p, n_seq, seq_len, chunk, nb):
    c = p['c_rwkv']
    n_heads = c // HEAD_DIM
    n_chunks = seq_len // chunk
    assert n_seq % nb == 0 and seq_len % chunk == 0
    vecs = [z.reshape(n_seq, seq_len, c) for z in vecs]
    tok = pl.BlockSpec((nb, chunk, c), lambda bi, ci: (bi, ci, 0))
    st = pl.BlockSpec((nb, n_heads, HEAD_DIM, HEAD_DIM), lambda bi, ci: (bi, 0, 0, 0))
    rowvec = pl.BlockSpec((1, c), lambda bi, ci: (0, 0))
    seg = p['seg'][:GROUP_LANES, :GROUP_LANES]
    y, s_out = pl.pallas_call(
        _rwkv_kernel,
        grid=(n_seq // nb, n_chunks),
        in_specs=[tok] * 7 + [st, rowvec, rowvec, rowvec, pl.BlockSpec(seg.shape, lambda bi, ci: (0, 0))],
        out_specs=[tok, st],
        out_shape=[jax.ShapeDtypeStruct((n_seq, seq_len, c), F32),
                   jax.ShapeDtypeStruct((n_seq, n_heads, HEAD_DIM, HEAD_DIM), F32)],
        scratch_shapes=[pltpu.VMEM((nb, c // GROUP_LANES, GROUP_LANES, GROUP_LANES), F32)],
        compiler_params=pltpu.CompilerParams(dimension_semantics=("arbitrary", "arbitrary"),
                                             vmem_limit_bytes=V7X_VMEM_LIMIT),
        name="rwkv",
    )(*vecs, s0, p['r_k'], p['ln_w'], p['ln_b'], seg)
    return y.reshape(n_seq * seq_len, c), s_out


def _attn_prompt_kernel(q_ref, k_ref, v_ref, o_ref, m_scr, l_scr, acc_scr):
    s_len = q_ref.shape[0]
    band = ATT_BAND
    n_blk = s_len // band

    lane = lax.broadcasted_iota(jnp.int32, (1, LANES), 1)
    head0 = lane < HEAD_DIM
    qi = lax.broadcasted_iota(jnp.int32, (band, 2 * band), 0)
    kj = lax.broadcasted_iota(jnp.int32, (band, 2 * band), 1)
    in_band = (kj >= qi) & (kj <= qi + band)
    scale = HEAD_DIM ** -0.5
    ones = jnp.ones((2 * band, LANES), BF16)

    for ci, (window, dil) in enumerate(DILATED_CONFIGS):
        assert window // dil == band
        per_res = n_blk // dil

        def body(it, carry, ci=ci, dil=dil, per_res=per_res):
            blocks = []
            for j in range(ATT_UNROLL):
                i = it * ATT_UNROLL + j
                res = i // per_res
                blk = i % per_res
                start = res + blk * (band * dil)
                prev = jnp.maximum(start - band * dil, 0)
                rows = pl.ds(start, band, stride=dil) if dil > 1 else pl.ds(start, band)
                prows = pl.ds(prev, band, stride=dil) if dil > 1 else pl.ds(prev, band)
                blocks.append((blk, rows, prows))
            each = lambda fn: [fn(j, *blocks[j]) for j in range(ATT_UNROLL)]
            q = each(lambda j, blk, rows, prows: q_ref[rows, :] * scale)
            kc = each(lambda j, blk, rows, prows:
                      jnp.concatenate([k_ref[prows, :], k_ref[rows, :]], axis=0).astype(BF16))
            vc = each(lambda j, blk, rows, prows:
                      jnp.concatenate([v_ref[prows, :], v_ref[rows, :]], axis=0).astype(BF16))
            mask = each(lambda j, blk, rows, prows: in_band & (kj >= jnp.where(blk > 0, 0, band)))
            s0 = each(lambda j, blk, rows, prows: jnp.where(mask[j], lax.dot_general(
                jnp.where(head0, q[j], 0.0).astype(BF16), kc[j], NT, preferred_element_type=F32), NEG_INF))
            s1 = each(lambda j, blk, rows, prows: jnp.where(mask[j], lax.dot_general(
                jnp.where(head0, 0.0, q[j]).astype(BF16), kc[j], NT, preferred_element_type=F32), NEG_INF))
            m0 = [jnp.max(z, axis=-1, keepdims=True) for z in s0]
            m1 = [jnp.max(z, axis=-1, keepdims=True) for z in s1]
            p0 = [jnp.exp(z - m).astype(BF16) for z, m in zip(s0, m0)]
            p1 = [jnp.exp(z - m).astype(BF16) for z, m in zip(s1, m1)]
            pv0 = [_dot(p, jnp.where(head0, v, ones)) for p, v in zip(p0, vc)]
            pv1 = [_dot(p, jnp.where(head0, ones, v)) for p, v in zip(p1, vc)]
            for j, (blk, rows, prows) in enumerate(blocks):
                m_scr[ci, rows, :] = jnp.where(head0, m0[j], m1[j])
                acc_scr[ci, rows, :] = jnp.where(head0, pv0[j], pv1[j])
                l_scr[ci, rows, :] = pltpu.roll(jnp.where(head0, pv1[j], pv0[j]), HEAD_DIM, axis=1)
            return carry

        lax.fori_loop(0, n_blk // ATT_UNROLL, body, 0)

    rows_per = 256

    def merge(i, carry):
        rows = pl.ds(pl.multiple_of(i * rows_per, rows_per), rows_per)
        ms = [m_scr[ci, rows, :] for ci in range(len(DILATED_CONFIGS))]
        m_all = functools.reduce(jnp.maximum, ms)
        num = jnp.zeros((rows_per, LANES), F32)
        den = jnp.zeros((rows_per, LANES), F32)
        for ci, m_c in enumerate(ms):
            w_c = jnp.exp(m_c - m_all)
            num = num + w_c * acc_scr[ci, rows, :]
            den = den + w_c * l_scr[ci, rows, :]
        o_ref[rows, :] = num / den
        return carry

    lax.fori_loop(0, s_len // rows_per, merge, 0)


def _attn_prompt_call(q, k, v, n_seq, seq_len):
    c = q.shape[1]
    n_pairs = c // LANES
    blk = pl.BlockSpec((seq_len, LANES), lambda bi, hi: (bi, hi))
    return pl.pallas_call(
        _attn_prompt_kernel,
        grid=(n_seq, n_pairs),
        in_specs=[blk, blk, blk],
        out_specs=blk,
        out_shape=jax.ShapeDtypeStruct((n_seq * seq_len, c), F32),
        scratch_shapes=[pltpu.VMEM((len(DILATED_CONFIGS), seq_len, LANES), F32)] * 3,
        compiler_params=pltpu.CompilerParams(dimension_semantics=("arbitrary", "arbitrary"),
                                             vmem_limit_bytes=V7X_VMEM_LIMIT),
        name="attn_prompt",
    )(q, k, v)


def _attn_sample_kernel(q_ref, kn_ref, vn_ref, kc_ref, vc_ref, o_ref, *, n_new):
    hd = HEAD_DIM
    _, n_heads, _, n_buf = kc_ref.shape
    t_pad = kn_ref.shape[1]
    c = q_ref.shape[2]
    q = q_ref[0] * (hd ** -0.5)
    lane_head = lax.broadcasted_iota(jnp.int32, (1, c), 1) // hd
    qs = jnp.concatenate([jnp.where(lane_head == h, q, 0.0) for h in range(n_heads)], axis=0).astype(BF16)
    n_rows = n_heads * t_pad
    t_idx = lax.broadcasted_iota(jnp.int32, (n_rows, 1), 0) % t_pad

    def multiplicity(dist):
        mult = jnp.zeros(dist.shape, F32)
        for window, dil in DILATED_CONFIGS:
            hit = (dist >= 0) & (dist <= window) & (dist % dil == 0)
            mult = mult + jnp.where(hit, 1.0, 0.0)
        return mult

    jc = lax.broadcasted_iota(jnp.int32, (1, n_buf), 1)
    mult_c = multiplicity(n_buf + t_idx - jc)
    jn = lax.broadcasted_iota(jnp.int32, (1, t_pad), 1)
    mult_n = jnp.where(jn < n_new, multiplicity(t_idx - jn), 0.0)

    sc = jnp.concatenate(
        [_dot(qs[h * t_pad:(h + 1) * t_pad, h * hd:(h + 1) * hd], kc_ref[0, h].astype(BF16))
         for h in range(n_heads)], axis=0)
    sn = lax.dot_general(qs, kn_ref[0].astype(BF16), NT, preferred_element_type=F32)
    sc = jnp.where(mult_c > 0.0, sc, NEG_INF)
    sn = jnp.where(mult_n > 0.0, sn, NEG_INF)
    m = jnp.maximum(jnp.max(sc, axis=-1, keepdims=True), jnp.max(sn, axis=-1, keepdims=True))
    pc = (mult_c * jnp.exp(sc - m)).astype(BF16)
    pn = mult_n * jnp.exp(sn - m)
    inv_l = 1.0 / (jnp.sum(pc.astype(F32), axis=-1, keepdims=True) + jnp.sum(pn, axis=-1, keepdims=True))
    o_new = _dot(pn.astype(BF16), vn_ref[0].astype(BF16)) * inv_l
    out = jnp.zeros((t_pad, c), F32)
    for h in range(n_heads):
        out = out + jnp.where(lane_head == h, o_new[h * t_pad:(h + 1) * t_pad, :], 0.0)
    o_buf = [lax.dot_general(pc[h * t_pad:(h + 1) * t_pad, :], vc_ref[0, h].astype(BF16), NT,
                             preferred_element_type=F32) * inv_l[h * t_pad:(h + 1) * t_pad, :]
             for h in range(n_heads)]
    o_ref[0] = out + jnp.concatenate(o_buf, axis=1)


def _attn_sample_call(q, kn, vn, k_buf, v_buf, n_new):
    b, t_pad, c = q.shape
    _, n_buf, n_heads, hd = k_buf.shape
    k_t = jnp.transpose(k_buf, (0, 2, 3, 1))
    v_t = jnp.transpose(v_buf, (0, 2, 3, 1))
    new = pl.BlockSpec((1, t_pad, c), lambda bi: (bi, 0, 0))
    buf = pl.BlockSpec((1, n_heads, hd, n_buf), lambda bi: (bi, 0, 0, 0))
    return pl.pallas_call(
        functools.partial(_attn_sample_kernel, n_new=n_new),
        grid=(b,),
        in_specs=[new, new, new, buf, buf],
        out_specs=new,
        out_shape=jax.ShapeDtypeStruct((b, t_pad, c), F32),
        compiler_params=pltpu.CompilerParams(dimension_semantics=("arbitrary",), vmem_limit_bytes=V7X_VMEM_LIMIT),
        name="attn_sample",
    )(q, kn, vn, k_t, v_t)


def _route_rows(logits):
    lane = lax.broadcasted_iota(jnp.int32, logits.shape, 1)
    lane_f = lane.astype(F32)
    first = lambda hit: jnp.min(jnp.where(hit, lane_f, float(LANES)), axis=-1, keepdims=True)
    is_g = lane < N_GROUPS
    lg = jnp.where(is_g, logits, NEG_INF)
    g_max = jnp.max(lg, axis=-1, keepdims=True)
    g_idx = first(lg == g_max)
    g_w = 1.0 / jnp.sum(jnp.where(is_g, jnp.exp(lg - g_max), 0.0), axis=-1, keepdims=True)
    lo = N_GROUPS + EXPERTS_PER_GROUP * g_idx
    le = jnp.where((lane_f >= lo) & (lane_f < lo + EXPERTS_PER_GROUP), logits, NEG_INF)
    e1 = jnp.max(le, axis=-1, keepdims=True)
    i1 = first(le == e1)
    le2 = jnp.where(lane_f == i1, NEG_INF, le)
    e2 = jnp.max(le2, axis=-1, keepdims=True)
    i2 = first(le2 == e2)
    ex = jnp.exp(e2 - e1)
    gate1 = g_w / (1.0 + ex)
    gate2 = g_w * ex / (1.0 + ex)
    out = jnp.where(lane == 0, gate1, jnp.where(lane == 1, gate2, 0.0))
    out = jnp.where(lane == 2, i1 - N_GROUPS, jnp.where(lane == 3, i2 - N_GROUPS, out))
    return out


def _post_kernel(x_ref, yr_ref, ya_ref, wo_ref, gffn_ref, rw_hi_ref, rw_lo_ref, rb_ref,
                 h_o, hn_o, lg_o):
    c = yr_ref.shape[1]
    h = (x_ref[...] + _dot(yr_ref[...].astype(BF16), wo_ref[:c, :]) + _dot(ya_ref[...].astype(BF16), wo_ref[c:, :]))
    ms = jnp.mean(h * h, axis=-1, keepdims=True)
    hn = (h * lax.rsqrt(ms + NORM_EPS)) * gffn_ref[...]
    h_o[...] = h
    hn_o[...] = hn.astype(BF16)
    hi = hn.astype(BF16)
    lo = (hn - hi.astype(F32)).astype(BF16)
    logits = (_dot(hi, rw_hi_ref[...]) + _dot(hi, rw_lo_ref[...]) + _dot(lo, rw_hi_ref[...])) + rb_ref[...]
    lg_o[...] = _route_rows(logits)


def _post_call(x2, yr, ya, p, tm):
    t, d = x2.shape
    c = yr.shape[1]
    full = lambda a: pl.BlockSpec(a.shape, lambda i: (0,) * a.ndim)
    tok = lambda w: pl.BlockSpec((tm, w), lambda i: (i, 0))
    weights = [p['wout'], p['gffn'], p['rw_hi'], p['rw_lo'], p['rb']]
    return pl.pallas_call(
        _post_kernel,
        grid=(t // tm,),
        in_specs=[tok(d), tok(c), tok(c)] + [full(w) for w in weights],
        out_specs=[tok(d), tok(d), tok(LANES)],
        out_shape=[jax.ShapeDtypeStruct((t, d), F32), jax.ShapeDtypeStruct((t, d), BF16),
                   jax.ShapeDtypeStruct((t, LANES), F32)],
        compiler_params=pltpu.CompilerParams(dimension_semantics=("arbitrary",), vmem_limit_bytes=V7X_VMEM_LIMIT),
        name="post",
    )(x2, yr, ya, *weights)


def _expert_kernel(be_ref, nb_ref, xs_ref, gate_ref, w1_ref, w3_ref, w2_ref, y_ref, w1_s, w3_s, w2_s):
    i = pl.program_id(0)
    live = i < nb_ref[0]

    @pl.when(live & ((i == 0) | (be_ref[i] != be_ref[jnp.maximum(i - 1, 0)])))
    def _():
        w1_s[...] = w1_ref[...].astype(BF16)
        w3_s[...] = w3_ref[...].astype(BF16)
        w2_s[...] = w2_ref[...].astype(BF16)

    @pl.when(live)
    def _():
        xs = xs_ref[...]
        h1 = _dot(xs, w1_s[...])
        h3 = _dot(xs, w3_s[...])
        act = (h1 * _sigmoid(h1)) * h3
        y_ref[...] = _dot(act.astype(BF16), w2_s[...]) * gate_ref[...]

    @pl.when(jnp.logical_not(live))
    def _():
        y_ref[...] = jnp.zeros_like(y_ref)


def _expert_call(blk_exp, n_used, xs, slot_gate, p, bm):
    n_slots, d = xs.shape
    de = p['w1'].shape[2]
    grid_spec = pltpu.PrefetchScalarGridSpec(
        num_scalar_prefetch=2,
        grid=(n_slots // bm,),
        in_specs=[pl.BlockSpec((bm, d), lambda i, be, nb: (i, 0)),
                  pl.BlockSpec((bm, 1), lambda i, be, nb: (i, 0)),
                  pl.BlockSpec((None, d, de), lambda i, be, nb: (be[i], 0, 0)),
                  pl.BlockSpec((None, d, de), lambda i, be, nb: (be[i], 0, 0)),
                  pl.BlockSpec((None, de, d), lambda i, be, nb: (be[i], 0, 0))],
        out_specs=pl.BlockSpec((bm, d), lambda i, be, nb: (i, 0)),
        scratch_shapes=[pltpu.VMEM((d, de), BF16), pltpu.VMEM((d, de), BF16), pltpu.VMEM((de, d), BF16)],
    )
    return pl.pallas_call(
        _expert_kernel,
        grid_spec=grid_spec,
        out_shape=jax.ShapeDtypeStruct((n_slots, d), F32),
        compiler_params=pltpu.CompilerParams(dimension_semantics=("arbitrary",), vmem_limit_bytes=V7X_VMEM_LIMIT),
        name="experts",
    )(blk_exp, n_used, xs, slot_gate, p['w1'], p['w3'], p['w2'])


def _final_kernel(h_ref, y1_ref, y2_ref, gfin_ref, o_ref):
    h = h_ref[...] + (y1_ref[...] + y2_ref[...])
    ms = jnp.mean(h * h, axis=-1, keepdims=True)
    o_ref[...] = (h * lax.rsqrt(ms + NORM_EPS)) * gfin_ref[...]


def _final_call(h, y1, y2, gfin, tm):
    t, d = h.shape
    tok = pl.BlockSpec((tm, d), lambda i: (i, 0))
    return pl.pallas_call(
        _final_kernel,
        grid=(t // tm,),
        in_specs=[tok, tok, tok, pl.BlockSpec((1, d), lambda i: (0, 0))],
        out_specs=tok,
        out_shape=jax.ShapeDtypeStruct((t, d), F32),
        compiler_params=pltpu.CompilerParams(dimension_semantics=("arbitrary",), vmem_limit_bytes=V7X_VMEM_LIMIT),
        name="final",
    )(h, y1, y2, gfin)


def _route(route, bm):
    n = route.shape[0]
    gates = route[:, 0:2].reshape(-1)
    eid = route[:, 2:4].astype(jnp.int32).reshape(-1)
    m = eid.shape[0]
    experts = jnp.arange(N_EXPERTS + 1, dtype=jnp.int32)
    e_sorted, order = lax.sort_key_val(eid, jnp.arange(m, dtype=jnp.int32))
    below = jnp.sum((eid[:, None] < experts[None, :]).astype(jnp.int32), axis=0)
    starts, counts = below[:-1], below[1:] - below[:-1]
    padded = (counts + bm - 1) // bm * bm
    p_ends = jnp.cumsum(padded)
    p_starts = p_ends - padded
    shift = p_starts - starts
    dest_sorted = jnp.arange(m, dtype=jnp.int32) + jnp.sum(
        jnp.where(e_sorted[:, None] == experts[None, :-1], shift[None, :], 0), axis=1)
    _, dest = lax.sort_key_val(order, dest_sorted)
    n_blocks = -(-m // bm) + N_EXPERTS
    blk_start = jnp.arange(n_blocks, dtype=jnp.int32) * bm
    blk_exp = jnp.minimum(jnp.sum((p_ends[None, :] <= blk_start[:, None]).astype(jnp.int32), axis=1), N_EXPERTS - 1)
    pick = lambda tbl: jnp.sum(jnp.where(blk_exp[:, None] == experts[None, :-1], tbl[None, :], 0), axis=1)
    pos = (blk_start - pick(p_starts))[:, None] + jnp.arange(bm, dtype=jnp.int32)[None, :]
    valid = pos < pick(counts)[:, None]
    src = jnp.where(valid, pick(starts)[:, None] + pos, 0).reshape(-1)
    src_assign = order.at[src].get(mode='promise_in_bounds')
    valid = valid.reshape(-1)
    slot_tok = jnp.where(valid, src_assign // 2, 0)
    slot_gate = jnp.where(valid, gates.at[src_assign].get(mode='promise_in_bounds'), 0.0)
    n_used = (p_ends[-1] // bm).astype(jnp.int32).reshape(1)
    return slot_tok, slot_gate.reshape(-1, 1), dest.reshape(n, 2), blk_exp.astype(jnp.int32), n_used


def _moe_and_final(x2, yr, ya, p, tm):
    h, hn, route = _post_call(x2, yr, ya, p, tm)
    slot_tok, slot_gate, dest, blk_exp, n_used = _route(route, EXPERT_TILE)
    xs = hn.at[slot_tok].get(mode='promise_in_bounds')
    yb = _expert_call(blk_exp, n_used, xs, slot_gate, p, EXPERT_TILE)
    y1 = yb.at[dest[:, 0]].get(mode='promise_in_bounds')
    y2 = yb.at[dest[:, 1]].get(mode='promise_in_bounds')
    return _final_call(h, y1, y2, p['gfin'], tm)


def _prep_params(layer, norm_mix_g, w_in, rwkv_mu_rkv, rwkv_mu_wag, rwkv_w0, rwkv_w1, rwkv_w2, rwkv_a0, rwkv_a1,
                 rwkv_a2, rwkv_g1, rwkv_g2, rwkv_k_k, rwkv_k_a, rwkv_r_k, rwkv_ln_w, rwkv_ln_b, w_out, norm_ffn_g,
                 router_group_w, router_group_b, router_expert_w, router_expert_b, expert_w1, expert_w3, expert_w2,
                 norm_final_g):
    d = w_in.shape[1]
    c = rwkv_w0.shape[1]
    row = lambda a: a.reshape(1, -1).astype(F32)
    lowrank = jnp.concatenate([rwkv_w1[layer], rwkv_a1[layer], rwkv_g1[layer]], axis=1)
    mx = rwkv_mu_wag[layer]
    r_w = rwkv_w1.shape[2]
    r_a = rwkv_a1.shape[2]
    r_g = rwkv_g1.shape[2]
    assert r_w + r_a == LANES and r_g == LANES
    mx_cols = jnp.concatenate([jnp.broadcast_to(mx[0][:, None], (d, r_w)), jnp.broadcast_to(mx[1][:, None], (d, r_a)),
                               jnp.broadcast_to(mx[2][:, None], (d, r_g))], axis=1)
    w2a2 = jnp.zeros((LANES, 2 * c), F32)
    w2a2 = w2a2.at[:r_w, :c].set(rwkv_w2[layer]).at[r_w:, c:].set(rwkv_a2[layer])
    head = jnp.arange(c) // HEAD_DIM
    rw = jnp.zeros((d, LANES), F32)
    rw = rw.at[:, :N_GROUPS].set(router_group_w[layer]).at[:, N_GROUPS:N_GROUPS + N_EXPERTS].set(router_expert_w[layer])
    rw_hi = rw.astype(BF16)
    rb = jnp.zeros((1, LANES), F32)
    rb = rb.at[0, :N_GROUPS].set(router_group_b[layer]).at[0, N_GROUPS:N_GROUPS + N_EXPERTS].set(router_expert_b[layer])
    return {
        'c_rwkv': c,
        'gmix': row(norm_mix_g[layer]),
        'win': jnp.concatenate([w_in[layer], lowrank], axis=1).astype(BF16),
        'wdx': (mx_cols * lowrank).astype(BF16),
        'mu_rkv': rwkv_mu_rkv[layer],
        'w0a0': jnp.concatenate([row(rwkv_w0[layer]), row(rwkv_a0[layer])], axis=1),
        'w2a2': w2a2.astype(BF16),
        'g2': rwkv_g2[layer].astype(BF16),
        'k_k': row(rwkv_k_k[layer]),
        'k_a': row(rwkv_k_a[layer]),
        'seg': (head[:, None] == head[None, :]).astype(BF16),
        'r_k': row(rwkv_r_k[layer]),
        'ln_w': row(rwkv_ln_w[layer]),
        'ln_b': row(rwkv_ln_b[layer]),
        'wout': w_out[layer].astype(BF16),
        'gffn': row(norm_ffn_g[layer]),
        'rw_hi': rw_hi,
        'rw_lo': (rw - rw_hi.astype(F32)).astype(BF16),
        'rb': rb,
        'w1': expert_w1[layer],
        'w3': expert_w3[layer],
        'w2': expert_w2[layer],
        'gfin': row(norm_final_g),
    }


def _prompt_group(x, p):
    b, s, d = x.shape
    c = p['c_rwkv']
    x2 = x.reshape(b * s, d)
    tm = PROJ_TILE
    flag = jnp.zeros((b * s, 1), F32)
    outs = _proj_call(x2, flag, p, s // tm, tm)
    r, lw, k, v, kk, bb, g, qa, ka, va, xl = outs
    s0 = jnp.zeros((b, c // HEAD_DIM, HEAD_DIM, HEAD_DIM), F32)
    yr, s_new = _rwkv_call((r, lw, k, v, kk, bb, g), s0, p, b, s, RWKV_CHUNK, 4)
    ya = _attn_prompt_call(qa, ka, va, b, s)
    y = _moe_and_final(x2, yr, ya, p, tm)
    shift = xl.reshape(b, s // tm, 8, d)[:, -1, 7, :]
    keep = min(max(w for w, _ in DILATED_CONFIGS), s)
    k_keep = ka.reshape(b, s, c // HEAD_DIM, HEAD_DIM)[:, s - keep:]
    v_keep = va.reshape(b, s, c // HEAD_DIM, HEAD_DIM)[:, s - keep:]
    return y.reshape(b, s, d), s_new, shift, k_keep, v_keep


def _sample_group(x, shift0, s0, k_buf, v_buf, p):
    b, t, d = x.shape
    c = p['c_rwkv']
    n_heads = c // HEAD_DIM
    t_pad = 8
    xc = jnp.concatenate([shift0[:, None, :], x, jnp.zeros((b, t_pad - 1 - t, d), x.dtype)], axis=1)
    flag = jnp.zeros((b, t_pad, 1), F32).at[:, 0].set(1.0)
    outs = _proj_call(xc.reshape(b * t_pad, d), flag.reshape(b * t_pad, 1), p, 1, b * t_pad, xl_rows=b * t_pad)
    xl = outs[10]
    live = (jnp.arange(t_pad) < t)[None, :, None]
    shifted = [jnp.where(live, jnp.roll(o.reshape(b, t_pad, c), -1, axis=1), 0.0) for o in outs[:10]]
    r, lw, k, v, kk, bb, g, qa, ka, va = shifted
    flat = lambda z: z.reshape(b * t_pad, c)
    yr, s_new = _rwkv_call(tuple(flat(z) for z in (r, lw, k, v, kk, bb, g)), s0, p, b, t_pad, t_pad, 8)
    ya = _attn_sample_call(qa, ka, va, k_buf, v_buf, t)
    x_pad = jnp.concatenate([x, jnp.zeros((b, t_pad - t, d), x.dtype)], axis=1).reshape(b * t_pad, d)
    y = _moe_and_final(x_pad, yr, flat(ya), p, b * t_pad // 2)
    y = y.reshape(b, t_pad, d)[:, :t]
    shift = xl.reshape(b, t_pad, d)[:, t]
    return (y, s_new, shift, ka[:, :t].reshape(b, t, n_heads, HEAD_DIM), va[:, :t].reshape(b, t, n_heads, HEAD_DIM))


def kernel(x_prompt, x_sample, state_rwkv, state_shift, cache_att_k, cache_att_v, norm_mix_g, w_in, rwkv_mu_rkv, rwkv_mu_wag, rwkv_w0, rwkv_w1, rwkv_w2, rwkv_a0, rwkv_a1, rwkv_a2, rwkv_g1, rwkv_g2, rwkv_k_k, rwkv_k_a, rwkv_r_k, rwkv_ln_w, rwkv_ln_b, w_out, norm_ffn_g, router_group_w, router_group_b, router_expert_w, router_expert_b, expert_w1, expert_w3, expert_w2, norm_final_g):
    assert w_in.shape[0] == 1, "single-layer trunk"
    p = _prep_params(0, norm_mix_g, w_in, rwkv_mu_rkv, rwkv_mu_wag, rwkv_w0, rwkv_w1, rwkv_w2, rwkv_a0, rwkv_a1,
                     rwkv_a2, rwkv_g1, rwkv_g2, rwkv_k_k, rwkv_k_a, rwkv_r_k, rwkv_ln_w, rwkv_ln_b, w_out,
                     norm_ffn_g, router_group_w, router_group_b, router_expert_w, router_expert_b, expert_w1,
                     expert_w3, expert_w2, norm_final_g)
    y_p, rw_p, sh_p, kc_p, vc_p = _prompt_group(x_prompt, p)
    y_s, rw_s, sh_s, kc_s, vc_s = _sample_group(x_sample, state_shift[0], state_rwkv[0], cache_att_k[0],
                                                cache_att_v[0], p)
    return (y_p, y_s, rw_p[None], sh_p[None], kc_p[None], vc_p[None], rw_s[None], sh_s[None], kc_s[None], vc_s[None])
```

```python
import functools
import math

import jax
import jax.numpy as jnp
from jax import lax
from jax.experimental import pallas as pl
from jax.experimental.pallas import tpu as pltpu
from jax.experimental.pallas import tpu_sc as plsc

F32 = jnp.float32
BF16 = jnp.bfloat16

HEAD_DIM = 64
GN_EPS = 64e-5
NORM_EPS = 1e-6
DILATED_CONFIGS = ((128, 1), (512, 4), (2048, 16))
N_GROUPS = 4
EXPERTS_PER_GROUP = 8
N_EXPERTS = N_GROUPS * EXPERTS_PER_GROUP
NEG_INF = -1e30

V7X_VMEM_LIMIT = 56 * 1024 * 1024
LANES = 128

PROJ_TILE = 512
RWKV_CHUNK = 64
ATT_BAND = 128
EXPERT_TILE = 512
ATT_UNROLL = 4
SC_GATHER_ROWS = 32

HIGHEST = lax.Precision.HIGHEST
NN = (((1,), (0,)), ((), ()))
NT = (((1,), (1,)), ((), ()))
TN = (((0,), (0,)), ((), ()))


def _dot(a, b, precision=None):
    return jnp.dot(a, b, preferred_element_type=F32, precision=precision)


def _dot_split(a, b_bf16):
    hi = a.astype(BF16)
    lo = (a - hi.astype(F32)).astype(BF16)
    return _dot(hi, b_bf16) + _dot(lo, b_bf16)


def _mm(a, b, dims, mode):
    if mode == 'f32':
        return lax.dot_general(a, b, dims, precision=HIGHEST, preferred_element_type=F32)
    a_hi = a.astype(BF16)
    b_hi = b.astype(BF16)
    out = lax.dot_general(a_hi, b_hi, dims, preferred_element_type=F32)
    if mode == 'x3':
        a_lo = (a - a_hi.astype(F32)).astype(BF16)
        b_lo = (b - b_hi.astype(F32)).astype(BF16)
        out = out + lax.dot_general(a_hi, b_lo, dims, preferred_element_type=F32)
        out = out + lax.dot_general(a_lo, b_hi, dims, preferred_element_type=F32)
    return out


RWKV_MODES = dict(A='bf16', AV='bf16', SQ='bf16', AP='bf16', RB='bf16', RK='bf16', WS='bf16', UP='bf16')


def _pre(x, mode):
    return x.astype(BF16) if mode == 'bf16' else x


def _sigmoid(z):
    return 1.0 / (1.0 + jnp.exp(-z))


def _proj_kernel(x_ref, flag_ref, gmix_ref, win_ref, wdx_ref, mu_ref, w0a0_ref, w2a2_ref, g2_ref,
                 kk_ref, ka_ref, seg_ref,
                 r_o, lw_o, k_o, v_o, kkn_o, b_o, g_o, qa_o, kat_o, vat_o, xl_o,
                 xn_carry, pj_carry, *, tiles_per_seq, c_rwkv):
    i = pl.program_id(0)

    @pl.when(i % tiles_per_seq == 0)
    def _():
        xn_carry[...] = jnp.zeros_like(xn_carry)
        pj_carry[...] = jnp.zeros_like(pj_carry)

    c = c_rwkv
    x = x_ref[...]
    tm = x.shape[0]
    ms = jnp.mean(x * x, axis=-1, keepdims=True)
    xn = (x * lax.rsqrt(ms + NORM_EPS)) * gmix_ref[...]
    xn = jnp.where(flag_ref[...] > 0.0, x, xn)
    row = lax.broadcasted_iota(jnp.int32, (tm, 1), 0)
    xn_prev = jnp.where(row == 0, xn_carry[7:8, :], pltpu.roll(xn, 1, axis=0))
    dx = xn_prev - xn

    proj = _dot(xn.astype(BF16), win_ref[...])
    cur = proj[:, :3 * c]
    prev = jnp.where(row == 0, pj_carry[7:8, :], pltpu.roll(cur, 1, axis=0))
    xn_carry[...] = xn[tm - 8:, :]
    pj_carry[...] = cur[tm - 8:, :]
    xl_rows = xl_o.shape[1]
    xl_o[0] = xn[tm - xl_rows:, :]

    mu = mu_ref[...]
    r = cur[:, :c] + mu[0:1] * (prev[:, :c] - cur[:, :c])
    k = cur[:, c:2 * c] + mu[1:2] * (prev[:, c:2 * c] - cur[:, c:2 * c])
    v = cur[:, 2 * c:3 * c] + mu[2:3] * (prev[:, 2 * c:3 * c] - cur[:, 2 * c:3 * c])

    lr = proj[:, 6 * c:] + _dot(dx.astype(BF16), wdx_ref[...])
    lane = lax.broadcasted_iota(jnp.int32, (1, LANES), 1)
    wa_in = jnp.where(lane < 64, jnp.tanh(lr[:, :LANES]), lr[:, :LANES])
    wa = _dot(wa_in.astype(BF16), w2a2_ref[...]) + w0a0_ref[...]
    z = -wa[:, :c]
    softplus = jnp.maximum(z, 0.0) + jnp.log1p(jnp.exp(-jnp.abs(z)))
    lw = -jnp.exp(-softplus - 0.5)
    a = _sigmoid(wa[:, c:])
    g = _dot(_sigmoid(lr[:, LANES:]).astype(BF16), g2_ref[...])

    kk = k * kk_ref[...]
    ss = _dot_split(kk * kk, seg_ref[...])
    kk = kk * lax.rsqrt(jnp.maximum(ss, 1e-24))

    r_o[...] = r
    lw_o[...] = lw
    k_o[...] = k * (1.0 + (a - 1.0) * ka_ref[...])
    v_o[...] = v
    kkn_o[...] = kk
    b_o[...] = kk * a
    g_o[...] = g
    qa_o[...] = proj[:, 3 * c:4 * c]
    kat_o[...] = proj[:, 4 * c:5 * c]
    vat_o[...] = proj[:, 5 * c:6 * c]


def _proj_call(x2, flag, p, tiles_per_seq, tm, xl_rows=8):
    t, d = x2.shape
    c = p['c_rwkv']
    n_tiles = t // tm
    full = lambda a: pl.BlockSpec(a.shape, lambda i: (0,) * a.ndim, pipeline_mode=pl.Buffered(1))
    tok = lambda w: pl.BlockSpec((tm, w), lambda i: (i, 0))
    weights = [p['gmix'], p['win'], p['wdx'], p['mu_rkv'], p['w0a0'], p['w2a2'], p['g2'], p['k_k'], p['k_a'], p['seg']]
    outs = pl.pallas_call(
        functools.partial(_proj_kernel, tiles_per_seq=tiles_per_seq, c_rwkv=c),
        grid=(n_tiles,),
        in_specs=[tok(d), tok(1)] + [full(w) for w in weights],
        out_specs=[tok(c)] * 10 + [pl.BlockSpec((1, xl_rows, d), lambda i: (i, 0, 0))],
        out_shape=[jax.ShapeDtypeStruct((t, c), F32)] * 10 + [jax.ShapeDtypeStruct((n_tiles, xl_rows, d), F32)],
        scratch_shapes=[pltpu.VMEM((8, d), F32), pltpu.VMEM((8, 3 * c), F32)],
        compiler_params=pltpu.CompilerParams(dimension_semantics=("arbitrary",), vmem_limit_bytes=V7X_VMEM_LIMIT),
        name="proj",
    )(x2, flag, *weights)
    return outs


GROUP_LANES = 256
GROUP_HEADS = GROUP_LANES // HEAD_DIM


def _rwkv_kernel(r_ref, lw_ref, k_ref, v_ref, kk_ref, b_ref, g_ref, s0_ref, rk_ref, lnw_ref, lnb_ref, seg_ref,
                 y_ref, sout_ref, s_scr):
    ci = pl.program_id(1)
    nb, L, c = r_ref.shape
    gw, gh, hd = GROUP_LANES, GROUP_HEADS, HEAD_DIM
    n_groups = c // gw
    md = RWKV_MODES

    lane_head = lax.broadcasted_iota(jnp.int32, (1, gw), 1) // hd
    head_masks = [lane_head == j for j in range(gh)]
    bd_state = (lax.broadcasted_iota(jnp.int32, (gw, gw), 0) // hd) == (lax.broadcasted_iota(jnp.int32, (gw, gw), 1) // hd)
    bd_time = (lax.broadcasted_iota(jnp.int32, (gh * L, gh * L), 0) // L) == (lax.broadcasted_iota(jnp.int32, (gh * L, gh * L), 1) // L)
    t_row = lax.broadcasted_iota(jnp.int32, (L, gh * L), 0)
    t_col = lax.broadcasted_iota(jnp.int32, (L, gh * L), 1) % L
    strict4 = t_row > t_col
    incl4 = t_row >= t_col
    incl = lax.broadcasted_iota(jnp.int32, (L, L), 0) >= lax.broadcasted_iota(jnp.int32, (L, L), 1)

    def stack(x):
        return jnp.concatenate([jnp.where(m, x, jnp.zeros_like(x)) for m in head_masks], axis=0)

    def block_diag(n):
        tiled = jnp.concatenate([n] * gh, axis=0)
        return jnp.where(bd_time, tiled, jnp.zeros_like(tiled))

    @pl.when(ci == 0)
    def _():
        for bi in range(nb):
            for gi in range(n_groups):
                s_in = s0_ref[bi, gi * gh:(gi + 1) * gh].reshape(gw, hd)
                s_scr[bi, gi] = jnp.where(bd_state, jnp.concatenate([s_in] * gh, axis=1), 0.0)

    n_apply = max(1, int(math.log2(L)))
    seg = seg_ref[...]
    pre = []
    for bi in range(nb):
        lw = lw_ref[bi]
        cs = _dot(incl.astype(F32), lw, HIGHEST)
        cp = cs - lw
        cm = cs[L // 2 - 1:L // 2, :]
        c_last = cs[L - 1:L, :]
        r, k, v, kk, b = r_ref[bi], k_ref[bi], v_ref[bi], kk_ref[bi], b_ref[bi]
        e_dn = jnp.exp(cm - cs)
        e_l = jnp.exp(c_last - cs)
        pre.append(dict(v=v, rt=r * jnp.exp(cs - cm), kkt=kk * jnp.exp(cp - cm), bt=b * e_dn, kt=k * e_dn,
                        kg=kk * jnp.exp(cp), rg=r * jnp.exp(cs), bh=b * e_l, kh=k * e_l, g_last=jnp.exp(c_last),
                        rkk=r * k * rk_ref[...]))

    chains = [(bi, gi) for bi in range(nb) for gi in range(n_groups)]
    col = lambda bi, gi, name: pre[bi][name][:, gi * gw:(gi + 1) * gw]
    each = lambda fn: [fn(i, bi, gi) for i, (bi, gi) in enumerate(chains)]

    vg = each(lambda i, bi, gi: col(bi, gi, 'v'))
    v_st = each(lambda i, bi, gi: stack(_pre(vg[i], md['AV'])))
    a_all = each(lambda i, bi, gi: _mm(
        jnp.concatenate([col(bi, gi, 'kkt'), col(bi, gi, 'rt')], axis=0),
        jnp.concatenate([stack(_pre(col(bi, gi, 'bt'), md['A'])), stack(_pre(col(bi, gi, 'kt'), md['A']))], axis=0),
        NT, md['A']))
    p_ak = each(lambda i, bi, gi: jnp.where(strict4, a_all[i][:L, gh * L:], 0.0))
    p_rb = each(lambda i, bi, gi: jnp.where(incl4, a_all[i][L:, :gh * L], 0.0))
    p_rk = each(lambda i, bi, gi: jnp.where(incl4, a_all[i][L:, gh * L:], 0.0))
    eye4 = (t_row == t_col).astype(F32)
    nm = each(lambda i, bi, gi: -jnp.where(strict4, a_all[i][:L, :gh * L], 0.0))
    t_inv = [eye4 + n for n in nm]
    for it in range(n_apply - 1):
        lhs = nm if it == 0 else [jnp.concatenate([n, t], axis=0) for n, t in zip(nm, t_inv)]
        both = each(lambda i, bi, gi: _mm(lhs[i], block_diag(_pre(nm[i], md['SQ'])), NN, md['SQ']))
        if it > 0:
            t_inv = [t + bo[L:] for t, bo in zip(t_inv, both)]
        nm = [bo[:L] for bo in both]
    t_inv = each(lambda i, bi, gi: t_inv[i] + _mm(t_inv[i], block_diag(_pre(nm[i], md['SQ'])), NN, md['SQ']))
    av = each(lambda i, bi, gi: _mm(p_ak[i], v_st[i], NN, md['AV']))
    x = each(lambda i, bi, gi: _mm(
        t_inv[i], jnp.concatenate([stack(_pre(col(bi, gi, 'kg'), md['AP'])), stack(_pre(av[i], md['AP']))], axis=1),
        NN, md['AP']))
    w_m = [xi[:, :gw] for xi in x]
    u0 = [-xi[:, gw:] for xi in x]
    rbw = each(lambda i, bi, gi: _mm(
        p_rb[i], jnp.concatenate([stack(_pre(w_m[i], md['RB'])), stack(_pre(u0[i], md['RB']))], axis=1),
        NN, md['RB']))
    rkv = each(lambda i, bi, gi: _mm(p_rk[i], v_st[i], NN, md['RK']))
    s_old = each(lambda i, bi, gi: s_scr[bi, gi])
    ws = each(lambda i, bi, gi: _mm(
        jnp.concatenate([w_m[i], col(bi, gi, 'rg') - rbw[i][:, :gw]], axis=0), s_old[i], NT, md['WS']))
    u = each(lambda i, bi, gi: u0[i] - ws[i][:L])
    y = each(lambda i, bi, gi: ws[i][L:] + rbw[i][:, gw:] + rkv[i])
    upd = each(lambda i, bi, gi: _mm(
        jnp.concatenate([u[i], vg[i]], axis=0),
        jnp.concatenate([col(bi, gi, 'bh'), col(bi, gi, 'kh')], axis=0), TN, md['UP']))
    for i, (bi, gi) in enumerate(chains):
        s_scr[bi, gi] = s_old[i] * col(bi, gi, 'g_last') + jnp.where(bd_state, upd[i], 0.0)

    inv = 1.0 / hd
    n_ch = len(chains)
    sums = _dot_split(jnp.concatenate(y + each(lambda i, bi, gi: col(bi, gi, 'rkk')), axis=0), seg)
    mean = [sums[i * L:(i + 1) * L] * inv for i in range(n_ch)]
    bonus = [sums[(n_ch + i) * L:(n_ch + i + 1) * L] * vg[i] for i in range(n_ch)]
    yc = [y[i] - mean[i] for i in range(n_ch)]
    sq = _dot_split(jnp.concatenate([z * z for z in yc], axis=0), seg)
    var = [sq[i * L:(i + 1) * L] * inv for i in range(n_ch)]
    for i, (bi, gi) in enumerate(chains):
        sl = slice(gi * gw, (gi + 1) * gw)
        yn = yc[i] * lax.rsqrt(var[i] + GN_EPS) * lnw_ref[:, sl] + lnb_ref[:, sl]
        y_ref[bi, :, sl] = (yn + bonus[i]) * g_ref[bi, :, sl]

    @pl.when(ci == pl.num_programs(1) - 1)
    def _():
        for bi in range(nb):
            for gi in range(n_groups):
                bd = s_scr[bi, gi]
                folded = bd[:, 0:hd]
                for j in range(1, gh):
                    folded = folded + bd[:, j * hd:(j + 1) * hd]
                sout_ref[bi, gi * gh:(gi + 1) * gh] = folded.reshape(gh, hd, hd)


def _rwkv_call(vecs, s0, p, n_seq, seq_len, chunk, nb):
    c = p['c_rwkv']
    n_heads = c // HEAD_DIM
    n_chunks = seq_len // chunk
    assert n_seq % nb == 0 and seq_len % chunk == 0
    vecs = [z.reshape(n_seq, seq_len, c) for z in vecs]
    tok = pl.BlockSpec((nb, chunk, c), lambda bi, ci: (bi, ci, 0))
    st = pl.BlockSpec((nb, n_heads, HEAD_DIM, HEAD_DIM), lambda bi, ci: (bi, 0, 0, 0))
    rowvec = pl.BlockSpec((1, c), lambda bi, ci: (0, 0))
    seg = p['seg'][:GROUP_LANES, :GROUP_LANES]
    y, s_out = pl.pallas_call(
        _rwkv_kernel,
        grid=(n_seq // nb, n_chunks),
        in_specs=[tok] * 7 + [st, rowvec, rowvec, rowvec, pl.BlockSpec(seg.shape, lambda bi, ci: (0, 0))],
        out_specs=[tok, st],
        out_shape=[jax.ShapeDtypeStruct((n_seq, seq_len, c), F32),
                   jax.ShapeDtypeStruct((n_seq, n_heads, HEAD_DIM, HEAD_DIM), F32)],
        scratch_shapes=[pltpu.VMEM((nb, c // GROUP_LANES, GROUP_LANES, GROUP_LANES), F32)],
        compiler_params=pltpu.CompilerParams(dimension_semantics=("arbitrary", "arbitrary"),
                                             vmem_limit_bytes=V7X_VMEM_LIMIT),
        name="rwkv",
    )(*vecs, s0, p['r_k'], p['ln_w'], p['ln_b'], seg)
    return y.reshape(n_seq * seq_len, c), s_out


def _attn_prompt_kernel(q_ref, k_ref, v_ref, o_ref, m_scr, l_scr, acc_scr):
    s_len = q_ref.shape[0]
    band = ATT_BAND
    n_blk = s_len // band

    lane = lax.broadcasted_iota(jnp.int32, (1, LANES), 1)
    head0 = lane < HEAD_DIM
    qi = lax.broadcasted_iota(jnp.int32, (band, 2 * band), 0)
    kj = lax.broadcasted_iota(jnp.int32, (band, 2 * band), 1)
    in_band = (kj >= qi) & (kj <= qi + band)
    scale = HEAD_DIM ** -0.5
    ones = jnp.ones((2 * band, LANES), BF16)

    for ci, (window, dil) in enumerate(DILATED_CONFIGS):
        assert window // dil == band
        per_res = n_blk // dil

        def body(it, carry, ci=ci, dil=dil, per_res=per_res):
            blocks = []
            for j in range(ATT_UNROLL):
                i = it * ATT_UNROLL + j
                res = i // per_res
                blk = i % per_res
                start = res + blk * (band * dil)
                prev = jnp.maximum(start - band * dil, 0)
                rows = pl.ds(start, band, stride=dil) if dil > 1 else pl.ds(start, band)
                prows = pl.ds(prev, band, stride=dil) if dil > 1 else pl.ds(prev, band)
                blocks.append((blk, rows, prows))
            each = lambda fn: [fn(j, *blocks[j]) for j in range(ATT_UNROLL)]
            q = each(lambda j, blk, rows, prows: q_ref[rows, :] * scale)
            kc = each(lambda j, blk, rows, prows:
                      jnp.concatenate([k_ref[prows, :], k_ref[rows, :]], axis=0).astype(BF16))
            vc = each(lambda j, blk, rows, prows:
                      jnp.concatenate([v_ref[prows, :], v_ref[rows, :]], axis=0).astype(BF16))
            mask = each(lambda j, blk, rows, prows: in_band & (kj >= jnp.where(blk > 0, 0, band)))
            s0 = each(lambda j, blk, rows, prows: jnp.where(mask[j], lax.dot_general(
                jnp.where(head0, q[j], 0.0).astype(BF16), kc[j], NT, preferred_element_type=F32), NEG_INF))
            s1 = each(lambda j, blk, rows, prows: jnp.where(mask[j], lax.dot_general(
                jnp.where(head0, 0.0, q[j]).astype(BF16), kc[j], NT, preferred_element_type=F32), NEG_INF))
            m0 = [jnp.max(z, axis=-1, keepdims=True) for z in s0]
            m1 = [jnp.max(z, axis=-1, keepdims=True) for z in s1]
            p0 = [jnp.exp(z - m).astype(BF16) for z, m in zip(s0, m0)]
            p1 = [jnp.exp(z - m).astype(BF16) for z, m in zip(s1, m1)]
            pv0 = [_dot(p, jnp.where(head0, v, ones)) for p, v in zip(p0, vc)]
            pv1 = [_dot(p, jnp.where(head0, ones, v)) for p, v in zip(p1, vc)]
            for j, (blk, rows, prows) in enumerate(blocks):
                m_scr[ci, rows, :] = jnp.where(head0, m0[j], m1[j])
                acc_scr[ci, rows, :] = jnp.where(head0, pv0[j], pv1[j])
                l_scr[ci, rows, :] = pltpu.roll(jnp.where(head0, pv1[j], pv0[j]), HEAD_DIM, axis=1)
            return carry

        lax.fori_loop(0, n_blk // ATT_UNROLL, body, 0)

    rows_per = 256

    def merge(i, carry):
        rows = pl.ds(pl.multiple_of(i * rows_per, rows_per), rows_per)
        ms = [m_scr[ci, rows, :] for ci in range(len(DILATED_CONFIGS))]
        m_all = functools.reduce(jnp.maximum, ms)
        num = jnp.zeros((rows_per, LANES), F32)
        den = jnp.zeros((rows_per, LANES), F32)
        for ci, m_c in enumerate(ms):
            w_c = jnp.exp(m_c - m_all)
            num = num + w_c * acc_scr[ci, rows, :]
            den = den + w_c * l_scr[ci, rows, :]
        o_ref[rows, :] = num / den
        return carry

    lax.fori_loop(0, s_len // rows_per, merge, 0)


def _attn_prompt_call(q, k, v, n_seq, seq_len):
    c = q.shape[1]
    n_pairs = c // LANES
    blk = pl.BlockSpec((seq_len, LANES), lambda bi, hi: (bi, hi))
    return pl.pallas_call(
        _attn_prompt_kernel,
        grid=(n_seq, n_pairs),
        in_specs=[blk, blk, blk],
        out_specs=blk,
        out_shape=jax.ShapeDtypeStruct((n_seq * seq_len, c), F32),
        scratch_shapes=[pltpu.VMEM((len(DILATED_CONFIGS), seq_len, LANES), F32)] * 3,
        compiler_params=pltpu.CompilerParams(dimension_semantics=("arbitrary", "arbitrary"),
                                             vmem_limit_bytes=V7X_VMEM_LIMIT),
        name="attn_prompt",
    )(q, k, v)


def _attn_sample_kernel(q_ref, kn_ref, vn_ref, kc_ref, vc_ref, o_ref, *, n_new):
    hd = HEAD_DIM
    _, n_heads, _, n_buf = kc_ref.shape
    t_pad = kn_ref.shape[1]
    c = q_ref.shape[2]
    q = q_ref[0] * (hd ** -0.5)
    lane_head = lax.broadcasted_iota(jnp.int32, (1, c), 1) // hd
    qs = jnp.concatenate([jnp.where(lane_head == h, q, 0.0) for h in range(n_heads)], axis=0).astype(BF16)
    n_rows = n_heads * t_pad
    t_idx = lax.broadcasted_iota(jnp.int32, (n_rows, 1), 0) % t_pad

    def multiplicity(dist):
        mult = jnp.zeros(dist.shape, F32)
        for window, dil in DILATED_CONFIGS:
            hit = (dist >= 0) & (dist <= window) & (dist % dil == 0)
            mult = mult + jnp.where(hit, 1.0, 0.0)
        return mult

    jc = lax.broadcasted_iota(jnp.int32, (1, n_buf), 1)
    mult_c = multiplicity(n_buf + t_idx - jc)
    jn = lax.broadcasted_iota(jnp.int32, (1, t_pad), 1)
    mult_n = jnp.where(jn < n_new, multiplicity(t_idx - jn), 0.0)

    sc = jnp.concatenate(
        [_dot(qs[h * t_pad:(h + 1) * t_pad, h * hd:(h + 1) * hd], kc_ref[0, h].astype(BF16))
         for h in range(n_heads)], axis=0)
    sn = lax.dot_general(qs, kn_ref[0].astype(BF16), NT, preferred_element_type=F32)
    sc = jnp.where(mult_c > 0.0, sc, NEG_INF)
    sn = jnp.where(mult_n > 0.0, sn, NEG_INF)
    m = jnp.maximum(jnp.max(sc, axis=-1, keepdims=True), jnp.max(sn, axis=-1, keepdims=True))
    pc = (mult_c * jnp.exp(sc - m)).astype(BF16)
    pn = mult_n * jnp.exp(sn - m)
    inv_l = 1.0 / (jnp.sum(pc.astype(F32), axis=-1, keepdims=True) + jnp.sum(pn, axis=-1, keepdims=True))
    o_new = _dot(pn.astype(BF16), vn_ref[0].astype(BF16)) * inv_l
    out = jnp.zeros((t_pad, c), F32)
    for h in range(n_heads):
        out = out + jnp.where(lane_head == h, o_new[h * t_pad:(h + 1) * t_pad, :], 0.0)
    o_buf = [lax.dot_general(pc[h * t_pad:(h + 1) * t_pad, :], vc_ref[0, h].astype(BF16), NT,
                             preferred_element_type=F32) * inv_l[h * t_pad:(h + 1) * t_pad, :]
             for h in range(n_heads)]
    o_ref[0] = out + jnp.concatenate(o_buf, axis=1)


def _attn_sample_call(q, kn, vn, k_buf, v_buf, n_new):
    b, t_pad, c = q.shape
    _, n_buf, n_heads, hd = k_buf.shape
    k_t = jnp.transpose(k_buf, (0, 2, 3, 1))
    v_t = jnp.transpose(v_buf, (0, 2, 3, 1))
    new = pl.BlockSpec((1, t_pad, c), lambda bi: (bi, 0, 0))
    buf = pl.BlockSpec((1, n_heads, hd, n_buf), lambda bi: (bi, 0, 0, 0))
    return pl.pallas_call(
        functools.partial(_attn_sample_kernel, n_new=n_new),
        grid=(b,),
        in_specs=[new, new, new, buf, buf],
        out_specs=new,
        out_shape=jax.ShapeDtypeStruct((b, t_pad, c), F32),
        compiler_params=pltpu.CompilerParams(dimension_semantics=("arbitrary",), vmem_limit_bytes=V7X_VMEM_LIMIT),
        name="attn_sample",
    )(q, kn, vn, k_t, v_t)


def _route_rows(logits):
    lane = lax.broadcasted_iota(jnp.int32, logits.shape, 1)
    lane_f = lane.astype(F32)
    first = lambda hit: jnp.min(jnp.where(hit, lane_f, float(LANES)), axis=-1, keepdims=True)
    is_g = lane < N_GROUPS
    lg = jnp.where(is_g, logits, NEG_INF)
    g_max = jnp.max(lg, axis=-1, keepdims=True)
    g_idx = first(lg == g_max)
    g_w = 1.0 / jnp.sum(jnp.where(is_g, jnp.exp(lg - g_max), 0.0), axis=-1, keepdims=True)
    lo = N_GROUPS + EXPERTS_PER_GROUP * g_idx
    le = jnp.where((lane_f >= lo) & (lane_f < lo + EXPERTS_PER_GROUP), logits, NEG_INF)
    e1 = jnp.max(le, axis=-1, keepdims=True)
    i1 = first(le == e1)
    le2 = jnp.where(lane_f == i1, NEG_INF, le)
    e2 = jnp.max(le2, axis=-1, keepdims=True)
    i2 = first(le2 == e2)
    ex = jnp.exp(e2 - e1)
    gate1 = g_w / (1.0 + ex)
    gate2 = g_w * ex / (1.0 + ex)
    out = jnp.where(lane == 0, gate1, jnp.where(lane == 1, gate2, 0.0))
    out = jnp.where(lane == 2, i1 - N_GROUPS, jnp.where(lane == 3, i2 - N_GROUPS, out))
    return out


def _post_kernel(x_ref, yr_ref, ya_ref, wo_ref, gffn_ref, rw_hi_ref, rw_lo_ref, rb_ref,
                 h_o, hn_o, lg_o):
    c = yr_ref.shape[1]
    h = (x_ref[...] + _dot(yr_ref[...].astype(BF16), wo_ref[:c, :]) + _dot(ya_ref[...].astype(BF16), wo_ref[c:, :]))
    ms = jnp.mean(h * h, axis=-1, keepdims=True)
    hn = (h * lax.rsqrt(ms + NORM_EPS)) * gffn_ref[...]
    h_o[...] = h
    hn_o[...] = hn
    hi = hn.astype(BF16)
    lo = (hn - hi.astype(F32)).astype(BF16)
    logits = (_dot(hi, rw_hi_ref[...]) + _dot(hi, rw_lo_ref[...]) + _dot(lo, rw_hi_ref[...])) + rb_ref[...]
    lg_o[...] = _route_rows(logits)


def _post_call(x2, yr, ya, p, tm):
    t, d = x2.shape
    c = yr.shape[1]
    full = lambda a: pl.BlockSpec(a.shape, lambda i: (0,) * a.ndim)
    tok = lambda w: pl.BlockSpec((tm, w), lambda i: (i, 0))
    weights = [p['wout'], p['gffn'], p['rw_hi'], p['rw_lo'], p['rb']]
    return pl.pallas_call(
        _post_kernel,
        grid=(t // tm,),
        in_specs=[tok(d), tok(c), tok(c)] + [full(w) for w in weights],
        out_specs=[tok(d), tok(d), tok(LANES)],
        out_shape=[jax.ShapeDtypeStruct((t, d), F32), jax.ShapeDtypeStruct((t, d), F32),
                   jax.ShapeDtypeStruct((t, LANES), F32)],
        compiler_params=pltpu.CompilerParams(dimension_semantics=("arbitrary",), vmem_limit_bytes=V7X_VMEM_LIMIT),
        name="post",
    )(x2, yr, ya, *weights)


def _expert_kernel(be_ref, nb_ref, xs_ref, w1_ref, w3_ref, w2_ref, y_ref, w1_s, w3_s, w2_s):
    i = pl.program_id(0)
    live = i < nb_ref[0]

    @pl.when(live & ((i == 0) | (be_ref[i] != be_ref[jnp.maximum(i - 1, 0)])))
    def _():
        w1_s[...] = w1_ref[...].astype(BF16)
        w3_s[...] = w3_ref[...].astype(BF16)
        w2_s[...] = w2_ref[...].astype(BF16)

    @pl.when(live)
    def _():
        xs = xs_ref[...].astype(BF16)
        h1 = _dot(xs, w1_s[...])
        h3 = _dot(xs, w3_s[...])
        act = (h1 * _sigmoid(h1)) * h3
        y_ref[...] = _dot(act.astype(BF16), w2_s[...])

    @pl.when(jnp.logical_not(live))
    def _():
        y_ref[...] = jnp.zeros_like(y_ref)


def _expert_call(blk_exp, n_used, xs, p, bm):
    n_slots, d = xs.shape
    de = p['w1'].shape[2]
    grid_spec = pltpu.PrefetchScalarGridSpec(
        num_scalar_prefetch=2,
        grid=(n_slots // bm,),
        in_specs=[pl.BlockSpec((bm, d), lambda i, be, nb: (i, 0)),
                  pl.BlockSpec((None, d, de), lambda i, be, nb: (be[i], 0, 0)),
                  pl.BlockSpec((None, d, de), lambda i, be, nb: (be[i], 0, 0)),
                  pl.BlockSpec((None, de, d), lambda i, be, nb: (be[i], 0, 0))],
        out_specs=pl.BlockSpec((bm, d), lambda i, be, nb: (i, 0)),
        scratch_shapes=[pltpu.VMEM((d, de), BF16), pltpu.VMEM((d, de), BF16), pltpu.VMEM((de, d), BF16)],
    )
    return pl.pallas_call(
        _expert_kernel,
        grid_spec=grid_spec,
        out_shape=jax.ShapeDtypeStruct((n_slots, d), F32),
        compiler_params=pltpu.CompilerParams(dimension_semantics=("arbitrary",), vmem_limit_bytes=V7X_VMEM_LIMIT),
        name="experts",
    )(blk_exp, n_used, xs, p['w1'], p['w3'], p['w2'])


def _final_kernel(h_ref, route_ref, y1_ref, y2_ref, gfin_ref, o_ref):
    route = route_ref[...]
    h = h_ref[...] + (route[:, 0:1] * y1_ref[...] + route[:, 1:2] * y2_ref[...])
    ms = jnp.mean(h * h, axis=-1, keepdims=True)
    o_ref[...] = (h * lax.rsqrt(ms + NORM_EPS)) * gfin_ref[...]


def _final_call(h, route, y12, gfin, tm):
    t, d = h.shape
    tok = pl.BlockSpec((tm, d), lambda i: (i, 0))
    routed = lambda a: pl.BlockSpec((None, tm, d), lambda i: (a, i, 0))
    return pl.pallas_call(
        _final_kernel,
        grid=(t // tm,),
        in_specs=[tok, pl.BlockSpec((tm, LANES), lambda i: (i, 0)), routed(0), routed(1),
                  pl.BlockSpec((1, d), lambda i: (0, 0))],
        out_specs=tok,
        out_shape=jax.ShapeDtypeStruct((t, d), F32),
        compiler_params=pltpu.CompilerParams(dimension_semantics=("arbitrary",), vmem_limit_bytes=V7X_VMEM_LIMIT),
        name="final",
    )(h, route, y12, y12, gfin)


def _gather_rows(table, idx):
    info = plsc.get_sparse_core_info()
    nc, ns = info.num_cores, info.num_subcores
    chunk = SC_GATHER_ROWS
    b, d = idx.shape[0], table.shape[1]
    assert b % (nc * ns * chunk * 2) == 0, "rows must split evenly into chunk pairs per subcore"
    per_w = b // (nc * ns)
    n_chunks = per_w // chunk
    mesh = plsc.VectorSubcoreMesh(core_axis_name="c", subcore_axis_name="s")

    @functools.partial(
        pl.kernel, mesh=mesh, out_type=jax.ShapeDtypeStruct((b, d), table.dtype),
        scratch_types=[pltpu.VMEM((per_w,), jnp.int32), pltpu.VMEM((2, chunk, d), table.dtype),
                       pltpu.SemaphoreType.DMA((2,)), pltpu.SemaphoreType.DMA((2,))])
    def gather(table_hbm, idx_hbm, out_hbm, idx_v, rows_v, fetch_sem, put_sem):
        base = (lax.axis_index("s") * nc + lax.axis_index("c")) * per_w
        pltpu.sync_copy(idx_hbm.at[pl.ds(base, per_w)], idx_v)

        def fetch(c, slot):
            off = pl.multiple_of(c * chunk, chunk)
            return pltpu.make_async_copy(table_hbm.at[idx_v.at[pl.ds(off, chunk)]], rows_v.at[slot],
                                         fetch_sem.at[slot])

        def put(c, slot):
            off = pl.multiple_of(c * chunk, chunk)
            return pltpu.make_async_copy(rows_v.at[slot], out_hbm.at[pl.ds(base + off, chunk)], put_sem.at[slot])

        fetch(0, 0).start()

        @pl.loop(0, n_chunks, step=2)
        def _(c):
            @pl.when(c > 0)
            def _():
                put(c - 1, 1).wait()
            fetch(c + 1, 1).start()
            fetch(c, 0).wait()
            put(c, 0).start()
            fetch(c + 1, 1).wait()
            put(c, 0).wait()

            @pl.when(c + 2 < n_chunks)
            def _():
                fetch(c + 2, 0).start()
            put(c + 1, 1).start()

        put(n_chunks - 1, 1).wait()

    return gather(table, idx)


def _route(route, bm):
    n = route.shape[0]
    eid = route[:, 2:4].astype(jnp.int32).reshape(-1)
    m = eid.shape[0]
    experts = jnp.arange(N_EXPERTS + 1, dtype=jnp.int32)
    e_sorted, order = lax.sort_key_val(eid, jnp.arange(m, dtype=jnp.int32))
    below = jnp.sum((eid[:, None] < experts[None, :]).astype(jnp.int32), axis=0)
    starts, counts = below[:-1], below[1:] - below[:-1]
    padded = (counts + bm - 1) // bm * bm
    p_ends = jnp.cumsum(padded)
    p_starts = p_ends - padded
    shift = p_starts - starts
    dest_sorted = jnp.arange(m, dtype=jnp.int32) + jnp.sum(
        jnp.where(e_sorted[:, None] == experts[None, :-1], shift[None, :], 0), axis=1)
    _, dest = lax.sort_key_val(order, dest_sorted)
    n_blocks = -(-m // bm) + N_EXPERTS
    blk_start = jnp.arange(n_blocks, dtype=jnp.int32) * bm
    blk_exp = jnp.minimum(jnp.sum((p_ends[None, :] <= blk_start[:, None]).astype(jnp.int32), axis=1), N_EXPERTS - 1)
    pick = lambda tbl: jnp.sum(jnp.where(blk_exp[:, None] == experts[None, :-1], tbl[None, :], 0), axis=1)
    pos = (blk_start - pick(p_starts))[:, None] + jnp.arange(bm, dtype=jnp.int32)[None, :]
    valid = pos < pick(counts)[:, None]
    src = jnp.where(valid, pick(starts)[:, None] + pos, 0).reshape(-1)
    src_assign = order.at[src].get(mode='promise_in_bounds')
    valid = valid.reshape(-1)
    slot_tok = jnp.where(valid, src_assign // 2, 0)
    n_used = (p_ends[-1] // bm).astype(jnp.int32).reshape(1)
    return slot_tok, dest.reshape(n, 2), blk_exp.astype(jnp.int32), n_used


def _moe_and_final(x2, yr, ya, p, tm):
    h, hn, route = _post_call(x2, yr, ya, p, tm)
    slot_tok, dest, blk_exp, n_used = _route(route, EXPERT_TILE)
    xs = _gather_rows(hn, slot_tok)
    yb = _expert_call(blk_exp, n_used, xs, p, EXPERT_TILE)
    y12 = _gather_rows(yb, dest.T.reshape(-1)).reshape(2, h.shape[0], h.shape[1])
    return _final_call(h, route, y12, p['gfin'], tm)


def _prep_params(layer, norm_mix_g, w_in, rwkv_mu_rkv, rwkv_mu_wag, rwkv_w0, rwkv_w1, rwkv_w2, rwkv_a0, rwkv_a1,
                 rwkv_a2, rwkv_g1, rwkv_g2, rwkv_k_k, rwkv_k_a, rwkv_r_k, rwkv_ln_w, rwkv_ln_b, w_out, norm_ffn_g,
                 router_group_w, router_group_b, router_expert_w, router_expert_b, expert_w1, expert_w3, expert_w2,
                 norm_final_g):
    d = w_in.shape[1]
    c = rwkv_w0.shape[1]
    row = lambda a: a.reshape(1, -1).astype(F32)
    lowrank = jnp.concatenate([rwkv_w1[layer], rwkv_a1[layer], rwkv_g1[layer]], axis=1)
    mx = rwkv_mu_wag[layer]
    r_w = rwkv_w1.shape[2]
    r_a = rwkv_a1.shape[2]
    r_g = rwkv_g1.shape[2]
    assert r_w + r_a == LANES and r_g == LANES
    mx_cols = jnp.concatenate([jnp.broadcast_to(mx[0][:, None], (d, r_w)), jnp.broadcast_to(mx[1][:, None], (d, r_a)),
                               jnp.broadcast_to(mx[2][:, None], (d, r_g))], axis=1)
    w2a2 = jnp.zeros((LANES, 2 * c), F32)
    w2a2 = w2a2.at[:r_w, :c].set(rwkv_w2[layer]).at[r_w:, c:].set(rwkv_a2[layer])
    head = jnp.arange(c) // HEAD_DIM
    rw = jnp.zeros((d, LANES), F32)
    rw = rw.at[:, :N_GROUPS].set(router_group_w[layer]).at[:, N_GROUPS:N_GROUPS + N_EXPERTS].set(router_expert_w[layer])
    rw_hi = rw.astype(BF16)
    rb = jnp.zeros((1, LANES), F32)
    rb = rb.at[0, :N_GROUPS].set(router_group_b[layer]).at[0, N_GROUPS:N_GROUPS + N_EXPERTS].set(router_expert_b[layer])
    return {
        'c_rwkv': c,
        'gmix': row(norm_mix_g[layer]),
        'win': jnp.concatenate([w_in[layer], lowrank], axis=1).astype(BF16),
        'wdx': (mx_cols * lowrank).astype(BF16),
        'mu_rkv': rwkv_mu_rkv[layer],
        'w0a0': jnp.concatenate([row(rwkv_w0[layer]), row(rwkv_a0[layer])], axis=1),
        'w2a2': w2a2.astype(BF16),
        'g2': rwkv_g2[layer].astype(BF16),
        'k_k': row(rwkv_k_k[layer]),
        'k_a': row(rwkv_k_a[layer]),
        'seg': (head[:, None] == head[None, :]).astype(BF16),
        'r_k': row(rwkv_r_k[layer]),
        'ln_w': row(rwkv_ln_w[layer]),
        'ln_b': row(rwkv_ln_b[layer]),
        'wout': w_out[layer].astype(BF16),
        'gffn': row(norm_ffn_g[layer]),
        'rw_hi': rw_hi,
        'rw_lo': (rw - rw_hi.astype(F32)).astype(BF16),
        'rb': rb,
        'w1': expert_w1[layer],
        'w3': expert_w3[layer],
        'w2': expert_w2[layer],
        'gfin': row(norm_final_g),
    }


def _prompt_group(x, p):
    b, s, d = x.shape
    c = p['c_rwkv']
    x2 = x.reshape(b * s, d)
    tm = PROJ_TILE
    flag = jnp.zeros((b * s, 1), F32)
    outs = _proj_call(x2, flag, p, s // tm, tm)
    r, lw, k, v, kk, bb, g, qa, ka, va, xl = outs
    s0 = jnp.zeros((b, c // HEAD_DIM, HEAD_DIM, HEAD_DIM), F32)
    yr, s_new = _rwkv_call((r, lw, k, v, kk, bb, g), s0, p, b, s, RWKV_CHUNK, 4)
    ya = _attn_prompt_call(qa, ka, va, b, s)
    y = _moe_and_final(x2, yr, ya, p, tm)
    shift = xl.reshape(b, s // tm, 8, d)[:, -1, 7, :]
    keep = min(max(w for w, _ in DILATED_CONFIGS), s)
    k_keep = ka.reshape(b, s, c // HEAD_DIM, HEAD_DIM)[:, s - keep:]
    v_keep = va.reshape(b, s, c // HEAD_DIM, HEAD_DIM)[:, s - keep:]
    return y.reshape(b, s, d), s_new, shift, k_keep, v_keep


def _sample_group(x, shift0, s0, k_buf, v_buf, p):
    b, t, d = x.shape
    c = p['c_rwkv']
    n_heads = c // HEAD_DIM
    t_pad = 8
    xc = jnp.concatenate([shift0[:, None, :], x, jnp.zeros((b, t_pad - 1 - t, d), x.dtype)], axis=1)
    flag = jnp.zeros((b, t_pad, 1), F32).at[:, 0].set(1.0)
    outs = _proj_call(xc.reshape(b * t_pad, d), flag.reshape(b * t_pad, 1), p, 1, b * t_pad, xl_rows=b * t_pad)
    xl = outs[10]
    live = (jnp.arange(t_pad) < t)[None, :, None]
    shifted = [jnp.where(live, jnp.roll(o.reshape(b, t_pad, c), -1, axis=1), 0.0) for o in outs[:10]]
    r, lw, k, v, kk, bb, g, qa, ka, va = shifted
    flat = lambda z: z.reshape(b * t_pad, c)
    yr, s_new = _rwkv_call(tuple(flat(z) for z in (r, lw, k, v, kk, bb, g)), s0, p, b, t_pad, t_pad, 8)
    ya = _attn_sample_call(qa, ka, va, k_buf, v_buf, t)
    x_pad = jnp.concatenate([x, jnp.zeros((b, t_pad - t, d), x.dtype)], axis=1).reshape(b * t_pad, d)
    y = _moe_and_final(x_pad, yr, flat(ya), p, b * t_pad // 2)
    y = y.reshape(b, t_pad, d)[:, :t]
    shift = xl.reshape(b, t_pad, d)[:, t]
    return (y, s_new, shift, ka[:, :t].reshape(b, t, n_heads, HEAD_DIM), va[:, :t].reshape(b, t, n_heads, HEAD_DIM))


def kernel(x_prompt, x_sample, state_rwkv, state_shift, cache_att_k, cache_att_v, norm_mix_g, w_in, rwkv_mu_rkv, rwkv_mu_wag, rwkv_w0, rwkv_w1, rwkv_w2, rwkv_a0, rwkv_a1, rwkv_a2, rwkv_g1, rwkv_g2, rwkv_k_k, rwkv_k_a, rwkv_r_k, rwkv_ln_w, rwkv_ln_b, w_out, norm_ffn_g, router_group_w, router_group_b, router_expert_w, router_expert_b, expert_w1, expert_w3, expert_w2, norm_final_g):
    assert w_in.shape[0] == 1, "single-layer trunk"
    p = _prep_params(0, norm_mix_g, w_in, rwkv_mu_rkv, rwkv_mu_wag, rwkv_w0, rwkv_w1, rwkv_w2, rwkv_a0, rwkv_a1,
                     rwkv_a2, rwkv_g1, rwkv_g2, rwkv_k_k, rwkv_k_a, rwkv_r_k, rwkv_ln_w, rwkv_ln_b, w_out,
                     norm_ffn_g, router_group_w, router_group_b, router_expert_w, router_expert_b, expert_w1,
                     expert_w3, expert_w2, norm_final_g)
    y_p, rw_p, sh_p, kc_p, vc_p = _prompt_group(x_prompt, p)
    y_s, rw_s, sh_s, kc_s, vc_s = _sample_group(x_sample, state_shift[0], state_rwkv[0], cache_att_k[0],
                                                cache_att_v[0], p)
    return (y_p, y_s, rw_p[None], sh_p[None], kc_p[None], vc_p[None], rw_s[None], sh_s[None], kc_s[None], vc_s[None])
```

```python
import functools
import math

import jax
import jax.numpy as jnp
from jax import lax
from jax.experimental import pallas as pl
from jax.experimental.pallas import tpu as pltpu
from jax.experimental.pallas import tpu_sc as plsc

F32 = jnp.float32
BF16 = jnp.bfloat16

HEAD_DIM = 64
GN_EPS = 64e-5
NORM_EPS = 1e-6
DILATED_CONFIGS = ((128, 1), (512, 4), (2048, 16))
N_GROUPS = 4
EXPERTS_PER_GROUP = 8
N_EXPERTS = N_GROUPS * EXPERTS_PER_GROUP
NEG_INF = -1e30

V7X_VMEM_LIMIT = 56 * 1024 * 1024
LANES = 128

PROJ_TILE = 512
RWKV_CHUNK = 64
ATT_BAND = 128
EXPERT_TILE = 512
ATT_UNROLL = 4
SC_GATHER_BYTES = 128 * 1024
SC_INDEX_LIMIT = 128
SLOT_BLOCK_MULTIPLE = 8

HIGHEST = lax.Precision.HIGHEST
NN = (((1,), (0,)), ((), ()))
NT = (((1,), (1,)), ((), ()))
TN = (((0,), (0,)), ((), ()))


def _dot(a, b, precision=None):
    return jnp.dot(a, b, preferred_element_type=F32, precision=precision)


def _dot_split(a, b_bf16):
    hi = a.astype(BF16)
    lo = (a - hi.astype(F32)).astype(BF16)
    return _dot(hi, b_bf16) + _dot(lo, b_bf16)


def _mm(a, b, dims, mode):
    if mode == 'f32':
        return lax.dot_general(a, b, dims, precision=HIGHEST, preferred_element_type=F32)
    a_hi = a.astype(BF16)
    b_hi = b.astype(BF16)
    out = lax.dot_general(a_hi, b_hi, dims, preferred_element_type=F32)
    if mode == 'x3':
        a_lo = (a - a_hi.astype(F32)).astype(BF16)
        b_lo = (b - b_hi.astype(F32)).astype(BF16)
        out = out + lax.dot_general(a_hi, b_lo, dims, preferred_element_type=F32)
        out = out + lax.dot_general(a_lo, b_hi, dims, preferred_element_type=F32)
    return out


RWKV_MODES = dict(A='bf16', AV='bf16', SQ='bf16', AP='bf16', RB='bf16', RK='bf16', WS='bf16', UP='bf16')


def _pre(x, mode):
    return x.astype(BF16) if mode == 'bf16' else x


def _round_up(x, k):
    return -(-x // k) * k


def _sigmoid(z):
    return 1.0 / (1.0 + jnp.exp(-z))


def _proj_kernel(x_ref, flag_ref, gmix_ref, win_ref, wdx_ref, mu_ref, w0a0_ref, w2a2_ref, g2_ref,
                 kk_ref, ka_ref, seg_ref,
                 r_o, lw_o, k_o, v_o, kkn_o, b_o, g_o, qa_o, kat_o, vat_o, xl_o,
                 xn_carry, pj_carry, *, tiles_per_seq, c_rwkv):
    i = pl.program_id(0)

    @pl.when(i % tiles_per_seq == 0)
    def _():
        xn_carry[...] = jnp.zeros_like(xn_carry)
        pj_carry[...] = jnp.zeros_like(pj_carry)

    c = c_rwkv
    x = x_ref[...]
    tm = x.shape[0]
    ms = jnp.mean(x * x, axis=-1, keepdims=True)
    xn = (x * lax.rsqrt(ms + NORM_EPS)) * gmix_ref[...]
    xn = jnp.where(flag_ref[...] > 0.0, x, xn)
    row = lax.broadcasted_iota(jnp.int32, (tm, 1), 0)
    xn_prev = jnp.where(row == 0, xn_carry[7:8, :], pltpu.roll(xn, 1, axis=0))
    dx = xn_prev - xn

    proj = _dot(xn.astype(BF16), win_ref[...])
    cur = proj[:, :3 * c]
    prev = jnp.where(row == 0, pj_carry[7:8, :], pltpu.roll(cur, 1, axis=0))
    xn_carry[...] = xn[tm - 8:, :]
    pj_carry[...] = cur[tm - 8:, :]
    xl_rows = xl_o.shape[1]
    xl_o[0] = xn[tm - xl_rows:, :]

    mu = mu_ref[...]
    r = cur[:, :c] + mu[0:1] * (prev[:, :c] - cur[:, :c])
    k = cur[:, c:2 * c] + mu[1:2] * (prev[:, c:2 * c] - cur[:, c:2 * c])
    v = cur[:, 2 * c:3 * c] + mu[2:3] * (prev[:, 2 * c:3 * c] - cur[:, 2 * c:3 * c])

    lr = proj[:, 6 * c:] + _dot(dx.astype(BF16), wdx_ref[...])
    lane = lax.broadcasted_iota(jnp.int32, (1, LANES), 1)
    wa_in = jnp.where(lane < 64, jnp.tanh(lr[:, :LANES]), lr[:, :LANES])
    wa = _dot(wa_in.astype(BF16), w2a2_ref[...]) + w0a0_ref[...]
    z = -wa[:, :c]
    softplus = jnp.maximum(z, 0.0) + jnp.log1p(jnp.exp(-jnp.abs(z)))
    lw = -jnp.exp(-softplus - 0.5)
    a = _sigmoid(wa[:, c:])
    g = _dot(_sigmoid(lr[:, LANES:]).astype(BF16), g2_ref[...])

    kk = k * kk_ref[...]
    ss = _dot_split(kk * kk, seg_ref[...])
    kk = kk * lax.rsqrt(jnp.maximum(ss, 1e-24))

    r_o[...] = r
    lw_o[...] = lw
    k_o[...] = k * (1.0 + (a - 1.0) * ka_ref[...])
    v_o[...] = v
    kkn_o[...] = kk
    b_o[...] = kk * a
    g_o[...] = g
    qa_o[...] = proj[:, 3 * c:4 * c]
    kat_o[...] = proj[:, 4 * c:5 * c]
    vat_o[...] = proj[:, 5 * c:6 * c]


def _proj_call(x2, flag, p, tiles_per_seq, tm, xl_rows=8):
    t, d = x2.shape
    c = p['c_rwkv']
    n_tiles = t // tm
    full = lambda a: pl.BlockSpec(a.shape, lambda i: (0,) * a.ndim, pipeline_mode=pl.Buffered(1))
    tok = lambda w: pl.BlockSpec((tm, w), lambda i: (i, 0))
    weights = [p['gmix'], p['win'], p['wdx'], p['mu_rkv'], p['w0a0'], p['w2a2'], p['g2'], p['k_k'], p['k_a'], p['seg']]
    outs = pl.pallas_call(
        functools.partial(_proj_kernel, tiles_per_seq=tiles_per_seq, c_rwkv=c),
        grid=(n_tiles,),
        in_specs=[tok(d), tok(1)] + [full(w) for w in weights],
        out_specs=[tok(c)] * 10 + [pl.BlockSpec((1, xl_rows, d), lambda i: (i, 0, 0))],
        out_shape=[jax.ShapeDtypeStruct((t, c), F32)] * 10 + [jax.ShapeDtypeStruct((n_tiles, xl_rows, d), F32)],
        scratch_shapes=[pltpu.VMEM((8, d), F32), pltpu.VMEM((8, 3 * c), F32)],
        compiler_params=pltpu.CompilerParams(dimension_semantics=("arbitrary",), vmem_limit_bytes=V7X_VMEM_LIMIT),
        name="proj",
    )(x2, flag, *weights)
    return outs


GROUP_LANES = 256
GROUP_HEADS = GROUP_LANES // HEAD_DIM


def _rwkv_kernel(r_ref, lw_ref, k_ref, v_ref, kk_ref, b_ref, g_ref, s0_ref, rk_ref, lnw_ref, lnb_ref, seg_ref,
                 y_ref, sout_ref, s_scr):
    ci = pl.program_id(1)
    nb, L, c = r_ref.shape
    gw, gh, hd = GROUP_LANES, GROUP_HEADS, HEAD_DIM
    n_groups = c // gw
    md = RWKV_MODES

    lane_head = lax.broadcasted_iota(jnp.int32, (1, gw), 1) // hd
    head_masks = [lane_head == j for j in range(gh)]
    bd_state = (lax.broadcasted_iota(jnp.int32, (gw, gw), 0) // hd) == (lax.broadcasted_iota(jnp.int32, (gw, gw), 1) // hd)
    bd_time = (lax.broadcasted_iota(jnp.int32, (gh * L, gh * L), 0) // L) == (lax.broadcasted_iota(jnp.int32, (gh * L, gh * L), 1) // L)
    t_row = lax.broadcasted_iota(jnp.int32, (L, gh * L), 0)
    t_col = lax.broadcasted_iota(jnp.int32, (L, gh * L), 1) % L
    strict4 = t_row > t_col
    incl4 = t_row >= t_col
    incl = lax.broadcasted_iota(jnp.int32, (L, L), 0) >= lax.broadcasted_iota(jnp.int32, (L, L), 1)

    def stack(x):
        return jnp.concatenate([jnp.where(m, x, jnp.zeros_like(x)) for m in head_masks], axis=0)

    def block_diag(n):
        tiled = jnp.concatenate([n] * gh, axis=0)
        return jnp.where(bd_time, tiled, jnp.zeros_like(tiled))

    @pl.when(ci == 0)
    def _():
        for bi in range(nb):
            for gi in range(n_groups):
                s_in = s0_ref[bi, gi * gh:(gi + 1) * gh].reshape(gw, hd)
                s_scr[bi, gi] = jnp.where(bd_state, jnp.concatenate([s_in] * gh, axis=1), 0.0)

    n_apply = max(1, int(math.log2(L)))
    seg = seg_ref[...]
    pre = []
    for bi in range(nb):
        lw = lw_ref[bi]
        cs = _dot(incl.astype(F32), lw, HIGHEST)
        cp = cs - lw
        cm = cs[L // 2 - 1:L // 2, :]
        c_last = cs[L - 1:L, :]
        r, k, v, kk, b = r_ref[bi], k_ref[bi], v_ref[bi], kk_ref[bi], b_ref[bi]
        e_dn = jnp.exp(cm - cs)
        e_l = jnp.exp(c_last - cs)
        pre.append(dict(v=v, rt=r * jnp.exp(cs - cm), kkt=kk * jnp.exp(cp - cm), bt=b * e_dn, kt=k * e_dn,
                        kg=kk * jnp.exp(cp), rg=r * jnp.exp(cs), bh=b * e_l, kh=k * e_l, g_last=jnp.exp(c_last),
                        rkk=r * k * rk_ref[...]))

    chains = [(bi, gi) for bi in range(nb) for gi in range(n_groups)]
    col = lambda bi, gi, name: pre[bi][name][:, gi * gw:(gi + 1) * gw]
    each = lambda fn: [fn(i, bi, gi) for i, (bi, gi) in enumerate(chains)]

    vg = each(lambda i, bi, gi: col(bi, gi, 'v'))
    v_st = each(lambda i, bi, gi: stack(_pre(vg[i], md['AV'])))
    a_all = each(lambda i, bi, gi: _mm(
        jnp.concatenate([col(bi, gi, 'kkt'), col(bi, gi, 'rt')], axis=0),
        jnp.concatenate([stack(_pre(col(bi, gi, 'bt'), md['A'])), stack(_pre(col(bi, gi, 'kt'), md['A']))], axis=0),
        NT, md['A']))
    p_ak = each(lambda i, bi, gi: jnp.where(strict4, a_all[i][:L, gh * L:], 0.0))
    p_rb = each(lambda i, bi, gi: jnp.where(incl4, a_all[i][L:, :gh * L], 0.0))
    p_rk = each(lambda i, bi, gi: jnp.where(incl4, a_all[i][L:, gh * L:], 0.0))
    eye4 = (t_row == t_col).astype(F32)
    nm = each(lambda i, bi, gi: -jnp.where(strict4, a_all[i][:L, :gh * L], 0.0))
    t_inv = [eye4 + n for n in nm]
    for it in range(n_apply - 1):
        lhs = nm if it == 0 else [jnp.concatenate([n, t], axis=0) for n, t in zip(nm, t_inv)]
        both = each(lambda i, bi, gi: _mm(lhs[i], block_diag(_pre(nm[i], md['SQ'])), NN, md['SQ']))
        if it > 0:
            t_inv = [t + bo[L:] for t, bo in zip(t_inv, both)]
        nm = [bo[:L] for bo in both]
    t_inv = each(lambda i, bi, gi: t_inv[i] + _mm(t_inv[i], block_diag(_pre(nm[i], md['SQ'])), NN, md['SQ']))
    av = each(lambda i, bi, gi: _mm(p_ak[i], v_st[i], NN, md['AV']))
    x = each(lambda i, bi, gi: _mm(
        t_inv[i], jnp.concatenate([stack(_pre(col(bi, gi, 'kg'), md['AP'])), stack(_pre(av[i], md['AP']))], axis=1),
        NN, md['AP']))
    w_m = [xi[:, :gw] for xi in x]
    u0 = [-xi[:, gw:] for xi in x]
    rbw = each(lambda i, bi, gi: _mm(
        p_rb[i], jnp.concatenate([stack(_pre(w_m[i], md['RB'])), stack(_pre(u0[i], md['RB']))], axis=1),
        NN, md['RB']))
    rkv = each(lambda i, bi, gi: _mm(p_rk[i], v_st[i], NN, md['RK']))
    s_old = each(lambda i, bi, gi: s_scr[bi, gi])
    ws = each(lambda i, bi, gi: _mm(
        jnp.concatenate([w_m[i], col(bi, gi, 'rg') - rbw[i][:, :gw]], axis=0), s_old[i], NT, md['WS']))
    u = each(lambda i, bi, gi: u0[i] - ws[i][:L])
    y = each(lambda i, bi, gi: ws[i][L:] + rbw[i][:, gw:] + rkv[i])
    upd = each(lambda i, bi, gi: _mm(
        jnp.concatenate([u[i], vg[i]], axis=0),
        jnp.concatenate([col(bi, gi, 'bh'), col(bi, gi, 'kh')], axis=0), TN, md['UP']))
    for i, (bi, gi) in enumerate(chains):
        s_scr[bi, gi] = s_old[i] * col(bi, gi, 'g_last') + jnp.where(bd_state, upd[i], 0.0)

    inv = 1.0 / hd
    n_ch = len(chains)
    sums = _dot_split(jnp.concatenate(y + each(lambda i, bi, gi: col(bi, gi, 'rkk')), axis=0), seg)
    mean = [sums[i * L:(i + 1) * L] * inv for i in range(n_ch)]
    bonus = [sums[(n_ch + i) * L:(n_ch + i + 1) * L] * vg[i] for i in range(n_ch)]
    yc = [y[i] - mean[i] for i in range(n_ch)]
    sq = _dot_split(jnp.concatenate([z * z for z in yc], axis=0), seg)
    var = [sq[i * L:(i + 1) * L] * inv for i in range(n_ch)]
    for i, (bi, gi) in enumerate(chains):
        sl = slice(gi * gw, (gi + 1) * gw)
        yn = yc[i] * lax.rsqrt(var[i] + GN_EPS) * lnw_ref[:, sl] + lnb_ref[:, sl]
        y_ref[bi, :, sl] = (yn + bonus[i]) * g_ref[bi, :, sl]

    @pl.when(ci == pl.num_programs(1) - 1)
    def _():
        for bi in range(nb):
            for gi in range(n_groups):
                bd = s_scr[bi, gi]
                folded = bd[:, 0:hd]
                for j in range(1, gh):
                    folded = folded + bd[:, j * hd:(j + 1) * hd]
                sout_ref[bi, gi * gh:(gi + 1) * gh] = folded.reshape(gh, hd, hd)


def _rwkv_call(vecs, s0, p, n_seq, seq_len, chunk, nb):
    c = p['c_rwkv']
    n_heads = c // HEAD_DIM
    n_chunks = seq_len // chunk
    assert n_seq % nb == 0 and seq_len % chunk == 0
    vecs = [z.reshape(n_seq, seq_len, c) for z in vecs]
    tok = pl.BlockSpec((nb, chunk, c), lambda bi, ci: (bi, ci, 0))
    st = pl.BlockSpec((nb, n_heads, HEAD_DIM, HEAD_DIM), lambda bi, ci: (bi, 0, 0, 0))
    rowvec = pl.BlockSpec((1, c), lambda bi, ci: (0, 0))
    seg = p['seg'][:GROUP_LANES, :GROUP_LANES]
    y, s_out = pl.pallas_call(
        _rwkv_kernel,
        grid=(n_seq // nb, n_chunks),
        in_specs=[tok] * 7 + [st, rowvec, rowvec, rowvec, pl.BlockSpec(seg.shape, lambda bi, ci: (0, 0))],
        out_specs=[tok, st],
        out_shape=[jax.ShapeDtypeStruct((n_seq, seq_len, c), F32),
                   jax.ShapeDtypeStruct((n_seq, n_heads, HEAD_DIM, HEAD_DIM), F32)],
        scratch_shapes=[pltpu.VMEM((nb, c // GROUP_LANES, GROUP_LANES, GROUP_LANES), F32)],
        compiler_params=pltpu.CompilerParams(dimension_semantics=("arbitrary", "arbitrary"),
                                             vmem_limit_bytes=V7X_VMEM_LIMIT),
        name="rwkv",
    )(*vecs, s0, p['r_k'], p['ln_w'], p['ln_b'], seg)
    return y.reshape(n_seq * seq_len, c), s_out


def _attn_prompt_kernel(q_ref, k_ref, v_ref, o_ref, m_scr, l_scr, acc_scr):
    s_len = q_ref.shape[0]
    band = ATT_BAND
    n_blk = s_len // band

    lane = lax.broadcasted_iota(jnp.int32, (1, LANES), 1)
    head0 = lane < HEAD_DIM
    qi = lax.broadcasted_iota(jnp.int32, (band, 2 * band), 0)
    kj = lax.broadcasted_iota(jnp.int32, (band, 2 * band), 1)
    in_band = (kj >= qi) & (kj <= qi + band)
    scale = HEAD_DIM ** -0.5
    ones = jnp.ones((2 * band, LANES), BF16)

    for ci, (window, dil) in enumerate(DILATED_CONFIGS):
        assert window // dil == band
        per_res = n_blk // dil

        def body(it, carry, ci=ci, dil=dil, per_res=per_res):
            blocks = []
            for j in range(ATT_UNROLL):
                i = it * ATT_UNROLL + j
                res = i // per_res
                blk = i % per_res
                start = res + blk * (band * dil)
                prev = jnp.maximum(start - band * dil, 0)
                rows = pl.ds(start, band, stride=dil) if dil > 1 else pl.ds(start, band)
                prows = pl.ds(prev, band, stride=dil) if dil > 1 else pl.ds(prev, band)
                blocks.append((blk, rows, prows))
            each = lambda fn: [fn(j, *blocks[j]) for j in range(ATT_UNROLL)]
            q = each(lambda j, blk, rows, prows: q_ref[rows, :] * scale)
            kc = each(lambda j, blk, rows, prows:
                      jnp.concatenate([k_ref[prows, :], k_ref[rows, :]], axis=0).astype(BF16))
            vc = each(lambda j, blk, rows, prows:
                      jnp.concatenate([v_ref[prows, :], v_ref[rows, :]], axis=0).astype(BF16))
            mask = each(lambda j, blk, rows, prows: in_band & (kj >= jnp.where(blk > 0, 0, band)))
            s0 = each(lambda j, blk, rows, prows: jnp.where(mask[j], lax.dot_general(
                jnp.where(head0, q[j], 0.0).astype(BF16), kc[j], NT, preferred_element_type=F32), NEG_INF))
            s1 = each(lambda j, blk, rows, prows: jnp.where(mask[j], lax.dot_general(
                jnp.where(head0, 0.0, q[j]).astype(BF16), kc[j], NT, preferred_element_type=F32), NEG_INF))
            m0 = [jnp.max(z, axis=-1, keepdims=True) for z in s0]
            m1 = [jnp.max(z, axis=-1, keepdims=True) for z in s1]
            p0 = [jnp.exp(z - m).astype(BF16) for z, m in zip(s0, m0)]
            p1 = [jnp.exp(z - m).astype(BF16) for z, m in zip(s1, m1)]
            pv0 = [_dot(p, jnp.where(head0, v, ones)) for p, v in zip(p0, vc)]
            pv1 = [_dot(p, jnp.where(head0, ones, v)) for p, v in zip(p1, vc)]
            for j, (blk, rows, prows) in enumerate(blocks):
                m_scr[ci, rows, :] = jnp.where(head0, m0[j], m1[j])
                acc_scr[ci, rows, :] = jnp.where(head0, pv0[j], pv1[j])
                l_scr[ci, rows, :] = pltpu.roll(jnp.where(head0, pv1[j], pv0[j]), HEAD_DIM, axis=1)
            return carry

        lax.fori_loop(0, n_blk // ATT_UNROLL, body, 0)

    rows_per = 256

    def merge(i, carry):
        rows = pl.ds(pl.multiple_of(i * rows_per, rows_per), rows_per)
        ms = [m_scr[ci, rows, :] for ci in range(len(DILATED_CONFIGS))]
        m_all = functools.reduce(jnp.maximum, ms)
        num = jnp.zeros((rows_per, LANES), F32)
        den = jnp.zeros((rows_per, LANES), F32)
        for ci, m_c in enumerate(ms):
            w_c = jnp.exp(m_c - m_all)
            num = num + w_c * acc_scr[ci, rows, :]
            den = den + w_c * l_scr[ci, rows, :]
        o_ref[rows, :] = num / den
        return carry

    lax.fori_loop(0, s_len // rows_per, merge, 0)


def _attn_prompt_call(q, k, v, n_seq, seq_len):
    c = q.shape[1]
    n_pairs = c // LANES
    blk = pl.BlockSpec((seq_len, LANES), lambda bi, hi: (bi, hi))
    return pl.pallas_call(
        _attn_prompt_kernel,
        grid=(n_seq, n_pairs),
        in_specs=[blk, blk, blk],
        out_specs=blk,
        out_shape=jax.ShapeDtypeStruct((n_seq * seq_len, c), F32),
        scratch_shapes=[pltpu.VMEM((len(DILATED_CONFIGS), seq_len, LANES), F32)] * 3,
        compiler_params=pltpu.CompilerParams(dimension_semantics=("arbitrary", "arbitrary"),
                                             vmem_limit_bytes=V7X_VMEM_LIMIT),
        name="attn_prompt",
    )(q, k, v)


def _attn_sample_kernel(q_ref, kn_ref, vn_ref, kc_ref, vc_ref, o_ref, *, n_new):
    hd = HEAD_DIM
    _, n_heads, _, n_buf = kc_ref.shape
    t_pad = kn_ref.shape[1]
    c = q_ref.shape[2]
    q = q_ref[0] * (hd ** -0.5)
    lane_head = lax.broadcasted_iota(jnp.int32, (1, c), 1) // hd
    qs = jnp.concatenate([jnp.where(lane_head == h, q, 0.0) for h in range(n_heads)], axis=0).astype(BF16)
    n_rows = n_heads * t_pad
    t_idx = lax.broadcasted_iota(jnp.int32, (n_rows, 1), 0) % t_pad

    def multiplicity(dist):
        mult = jnp.zeros(dist.shape, F32)
        for window, dil in DILATED_CONFIGS:
            hit = (dist >= 0) & (dist <= window) & (dist % dil == 0)
            mult = mult + jnp.where(hit, 1.0, 0.0)
        return mult

    jc = lax.broadcasted_iota(jnp.int32, (1, n_buf), 1)
    mult_c = multiplicity(n_buf + t_idx - jc)
    jn = lax.broadcasted_iota(jnp.int32, (1, t_pad), 1)
    mult_n = jnp.where(jn < n_new, multiplicity(t_idx - jn), 0.0)

    sc = jnp.concatenate(
        [_dot(qs[h * t_pad:(h + 1) * t_pad, h * hd:(h + 1) * hd], kc_ref[0, h].astype(BF16))
         for h in range(n_heads)], axis=0)
    sn = lax.dot_general(qs, kn_ref[0].astype(BF16), NT, preferred_element_type=F32)
    sc = jnp.where(mult_c > 0.0, sc, NEG_INF)
    sn = jnp.where(mult_n > 0.0, sn, NEG_INF)
    m = jnp.maximum(jnp.max(sc, axis=-1, keepdims=True), jnp.max(sn, axis=-1, keepdims=True))
    pc = (mult_c * jnp.exp(sc - m)).astype(BF16)
    pn = mult_n * jnp.exp(sn - m)
    inv_l = 1.0 / (jnp.sum(pc.astype(F32), axis=-1, keepdims=True) + jnp.sum(pn, axis=-1, keepdims=True))
    o_new = _dot(pn.astype(BF16), vn_ref[0].astype(BF16)) * inv_l
    out = jnp.zeros((t_pad, c), F32)
    for h in range(n_heads):
        out = out + jnp.where(lane_head == h, o_new[h * t_pad:(h + 1) * t_pad, :], 0.0)
    o_buf = [lax.dot_general(pc[h * t_pad:(h + 1) * t_pad, :], vc_ref[0, h].astype(BF16), NT,
                             preferred_element_type=F32) * inv_l[h * t_pad:(h + 1) * t_pad, :]
             for h in range(n_heads)]
    o_ref[0] = out + jnp.concatenate(o_buf, axis=1)


def _attn_sample_call(q, kn, vn, k_buf, v_buf, n_new):
    b, t_pad, c = q.shape
    _, n_buf, n_heads, hd = k_buf.shape
    k_t = jnp.transpose(k_buf, (0, 2, 3, 1))
    v_t = jnp.transpose(v_buf, (0, 2, 3, 1))
    new = pl.BlockSpec((1, t_pad, c), lambda bi: (bi, 0, 0))
    buf = pl.BlockSpec((1, n_heads, hd, n_buf), lambda bi: (bi, 0, 0, 0))
    return pl.pallas_call(
        functools.partial(_attn_sample_kernel, n_new=n_new),
        grid=(b,),
        in_specs=[new, new, new, buf, buf],
        out_specs=new,
        out_shape=jax.ShapeDtypeStruct((b, t_pad, c), F32),
        compiler_params=pltpu.CompilerParams(dimension_semantics=("arbitrary",), vmem_limit_bytes=V7X_VMEM_LIMIT),
        name="attn_sample",
    )(q, kn, vn, k_t, v_t)


def _route_rows(logits):
    lane = lax.broadcasted_iota(jnp.int32, logits.shape, 1)
    lane_f = lane.astype(F32)
    first = lambda hit: jnp.min(jnp.where(hit, lane_f, float(LANES)), axis=-1, keepdims=True)
    is_g = lane < N_GROUPS
    lg = jnp.where(is_g, logits, NEG_INF)
    g_max = jnp.max(lg, axis=-1, keepdims=True)
    g_idx = first(lg == g_max)
    g_w = 1.0 / jnp.sum(jnp.where(is_g, jnp.exp(lg - g_max), 0.0), axis=-1, keepdims=True)
    lo = N_GROUPS + EXPERTS_PER_GROUP * g_idx
    le = jnp.where((lane_f >= lo) & (lane_f < lo + EXPERTS_PER_GROUP), logits, NEG_INF)
    e1 = jnp.max(le, axis=-1, keepdims=True)
    i1 = first(le == e1)
    le2 = jnp.where(lane_f == i1, NEG_INF, le)
    e2 = jnp.max(le2, axis=-1, keepdims=True)
    i2 = first(le2 == e2)
    ex = jnp.exp(e2 - e1)
    gate1 = g_w / (1.0 + ex)
    gate2 = g_w * ex / (1.0 + ex)
    out = jnp.where(lane == 0, gate1, jnp.where(lane == 1, gate2, 0.0))
    out = jnp.where(lane == 2, i1 - N_GROUPS, jnp.where(lane == 3, i2 - N_GROUPS, out))
    return out


def _post_kernel(x_ref, yr_ref, ya_ref, wo_ref, gffn_ref, rw_hi_ref, rw_lo_ref, rb_ref,
                 h_o, hn_o, lg_o):
    c = yr_ref.shape[1]
    h = (x_ref[...] + _dot(yr_ref[...].astype(BF16), wo_ref[:c, :]) + _dot(ya_ref[...].astype(BF16), wo_ref[c:, :]))
    ms = jnp.mean(h * h, axis=-1, keepdims=True)
    hn = (h * lax.rsqrt(ms + NORM_EPS)) * gffn_ref[...]
    h_o[...] = h
    bits = pltpu.bitcast(hn.astype(BF16).astype(F32), jnp.uint32)
    half = hn.shape[1] // 2
    hn_o[...] = (bits[:, :half] >> 16) | (bits[:, half:] & jnp.uint32(0xFFFF0000))
    hi = hn.astype(BF16)
    lo = (hn - hi.astype(F32)).astype(BF16)
    logits = (_dot(hi, rw_hi_ref[...]) + _dot(hi, rw_lo_ref[...]) + _dot(lo, rw_hi_ref[...])) + rb_ref[...]
    lg_o[...] = _route_rows(logits)


def _post_call(x2, yr, ya, p, tm):
    t, d = x2.shape
    c = yr.shape[1]
    full = lambda a: pl.BlockSpec(a.shape, lambda i: (0,) * a.ndim)
    tok = lambda w: pl.BlockSpec((tm, w), lambda i: (i, 0))
    weights = [p['wout'], p['gffn'], p['rw_hi'], p['rw_lo'], p['rb']]
    return pl.pallas_call(
        _post_kernel,
        grid=(t // tm,),
        in_specs=[tok(d), tok(c), tok(c)] + [full(w) for w in weights],
        out_specs=[tok(d), tok(d // 2), tok(LANES)],
        out_shape=[jax.ShapeDtypeStruct((t, d), F32), jax.ShapeDtypeStruct((t, d // 2), jnp.uint32),
                   jax.ShapeDtypeStruct((t, LANES), F32)],
        compiler_params=pltpu.CompilerParams(dimension_semantics=("arbitrary",), vmem_limit_bytes=V7X_VMEM_LIMIT),
        name="post",
    )(x2, yr, ya, *weights)


def _expert_kernel(be_ref, nb_ref, xs_ref, w1_ref, w3_ref, w2_ref, y_ref, w1_s, w3_s, w2_s):
    i = pl.program_id(0)
    live = i < nb_ref[0]

    @pl.when(live & ((i == 0) | (be_ref[i] != be_ref[jnp.maximum(i - 1, 0)])))
    def _():
        w1_s[...] = w1_ref[...].astype(BF16)
        w3_s[...] = w3_ref[...].astype(BF16)
        w2_s[...] = w2_ref[...].astype(BF16)

    @pl.when(live)
    def _():
        packed = xs_ref[...]
        half = packed.shape[1]
        lo = pltpu.bitcast(packed << 16, F32).astype(BF16)
        hi = pltpu.bitcast(packed & jnp.uint32(0xFFFF0000), F32).astype(BF16)
        h1 = _dot(lo, w1_s[:half, :]) + _dot(hi, w1_s[half:, :])
        h3 = _dot(lo, w3_s[:half, :]) + _dot(hi, w3_s[half:, :])
        act = (h1 * _sigmoid(h1)) * h3
        y_ref[...] = _dot(act.astype(BF16), w2_s[...])

    @pl.when(jnp.logical_not(live))
    def _():
        y_ref[...] = jnp.zeros_like(y_ref)


def _expert_call(blk_exp, n_used, xs, p, bm):
    n_slots = xs.shape[0]
    _, d, de = p['w1'].shape
    grid_spec = pltpu.PrefetchScalarGridSpec(
        num_scalar_prefetch=2,
        grid=(n_slots // bm,),
        in_specs=[pl.BlockSpec((bm, d // 2), lambda i, be, nb: (i, 0)),
                  pl.BlockSpec((None, d, de), lambda i, be, nb: (be[i], 0, 0)),
                  pl.BlockSpec((None, d, de), lambda i, be, nb: (be[i], 0, 0)),
                  pl.BlockSpec((None, de, d), lambda i, be, nb: (be[i], 0, 0))],
        out_specs=pl.BlockSpec((bm, d), lambda i, be, nb: (i, 0)),
        scratch_shapes=[pltpu.VMEM((d, de), BF16), pltpu.VMEM((d, de), BF16), pltpu.VMEM((de, d), BF16)],
    )
    return pl.pallas_call(
        _expert_kernel,
        grid_spec=grid_spec,
        out_shape=jax.ShapeDtypeStruct((n_slots, d), F32),
        compiler_params=pltpu.CompilerParams(dimension_semantics=("arbitrary",), vmem_limit_bytes=V7X_VMEM_LIMIT),
        name="experts",
    )(blk_exp, n_used, xs, p['w1'], p['w3'], p['w2'])


def _final_kernel(h_ref, route_ref, y1_ref, y2_ref, gfin_ref, o_ref):
    route = route_ref[...]
    h = h_ref[...] + (route[:, 0:1] * y1_ref[...] + route[:, 1:2] * y2_ref[...])
    ms = jnp.mean(h * h, axis=-1, keepdims=True)
    o_ref[...] = (h * lax.rsqrt(ms + NORM_EPS)) * gfin_ref[...]


def _final_call(h, route, y12, gfin, tm):
    t, d = h.shape
    tok = pl.BlockSpec((tm, d), lambda i: (i, 0))
    routed = lambda a: pl.BlockSpec((None, tm, d), lambda i: (a, i, 0))
    return pl.pallas_call(
        _final_kernel,
        grid=(t // tm,),
        in_specs=[tok, pl.BlockSpec((tm, LANES), lambda i: (i, 0)), routed(0), routed(1),
                  pl.BlockSpec((1, d), lambda i: (0, 0))],
        out_specs=tok,
        out_shape=jax.ShapeDtypeStruct((t, d), F32),
        compiler_params=pltpu.CompilerParams(dimension_semantics=("arbitrary",), vmem_limit_bytes=V7X_VMEM_LIMIT),
        name="final",
    )(h, route, y12, y12, gfin)


def _gather_rows(table, idx):
    info = plsc.get_sparse_core_info()
    nc, ns = info.num_cores, info.num_subcores
    b, d = idx.shape[0], table.shape[1]
    chunk = min(SC_INDEX_LIMIT, SC_GATHER_BYTES // (d * table.dtype.itemsize))
    assert b % (nc * ns * chunk * 2) == 0, "rows must split evenly into chunk pairs per subcore"
    per_w = b // (nc * ns)
    n_chunks = per_w // chunk
    mesh = plsc.VectorSubcoreMesh(core_axis_name="c", subcore_axis_name="s")

    @functools.partial(
        pl.kernel, mesh=mesh, out_type=jax.ShapeDtypeStruct((b, d), table.dtype),
        scratch_types=[pltpu.VMEM((per_w,), jnp.int32), pltpu.VMEM((2, chunk, d), table.dtype),
                       pltpu.SemaphoreType.DMA((2,)), pltpu.SemaphoreType.DMA((2,))])
    def gather(table_hbm, idx_hbm, out_hbm, idx_v, rows_v, fetch_sem, put_sem):
        base = (lax.axis_index("s") * nc + lax.axis_index("c")) * per_w
        pltpu.sync_copy(idx_hbm.at[pl.ds(base, per_w)], idx_v)

        def fetch(c, slot):
            off = pl.multiple_of(c * chunk, chunk)
            return pltpu.make_async_copy(table_hbm.at[idx_v.at[pl.ds(off, chunk)]], rows_v.at[slot],
                                         fetch_sem.at[slot])

        def put(c, slot):
            off = pl.multiple_of(c * chunk, chunk)
            return pltpu.make_async_copy(rows_v.at[slot], out_hbm.at[pl.ds(base + off, chunk)], put_sem.at[slot])

        fetch(0, 0).start()

        @pl.loop(0, n_chunks, step=2)
        def _(c):
            @pl.when(c > 0)
            def _():
                put(c - 1, 1).wait()
            fetch(c + 1, 1).start()
            fetch(c, 0).wait()
            put(c, 0).start()
            fetch(c + 1, 1).wait()
            put(c, 0).wait()

            @pl.when(c + 2 < n_chunks)
            def _():
                fetch(c + 2, 0).start()
            put(c + 1, 1).start()

        put(n_chunks - 1, 1).wait()

    return gather(table, idx)


def _route(route, bm):
    n = route.shape[0]
    eid = route[:, 2:4].astype(jnp.int32).reshape(-1)
    m = eid.shape[0]
    experts = jnp.arange(N_EXPERTS + 1, dtype=jnp.int32)
    e_sorted, order = lax.sort_key_val(eid, jnp.arange(m, dtype=jnp.int32))
    below = jnp.sum((eid[:, None] < experts[None, :]).astype(jnp.int32), axis=0)
    starts, counts = below[:-1], below[1:] - below[:-1]
    padded = (counts + bm - 1) // bm * bm
    p_ends = jnp.cumsum(padded)
    p_starts = p_ends - padded
    shift = p_starts - starts
    dest_sorted = jnp.arange(m, dtype=jnp.int32) + jnp.sum(
        jnp.where(e_sorted[:, None] == experts[None, :-1], shift[None, :], 0), axis=1)
    _, dest = lax.sort_key_val(order, dest_sorted)
    n_blocks = _round_up(-(-m // bm) + N_EXPERTS, SLOT_BLOCK_MULTIPLE)
    blk_start = jnp.arange(n_blocks, dtype=jnp.int32) * bm
    blk_exp = jnp.minimum(jnp.sum((p_ends[None, :] <= blk_start[:, None]).astype(jnp.int32), axis=1), N_EXPERTS - 1)
    pick = lambda tbl: jnp.sum(jnp.where(blk_exp[:, None] == experts[None, :-1], tbl[None, :], 0), axis=1)
    pos = (blk_start - pick(p_starts))[:, None] + jnp.arange(bm, dtype=jnp.int32)[None, :]
    valid = pos < pick(counts)[:, None]
    src = jnp.where(valid, pick(starts)[:, None] + pos, 0).reshape(-1)
    src_assign = order.at[src].get(mode='promise_in_bounds')
    valid = valid.reshape(-1)
    slot_tok = jnp.where(valid, src_assign // 2, jnp.arange(valid.shape[0], dtype=jnp.int32) % n)
    n_used = (p_ends[-1] // bm).astype(jnp.int32).reshape(1)
    return slot_tok, dest.reshape(n, 2), blk_exp.astype(jnp.int32), n_used


def _moe_and_final(x2, yr, ya, p, tm):
    h, hn, route = _post_call(x2, yr, ya, p, tm)
    slot_tok, dest, blk_exp, n_used = _route(route, EXPERT_TILE)
    xs = _gather_rows(hn, slot_tok)
    yb = _expert_call(blk_exp, n_used, xs, p, EXPERT_TILE)
    y12 = _gather_rows(yb, dest.T.reshape(-1)).reshape(2, h.shape[0], h.shape[1])
    return _final_call(h, route, y12, p['gfin'], tm)


def _prep_params(layer, norm_mix_g, w_in, rwkv_mu_rkv, rwkv_mu_wag, rwkv_w0, rwkv_w1, rwkv_w2, rwkv_a0, rwkv_a1,
                 rwkv_a2, rwkv_g1, rwkv_g2, rwkv_k_k, rwkv_k_a, rwkv_r_k, rwkv_ln_w, rwkv_ln_b, w_out, norm_ffn_g,
                 router_group_w, router_group_b, router_expert_w, router_expert_b, expert_w1, expert_w3, expert_w2,
                 norm_final_g):
    d = w_in.shape[1]
    c = rwkv_w0.shape[1]
    row = lambda a: a.reshape(1, -1).astype(F32)
    lowrank = jnp.concatenate([rwkv_w1[layer], rwkv_a1[layer], rwkv_g1[layer]], axis=1)
    mx = rwkv_mu_wag[layer]
    r_w = rwkv_w1.shape[2]
    r_a = rwkv_a1.shape[2]
    r_g = rwkv_g1.shape[2]
    assert r_w + r_a == LANES and r_g == LANES
    mx_cols = jnp.concatenate([jnp.broadcast_to(mx[0][:, None], (d, r_w)), jnp.broadcast_to(mx[1][:, None], (d, r_a)),
                               jnp.broadcast_to(mx[2][:, None], (d, r_g))], axis=1)
    w2a2 = jnp.zeros((LANES, 2 * c), F32)
    w2a2 = w2a2.at[:r_w, :c].set(rwkv_w2[layer]).at[r_w:, c:].set(rwkv_a2[layer])
    head = jnp.arange(c) // HEAD_DIM
    rw = jnp.zeros((d, LANES), F32)
    rw = rw.at[:, :N_GROUPS].set(router_group_w[layer]).at[:, N_GROUPS:N_GROUPS + N_EXPERTS].set(router_expert_w[layer])
    rw_hi = rw.astype(BF16)
    rb = jnp.zeros((1, LANES), F32)
    rb = rb.at[0, :N_GROUPS].set(router_group_b[layer]).at[0, N_GROUPS:N_GROUPS + N_EXPERTS].set(router_expert_b[layer])
    return {
        'c_rwkv': c,
        'gmix': row(norm_mix_g[layer]),
        'win': jnp.concatenate([w_in[layer], lowrank], axis=1).astype(BF16),
        'wdx': (mx_cols * lowrank).astype(BF16),
        'mu_rkv': rwkv_mu_rkv[layer],
        'w0a0': jnp.concatenate([row(rwkv_w0[layer]), row(rwkv_a0[layer])], axis=1),
        'w2a2': w2a2.astype(BF16),
        'g2': rwkv_g2[layer].astype(BF16),
        'k_k': row(rwkv_k_k[layer]),
        'k_a': row(rwkv_k_a[layer]),
        'seg': (head[:, None] == head[None, :]).astype(BF16),
        'r_k': row(rwkv_r_k[layer]),
        'ln_w': row(rwkv_ln_w[layer]),
        'ln_b': row(rwkv_ln_b[layer]),
        'wout': w_out[layer].astype(BF16),
        'gffn': row(norm_ffn_g[layer]),
        'rw_hi': rw_hi,
        'rw_lo': (rw - rw_hi.astype(F32)).astype(BF16),
        'rb': rb,
        'w1': expert_w1[layer],
        'w3': expert_w3[layer],
        'w2': expert_w2[layer],
        'gfin': row(norm_final_g),
    }


def _prompt_group(x, p):
    b, s, d = x.shape
    c = p['c_rwkv']
    x2 = x.reshape(b * s, d)
    tm = PROJ_TILE
    flag = jnp.zeros((b * s, 1), F32)
    outs = _proj_call(x2, flag, p, s // tm, tm)
    r, lw, k, v, kk, bb, g, qa, ka, va, xl = outs
    s0 = jnp.zeros((b, c // HEAD_DIM, HEAD_DIM, HEAD_DIM), F32)
    yr, s_new = _rwkv_call((r, lw, k, v, kk, bb, g), s0, p, b, s, RWKV_CHUNK, 4)
    ya = _attn_prompt_call(qa, ka, va, b, s)
    y = _moe_and_final(x2, yr, ya, p, tm)
    shift = xl.reshape(b, s // tm, 8, d)[:, -1, 7, :]
    keep = min(max(w for w, _ in DILATED_CONFIGS), s)
    k_keep = ka.reshape(b, s, c // HEAD_DIM, HEAD_DIM)[:, s - keep:]
    v_keep = va.reshape(b, s, c // HEAD_DIM, HEAD_DIM)[:, s - keep:]
    return y.reshape(b, s, d), s_new, shift, k_keep, v_keep


def _sample_group(x, shift0, s0, k_buf, v_buf, p):
    b, t, d = x.shape
    c = p['c_rwkv']
    n_heads = c // HEAD_DIM
    t_pad = 8
    xc = jnp.concatenate([shift0[:, None, :], x, jnp.zeros((b, t_pad - 1 - t, d), x.dtype)], axis=1)
    flag = jnp.zeros((b, t_pad, 1), F32).at[:, 0].set(1.0)
    outs = _proj_call(xc.reshape(b * t_pad, d), flag.reshape(b * t_pad, 1), p, 1, b * t_pad, xl_rows=b * t_pad)
    xl = outs[10]
    live = (jnp.arange(t_pad) < t)[None, :, None]
    shifted = [jnp.where(live, jnp.roll(o.reshape(b, t_pad, c), -1, axis=1), 0.0) for o in outs[:10]]
    r, lw, k, v, kk, bb, g, qa, ka, va = shifted
    flat = lambda z: z.reshape(b * t_pad, c)
    yr, s_new = _rwkv_call(tuple(flat(z) for z in (r, lw, k, v, kk, bb, g)), s0, p, b, t_pad, t_pad, 8)
    ya = _attn_sample_call(qa, ka, va, k_buf, v_buf, t)
    x_pad = jnp.concatenate([x, jnp.zeros((b, t_pad - t, d), x.dtype)], axis=1).reshape(b * t_pad, d)
    y = _moe_and_final(x_pad, yr, flat(ya), p, b * t_pad // 2)
    y = y.reshape(b, t_pad, d)[:, :t]
    shift = xl.reshape(b, t_pad, d)[:, t]
    return (y, s_new, shift, ka[:, :t].reshape(b, t, n_heads, HEAD_DIM), va[:, :t].reshape(b, t, n_heads, HEAD_DIM))


def kernel(x_prompt, x_sample, state_rwkv, state_shift, cache_att_k, cache_att_v, norm_mix_g, w_in, rwkv_mu_rkv, rwkv_mu_wag, rwkv_w0, rwkv_w1, rwkv_w2, rwkv_a0, rwkv_a1, rwkv_a2, rwkv_g1, rwkv_g2, rwkv_k_k, rwkv_k_a, rwkv_r_k, rwkv_ln_w, rwkv_ln_b, w_out, norm_ffn_g, router_group_w, router_group_b, router_expert_w, router_expert_b, expert_w1, expert_w3, expert_w2, norm_final_g):
    assert w_in.shape[0] == 1, "single-layer trunk"
    p = _prep_params(0, norm_mix_g, w_in, rwkv_mu_rkv, rwkv_mu_wag, rwkv_w0, rwkv_w1, rwkv_w2, rwkv_a0, rwkv_a1,
                     rwkv_a2, rwkv_g1, rwkv_g2, rwkv_k_k, rwkv_k_a, rwkv_r_k, rwkv_ln_w, rwkv_ln_b, w_out,
                     norm_ffn_g, router_group_w, router_group_b, router_expert_w, router_expert_b, expert_w1,
                     expert_w3, expert_w2, norm_final_g)
    y_p, rw_p, sh_p, kc_p, vc_p = _prompt_group(x_prompt, p)
    y_s, rw_s, sh_s, kc_s, vc_s = _sample_group(x_sample, state_shift[0], state_rwkv[0], cache_att_k[0],
                                                cache_att_v[0], p)
    return (y_p, y_s, rw_p[None], sh_p[None], kc_p[None], vc_p[None], rw_s[None], sh_s[None], kc_s[None], vc_s[None])
```

```python
import functools
import math

import jax
import jax.numpy as jnp
from jax import lax
from jax.experimental import pallas as pl
from jax.experimental.pallas import tpu as pltpu
from jax.experimental.pallas import tpu_sc as plsc

F32 = jnp.float32
BF16 = jnp.bfloat16

HEAD_DIM = 64
GN_EPS = 64e-5
NORM_EPS = 1e-6
DILATED_CONFIGS = ((128, 1), (512, 4), (2048, 16))
N_GROUPS = 4
EXPERTS_PER_GROUP = 8
N_EXPERTS = N_GROUPS * EXPERTS_PER_GROUP
NEG_INF = -1e30

V7X_VMEM_LIMIT = 56 * 1024 * 1024
LANES = 128

PROJ_TILE = 512
RWKV_CHUNK = 64
ATT_BAND = 128
EXPERT_TILE = 512
ATT_UNROLL = 4
SC_GATHER_BYTES = 128 * 1024
SC_INDEX_LIMIT = 128
SLOT_MULTIPLE = 4096
SAMPLE_EXPERT_TILE = 128

HIGHEST = lax.Precision.HIGHEST
NN = (((1,), (0,)), ((), ()))
NT = (((1,), (1,)), ((), ()))
TN = (((0,), (0,)), ((), ()))


def _dot(a, b, precision=None):
    return jnp.dot(a, b, preferred_element_type=F32, precision=precision)


def _dot_split(a, b_bf16):
    hi = a.astype(BF16)
    lo = (a - hi.astype(F32)).astype(BF16)
    return _dot(hi, b_bf16) + _dot(lo, b_bf16)


def _mm(a, b, dims, mode):
    if mode == 'f32':
        return lax.dot_general(a, b, dims, precision=HIGHEST, preferred_element_type=F32)
    a_hi = a.astype(BF16)
    b_hi = b.astype(BF16)
    out = lax.dot_general(a_hi, b_hi, dims, preferred_element_type=F32)
    if mode == 'x3':
        a_lo = (a - a_hi.astype(F32)).astype(BF16)
        b_lo = (b - b_hi.astype(F32)).astype(BF16)
        out = out + lax.dot_general(a_hi, b_lo, dims, preferred_element_type=F32)
        out = out + lax.dot_general(a_lo, b_hi, dims, preferred_element_type=F32)
    return out


RWKV_MODES = dict(A='bf16', AV='bf16', SQ='bf16', AP='bf16', RB='bf16', RK='bf16', WS='bf16', UP='bf16')


def _pre(x, mode):
    return x.astype(BF16) if mode == 'bf16' else x


def _round_up(x, k):
    return -(-x // k) * k


def _sigmoid(z):
    return 1.0 / (1.0 + jnp.exp(-z))


def _proj_kernel(x_ref, flag_ref, gmix_ref, win_ref, wdx_ref, mu_ref, w0a0_ref, w2a2_ref, g2_ref,
                 kk_ref, ka_ref, seg_ref,
                 r_o, lw_o, k_o, v_o, kkn_o, b_o, g_o, qa_o, kat_o, vat_o, xl_o,
                 xn_carry, pj_carry, *, tiles_per_seq, c_rwkv):
    i = pl.program_id(0)

    @pl.when(i % tiles_per_seq == 0)
    def _():
        xn_carry[...] = jnp.zeros_like(xn_carry)
        pj_carry[...] = jnp.zeros_like(pj_carry)

    c = c_rwkv
    x = x_ref[...]
    tm = x.shape[0]
    ms = jnp.mean(x * x, axis=-1, keepdims=True)
    xn = (x * lax.rsqrt(ms + NORM_EPS)) * gmix_ref[...]
    xn = jnp.where(flag_ref[...] > 0.0, x, xn)
    row = lax.broadcasted_iota(jnp.int32, (tm, 1), 0)
    xn_prev = jnp.where(row == 0, xn_carry[7:8, :], pltpu.roll(xn, 1, axis=0))
    dx = xn_prev - xn

    proj = _dot(xn.astype(BF16), win_ref[...])
    cur = proj[:, :3 * c]
    prev = jnp.where(row == 0, pj_carry[7:8, :], pltpu.roll(cur, 1, axis=0))
    xn_carry[...] = xn[tm - 8:, :]
    pj_carry[...] = cur[tm - 8:, :]
    xl_rows = xl_o.shape[1]
    xl_o[0] = xn[tm - xl_rows:, :]

    mu = mu_ref[...]
    r = cur[:, :c] + mu[0:1] * (prev[:, :c] - cur[:, :c])
    k = cur[:, c:2 * c] + mu[1:2] * (prev[:, c:2 * c] - cur[:, c:2 * c])
    v = cur[:, 2 * c:3 * c] + mu[2:3] * (prev[:, 2 * c:3 * c] - cur[:, 2 * c:3 * c])

    lr = proj[:, 6 * c:] + _dot(dx.astype(BF16), wdx_ref[...])
    lane = lax.broadcasted_iota(jnp.int32, (1, LANES), 1)
    wa_in = jnp.where(lane < 64, jnp.tanh(lr[:, :LANES]), lr[:, :LANES])
    wa = _dot(wa_in.astype(BF16), w2a2_ref[...]) + w0a0_ref[...]
    z = -wa[:, :c]
    softplus = jnp.maximum(z, 0.0) + jnp.log1p(jnp.exp(-jnp.abs(z)))
    lw = -jnp.exp(-softplus - 0.5)
    a = _sigmoid(wa[:, c:])
    g = _dot(_sigmoid(lr[:, LANES:]).astype(BF16), g2_ref[...])

    kk = k * kk_ref[...]
    ss = _dot_split(kk * kk, seg_ref[...])
    kk = kk * lax.rsqrt(jnp.maximum(ss, 1e-24))

    r_o[...] = r
    lw_o[...] = lw
    k_o[...] = k * (1.0 + (a - 1.0) * ka_ref[...])
    v_o[...] = v
    kkn_o[...] = kk
    b_o[...] = kk * a
    g_o[...] = g
    qa_o[...] = proj[:, 3 * c:4 * c]
    kat_o[...] = proj[:, 4 * c:5 * c]
    vat_o[...] = proj[:, 5 * c:6 * c]


def _proj_call(x2, flag, p, tiles_per_seq, tm, xl_rows=8):
    t, d = x2.shape
    c = p['c_rwkv']
    n_tiles = t // tm
    full = lambda a: pl.BlockSpec(a.shape, lambda i: (0,) * a.ndim, pipeline_mode=pl.Buffered(1))
    tok = lambda w: pl.BlockSpec((tm, w), lambda i: (i, 0))
    weights = [p['gmix'], p['win'], p['wdx'], p['mu_rkv'], p['w0a0'], p['w2a2'], p['g2'], p['k_k'], p['k_a'], p['seg']]
    outs = pl.pallas_call(
        functools.partial(_proj_kernel, tiles_per_seq=tiles_per_seq, c_rwkv=c),
        grid=(n_tiles,),
        in_specs=[tok(d), tok(1)] + [full(w) for w in weights],
        out_specs=[tok(c)] * 10 + [pl.BlockSpec((1, xl_rows, d), lambda i: (i, 0, 0))],
        out_shape=[jax.ShapeDtypeStruct((t, c), F32)] * 10 + [jax.ShapeDtypeStruct((n_tiles, xl_rows, d), F32)],
        scratch_shapes=[pltpu.VMEM((8, d), F32), pltpu.VMEM((8, 3 * c), F32)],
        compiler_params=pltpu.CompilerParams(dimension_semantics=("arbitrary",), vmem_limit_bytes=V7X_VMEM_LIMIT),
        name="proj",
    )(x2, flag, *weights)
    return outs


GROUP_LANES = 256
GROUP_HEADS = GROUP_LANES // HEAD_DIM


def _rwkv_kernel(r_ref, lw_ref, k_ref, v_ref, kk_ref, b_ref, g_ref, s0_ref, rk_ref, lnw_ref, lnb_ref, seg_ref,
                 y_ref, sout_ref, s_scr):
    ci = pl.program_id(1)
    nb, L, c = r_ref.shape
    gw, gh, hd = GROUP_LANES, GROUP_HEADS, HEAD_DIM
    n_groups = c // gw
    md = RWKV_MODES

    lane_head = lax.broadcasted_iota(jnp.int32, (1, gw), 1) // hd
    head_masks = [lane_head == j for j in range(gh)]
    bd_state = (lax.broadcasted_iota(jnp.int32, (gw, gw), 0) // hd) == (lax.broadcasted_iota(jnp.int32, (gw, gw), 1) // hd)
    bd_time = (lax.broadcasted_iota(jnp.int32, (gh * L, gh * L), 0) // L) == (lax.broadcasted_iota(jnp.int32, (gh * L, gh * L), 1) // L)
    t_row = lax.broadcasted_iota(jnp.int32, (L, gh * L), 0)
    t_col = lax.broadcasted_iota(jnp.int32, (L, gh * L), 1) % L
    strict4 = t_row > t_col
    incl4 = t_row >= t_col
    incl = lax.broadcasted_iota(jnp.int32, (L, L), 0) >= lax.broadcasted_iota(jnp.int32, (L, L), 1)

    def stack(x):
        return jnp.concatenate([jnp.where(m, x, jnp.zeros_like(x)) for m in head_masks], axis=0)

    def block_diag(n):
        tiled = jnp.concatenate([n] * gh, axis=0)
        return jnp.where(bd_time, tiled, jnp.zeros_like(tiled))

    @pl.when(ci == 0)
    def _():
        for bi in range(nb):
            for gi in range(n_groups):
                s_in = s0_ref[bi, gi * gh:(gi + 1) * gh].reshape(gw, hd)
                s_scr[bi, gi] = jnp.where(bd_state, jnp.concatenate([s_in] * gh, axis=1), 0.0)

    n_apply = max(1, int(math.log2(L)))
    seg = seg_ref[...]
    pre = []
    for bi in range(nb):
        lw = lw_ref[bi]
        lw_hi = lw.astype(BF16)
        lw_r = lw - lw_hi.astype(F32)
        lw_mid = lw_r.astype(BF16)
        lw_lo = (lw_r - lw_mid.astype(F32)).astype(BF16)
        cs3 = _dot(incl.astype(BF16), jnp.concatenate([lw_hi, lw_mid, lw_lo], axis=1))
        cs = cs3[:, :c] + cs3[:, c:2 * c] + cs3[:, 2 * c:]
        cp = cs - lw
        cm = cs[L // 2 - 1:L // 2, :]
        c_last = cs[L - 1:L, :]
        r, k, v, kk, b = r_ref[bi], k_ref[bi], v_ref[bi], kk_ref[bi], b_ref[bi]
        e_dn = jnp.exp(cm - cs)
        e_l = jnp.exp(c_last - cs)
        pre.append(dict(v=v, rt=r * jnp.exp(cs - cm), kkt=kk * jnp.exp(cp - cm), bt=b * e_dn, kt=k * e_dn,
                        kg=kk * jnp.exp(cp), rg=r * jnp.exp(cs), bh=b * e_l, kh=k * e_l, g_last=jnp.exp(c_last),
                        rkk=r * k * rk_ref[...]))

    chains = [(bi, gi) for bi in range(nb) for gi in range(n_groups)]
    col = lambda bi, gi, name: pre[bi][name][:, gi * gw:(gi + 1) * gw]
    each = lambda fn: [fn(i, bi, gi) for i, (bi, gi) in enumerate(chains)]

    vg = each(lambda i, bi, gi: col(bi, gi, 'v'))
    v_st = each(lambda i, bi, gi: stack(_pre(vg[i], md['AV'])))
    a_all = each(lambda i, bi, gi: _mm(
        jnp.concatenate([col(bi, gi, 'kkt'), col(bi, gi, 'rt')], axis=0),
        jnp.concatenate([stack(_pre(col(bi, gi, 'bt'), md['A'])), stack(_pre(col(bi, gi, 'kt'), md['A']))], axis=0),
        NT, md['A']))
    p_ak = each(lambda i, bi, gi: jnp.where(strict4, a_all[i][:L, gh * L:], 0.0))
    p_rb = each(lambda i, bi, gi: jnp.where(incl4, a_all[i][L:, :gh * L], 0.0))
    p_rk = each(lambda i, bi, gi: jnp.where(incl4, a_all[i][L:, gh * L:], 0.0))
    eye4 = (t_row == t_col).astype(F32)
    nm = each(lambda i, bi, gi: -jnp.where(strict4, a_all[i][:L, :gh * L], 0.0))
    t_inv = [eye4 + n for n in nm]
    for it in range(n_apply - 1):
        lhs = nm if it == 0 else [jnp.concatenate([n, t], axis=0) for n, t in zip(nm, t_inv)]
        both = each(lambda i, bi, gi: _mm(lhs[i], block_diag(_pre(nm[i], md['SQ'])), NN, md['SQ']))
        if it > 0:
            t_inv = [t + bo[L:] for t, bo in zip(t_inv, both)]
        nm = [bo[:L] for bo in both]
    t_inv = each(lambda i, bi, gi: t_inv[i] + _mm(t_inv[i], block_diag(_pre(nm[i], md['SQ'])), NN, md['SQ']))
    av = each(lambda i, bi, gi: _mm(p_ak[i], v_st[i], NN, md['AV']))
    x = each(lambda i, bi, gi: _mm(
        t_inv[i], jnp.concatenate([stack(_pre(col(bi, gi, 'kg'), md['AP'])), stack(_pre(av[i], md['AP']))], axis=1),
        NN, md['AP']))
    w_m = [xi[:, :gw] for xi in x]
    u0 = [-xi[:, gw:] for xi in x]
    rbw = each(lambda i, bi, gi: _mm(
        p_rb[i], jnp.concatenate([stack(_pre(w_m[i], md['RB'])), stack(_pre(u0[i], md['RB']))], axis=1),
        NN, md['RB']))
    rkv = each(lambda i, bi, gi: _mm(p_rk[i], v_st[i], NN, md['RK']))
    s_old = each(lambda i, bi, gi: s_scr[bi, gi])
    ws = each(lambda i, bi, gi: _mm(
        jnp.concatenate([w_m[i], col(bi, gi, 'rg') - rbw[i][:, :gw]], axis=0), s_old[i], NT, md['WS']))
    u = each(lambda i, bi, gi: u0[i] - ws[i][:L])
    y = each(lambda i, bi, gi: ws[i][L:] + rbw[i][:, gw:] + rkv[i])
    upd = each(lambda i, bi, gi: _mm(
        jnp.concatenate([u[i], vg[i]], axis=0),
        jnp.concatenate([col(bi, gi, 'bh'), col(bi, gi, 'kh')], axis=0), TN, md['UP']))
    for i, (bi, gi) in enumerate(chains):
        s_scr[bi, gi] = s_old[i] * col(bi, gi, 'g_last') + jnp.where(bd_state, upd[i], 0.0)

    inv = 1.0 / hd
    n_ch = len(chains)
    sums = _dot_split(jnp.concatenate(y + each(lambda i, bi, gi: col(bi, gi, 'rkk')), axis=0), seg)
    mean = [sums[i * L:(i + 1) * L] * inv for i in range(n_ch)]
    bonus = [sums[(n_ch + i) * L:(n_ch + i + 1) * L] * vg[i] for i in range(n_ch)]
    yc = [y[i] - mean[i] for i in range(n_ch)]
    sq = _dot_split(jnp.concatenate([z * z for z in yc], axis=0), seg)
    var = [sq[i * L:(i + 1) * L] * inv for i in range(n_ch)]
    for i, (bi, gi) in enumerate(chains):
        sl = slice(gi * gw, (gi + 1) * gw)
        yn = yc[i] * lax.rsqrt(var[i] + GN_EPS) * lnw_ref[:, sl] + lnb_ref[:, sl]
        y_ref[bi, :, sl] = (yn + bonus[i]) * g_ref[bi, :, sl]

    @pl.when(ci == pl.num_programs(1) - 1)
    def _():
        for bi in range(nb):
            for gi in range(n_groups):
                bd = s_scr[bi, gi]
                folded = bd[:, 0:hd]
                for j in range(1, gh):
                    folded = folded + bd[:, j * hd:(j + 1) * hd]
                sout_ref[bi, gi * gh:(gi + 1) * gh] = folded.reshape(gh, hd, hd)


def _rwkv_call(vecs, s0, p, n_seq, seq_len, chunk, nb):
    c = p['c_rwkv']
    n_heads = c // HEAD_DIM
    n_chunks = seq_len // chunk
    assert n_seq % nb == 0 and seq_len % chunk == 0
    vecs = [z.reshape(n_seq, seq_len, c) for z in vecs]
    tok = pl.BlockSpec((nb, chunk, c), lambda bi, ci: (bi, ci, 0))
    st = pl.BlockSpec((nb, n_heads, HEAD_DIM, HEAD_DIM), lambda bi, ci: (bi, 0, 0, 0))
    rowvec = pl.BlockSpec((1, c), lambda bi, ci: (0, 0))
    seg = p['seg'][:GROUP_LANES, :GROUP_LANES]
    y, s_out = pl.pallas_call(
        _rwkv_kernel,
        grid=(n_seq // nb, n_chunks),
        in_specs=[tok] * 7 + [st, rowvec, rowvec, rowvec, pl.BlockSpec(seg.shape, lambda bi, ci: (0, 0))],
        out_specs=[tok, st],
        out_shape=[jax.ShapeDtypeStruct((n_seq, seq_len, c), F32),
                   jax.ShapeDtypeStruct((n_seq, n_heads, HEAD_DIM, HEAD_DIM), F32)],
        scratch_shapes=[pltpu.VMEM((nb, c // GROUP_LANES, GROUP_LANES, GROUP_LANES), F32)],
        compiler_params=pltpu.CompilerParams(dimension_semantics=("arbitrary", "arbitrary"),
                                             vmem_limit_bytes=V7X_VMEM_LIMIT),
        name="rwkv",
    )(*vecs, s0, p['r_k'], p['ln_w'], p['ln_b'], seg)
    return y.reshape(n_seq * seq_len, c), s_out


def _attn_prompt_kernel(q_ref, k_ref, v_ref, o_ref, m_scr, l_scr, acc_scr):
    s_len = q_ref.shape[0]
    band = ATT_BAND
    n_blk = s_len // band

    lane = lax.broadcasted_iota(jnp.int32, (1, LANES), 1)
    head0 = lane < HEAD_DIM
    qi = lax.broadcasted_iota(jnp.int32, (band, 2 * band), 0)
    kj = lax.broadcasted_iota(jnp.int32, (band, 2 * band), 1)
    in_band = (kj >= qi) & (kj <= qi + band)
    in_band2 = jnp.concatenate([in_band, in_band], axis=0)
    kj2 = jnp.concatenate([kj, kj], axis=0)
    scale = HEAD_DIM ** -0.5
    ones = jnp.ones((2 * band, LANES), BF16)

    for ci, (window, dil) in enumerate(DILATED_CONFIGS):
        assert window // dil == band
        per_res = n_blk // dil

        run = min(ATT_UNROLL, per_res)

        def body(it, carry, ci=ci, dil=dil, per_res=per_res, run=run):
            span = band * dil
            tile = lambda start: pl.ds(start, band, stride=dil) if dil > 1 else pl.ds(start, band)
            blocks = []
            tiles = []
            first_dyn = []
            for r in range(ATT_UNROLL // run):
                i0 = it * ATT_UNROLL + r * run
                blk0 = i0 % per_res
                start0 = i0 // per_res + blk0 * span
                base = len(tiles)
                if per_res > run:
                    tiles.append(tile(jnp.maximum(start0 - span, 0)))
                    first_dyn.append(blk0 == 0)
                else:
                    tiles.append(None)
                    first_dyn.append(None)
                for t in range(run):
                    tiles.append(tile(start0 + t * span))
                    blocks.append((tiles[-1], base + t, r if t == 0 else None))
            kt = [None if w is None else k_ref[w, :].astype(BF16) for w in tiles]
            vt = [None if w is None else v_ref[w, :].astype(BF16) for w in tiles]
            prev = lambda ts, i: ts[i + 1] if ts[i] is None else ts[i]
            q = [q_ref[rows, :] * scale for rows, _, _ in blocks]

            def mask_of(first):
                if first is None:
                    return in_band2
                if first_dyn[first] is None:
                    return in_band2 & (kj2 >= band)
                return in_band2 & (kj2 >= jnp.where(first_dyn[first], band, 0))

            s = [jnp.where(mask_of(first), lax.dot_general(
                jnp.concatenate([jnp.where(head0, qj, 0.0), jnp.where(head0, 0.0, qj)], axis=0).astype(BF16),
                jnp.concatenate([prev(kt, i), kt[i + 1]], axis=0), NT, preferred_element_type=F32), NEG_INF)
                 for qj, (_, i, first) in zip(q, blocks)]
            m = [jnp.max(z, axis=-1, keepdims=True) for z in s]
            p = [jnp.exp(z - mx).astype(BF16) for z, mx in zip(s, m)]
            o = [_dot(pj, jnp.concatenate([jnp.concatenate([prev(vt, i), vt[i + 1]], axis=0), ones], axis=1))
                 for pj, (_, i, _) in zip(p, blocks)]
            for j, (rows, _, _) in enumerate(blocks):
                m_scr[ci, rows, :] = jnp.where(head0, m[j][:band], m[j][band:])
                acc_scr[ci, rows, :] = jnp.where(head0, o[j][:band, :LANES], o[j][band:, :LANES])
                l_scr[ci, rows, :] = jnp.where(head0, o[j][:band, LANES:], o[j][band:, LANES:])
            return carry

        lax.fori_loop(0, n_blk // ATT_UNROLL, body, 0)

    rows_per = 256

    def merge(i, carry):
        rows = pl.ds(pl.multiple_of(i * rows_per, rows_per), rows_per)
        ms = [m_scr[ci, rows, :] for ci in range(len(DILATED_CONFIGS))]
        m_all = functools.reduce(jnp.maximum, ms)
        num = jnp.zeros((rows_per, LANES), F32)
        den = jnp.zeros((rows_per, LANES), F32)
        for ci, m_c in enumerate(ms):
            w_c = jnp.exp(m_c - m_all)
            num = num + w_c * acc_scr[ci, rows, :]
            den = den + w_c * l_scr[ci, rows, :]
        o_ref[rows, :] = num / den
        return carry

    lax.fori_loop(0, s_len // rows_per, merge, 0)


def _attn_prompt_call(q, k, v, n_seq, seq_len):
    c = q.shape[1]
    n_pairs = c // LANES
    blk = pl.BlockSpec((seq_len, LANES), lambda bi, hi: (bi, hi))
    return pl.pallas_call(
        _attn_prompt_kernel,
        grid=(n_seq, n_pairs),
        in_specs=[blk, blk, blk],
        out_specs=blk,
        out_shape=jax.ShapeDtypeStruct((n_seq * seq_len, c), F32),
        scratch_shapes=[pltpu.VMEM((len(DILATED_CONFIGS), seq_len, LANES), F32)] * 3,
        compiler_params=pltpu.CompilerParams(dimension_semantics=("arbitrary", "arbitrary"),
                                             vmem_limit_bytes=V7X_VMEM_LIMIT),
        name="attn_prompt",
    )(q, k, v)


def _attn_sample_kernel(q_ref, kn_ref, vn_ref, kc_ref, vc_ref, o_ref, *, n_new):
    hd = HEAD_DIM
    _, n_heads, _, n_buf = kc_ref.shape
    t_pad = kn_ref.shape[1]
    c = q_ref.shape[2]
    q = q_ref[0] * (hd ** -0.5)
    lane_head = lax.broadcasted_iota(jnp.int32, (1, c), 1) // hd
    qs = jnp.concatenate([jnp.where(lane_head == h, q, 0.0) for h in range(n_heads)], axis=0).astype(BF16)
    n_rows = n_heads * t_pad
    t_idx = lax.broadcasted_iota(jnp.int32, (n_rows, 1), 0) % t_pad

    def multiplicity(dist):
        mult = jnp.zeros(dist.shape, F32)
        for window, dil in DILATED_CONFIGS:
            hit = (dist >= 0) & (dist <= window) & (dist % dil == 0)
            mult = mult + jnp.where(hit, 1.0, 0.0)
        return mult

    jc = lax.broadcasted_iota(jnp.int32, (1, n_buf), 1)
    mult_c = multiplicity(n_buf + t_idx - jc)
    jn = lax.broadcasted_iota(jnp.int32, (1, t_pad), 1)
    mult_n = jnp.where(jn < n_new, multiplicity(t_idx - jn), 0.0)

    sc = jnp.concatenate(
        [_dot(qs[h * t_pad:(h + 1) * t_pad, h * hd:(h + 1) * hd], kc_ref[0, h].astype(BF16))
         for h in range(n_heads)], axis=0)
    sn = lax.dot_general(qs, kn_ref[0].astype(BF16), NT, preferred_element_type=F32)
    sc = jnp.where(mult_c > 0.0, sc, NEG_INF)
    sn = jnp.where(mult_n > 0.0, sn, NEG_INF)
    m = jnp.maximum(jnp.max(sc, axis=-1, keepdims=True), jnp.max(sn, axis=-1, keepdims=True))
    pc = (mult_c * jnp.exp(sc - m)).astype(BF16)
    pn = mult_n * jnp.exp(sn - m)
    inv_l = 1.0 / (jnp.sum(pc.astype(F32), axis=-1, keepdims=True) + jnp.sum(pn, axis=-1, keepdims=True))
    o_new = _dot(pn.astype(BF16), vn_ref[0].astype(BF16)) * inv_l
    out = jnp.zeros((t_pad, c), F32)
    for h in range(n_heads):
        out = out + jnp.where(lane_head == h, o_new[h * t_pad:(h + 1) * t_pad, :], 0.0)
    o_buf = [lax.dot_general(pc[h * t_pad:(h + 1) * t_pad, :], vc_ref[0, h].astype(BF16), NT,
                             preferred_element_type=F32) * inv_l[h * t_pad:(h + 1) * t_pad, :]
             for h in range(n_heads)]
    o_ref[0] = out + jnp.concatenate(o_buf, axis=1)


def _attn_sample_call(q, kn, vn, k_buf, v_buf, n_new):
    b, t_pad, c = q.shape
    _, n_buf, n_heads, hd = k_buf.shape
    k_t = jnp.transpose(k_buf, (0, 2, 3, 1))
    v_t = jnp.transpose(v_buf, (0, 2, 3, 1))
    new = pl.BlockSpec((1, t_pad, c), lambda bi: (bi, 0, 0))
    buf = pl.BlockSpec((1, n_heads, hd, n_buf), lambda bi: (bi, 0, 0, 0))
    return pl.pallas_call(
        functools.partial(_attn_sample_kernel, n_new=n_new),
        grid=(b,),
        in_specs=[new, new, new, buf, buf],
        out_specs=new,
        out_shape=jax.ShapeDtypeStruct((b, t_pad, c), F32),
        compiler_params=pltpu.CompilerParams(dimension_semantics=("arbitrary",), vmem_limit_bytes=V7X_VMEM_LIMIT),
        name="attn_sample",
    )(q, kn, vn, k_t, v_t)


def _route_rows(logits):
    lane = lax.broadcasted_iota(jnp.int32, logits.shape, 1)
    lane_f = lane.astype(F32)
    first = lambda hit: jnp.min(jnp.where(hit, lane_f, float(LANES)), axis=-1, keepdims=True)
    is_g = lane < N_GROUPS
    lg = jnp.where(is_g, logits, NEG_INF)
    g_max = jnp.max(lg, axis=-1, keepdims=True)
    g_idx = first(lg == g_max)
    g_w = 1.0 / jnp.sum(jnp.where(is_g, jnp.exp(lg - g_max), 0.0), axis=-1, keepdims=True)
    lo = N_GROUPS + EXPERTS_PER_GROUP * g_idx
    le = jnp.where((lane_f >= lo) & (lane_f < lo + EXPERTS_PER_GROUP), logits, NEG_INF)
    e1 = jnp.max(le, axis=-1, keepdims=True)
    i1 = first(le == e1)
    le2 = jnp.where(lane_f == i1, NEG_INF, le)
    e2 = jnp.max(le2, axis=-1, keepdims=True)
    i2 = first(le2 == e2)
    ex = jnp.exp(e2 - e1)
    gate1 = g_w / (1.0 + ex)
    gate2 = g_w * ex / (1.0 + ex)
    out = jnp.where(lane == 0, gate1, jnp.where(lane == 1, gate2, 0.0))
    out = jnp.where(lane == 2, i1 - N_GROUPS, jnp.where(lane == 3, i2 - N_GROUPS, out))
    return out


def _post_kernel(x_ref, yr_ref, ya_ref, wo_ref, gffn_ref, rw_hi_ref, rw_lo_ref, rb_ref,
                 h_o, hn_o, lg_o):
    c = yr_ref.shape[1]
    h = (x_ref[...] + _dot(yr_ref[...].astype(BF16), wo_ref[:c, :]) + _dot(ya_ref[...].astype(BF16), wo_ref[c:, :]))
    ms = jnp.mean(h * h, axis=-1, keepdims=True)
    hn = (h * lax.rsqrt(ms + NORM_EPS)) * gffn_ref[...]
    h_o[...] = h
    bits = pltpu.bitcast(hn.astype(BF16).astype(F32), jnp.uint32)
    half = hn.shape[1] // 2
    hn_o[...] = (bits[:, :half] >> 16) | (bits[:, half:] & jnp.uint32(0xFFFF0000))
    hi = hn.astype(BF16)
    lo = (hn - hi.astype(F32)).astype(BF16)
    logits = (_dot(hi, rw_hi_ref[...]) + _dot(hi, rw_lo_ref[...]) + _dot(lo, rw_hi_ref[...])) + rb_ref[...]
    lg_o[...] = _route_rows(logits)


def _post_call(x2, yr, ya, p, tm):
    t, d = x2.shape
    c = yr.shape[1]
    full = lambda a: pl.BlockSpec(a.shape, lambda i: (0,) * a.ndim)
    tok = lambda w: pl.BlockSpec((tm, w), lambda i: (i, 0))
    weights = [p['wout'], p['gffn'], p['rw_hi'], p['rw_lo'], p['rb']]
    return pl.pallas_call(
        _post_kernel,
        grid=(t // tm,),
        in_specs=[tok(d), tok(c), tok(c)] + [full(w) for w in weights],
        out_specs=[tok(d), tok(d // 2), tok(LANES)],
        out_shape=[jax.ShapeDtypeStruct((t, d), F32), jax.ShapeDtypeStruct((t, d // 2), jnp.uint32),
                   jax.ShapeDtypeStruct((t, LANES), F32)],
        compiler_params=pltpu.CompilerParams(dimension_semantics=("arbitrary",), vmem_limit_bytes=V7X_VMEM_LIMIT),
        name="post",
    )(x2, yr, ya, *weights)


def _expert_kernel(be_ref, nb_ref, xs_ref, w1_ref, w3_ref, w2_ref, y_ref, w1_s, w3_s, w2_s):
    i = pl.program_id(0)
    live = i < nb_ref[0]

    @pl.when(live & ((i == 0) | (be_ref[i] != be_ref[jnp.maximum(i - 1, 0)])))
    def _():
        w1_s[...] = w1_ref[...].astype(BF16)
        w3_s[...] = w3_ref[...].astype(BF16)
        w2_s[...] = w2_ref[...].astype(BF16)

    @pl.when(live)
    def _():
        packed = xs_ref[...]
        half = packed.shape[1]
        lo = pltpu.bitcast(packed << 16, F32).astype(BF16)
        hi = pltpu.bitcast(packed & jnp.uint32(0xFFFF0000), F32).astype(BF16)
        h1 = _dot(lo, w1_s[:half, :]) + _dot(hi, w1_s[half:, :])
        h3 = _dot(lo, w3_s[:half, :]) + _dot(hi, w3_s[half:, :])
        act = (h1 * _sigmoid(h1)) * h3
        y_ref[...] = _dot(act.astype(BF16), w2_s[...])

    @pl.when(jnp.logical_not(live))
    def _():
        y_ref[...] = jnp.zeros_like(y_ref)


def _expert_call(blk_exp, n_used, xs, p, bm):
    n_slots = xs.shape[0]
    _, d, de = p['w1'].shape
    grid_spec = pltpu.PrefetchScalarGridSpec(
        num_scalar_prefetch=2,
        grid=(n_slots // bm,),
        in_specs=[pl.BlockSpec((bm, d // 2), lambda i, be, nb: (i, 0)),
                  pl.BlockSpec((None, d, de), lambda i, be, nb: (be[i], 0, 0)),
                  pl.BlockSpec((None, d, de), lambda i, be, nb: (be[i], 0, 0)),
                  pl.BlockSpec((None, de, d), lambda i, be, nb: (be[i], 0, 0))],
        out_specs=pl.BlockSpec((bm, d), lambda i, be, nb: (i, 0)),
        scratch_shapes=[pltpu.VMEM((d, de), BF16), pltpu.VMEM((d, de), BF16), pltpu.VMEM((de, d), BF16)],
    )
    return pl.pallas_call(
        _expert_kernel,
        grid_spec=grid_spec,
        out_shape=jax.ShapeDtypeStruct((n_slots, d), F32),
        compiler_params=pltpu.CompilerParams(dimension_semantics=("arbitrary",), vmem_limit_bytes=V7X_VMEM_LIMIT),
        name="experts",
    )(blk_exp, n_used, xs, p['w1'], p['w3'], p['w2'])


def _final_kernel(h_ref, route_ref, y1_ref, y2_ref, gfin_ref, o_ref):
    route = route_ref[...]
    h = h_ref[...] + (route[:, 0:1] * y1_ref[...] + route[:, 1:2] * y2_ref[...])
    ms = jnp.mean(h * h, axis=-1, keepdims=True)
    o_ref[...] = (h * lax.rsqrt(ms + NORM_EPS)) * gfin_ref[...]


def _final_call(h, route, y12, gfin, tm):
    t, d = h.shape
    tok = pl.BlockSpec((tm, d), lambda i: (i, 0))
    routed = lambda a: pl.BlockSpec((None, tm, d), lambda i: (a, i, 0))
    return pl.pallas_call(
        _final_kernel,
        grid=(t // tm,),
        in_specs=[tok, pl.BlockSpec((tm, LANES), lambda i: (i, 0)), routed(0), routed(1),
                  pl.BlockSpec((1, d), lambda i: (0, 0))],
        out_specs=tok,
        out_shape=jax.ShapeDtypeStruct((t, d), F32),
        compiler_params=pltpu.CompilerParams(dimension_semantics=("arbitrary",), vmem_limit_bytes=V7X_VMEM_LIMIT),
        name="final",
    )(h, route, y12, y12, gfin)


def _gather_rows(table, idx):
    info = plsc.get_sparse_core_info()
    nc, ns = info.num_cores, info.num_subcores
    b, d = idx.shape[0], table.shape[1]
    chunk = min(SC_INDEX_LIMIT, SC_GATHER_BYTES // (d * table.dtype.itemsize))
    assert b % (nc * ns * chunk * 2) == 0, "rows must split evenly into chunk pairs per subcore"
    per_w = b // (nc * ns)
    n_chunks = per_w // chunk
    mesh = plsc.VectorSubcoreMesh(core_axis_name="c", subcore_axis_name="s")

    @functools.partial(
        pl.kernel, mesh=mesh, out_type=jax.ShapeDtypeStruct((b, d), table.dtype),
        scratch_types=[pltpu.VMEM((per_w,), jnp.int32), pltpu.VMEM((2, chunk, d), table.dtype),
                       pltpu.SemaphoreType.DMA((2,)), pltpu.SemaphoreType.DMA((2,))])
    def gather(table_hbm, idx_hbm, out_hbm, idx_v, rows_v, fetch_sem, put_sem):
        base = (lax.axis_index("s") * nc + lax.axis_index("c")) * per_w
        pltpu.sync_copy(idx_hbm.at[pl.ds(base, per_w)], idx_v)

        def fetch(c, slot):
            off = pl.multiple_of(c * chunk, chunk)
            return pltpu.make_async_copy(table_hbm.at[idx_v.at[pl.ds(off, chunk)]], rows_v.at[slot],
                                         fetch_sem.at[slot])

        def put(c, slot):
            off = pl.multiple_of(c * chunk, chunk)
            return pltpu.make_async_copy(rows_v.at[slot], out_hbm.at[pl.ds(base + off, chunk)], put_sem.at[slot])

        fetch(0, 0).start()

        @pl.loop(0, n_chunks, step=2)
        def _(c):
            @pl.when(c > 0)
            def _():
                put(c - 1, 1).wait()
            fetch(c + 1, 1).start()
            fetch(c, 0).wait()
            put(c, 0).start()
            fetch(c + 1, 1).wait()
            put(c, 0).wait()

            @pl.when(c + 2 < n_chunks)
            def _():
                fetch(c + 2, 0).start()
            put(c + 1, 1).start()

        put(n_chunks - 1, 1).wait()

    return gather(table, idx)


def _route(route, bm):
    n = route.shape[0]
    eid = route[:, 2:4].astype(jnp.int32).reshape(-1)
    m = eid.shape[0]
    experts = jnp.arange(N_EXPERTS + 1, dtype=jnp.int32)
    e_sorted, order = lax.sort_key_val(eid, jnp.arange(m, dtype=jnp.int32))
    below = jnp.sum((eid[:, None] < experts[None, :]).astype(jnp.int32), axis=0)
    starts, counts = below[:-1], below[1:] - below[:-1]
    padded = (counts + bm - 1) // bm * bm
    p_ends = jnp.cumsum(padded)
    p_starts = p_ends - padded
    shift = p_starts - starts
    dest_sorted = jnp.arange(m, dtype=jnp.int32) + jnp.sum(
        jnp.where(e_sorted[:, None] == experts[None, :-1], shift[None, :], 0), axis=1)
    _, dest = lax.sort_key_val(order, dest_sorted)
    n_blocks = _round_up(-(-m // bm) + N_EXPERTS, SLOT_MULTIPLE // bm)
    blk_start = jnp.arange(n_blocks, dtype=jnp.int32) * bm
    blk_exp = jnp.minimum(jnp.sum((p_ends[None, :] <= blk_start[:, None]).astype(jnp.int32), axis=1), N_EXPERTS - 1)
    pick = lambda tbl: jnp.sum(jnp.where(blk_exp[:, None] == experts[None, :-1], tbl[None, :], 0), axis=1)
    pos = (blk_start - pick(p_starts))[:, None] + jnp.arange(bm, dtype=jnp.int32)[None, :]
    valid = pos < pick(counts)[:, None]
    src = jnp.where(valid, pick(starts)[:, None] + pos, 0).reshape(-1)
    src_assign = order.at[src].get(mode='promise_in_bounds')
    valid = valid.reshape(-1)
    slot_tok = jnp.where(valid, src_assign // 2, jnp.arange(valid.shape[0], dtype=jnp.int32) % n)
    n_used = (p_ends[-1] // bm).astype(jnp.int32).reshape(1)
    return slot_tok, dest.reshape(n, 2), blk_exp.astype(jnp.int32), n_used


def _moe_and_final(x2, yr, ya, p, tm, bm):
    h, hn, route = _post_call(x2, yr, ya, p, tm)
    slot_tok, dest, blk_exp, n_used = _route(route, bm)
    xs = _gather_rows(hn, slot_tok)
    yb = _expert_call(blk_exp, n_used, xs, p, bm)
    y12 = _gather_rows(yb, dest.T.reshape(-1)).reshape(2, h.shape[0], h.shape[1])
    return _final_call(h, route, y12, p['gfin'], tm)


def _prep_params(layer, norm_mix_g, w_in, rwkv_mu_rkv, rwkv_mu_wag, rwkv_w0, rwkv_w1, rwkv_w2, rwkv_a0, rwkv_a1,
                 rwkv_a2, rwkv_g1, rwkv_g2, rwkv_k_k, rwkv_k_a, rwkv_r_k, rwkv_ln_w, rwkv_ln_b, w_out, norm_ffn_g,
                 router_group_w, router_group_b, router_expert_w, router_expert_b, expert_w1, expert_w3, expert_w2,
                 norm_final_g):
    d = w_in.shape[1]
    c = rwkv_w0.shape[1]
    row = lambda a: a.reshape(1, -1).astype(F32)
    lowrank = jnp.concatenate([rwkv_w1[layer], rwkv_a1[layer], rwkv_g1[layer]], axis=1)
    mx = rwkv_mu_wag[layer]
    r_w = rwkv_w1.shape[2]
    r_a = rwkv_a1.shape[2]
    r_g = rwkv_g1.shape[2]
    assert r_w + r_a == LANES and r_g == LANES
    mx_cols = jnp.concatenate([jnp.broadcast_to(mx[0][:, None], (d, r_w)), jnp.broadcast_to(mx[1][:, None], (d, r_a)),
                               jnp.broadcast_to(mx[2][:, None], (d, r_g))], axis=1)
    w2a2 = jnp.zeros((LANES, 2 * c), F32)
    w2a2 = w2a2.at[:r_w, :c].set(rwkv_w2[layer]).at[r_w:, c:].set(rwkv_a2[layer])
    head = jnp.arange(c) // HEAD_DIM
    rw = jnp.zeros((d, LANES), F32)
    rw = rw.at[:, :N_GROUPS].set(router_group_w[layer]).at[:, N_GROUPS:N_GROUPS + N_EXPERTS].set(router_expert_w[layer])
    rw_hi = rw.astype(BF16)
    rb = jnp.zeros((1, LANES), F32)
    rb = rb.at[0, :N_GROUPS].set(router_group_b[layer]).at[0, N_GROUPS:N_GROUPS + N_EXPERTS].set(router_expert_b[layer])
    return {
        'c_rwkv': c,
        'gmix': row(norm_mix_g[layer]),
        'win': jnp.concatenate([w_in[layer], lowrank], axis=1).astype(BF16),
        'wdx': (mx_cols * lowrank).astype(BF16),
        'mu_rkv': rwkv_mu_rkv[layer],
        'w0a0': jnp.concatenate([row(rwkv_w0[layer]), row(rwkv_a0[layer])], axis=1),
        'w2a2': w2a2.astype(BF16),
        'g2': rwkv_g2[layer].astype(BF16),
        'k_k': row(rwkv_k_k[layer]),
        'k_a': row(rwkv_k_a[layer]),
        'seg': (head[:, None] == head[None, :]).astype(BF16),
        'r_k': row(rwkv_r_k[layer]),
        'ln_w': row(rwkv_ln_w[layer]),
        'ln_b': row(rwkv_ln_b[layer]),
        'wout': w_out[layer].astype(BF16),
        'gffn': row(norm_ffn_g[layer]),
        'rw_hi': rw_hi,
        'rw_lo': (rw - rw_hi.astype(F32)).astype(BF16),
        'rb': rb,
        'w1': expert_w1[layer],
        'w3': expert_w3[layer],
        'w2': expert_w2[layer],
        'gfin': row(norm_final_g),
    }


def _prompt_group(x, p):
    b, s, d = x.shape
    c = p['c_rwkv']
    x2 = x.reshape(b * s, d)
    tm = PROJ_TILE
    flag = jnp.zeros((b * s, 1), F32)
    outs = _proj_call(x2, flag, p, s // tm, tm)
    r, lw, k, v, kk, bb, g, qa, ka, va, xl = outs
    s0 = jnp.zeros((b, c // HEAD_DIM, HEAD_DIM, HEAD_DIM), F32)
    yr, s_new = _rwkv_call((r, lw, k, v, kk, bb, g), s0, p, b, s, RWKV_CHUNK, 4)
    ya = _attn_prompt_call(qa, ka, va, b, s)
    y = _moe_and_final(x2, yr, ya, p, tm, EXPERT_TILE)
    shift = xl.reshape(b, s // tm, 8, d)[:, -1, 7, :]
    keep = min(max(w for w, _ in DILATED_CONFIGS), s)
    k_keep = ka.reshape(b, s, c // HEAD_DIM, HEAD_DIM)[:, s - keep:]
    v_keep = va.reshape(b, s, c // HEAD_DIM, HEAD_DIM)[:, s - keep:]
    return y.reshape(b, s, d), s_new, shift, k_keep, v_keep


def _sample_group(x, shift0, s0, k_buf, v_buf, p):
    b, t, d = x.shape
    c = p['c_rwkv']
    n_heads = c // HEAD_DIM
    t_pad = 8
    xc = jnp.concatenate([shift0[:, None, :], x, jnp.zeros((b, t_pad - 1 - t, d), x.dtype)], axis=1)
    flag = jnp.zeros((b, t_pad, 1), F32).at[:, 0].set(1.0)
    outs = _proj_call(xc.reshape(b * t_pad, d), flag.reshape(b * t_pad, 1), p, 1, b * t_pad, xl_rows=b * t_pad)
    xl = outs[10]
    live = (jnp.arange(t_pad) < t)[None, :, None]
    shifted = [jnp.where(live, jnp.roll(o.reshape(b, t_pad, c), -1, axis=1), 0.0) for o in outs[:10]]
    r, lw, k, v, kk, bb, g, qa, ka, va = shifted
    flat = lambda z: z.reshape(b * t_pad, c)
    yr, s_new = _rwkv_call(tuple(flat(z) for z in (r, lw, k, v, kk, bb, g)), s0, p, b, t_pad, t_pad, 8)
    ya = _attn_sample_call(qa, ka, va, k_buf, v_buf, t)
    x_pad = jnp.concatenate([x, jnp.zeros((b, t_pad - t, d), x.dtype)], axis=1).reshape(b * t_pad, d)
    y = _moe_and_final(x_pad, yr, flat(ya), p, b * t_pad // 2, SAMPLE_EXPERT_TILE)
    y = y.reshape(b, t_pad, d)[:, :t]
    shift = xl.reshape(b, t_pad, d)[:, t]
    return (y, s_new, shift, ka[:, :t].reshape(b, t, n_heads, HEAD_DIM), va[:, :t].reshape(b, t, n_heads, HEAD_DIM))


def kernel(x_prompt, x_sample, state_rwkv, state_shift, cache_att_k, cache_att_v, norm_mix_g, w_in, rwkv_mu_rkv, rwkv_mu_wag, rwkv_w0, rwkv_w1, rwkv_w2, rwkv_a0, rwkv_a1, rwkv_a2, rwkv_g1, rwkv_g2, rwkv_k_k, rwkv_k_a, rwkv_r_k, rwkv_ln_w, rwkv_ln_b, w_out, norm_ffn_g, router_group_w, router_group_b, router_expert_w, router_expert_b, expert_w1, expert_w3, expert_w2, norm_final_g):
    assert w_in.shape[0] == 1, "single-layer trunk"
    p = _prep_params(0, norm_mix_g, w_in, rwkv_mu_rkv, rwkv_mu_wag, rwkv_w0, rwkv_w1, rwkv_w2, rwkv_a0, rwkv_a1,
                     rwkv_a2, rwkv_g1, rwkv_g2, rwkv_k_k, rwkv_k_a, rwkv_r_k, rwkv_ln_w, rwkv_ln_b, w_out,
                     norm_ffn_g, router_group_w, router_group_b, router_expert_w, router_expert_b, expert_w1,
                     expert_w3, expert_w2, norm_final_g)
    y_p, rw_p, sh_p, kc_p, vc_p = _prompt_group(x_prompt, p)
    y_s, rw_s, sh_s, kc_s, vc_s = _sample_group(x_sample, state_shift[0], state_rwkv[0], cache_att_k[0],
                                                cache_att_v[0], p)
    return (y_p, y_s, rw_p[None], sh_p[None], kc_p[None], vc_p[None], rw_s[None], sh_s[None], kc_s[None], vc_s[None])
```

```python
import functools
import math

import jax
import jax.numpy as jnp
from jax import lax
from jax.experimental import pallas as pl
from jax.experimental.pallas import tpu as pltpu
from jax.experimental.pallas import tpu_sc as plsc

F32 = jnp.float32
BF16 = jnp.bfloat16

HEAD_DIM = 64
GN_EPS = 64e-5
NORM_EPS = 1e-6
DILATED_CONFIGS = ((128, 1), (512, 4), (2048, 16))
N_GROUPS = 4
EXPERTS_PER_GROUP = 8
N_EXPERTS = N_GROUPS * EXPERTS_PER_GROUP
NEG_INF = -1e30

V7X_VMEM_LIMIT = 56 * 1024 * 1024
LANES = 128

PROJ_TILE = 512
RWKV_CHUNK = 64
ATT_BAND = 128
EXPERT_TILE = 512
ATT_UNROLL = 4
SC_GATHER_BYTES = 128 * 1024
SC_INDEX_LIMIT = 128
V7X_SC_VECTOR_SUBCORES = 32

HIGHEST = lax.Precision.HIGHEST
NN = (((1,), (0,)), ((), ()))
NT = (((1,), (1,)), ((), ()))
TN = (((0,), (0,)), ((), ()))


def _dot(a, b, precision=None):
    return jnp.dot(a, b, preferred_element_type=F32, precision=precision)


def _dot_split(a, b_bf16):
    hi = a.astype(BF16)
    lo = (a - hi.astype(F32)).astype(BF16)
    return _dot(hi, b_bf16) + _dot(lo, b_bf16)


def _mm(a, b, dims, mode):
    if mode == 'f32':
        return lax.dot_general(a, b, dims, precision=HIGHEST, preferred_element_type=F32)
    a_hi = a.astype(BF16)
    b_hi = b.astype(BF16)
    out = lax.dot_general(a_hi, b_hi, dims, preferred_element_type=F32)
    if mode == 'x3':
        a_lo = (a - a_hi.astype(F32)).astype(BF16)
        b_lo = (b - b_hi.astype(F32)).astype(BF16)
        out = out + lax.dot_general(a_hi, b_lo, dims, preferred_element_type=F32)
        out = out + lax.dot_general(a_lo, b_hi, dims, preferred_element_type=F32)
    return out


RWKV_MODES = dict(A='bf16', AV='bf16', SQ='bf16', AP='bf16', RB='bf16', RK='bf16', WS='bf16', UP='bf16')


def _pre(x, mode):
    return x.astype(BF16) if mode == 'bf16' else x


def _round_up(x, k):
    return -(-x // k) * k


def _sigmoid(z):
    return 1.0 / (1.0 + jnp.exp(-z))


def _proj_kernel(x_ref, flag_ref, gmix_ref, win_ref, wdx_ref, mu_ref, w0a0_ref, w2a2_ref, g2_ref,
                 kk_ref, ka_ref, seg_ref,
                 r_o, lw_o, k_o, v_o, kkn_o, b_o, g_o, qa_o, kat_o, vat_o, xl_o,
                 xn_carry, pj_carry, *, tiles_per_seq, c_rwkv):
    i = pl.program_id(0)

    @pl.when(i % tiles_per_seq == 0)
    def _():
        xn_carry[...] = jnp.zeros_like(xn_carry)
        pj_carry[...] = jnp.zeros_like(pj_carry)

    c = c_rwkv
    x = x_ref[...]
    tm = x.shape[0]
    ms = jnp.mean(x * x, axis=-1, keepdims=True)
    xn = (x * lax.rsqrt(ms + NORM_EPS)) * gmix_ref[...]
    xn = jnp.where(flag_ref[...] > 0.0, x, xn)
    row = lax.broadcasted_iota(jnp.int32, (tm, 1), 0)
    xn_prev = jnp.where(row == 0, xn_carry[7:8, :], pltpu.roll(xn, 1, axis=0))
    dx = xn_prev - xn

    proj = _dot(xn.astype(BF16), win_ref[...])
    cur = proj[:, :3 * c]
    prev = jnp.where(row == 0, pj_carry[7:8, :], pltpu.roll(cur, 1, axis=0))
    xn_carry[...] = xn[tm - 8:, :]
    pj_carry[...] = cur[tm - 8:, :]
    xl_rows = xl_o.shape[1]
    xl_o[0] = xn[tm - xl_rows:, :]

    mu = mu_ref[...]
    r = cur[:, :c] + mu[0:1] * (prev[:, :c] - cur[:, :c])
    k = cur[:, c:2 * c] + mu[1:2] * (prev[:, c:2 * c] - cur[:, c:2 * c])
    v = cur[:, 2 * c:3 * c] + mu[2:3] * (prev[:, 2 * c:3 * c] - cur[:, 2 * c:3 * c])

    lr = proj[:, 6 * c:] + _dot(dx.astype(BF16), wdx_ref[...])
    lane = lax.broadcasted_iota(jnp.int32, (1, LANES), 1)
    wa_in = jnp.where(lane < 64, jnp.tanh(lr[:, :LANES]), lr[:, :LANES])
    wa = _dot(wa_in.astype(BF16), w2a2_ref[...]) + w0a0_ref[...]
    z = -wa[:, :c]
    softplus = jnp.maximum(z, 0.0) + jnp.log1p(jnp.exp(-jnp.abs(z)))
    lw = -jnp.exp(-softplus - 0.5)
    a = _sigmoid(wa[:, c:])
    g = _dot(_sigmoid(lr[:, LANES:]).astype(BF16), g2_ref[...])

    kk = k * kk_ref[...]
    ss = _dot_split(kk * kk, seg_ref[...])
    kk = kk * lax.rsqrt(jnp.maximum(ss, 1e-24))

    r_o[...] = r
    lw_o[...] = lw
    k_o[...] = k * (1.0 + (a - 1.0) * ka_ref[...])
    v_o[...] = v
    kkn_o[...] = kk
    b_o[...] = kk * a
    g_o[...] = g
    qa_o[...] = proj[:, 3 * c:4 * c]
    kat_o[...] = proj[:, 4 * c:5 * c]
    vat_o[...] = proj[:, 5 * c:6 * c]


def _proj_call(x2, flag, p, tiles_per_seq, tm, xl_rows=8):
    t, d = x2.shape
    c = p['c_rwkv']
    n_tiles = t // tm
    full = lambda a: pl.BlockSpec(a.shape, lambda i: (0,) * a.ndim, pipeline_mode=pl.Buffered(1))
    tok = lambda w: pl.BlockSpec((tm, w), lambda i: (i, 0))
    weights = [p['gmix'], p['win'], p['wdx'], p['mu_rkv'], p['w0a0'], p['w2a2'], p['g2'], p['k_k'], p['k_a'], p['seg']]
    outs = pl.pallas_call(
        functools.partial(_proj_kernel, tiles_per_seq=tiles_per_seq, c_rwkv=c),
        grid=(n_tiles,),
        in_specs=[tok(d), tok(1)] + [full(w) for w in weights],
        out_specs=[tok(c)] * 10 + [pl.BlockSpec((1, xl_rows, d), lambda i: (i, 0, 0))],
        out_shape=[jax.ShapeDtypeStruct((t, c), F32)] * 10 + [jax.ShapeDtypeStruct((n_tiles, xl_rows, d), F32)],
        scratch_shapes=[pltpu.VMEM((8, d), F32), pltpu.VMEM((8, 3 * c), F32)],
        compiler_params=pltpu.CompilerParams(dimension_semantics=("arbitrary",), vmem_limit_bytes=V7X_VMEM_LIMIT),
        name="proj",
    )(x2, flag, *weights)
    return outs


GROUP_LANES = 256
GROUP_HEADS = GROUP_LANES // HEAD_DIM


def _rwkv_kernel(r_ref, lw_ref, k_ref, v_ref, kk_ref, b_ref, g_ref, s0_ref, rk_ref, lnw_ref, lnb_ref, seg_ref,
                 y_ref, sout_ref, s_scr):
    ci = pl.program_id(1)
    nb, L, c = r_ref.shape
    gw, gh, hd = GROUP_LANES, GROUP_HEADS, HEAD_DIM
    n_groups = c // gw
    md = RWKV_MODES

    lane_head = lax.broadcasted_iota(jnp.int32, (1, gw), 1) // hd
    head_masks = [lane_head == j for j in range(gh)]
    bd_state = (lax.broadcasted_iota(jnp.int32, (gw, gw), 0) // hd) == (lax.broadcasted_iota(jnp.int32, (gw, gw), 1) // hd)
    bd_time = (lax.broadcasted_iota(jnp.int32, (gh * L, gh * L), 0) // L) == (lax.broadcasted_iota(jnp.int32, (gh * L, gh * L), 1) // L)
    t_row = lax.broadcasted_iota(jnp.int32, (L, gh * L), 0)
    t_col = lax.broadcasted_iota(jnp.int32, (L, gh * L), 1) % L
    strict4 = t_row > t_col
    incl4 = t_row >= t_col
    incl = lax.broadcasted_iota(jnp.int32, (L, L), 0) >= lax.broadcasted_iota(jnp.int32, (L, L), 1)

    def stack(x):
        return jnp.concatenate([jnp.where(m, x, jnp.zeros_like(x)) for m in head_masks], axis=0)

    def block_diag(n):
        tiled = jnp.concatenate([n] * gh, axis=0)
        return jnp.where(bd_time, tiled, jnp.zeros_like(tiled))

    @pl.when(ci == 0)
    def _():
        for bi in range(nb):
            for gi in range(n_groups):
                s_in = s0_ref[bi, gi * gh:(gi + 1) * gh].reshape(gw, hd)
                s_scr[bi, gi] = jnp.where(bd_state, jnp.concatenate([s_in] * gh, axis=1), 0.0)

    n_apply = max(1, int(math.log2(L)))
    seg = seg_ref[...]
    pre = []
    for bi in range(nb):
        lw = lw_ref[bi]
        lw_hi = lw.astype(BF16)
        lw_r = lw - lw_hi.astype(F32)
        lw_mid = lw_r.astype(BF16)
        lw_lo = (lw_r - lw_mid.astype(F32)).astype(BF16)
        cs3 = _dot(incl.astype(BF16), jnp.concatenate([lw_hi, lw_mid, lw_lo], axis=1))
        cs = cs3[:, :c] + cs3[:, c:2 * c] + cs3[:, 2 * c:]
        cp = cs - lw
        cm = cs[L // 2 - 1:L // 2, :]
        c_last = cs[L - 1:L, :]
        r, k, v, kk, b = r_ref[bi], k_ref[bi], v_ref[bi], kk_ref[bi], b_ref[bi]
        e_dn = jnp.exp(cm - cs)
        e_l = jnp.exp(c_last - cs)
        pre.append(dict(v=v, rt=r * jnp.exp(cs - cm), kkt=kk * jnp.exp(cp - cm), bt=b * e_dn, kt=k * e_dn,
                        kg=kk * jnp.exp(cp), rg=r * jnp.exp(cs), bh=b * e_l, kh=k * e_l, g_last=jnp.exp(c_last),
                        rkk=r * k * rk_ref[...]))

    chains = [(bi, gi) for bi in range(nb) for gi in range(n_groups)]
    col = lambda bi, gi, name: pre[bi][name][:, gi * gw:(gi + 1) * gw]
    each = lambda fn: [fn(i, bi, gi) for i, (bi, gi) in enumerate(chains)]

    vg = each(lambda i, bi, gi: col(bi, gi, 'v'))
    v_st = each(lambda i, bi, gi: stack(_pre(vg[i], md['AV'])))
    a_all = each(lambda i, bi, gi: _mm(
        jnp.concatenate([col(bi, gi, 'kkt'), col(bi, gi, 'rt')], axis=0),
        jnp.concatenate([stack(_pre(col(bi, gi, 'bt'), md['A'])), stack(_pre(col(bi, gi, 'kt'), md['A']))], axis=0),
        NT, md['A']))
    p_ak = each(lambda i, bi, gi: jnp.where(strict4, a_all[i][:L, gh * L:], 0.0))
    p_rb = each(lambda i, bi, gi: jnp.where(incl4, a_all[i][L:, :gh * L], 0.0))
    p_rk = each(lambda i, bi, gi: jnp.where(incl4, a_all[i][L:, gh * L:], 0.0))
    eye4 = (t_row == t_col).astype(F32)
    nm = each(lambda i, bi, gi: -jnp.where(strict4, a_all[i][:L, :gh * L], 0.0))
    t_inv = [eye4 + n for n in nm]
    for it in range(n_apply - 1):
        lhs = nm if it == 0 else [jnp.concatenate([n, t], axis=0) for n, t in zip(nm, t_inv)]
        both = each(lambda i, bi, gi: _mm(lhs[i], block_diag(_pre(nm[i], md['SQ'])), NN, md['SQ']))
        if it > 0:
            t_inv = [t + bo[L:] for t, bo in zip(t_inv, both)]
        nm = [bo[:L] for bo in both]
    t_inv = each(lambda i, bi, gi: t_inv[i] + _mm(t_inv[i], block_diag(_pre(nm[i], md['SQ'])), NN, md['SQ']))
    av = each(lambda i, bi, gi: _mm(p_ak[i], v_st[i], NN, md['AV']))
    x = each(lambda i, bi, gi: _mm(
        t_inv[i], jnp.concatenate([stack(_pre(col(bi, gi, 'kg'), md['AP'])), stack(_pre(av[i], md['AP']))], axis=1),
        NN, md['AP']))
    w_m = [xi[:, :gw] for xi in x]
    u0 = [-xi[:, gw:] for xi in x]
    rbw = each(lambda i, bi, gi: _mm(
        p_rb[i], jnp.concatenate([stack(_pre(w_m[i], md['RB'])), stack(_pre(u0[i], md['RB']))], axis=1),
        NN, md['RB']))
    rkv = each(lambda i, bi, gi: _mm(p_rk[i], v_st[i], NN, md['RK']))
    s_old = each(lambda i, bi, gi: s_scr[bi, gi])
    ws = each(lambda i, bi, gi: _mm(
        jnp.concatenate([w_m[i], col(bi, gi, 'rg') - rbw[i][:, :gw]], axis=0), s_old[i], NT, md['WS']))
    u = each(lambda i, bi, gi: u0[i] - ws[i][:L])
    y = each(lambda i, bi, gi: ws[i][L:] + rbw[i][:, gw:] + rkv[i])
    upd = each(lambda i, bi, gi: _mm(
        jnp.concatenate([u[i], vg[i]], axis=0),
        jnp.concatenate([col(bi, gi, 'bh'), col(bi, gi, 'kh')], axis=0), TN, md['UP']))
    for i, (bi, gi) in enumerate(chains):
        s_scr[bi, gi] = s_old[i] * col(bi, gi, 'g_last') + jnp.where(bd_state, upd[i], 0.0)

    inv = 1.0 / hd
    n_ch = len(chains)
    sums = _dot_split(jnp.concatenate(y + each(lambda i, bi, gi: col(bi, gi, 'rkk')), axis=0), seg)
    mean = [sums[i * L:(i + 1) * L] * inv for i in range(n_ch)]
    bonus = [sums[(n_ch + i) * L:(n_ch + i + 1) * L] * vg[i] for i in range(n_ch)]
    yc = [y[i] - mean[i] for i in range(n_ch)]
    sq = _dot_split(jnp.concatenate([z * z for z in yc], axis=0), seg)
    var = [sq[i * L:(i + 1) * L] * inv for i in range(n_ch)]
    for i, (bi, gi) in enumerate(chains):
        sl = slice(gi * gw, (gi + 1) * gw)
        yn = yc[i] * lax.rsqrt(var[i] + GN_EPS) * lnw_ref[:, sl] + lnb_ref[:, sl]
        y_ref[bi, :, sl] = (yn + bonus[i]) * g_ref[bi, :, sl]

    @pl.when(ci == pl.num_programs(1) - 1)
    def _():
        for bi in range(nb):
            for gi in range(n_groups):
                bd = s_scr[bi, gi]
                folded = bd[:, 0:hd]
                for j in range(1, gh):
                    folded = folded + bd[:, j * hd:(j + 1) * hd]
                sout_ref[bi, gi * gh:(gi + 1) * gh] = folded.reshape(gh, hd, hd)


def _rwkv_call(vecs, s0, p, n_seq, seq_len, chunk, nb):
    c = p['c_rwkv']
    n_heads = c // HEAD_DIM
    n_chunks = seq_len // chunk
    assert n_seq % nb == 0 and seq_len % chunk == 0
    vecs = [z.reshape(n_seq, seq_len, c) for z in vecs]
    tok = pl.BlockSpec((nb, chunk, c), lambda bi, ci: (bi, ci, 0))
    st = pl.BlockSpec((nb, n_heads, HEAD_DIM, HEAD_DIM), lambda bi, ci: (bi, 0, 0, 0))
    rowvec = pl.BlockSpec((1, c), lambda bi, ci: (0, 0))
    seg = p['seg'][:GROUP_LANES, :GROUP_LANES]
    y, s_out = pl.pallas_call(
        _rwkv_kernel,
        grid=(n_seq // nb, n_chunks),
        in_specs=[tok] * 7 + [st, rowvec, rowvec, rowvec, pl.BlockSpec(seg.shape, lambda bi, ci: (0, 0))],
        out_specs=[tok, st],
        out_shape=[jax.ShapeDtypeStruct((n_seq, seq_len, c), F32),
                   jax.ShapeDtypeStruct((n_seq, n_heads, HEAD_DIM, HEAD_DIM), F32)],
        scratch_shapes=[pltpu.VMEM((nb, c // GROUP_LANES, GROUP_LANES, GROUP_LANES), F32)],
        compiler_params=pltpu.CompilerParams(dimension_semantics=("arbitrary", "arbitrary"),
                                             vmem_limit_bytes=V7X_VMEM_LIMIT),
        name="rwkv",
    )(*vecs, s0, p['r_k'], p['ln_w'], p['ln_b'], seg)
    return y.reshape(n_seq * seq_len, c), s_out


def _attn_prompt_kernel(q_ref, k_ref, v_ref, o_ref, m_scr, l_scr, acc_scr):
    s_len = q_ref.shape[0]
    band = ATT_BAND
    n_blk = s_len // band

    lane = lax.broadcasted_iota(jnp.int32, (1, LANES), 1)
    head0 = lane < HEAD_DIM
    qi = lax.broadcasted_iota(jnp.int32, (band, 2 * band), 0)
    kj = lax.broadcasted_iota(jnp.int32, (band, 2 * band), 1)
    in_band = (kj >= qi) & (kj <= qi + band)
    in_band2 = jnp.concatenate([in_band, in_band], axis=0)
    kj2 = jnp.concatenate([kj, kj], axis=0)
    scale = HEAD_DIM ** -0.5
    ones = jnp.ones((2 * band, LANES), BF16)

    for ci, (window, dil) in enumerate(DILATED_CONFIGS):
        assert window // dil == band
        per_res = n_blk // dil

        run = min(ATT_UNROLL, per_res)

        def body(it, carry, ci=ci, dil=dil, per_res=per_res, run=run):
            span = band * dil
            tile = lambda start: pl.ds(start, band, stride=dil) if dil > 1 else pl.ds(start, band)
            blocks = []
            tiles = []
            first_dyn = []
            for r in range(ATT_UNROLL // run):
                i0 = it * ATT_UNROLL + r * run
                blk0 = i0 % per_res
                start0 = i0 // per_res + blk0 * span
                base = len(tiles)
                if per_res > run:
                    tiles.append(tile(jnp.maximum(start0 - span, 0)))
                    first_dyn.append(blk0 == 0)
                else:
                    tiles.append(None)
                    first_dyn.append(None)
                for t in range(run):
                    tiles.append(tile(start0 + t * span))
                    blocks.append((tiles[-1], base + t, r if t == 0 else None))
            kt = [None if w is None else k_ref[w, :].astype(BF16) for w in tiles]
            vt = [None if w is None else v_ref[w, :].astype(BF16) for w in tiles]
            prev = lambda ts, i: ts[i + 1] if ts[i] is None else ts[i]
            q = [q_ref[rows, :] * scale for rows, _, _ in blocks]

            def mask_of(first):
                if first is None:
                    return in_band2
                if first_dyn[first] is None:
                    return in_band2 & (kj2 >= band)
                return in_band2 & (kj2 >= jnp.where(first_dyn[first], band, 0))

            s = [jnp.where(mask_of(first), lax.dot_general(
                jnp.concatenate([jnp.where(head0, qj, 0.0), jnp.where(head0, 0.0, qj)], axis=0).astype(BF16),
                jnp.concatenate([prev(kt, i), kt[i + 1]], axis=0), NT, preferred_element_type=F32), NEG_INF)
                 for qj, (_, i, first) in zip(q, blocks)]
            m = [jnp.max(z, axis=-1, keepdims=True) for z in s]
            p = [jnp.exp(z - mx).astype(BF16) for z, mx in zip(s, m)]
            o = [_dot(pj, jnp.concatenate([jnp.concatenate([prev(vt, i), vt[i + 1]], axis=0), ones], axis=1))
                 for pj, (_, i, _) in zip(p, blocks)]
            for j, (rows, _, _) in enumerate(blocks):
                m_scr[ci, rows, :] = jnp.where(head0, m[j][:band], m[j][band:])
                acc_scr[ci, rows, :] = jnp.where(head0, o[j][:band, :LANES], o[j][band:, :LANES])
                l_scr[ci, rows, :] = jnp.where(head0, o[j][:band, LANES:], o[j][band:, LANES:])
            return carry

        lax.fori_loop(0, n_blk // ATT_UNROLL, body, 0)

    rows_per = 256

    def merge(i, carry):
        rows = pl.ds(pl.multiple_of(i * rows_per, rows_per), rows_per)
        ms = [m_scr[ci, rows, :] for ci in range(len(DILATED_CONFIGS))]
        m_all = functools.reduce(jnp.maximum, ms)
        num = jnp.zeros((rows_per, LANES), F32)
        den = jnp.zeros((rows_per, LANES), F32)
        for ci, m_c in enumerate(ms):
            w_c = jnp.exp(m_c - m_all)
            num = num + w_c * acc_scr[ci, rows, :]
            den = den + w_c * l_scr[ci, rows, :]
        o_ref[rows, :] = num / den
        return carry

    lax.fori_loop(0, s_len // rows_per, merge, 0)


def _attn_prompt_call(q, k, v, n_seq, seq_len):
    c = q.shape[1]
    n_pairs = c // LANES
    blk = pl.BlockSpec((seq_len, LANES), lambda bi, hi: (bi, hi))
    return pl.pallas_call(
        _attn_prompt_kernel,
        grid=(n_seq, n_pairs),
        in_specs=[blk, blk, blk],
        out_specs=blk,
        out_shape=jax.ShapeDtypeStruct((n_seq * seq_len, c), F32),
        scratch_shapes=[pltpu.VMEM((len(DILATED_CONFIGS), seq_len, LANES), F32)] * 3,
        compiler_params=pltpu.CompilerParams(dimension_semantics=("arbitrary", "arbitrary"),
                                             vmem_limit_bytes=V7X_VMEM_LIMIT),
        name="attn_prompt",
    )(q, k, v)


def _attn_sample_kernel(q_ref, kn_ref, vn_ref, kc_ref, vc_ref, o_ref, *, n_new):
    hd = HEAD_DIM
    _, n_heads, _, n_buf = kc_ref.shape
    t_pad = kn_ref.shape[1]
    c = q_ref.shape[2]
    q = q_ref[0] * (hd ** -0.5)
    lane_head = lax.broadcasted_iota(jnp.int32, (1, c), 1) // hd
    qs = jnp.concatenate([jnp.where(lane_head == h, q, 0.0) for h in range(n_heads)], axis=0).astype(BF16)
    n_rows = n_heads * t_pad
    t_idx = lax.broadcasted_iota(jnp.int32, (n_rows, 1), 0) % t_pad

    def multiplicity(dist):
        mult = jnp.zeros(dist.shape, F32)
        for window, dil in DILATED_CONFIGS:
            hit = (dist >= 0) & (dist <= window) & (dist % dil == 0)
            mult = mult + jnp.where(hit, 1.0, 0.0)
        return mult

    jc = lax.broadcasted_iota(jnp.int32, (1, n_buf), 1)
    mult_c = multiplicity(n_buf + t_idx - jc)
    jn = lax.broadcasted_iota(jnp.int32, (1, t_pad), 1)
    mult_n = jnp.where(jn < n_new, multiplicity(t_idx - jn), 0.0)

    sc = jnp.concatenate(
        [_dot(qs[h * t_pad:(h + 1) * t_pad, h * hd:(h + 1) * hd], kc_ref[0, h].astype(BF16))
         for h in range(n_heads)], axis=0)
    sn = lax.dot_general(qs, kn_ref[0].astype(BF16), NT, preferred_element_type=F32)
    sc = jnp.where(mult_c > 0.0, sc, NEG_INF)
    sn = jnp.where(mult_n > 0.0, sn, NEG_INF)
    m = jnp.maximum(jnp.max(sc, axis=-1, keepdims=True), jnp.max(sn, axis=-1, keepdims=True))
    pc = (mult_c * jnp.exp(sc - m)).astype(BF16)
    pn = mult_n * jnp.exp(sn - m)
    inv_l = 1.0 / (jnp.sum(pc.astype(F32), axis=-1, keepdims=True) + jnp.sum(pn, axis=-1, keepdims=True))
    o_new = _dot(pn.astype(BF16), vn_ref[0].astype(BF16)) * inv_l
    out = jnp.zeros((t_pad, c), F32)
    for h in range(n_heads):
        out = out + jnp.where(lane_head == h, o_new[h * t_pad:(h + 1) * t_pad, :], 0.0)
    o_buf = [lax.dot_general(pc[h * t_pad:(h + 1) * t_pad, :], vc_ref[0, h].astype(BF16), NT,
                             preferred_element_type=F32) * inv_l[h * t_pad:(h + 1) * t_pad, :]
             for h in range(n_heads)]
    o_ref[0] = out + jnp.concatenate(o_buf, axis=1)


def _attn_sample_call(q, kn, vn, k_buf, v_buf, n_new):
    b, t_pad, c = q.shape
    _, n_buf, n_heads, hd = k_buf.shape
    k_t = jnp.transpose(k_buf, (0, 2, 3, 1))
    v_t = jnp.transpose(v_buf, (0, 2, 3, 1))
    new = pl.BlockSpec((1, t_pad, c), lambda bi: (bi, 0, 0))
    buf = pl.BlockSpec((1, n_heads, hd, n_buf), lambda bi: (bi, 0, 0, 0))
    return pl.pallas_call(
        functools.partial(_attn_sample_kernel, n_new=n_new),
        grid=(b,),
        in_specs=[new, new, new, buf, buf],
        out_specs=new,
        out_shape=jax.ShapeDtypeStruct((b, t_pad, c), F32),
        compiler_params=pltpu.CompilerParams(dimension_semantics=("arbitrary",), vmem_limit_bytes=V7X_VMEM_LIMIT),
        name="attn_sample",
    )(q, kn, vn, k_t, v_t)


def _route_rows(logits):
    lane = lax.broadcasted_iota(jnp.int32, logits.shape, 1)
    lane_f = lane.astype(F32)
    first = lambda hit: jnp.min(jnp.where(hit, lane_f, float(LANES)), axis=-1, keepdims=True)
    is_g = lane < N_GROUPS
    lg = jnp.where(is_g, logits, NEG_INF)
    g_max = jnp.max(lg, axis=-1, keepdims=True)
    g_idx = first(lg == g_max)
    g_w = 1.0 / jnp.sum(jnp.where(is_g, jnp.exp(lg - g_max), 0.0), axis=-1, keepdims=True)
    lo = N_GROUPS + EXPERTS_PER_GROUP * g_idx
    le = jnp.where((lane_f >= lo) & (lane_f < lo + EXPERTS_PER_GROUP), logits, NEG_INF)
    e1 = jnp.max(le, axis=-1, keepdims=True)
    i1 = first(le == e1)
    le2 = jnp.where(lane_f == i1, NEG_INF, le)
    e2 = jnp.max(le2, axis=-1, keepdims=True)
    i2 = first(le2 == e2)
    ex = jnp.exp(e2 - e1)
    gate1 = g_w / (1.0 + ex)
    gate2 = g_w * ex / (1.0 + ex)
    out = jnp.where(lane == 0, gate1, jnp.where(lane == 1, gate2, 0.0))
    out = jnp.where(lane == 2, i1 - N_GROUPS, jnp.where(lane == 3, i2 - N_GROUPS, out))
    return out


def _post_kernel(x_ref, yr_ref, ya_ref, wo_ref, gffn_ref, rw_hi_ref, rw_lo_ref, rb_ref, *rest):
    h_o, hn_o, lg_o = rest[-3:]
    c = yr_ref.shape[1]
    h = (x_ref[...] + _dot(yr_ref[...].astype(BF16), wo_ref[:c, :]) + _dot(ya_ref[...].astype(BF16), wo_ref[c:, :]))
    ms = jnp.mean(h * h, axis=-1, keepdims=True)
    hn = (h * lax.rsqrt(ms + NORM_EPS)) * gffn_ref[...]
    h_o[...] = h
    bits = pltpu.bitcast(hn.astype(BF16).astype(F32), jnp.uint32)
    half = hn.shape[1] // 2
    hn_o[...] = (bits[:, :half] >> 16) | (bits[:, half:] & jnp.uint32(0xFFFF0000))
    hi = hn.astype(BF16)
    lo = (hn - hi.astype(F32)).astype(BF16)
    logits = (_dot(hi, rw_hi_ref[...]) + _dot(hi, rw_lo_ref[...]) + _dot(lo, rw_hi_ref[...])) + rb_ref[...]
    lg_o[...] = _route_rows(logits)


def _post_call(x2, yr, ya, p, tm, total_rows, row_offset, hn_all=None):
    t, d = x2.shape
    c = yr.shape[1]
    assert row_offset % tm == 0 and t % tm == 0 and total_rows % tm == 0
    off = row_offset // tm
    last = t // tm - 1
    steps = t // tm if hn_all is not None else total_rows // tm - off
    full = lambda a: pl.BlockSpec(a.shape, lambda i: (0,) * a.ndim)
    tok = lambda w: pl.BlockSpec((tm, w), lambda i: (jnp.minimum(i, last), 0))
    weights = [p['wout'], p['gffn'], p['rw_hi'], p['rw_lo'], p['rb']]
    args = [x2, yr, ya, *weights]
    in_specs = [tok(d), tok(c), tok(c)] + [full(w) for w in weights]
    aliases = {}
    if hn_all is not None:
        aliases = {len(args): 1}
        args.append(hn_all)
        in_specs.append(pl.BlockSpec(memory_space=pl.ANY))
    return pl.pallas_call(
        _post_kernel,
        grid=(steps,),
        in_specs=in_specs,
        out_specs=[tok(d), pl.BlockSpec((tm, d // 2), lambda i: (i + off, 0)), tok(LANES)],
        out_shape=[jax.ShapeDtypeStruct((t, d), F32), jax.ShapeDtypeStruct((total_rows, d // 2), jnp.uint32),
                   jax.ShapeDtypeStruct((t, LANES), F32)],
        input_output_aliases=aliases,
        compiler_params=pltpu.CompilerParams(dimension_semantics=("arbitrary",), vmem_limit_bytes=V7X_VMEM_LIMIT),
        name="post",
    )(*args)


def _expert_kernel(be_ref, nb_ref, xs_ref, w1_ref, w3_ref, w2_ref, y_ref, w1_s, w3_s, w2_s):
    i = pl.program_id(0)
    live = i < nb_ref[0]

    @pl.when(live & ((i == 0) | (be_ref[i] != be_ref[jnp.maximum(i - 1, 0)])))
    def _():
        w1_s[...] = w1_ref[...].astype(BF16)
        w3_s[...] = w3_ref[...].astype(BF16)
        w2_s[...] = w2_ref[...].astype(BF16)

    @pl.when(live)
    def _():
        packed = xs_ref[...]
        half = packed.shape[1]
        lo = pltpu.bitcast(packed << 16, F32).astype(BF16)
        hi = pltpu.bitcast(packed & jnp.uint32(0xFFFF0000), F32).astype(BF16)
        h1 = _dot(lo, w1_s[:half, :]) + _dot(hi, w1_s[half:, :])
        h3 = _dot(lo, w3_s[:half, :]) + _dot(hi, w3_s[half:, :])
        act = (h1 * _sigmoid(h1)) * h3
        y_ref[...] = _dot(act.astype(BF16), w2_s[...])

    @pl.when(jnp.logical_not(live))
    def _():
        y_ref[...] = jnp.zeros_like(y_ref)


def _expert_call(blk_exp, n_used, xs, p, bm):
    n_slots = xs.shape[0]
    _, d, de = p['w1'].shape
    grid_spec = pltpu.PrefetchScalarGridSpec(
        num_scalar_prefetch=2,
        grid=(n_slots // bm,),
        in_specs=[pl.BlockSpec((bm, d // 2), lambda i, be, nb: (i, 0)),
                  pl.BlockSpec((None, d, de), lambda i, be, nb: (be[i], 0, 0)),
                  pl.BlockSpec((None, d, de), lambda i, be, nb: (be[i], 0, 0)),
                  pl.BlockSpec((None, de, d), lambda i, be, nb: (be[i], 0, 0))],
        out_specs=pl.BlockSpec((bm, d), lambda i, be, nb: (i, 0)),
        scratch_shapes=[pltpu.VMEM((d, de), BF16), pltpu.VMEM((d, de), BF16), pltpu.VMEM((de, d), BF16)],
    )
    return pl.pallas_call(
        _expert_kernel,
        grid_spec=grid_spec,
        out_shape=jax.ShapeDtypeStruct((n_slots, d), F32),
        compiler_params=pltpu.CompilerParams(dimension_semantics=("arbitrary",), vmem_limit_bytes=V7X_VMEM_LIMIT),
        name="experts",
    )(blk_exp, n_used, xs, p['w1'], p['w3'], p['w2'])


def _final_kernel(h_ref, route_ref, y1_ref, y2_ref, gfin_ref, *rest):
    o_ref = rest[-1]
    route = route_ref[...]
    h = h_ref[...] + (route[:, 0:1] * y1_ref[...] + route[:, 1:2] * y2_ref[...])
    ms = jnp.mean(h * h, axis=-1, keepdims=True)
    o_ref[...] = (h * lax.rsqrt(ms + NORM_EPS)) * gfin_ref[...]


def _final_call(h, route_all, y12, gfin, tm, h_lo, n_rows, route_lo, y_lo, out=None):
    d = h.shape[1]
    assert h_lo % tm == 0 and n_rows % tm == 0 and route_lo % tm == 0 and y_lo % tm == 0
    hb, rb, yb = h_lo // tm, route_lo // tm, y_lo // tm
    routed = lambda a: pl.BlockSpec((None, tm, d), lambda i: (a, i + yb, 0))
    args = [h, route_all, y12, y12, gfin]
    in_specs = [pl.BlockSpec((tm, d), lambda i: (i + hb, 0)), pl.BlockSpec((tm, LANES), lambda i: (i + rb, 0)),
                routed(0), routed(1), pl.BlockSpec((1, d), lambda i: (0, 0))]
    aliases = {}
    if out is not None:
        aliases = {len(args): 0}
        args.append(out)
        in_specs.append(pl.BlockSpec(memory_space=pl.ANY))
    return pl.pallas_call(
        _final_kernel,
        grid=(n_rows // tm,),
        in_specs=in_specs,
        out_specs=pl.BlockSpec((tm, d), lambda i: (i + hb, 0)),
        out_shape=jax.ShapeDtypeStruct(h.shape, F32),
        input_output_aliases=aliases,
        compiler_params=pltpu.CompilerParams(dimension_semantics=("arbitrary",), vmem_limit_bytes=V7X_VMEM_LIMIT),
        name="final",
    )(*args)


def _gather_chunk(d, dtype):
    return min(SC_INDEX_LIMIT, SC_GATHER_BYTES // (d * jnp.dtype(dtype).itemsize))


def _gather_multiple(d, dtype):
    return V7X_SC_VECTOR_SUBCORES * _gather_chunk(d, dtype) * 2


def _gather_rows(table, idx):
    info = plsc.get_sparse_core_info()
    nc, ns = info.num_cores, info.num_subcores
    b, d = idx.shape[0], table.shape[1]
    chunk = _gather_chunk(d, table.dtype)
    assert nc * ns == V7X_SC_VECTOR_SUBCORES and b % _gather_multiple(d, table.dtype) == 0
    per_w = b // (nc * ns)
    n_chunks = per_w // chunk
    mesh = plsc.VectorSubcoreMesh(core_axis_name="c", subcore_axis_name="s")

    @functools.partial(
        pl.kernel, mesh=mesh, out_type=jax.ShapeDtypeStruct((b, d), table.dtype),
        scratch_types=[pltpu.VMEM((per_w,), jnp.int32), pltpu.VMEM((2, chunk, d), table.dtype),
                       pltpu.SemaphoreType.DMA((2,)), pltpu.SemaphoreType.DMA((2,))])
    def gather(table_hbm, idx_hbm, out_hbm, idx_v, rows_v, fetch_sem, put_sem):
        base = (lax.axis_index("s") * nc + lax.axis_index("c")) * per_w
        pltpu.sync_copy(idx_hbm.at[pl.ds(base, per_w)], idx_v)

        def fetch(c, slot):
            off = pl.multiple_of(c * chunk, chunk)
            return pltpu.make_async_copy(table_hbm.at[idx_v.at[pl.ds(off, chunk)]], rows_v.at[slot],
                                         fetch_sem.at[slot])

        def put(c, slot):
            off = pl.multiple_of(c * chunk, chunk)
            return pltpu.make_async_copy(rows_v.at[slot], out_hbm.at[pl.ds(base + off, chunk)], put_sem.at[slot])

        fetch(0, 0).start()

        @pl.loop(0, n_chunks, step=2)
        def _(c):
            @pl.when(c > 0)
            def _():
                put(c - 1, 1).wait()
            fetch(c + 1, 1).start()
            fetch(c, 0).wait()
            put(c, 0).start()
            fetch(c + 1, 1).wait()
            put(c, 0).wait()

            @pl.when(c + 2 < n_chunks)
            def _():
                fetch(c + 2, 0).start()
            put(c + 1, 1).start()

        put(n_chunks - 1, 1).wait()

    return gather(table, idx)


def _route(route, bm, slot_multiple):
    n = route.shape[0]
    eid = route[:, 2:4].astype(jnp.int32).reshape(-1)
    m = eid.shape[0]
    experts = jnp.arange(N_EXPERTS + 1, dtype=jnp.int32)
    e_sorted, order = lax.sort_key_val(eid, jnp.arange(m, dtype=jnp.int32))
    below = jnp.sum((eid[:, None] < experts[None, :]).astype(jnp.int32), axis=0)
    starts, counts = below[:-1], below[1:] - below[:-1]
    padded = (counts + bm - 1) // bm * bm
    p_ends = jnp.cumsum(padded)
    p_starts = p_ends - padded
    shift = p_starts - starts
    dest_sorted = jnp.arange(m, dtype=jnp.int32) + jnp.sum(
        jnp.where(e_sorted[:, None] == experts[None, :-1], shift[None, :], 0), axis=1)
    _, dest = lax.sort_key_val(order, dest_sorted)
    n_blocks = _round_up((-(-m // bm) + N_EXPERTS) * bm, slot_multiple) // bm
    blk_start = jnp.arange(n_blocks, dtype=jnp.int32) * bm
    blk_exp = jnp.minimum(jnp.sum((p_ends[None, :] <= blk_start[:, None]).astype(jnp.int32), axis=1), N_EXPERTS - 1)
    pick = lambda tbl: jnp.sum(jnp.where(blk_exp[:, None] == experts[None, :-1], tbl[None, :], 0), axis=1)
    pos = (blk_start - pick(p_starts))[:, None] + jnp.arange(bm, dtype=jnp.int32)[None, :]
    valid = pos < pick(counts)[:, None]
    src = jnp.where(valid, pick(starts)[:, None] + pos, 0).reshape(-1)
    src_assign = order.at[src].get(mode='promise_in_bounds')
    valid = valid.reshape(-1)
    slot_tok = jnp.where(valid, src_assign // 2, jnp.arange(valid.shape[0], dtype=jnp.int32) % n)
    n_used = (p_ends[-1] // bm).astype(jnp.int32).reshape(1)
    return slot_tok, dest.reshape(n, 2), blk_exp.astype(jnp.int32), n_used


def _moe_and_final(groups, p):
    tm, bm = PROJ_TILE, EXPERT_TILE
    sizes = [g[0].shape[0] for g in groups]
    starts = [sum(sizes[:i]) for i in range(len(sizes))]
    total = sum(sizes)
    hs, routes, hn_all = [], [], None
    for (x2, yr, ya), lo in zip(groups, starts):
        h, hn_all, route = _post_call(x2, yr, ya, p, tm, total, lo, hn_all)
        hs.append(h)
        routes.append(route)
    route_all = jnp.concatenate(routes, axis=0)
    d = hs[0].shape[1]
    slot_tok, dest, blk_exp, n_used = _route(route_all, bm, math.lcm(bm, _gather_multiple(d // 2, jnp.uint32)))
    xs = _gather_rows(hn_all, slot_tok)
    yb = _expert_call(blk_exp, n_used, xs, p, bm)
    cut = _round_up(total // 2, math.lcm(tm, _gather_multiple(d, F32) // 2))
    windows = [(0, cut), (cut, total)]
    y12 = [_gather_rows(yb, dest[lo:hi].T.reshape(-1)).reshape(2, hi - lo, d) for lo, hi in windows]
    outs = [None] * len(groups)
    for (w_lo, w_hi), y_w in zip(windows, y12):
        for gi, (g_lo, size) in enumerate(zip(starts, sizes)):
            lo, hi = max(w_lo, g_lo), min(w_hi, g_lo + size)
            if lo < hi:
                outs[gi] = _final_call(hs[gi], route_all, y_w, p['gfin'], tm, lo - g_lo, hi - lo, lo, lo - w_lo,
                                       outs[gi])
    return outs


def _prep_params(layer, norm_mix_g, w_in, rwkv_mu_rkv, rwkv_mu_wag, rwkv_w0, rwkv_w1, rwkv_w2, rwkv_a0, rwkv_a1,
                 rwkv_a2, rwkv_g1, rwkv_g2, rwkv_k_k, rwkv_k_a, rwkv_r_k, rwkv_ln_w, rwkv_ln_b, w_out, norm_ffn_g,
                 router_group_w, router_group_b, router_expert_w, router_expert_b, expert_w1, expert_w3, expert_w2,
                 norm_final_g):
    d = w_in.shape[1]
    c = rwkv_w0.shape[1]
    row = lambda a: a.reshape(1, -1).astype(F32)
    lowrank = jnp.concatenate([rwkv_w1[layer], rwkv_a1[layer], rwkv_g1[layer]], axis=1)
    mx = rwkv_mu_wag[layer]
    r_w = rwkv_w1.shape[2]
    r_a = rwkv_a1.shape[2]
    r_g = rwkv_g1.shape[2]
    assert r_w + r_a == LANES and r_g == LANES
    mx_cols = jnp.concatenate([jnp.broadcast_to(mx[0][:, None], (d, r_w)), jnp.broadcast_to(mx[1][:, None], (d, r_a)),
                               jnp.broadcast_to(mx[2][:, None], (d, r_g))], axis=1)
    w2a2 = jnp.zeros((LANES, 2 * c), F32)
    w2a2 = w2a2.at[:r_w, :c].set(rwkv_w2[layer]).at[r_w:, c:].set(rwkv_a2[layer])
    head = jnp.arange(c) // HEAD_DIM
    rw = jnp.zeros((d, LANES), F32)
    rw = rw.at[:, :N_GROUPS].set(router_group_w[layer]).at[:, N_GROUPS:N_GROUPS + N_EXPERTS].set(router_expert_w[layer])
    rw_hi = rw.astype(BF16)
    rb = jnp.zeros((1, LANES), F32)
    rb = rb.at[0, :N_GROUPS].set(router_group_b[layer]).at[0, N_GROUPS:N_GROUPS + N_EXPERTS].set(router_expert_b[layer])
    return {
        'c_rwkv': c,
        'gmix': row(norm_mix_g[layer]),
        'win': jnp.concatenate([w_in[layer], lowrank], axis=1).astype(BF16),
        'wdx': (mx_cols * lowrank).astype(BF16),
        'mu_rkv': rwkv_mu_rkv[layer],
        'w0a0': jnp.concatenate([row(rwkv_w0[layer]), row(rwkv_a0[layer])], axis=1),
        'w2a2': w2a2.astype(BF16),
        'g2': rwkv_g2[layer].astype(BF16),
        'k_k': row(rwkv_k_k[layer]),
        'k_a': row(rwkv_k_a[layer]),
        'seg': (head[:, None] == head[None, :]).astype(BF16),
        'r_k': row(rwkv_r_k[layer]),
        'ln_w': row(rwkv_ln_w[layer]),
        'ln_b': row(rwkv_ln_b[layer]),
        'wout': w_out[layer].astype(BF16),
        'gffn': row(norm_ffn_g[layer]),
        'rw_hi': rw_hi,
        'rw_lo': (rw - rw_hi.astype(F32)).astype(BF16),
        'rb': rb,
        'w1': expert_w1[layer],
        'w3': expert_w3[layer],
        'w2': expert_w2[layer],
        'gfin': row(norm_final_g),
    }


def _prompt_mix(x, p):
    b, s, d = x.shape
    c = p['c_rwkv']
    x2 = x.reshape(b * s, d)
    tm = PROJ_TILE
    flag = jnp.zeros((b * s, 1), F32)
    outs = _proj_call(x2, flag, p, s // tm, tm)
    r, lw, k, v, kk, bb, g, qa, ka, va, xl = outs
    s0 = jnp.zeros((b, c // HEAD_DIM, HEAD_DIM, HEAD_DIM), F32)
    yr, s_new = _rwkv_call((r, lw, k, v, kk, bb, g), s0, p, b, s, RWKV_CHUNK, 4)
    ya = _attn_prompt_call(qa, ka, va, b, s)
    shift = xl.reshape(b, s // tm, 8, d)[:, -1, 7, :]
    keep = min(max(w for w, _ in DILATED_CONFIGS), s)
    k_keep = ka.reshape(b, s, c // HEAD_DIM, HEAD_DIM)[:, s - keep:]
    v_keep = va.reshape(b, s, c // HEAD_DIM, HEAD_DIM)[:, s - keep:]
    return (x2, yr, ya), (s_new, shift, k_keep, v_keep)


def _sample_mix(x, shift0, s0, k_buf, v_buf, p):
    b, t, d = x.shape
    c = p['c_rwkv']
    n_heads = c // HEAD_DIM
    t_pad = 8
    xc = jnp.concatenate([shift0[:, None, :], x, jnp.zeros((b, t_pad - 1 - t, d), x.dtype)], axis=1)
    flag = jnp.zeros((b, t_pad, 1), F32).at[:, 0].set(1.0)
    outs = _proj_call(xc.reshape(b * t_pad, d), flag.reshape(b * t_pad, 1), p, 1, b * t_pad, xl_rows=b * t_pad)
    xl = outs[10]
    live = (jnp.arange(t_pad) < t)[None, :, None]
    shifted = [jnp.where(live, jnp.roll(o.reshape(b, t_pad, c), -1, axis=1), 0.0) for o in outs[:10]]
    r, lw, k, v, kk, bb, g, qa, ka, va = shifted
    flat = lambda z: z.reshape(b * t_pad, c)
    yr, s_new = _rwkv_call(tuple(flat(z) for z in (r, lw, k, v, kk, bb, g)), s0, p, b, t_pad, t_pad, 8)
    ya = _attn_sample_call(qa, ka, va, k_buf, v_buf, t)
    x_pad = jnp.concatenate([x, jnp.zeros((b, t_pad - t, d), x.dtype)], axis=1).reshape(b * t_pad, d)
    shift = xl.reshape(b, t_pad, d)[:, t]
    return ((x_pad, yr, flat(ya)),
            (s_new, shift, ka[:, :t].reshape(b, t, n_heads, HEAD_DIM), va[:, :t].reshape(b, t, n_heads, HEAD_DIM)))


def kernel(x_prompt, x_sample, state_rwkv, state_shift, cache_att_k, cache_att_v, norm_mix_g, w_in, rwkv_mu_rkv, rwkv_mu_wag, rwkv_w0, rwkv_w1, rwkv_w2, rwkv_a0, rwkv_a1, rwkv_a2, rwkv_g1, rwkv_g2, rwkv_k_k, rwkv_k_a, rwkv_r_k, rwkv_ln_w, rwkv_ln_b, w_out, norm_ffn_g, router_group_w, router_group_b, router_expert_w, router_expert_b, expert_w1, expert_w3, expert_w2, norm_final_g):
    assert w_in.shape[0] == 1, "single-layer trunk"
    p = _prep_params(0, norm_mix_g, w_in, rwkv_mu_rkv, rwkv_mu_wag, rwkv_w0, rwkv_w1, rwkv_w2, rwkv_a0, rwkv_a1,
                     rwkv_a2, rwkv_g1, rwkv_g2, rwkv_k_k, rwkv_k_a, rwkv_r_k, rwkv_ln_w, rwkv_ln_b, w_out,
                     norm_ffn_g, router_group_w, router_group_b, router_expert_w, router_expert_b, expert_w1,
                     expert_w3, expert_w2, norm_final_g)
    moe_p, (rw_p, sh_p, kc_p, vc_p) = _prompt_mix(x_prompt, p)
    moe_s, (rw_s, sh_s, kc_s, vc_s) = _sample_mix(x_sample, state_shift[0], state_rwkv[0], cache_att_k[0],
                                                   cache_att_v[0], p)
    y_p, y_s = _moe_and_final([moe_p, moe_s], p)
    y_p = y_p.reshape(x_prompt.shape)
    y_s = y_s.reshape(x_sample.shape[0], -1, x_sample.shape[2])[:, :x_sample.shape[1]]
    return (y_p, y_s, rw_p[None], sh_p[None], kc_p[None], vc_p[None], rw_s[None], sh_s[None], kc_s[None], vc_s[None])
```

```python
import functools
import math

import jax
import jax.numpy as jnp
from jax import lax
from jax.experimental import pallas as pl
from jax.experimental.pallas import tpu as pltpu
from jax.experimental.pallas import tpu_sc as plsc

F32 = jnp.float32
BF16 = jnp.bfloat16

HEAD_DIM = 64
GN_EPS = 64e-5
NORM_EPS = 1e-6
DILATED_CONFIGS = ((128, 1), (512, 4), (2048, 16))
N_GROUPS = 4
EXPERTS_PER_GROUP = 8
N_EXPERTS = N_GROUPS * EXPERTS_PER_GROUP
NEG_INF = -1e30

V7X_VMEM_LIMIT = 56 * 1024 * 1024
LANES = 128

PROJ_TILE = 512
RWKV_CHUNK = 64
ATT_BAND = 128
EXPERT_TILE = 512
ATT_UNROLL = {1: 8, 4: 8, 16: 4}
SC_GATHER_BYTES = 128 * 1024
SC_INDEX_LIMIT = 128
SLOT_MULTIPLE = 4096
SAMPLE_EXPERT_TILE = 128

HIGHEST = lax.Precision.HIGHEST
NN = (((1,), (0,)), ((), ()))
NT = (((1,), (1,)), ((), ()))
TN = (((0,), (0,)), ((), ()))


def _dot(a, b, precision=None):
    return jnp.dot(a, b, preferred_element_type=F32, precision=precision)


def _dot_split(a, b_bf16):
    hi = a.astype(BF16)
    lo = (a - hi.astype(F32)).astype(BF16)
    return _dot(hi, b_bf16) + _dot(lo, b_bf16)


def _mm(a, b, dims, mode):
    if mode == 'f32':
        return lax.dot_general(a, b, dims, precision=HIGHEST, preferred_element_type=F32)
    a_hi = a.astype(BF16)
    b_hi = b.astype(BF16)
    out = lax.dot_general(a_hi, b_hi, dims, preferred_element_type=F32)
    if mode == 'x3':
        a_lo = (a - a_hi.astype(F32)).astype(BF16)
        b_lo = (b - b_hi.astype(F32)).astype(BF16)
        out = out + lax.dot_general(a_hi, b_lo, dims, preferred_element_type=F32)
        out = out + lax.dot_general(a_lo, b_hi, dims, preferred_element_type=F32)
    return out


RWKV_MODES = dict(A='bf16', AV='bf16', SQ='bf16', AP='bf16', RB='bf16', RK='bf16', WS='bf16', UP='bf16')


def _pre(x, mode):
    return x.astype(BF16) if mode == 'bf16' else x


def _round_up(x, k):
    return -(-x // k) * k


def _sigmoid(z):
    return 1.0 / (1.0 + jnp.exp(-z))


def _proj_kernel(x_ref, flag_ref, gmix_ref, win_ref, wdx_ref, mu_ref, w0a0_ref, w2a2_ref, g2_ref,
                 kk_ref, ka_ref, seg_ref,
                 r_o, lw_o, k_o, v_o, kkn_o, b_o, g_o, qa_o, kat_o, vat_o, xl_o, *rest,
                 tiles_per_seq, c_rwkv, keep_tiles):
    xn_carry, pj_carry = rest[-2:]
    i = pl.program_id(0)

    @pl.when(i % tiles_per_seq == 0)
    def _():
        xn_carry[...] = jnp.zeros_like(xn_carry)
        pj_carry[...] = jnp.zeros_like(pj_carry)

    c = c_rwkv
    x = x_ref[...]
    tm = x.shape[0]
    ms = jnp.mean(x * x, axis=-1, keepdims=True)
    xn = (x * lax.rsqrt(ms + NORM_EPS)) * gmix_ref[...]
    xn = jnp.where(flag_ref[...] > 0.0, x, xn)
    row = lax.broadcasted_iota(jnp.int32, (tm, 1), 0)
    xn_prev = jnp.where(row == 0, xn_carry[7:8, :], pltpu.roll(xn, 1, axis=0))
    dx = xn_prev - xn

    proj = _dot(xn.astype(BF16), win_ref[...])
    cur = proj[:, :3 * c]
    prev = jnp.where(row == 0, pj_carry[7:8, :], pltpu.roll(cur, 1, axis=0))
    xn_carry[...] = xn[tm - 8:, :]
    pj_carry[...] = cur[tm - 8:, :]
    xl_rows = xl_o.shape[1]
    xl_o[0] = xn[tm - xl_rows:, :]

    mu = mu_ref[...]
    r = cur[:, :c] + mu[0:1] * (prev[:, :c] - cur[:, :c])
    k = cur[:, c:2 * c] + mu[1:2] * (prev[:, c:2 * c] - cur[:, c:2 * c])
    v = cur[:, 2 * c:3 * c] + mu[2:3] * (prev[:, 2 * c:3 * c] - cur[:, 2 * c:3 * c])

    lr = proj[:, 6 * c:] + _dot(dx.astype(BF16), wdx_ref[...])
    lane = lax.broadcasted_iota(jnp.int32, (1, LANES), 1)
    wa_in = jnp.where(lane < 64, jnp.tanh(lr[:, :LANES]), lr[:, :LANES])
    wa = _dot(wa_in.astype(BF16), w2a2_ref[...]) + w0a0_ref[...]
    z = -wa[:, :c]
    softplus = jnp.maximum(z, 0.0) + jnp.log1p(jnp.exp(-jnp.abs(z)))
    lw = -jnp.exp(-softplus - 0.5)
    a = _sigmoid(wa[:, c:])
    g = _dot(_sigmoid(lr[:, LANES:]).astype(BF16), g2_ref[...])

    kk = k * kk_ref[...]
    ss = _dot_split(kk * kk, seg_ref[...])
    kk = kk * lax.rsqrt(jnp.maximum(ss, 1e-24))

    r_o[...] = r
    lw_o[...] = lw
    k_o[...] = k * (1.0 + (a - 1.0) * ka_ref[...])
    v_o[...] = v
    kkn_o[...] = kk
    b_o[...] = kk * a
    g_o[...] = g
    qa_o[...] = proj[:, 3 * c:4 * c]
    kat_o[...] = proj[:, 4 * c:5 * c]
    vat_o[...] = proj[:, 5 * c:6 * c]
    if keep_tiles:
        kt_o, vt_o = rest[:2]

        @pl.when(i % tiles_per_seq >= tiles_per_seq - keep_tiles)
        def _():
            kt_o[...] = proj[:, 4 * c:5 * c].T
            vt_o[...] = proj[:, 5 * c:6 * c].T


def _proj_call(x2, flag, p, tiles_per_seq, tm, xl_rows=8, keep_tiles=0):
    t, d = x2.shape
    c = p['c_rwkv']
    n_tiles = t // tm
    n_seq = n_tiles // tiles_per_seq
    first = tiles_per_seq - keep_tiles
    kept = pl.BlockSpec((None, c, tm), lambda i: (i // tiles_per_seq, 0, jnp.maximum(i % tiles_per_seq - first, 0)))
    kept_specs = [kept, kept] if keep_tiles else []
    kept_shapes = [jax.ShapeDtypeStruct((n_seq, c, keep_tiles * tm), F32)] * 2 if keep_tiles else []
    full = lambda a: pl.BlockSpec(a.shape, lambda i: (0,) * a.ndim, pipeline_mode=pl.Buffered(1))
    tok = lambda w: pl.BlockSpec((tm, w), lambda i: (i, 0))
    weights = [p['gmix'], p['win'], p['wdx'], p['mu_rkv'], p['w0a0'], p['w2a2'], p['g2'], p['k_k'], p['k_a'], p['seg']]
    outs = pl.pallas_call(
        functools.partial(_proj_kernel, tiles_per_seq=tiles_per_seq, c_rwkv=c, keep_tiles=keep_tiles),
        grid=(n_tiles,),
        in_specs=[tok(d), tok(1)] + [full(w) for w in weights],
        out_specs=[tok(c)] * 10 + [pl.BlockSpec((1, xl_rows, d), lambda i: (i, 0, 0))] + kept_specs,
        out_shape=([jax.ShapeDtypeStruct((t, c), F32)] * 10 + [jax.ShapeDtypeStruct((n_tiles, xl_rows, d), F32)]
                   + kept_shapes),
        scratch_shapes=[pltpu.VMEM((8, d), F32), pltpu.VMEM((8, 3 * c), F32)],
        compiler_params=pltpu.CompilerParams(dimension_semantics=("arbitrary",), vmem_limit_bytes=V7X_VMEM_LIMIT),
        name="proj",
    )(x2, flag, *weights)
    return outs


GROUP_LANES = 256
GROUP_HEADS = GROUP_LANES // HEAD_DIM


def _rwkv_kernel(r_ref, lw_ref, k_ref, v_ref, kk_ref, b_ref, g_ref, s0_ref, rk_ref, lnw_ref, lnb_ref, seg_ref,
                 y_ref, sout_ref, s_scr):
    ci = pl.program_id(1)
    nb, L, c = r_ref.shape
    gw, gh, hd = GROUP_LANES, GROUP_HEADS, HEAD_DIM
    n_groups = c // gw
    md = RWKV_MODES

    lane_head = lax.broadcasted_iota(jnp.int32, (1, gw), 1) // hd
    head_masks = [lane_head == j for j in range(gh)]
    bd_state = (lax.broadcasted_iota(jnp.int32, (gw, gw), 0) // hd) == (lax.broadcasted_iota(jnp.int32, (gw, gw), 1) // hd)
    bd_time = (lax.broadcasted_iota(jnp.int32, (gh * L, gh * L), 0) // L) == (lax.broadcasted_iota(jnp.int32, (gh * L, gh * L), 1) // L)
    t_row = lax.broadcasted_iota(jnp.int32, (L, gh * L), 0)
    t_col = lax.broadcasted_iota(jnp.int32, (L, gh * L), 1) % L
    strict4 = t_row > t_col
    incl4 = t_row >= t_col
    incl = lax.broadcasted_iota(jnp.int32, (L, L), 0) >= lax.broadcasted_iota(jnp.int32, (L, L), 1)

    def stack(x):
        return jnp.concatenate([jnp.where(m, x, jnp.zeros_like(x)) for m in head_masks], axis=0)

    def block_diag(n):
        tiled = jnp.concatenate([n] * gh, axis=0)
        return jnp.where(bd_time, tiled, jnp.zeros_like(tiled))

    @pl.when(ci == 0)
    def _():
        for bi in range(nb):
            for gi in range(n_groups):
                s_in = s0_ref[bi, gi * gh:(gi + 1) * gh].reshape(gw, hd)
                s_scr[bi, gi] = jnp.where(bd_state, jnp.concatenate([s_in] * gh, axis=1), 0.0)

    n_apply = max(1, int(math.log2(L)))
    seg = seg_ref[...]
    pre = []
    for bi in range(nb):
        lw = lw_ref[bi]
        lw_hi = lw.astype(BF16)
        lw_r = lw - lw_hi.astype(F32)
        lw_mid = lw_r.astype(BF16)
        lw_lo = (lw_r - lw_mid.astype(F32)).astype(BF16)
        cs3 = _dot(incl.astype(BF16), jnp.concatenate([lw_hi, lw_mid, lw_lo], axis=1))
        cs = cs3[:, :c] + cs3[:, c:2 * c] + cs3[:, 2 * c:]
        cp = cs - lw
        cm = cs[L // 2 - 1:L // 2, :]
        c_last = cs[L - 1:L, :]
        r, k, v, kk, b = r_ref[bi], k_ref[bi], v_ref[bi], kk_ref[bi], b_ref[bi]
        e_dn = jnp.exp(cm - cs)
        e_l = jnp.exp(c_last - cs)
        pre.append(dict(v=v, rt=r * jnp.exp(cs - cm), kkt=kk * jnp.exp(cp - cm), bt=b * e_dn, kt=k * e_dn,
                        kg=kk * jnp.exp(cp), rg=r * jnp.exp(cs), bh=b * e_l, kh=k * e_l, g_last=jnp.exp(c_last),
                        rkk=r * k * rk_ref[...]))

    chains = [(bi, gi) for bi in range(nb) for gi in range(n_groups)]
    col = lambda bi, gi, name: pre[bi][name][:, gi * gw:(gi + 1) * gw]
    each = lambda fn: [fn(i, bi, gi) for i, (bi, gi) in enumerate(chains)]

    vg = each(lambda i, bi, gi: col(bi, gi, 'v'))
    v_st = each(lambda i, bi, gi: stack(_pre(vg[i], md['AV'])))
    a_all = each(lambda i, bi, gi: _mm(
        jnp.concatenate([col(bi, gi, 'kkt'), col(bi, gi, 'rt')], axis=0),
        jnp.concatenate([stack(_pre(col(bi, gi, 'bt'), md['A'])), stack(_pre(col(bi, gi, 'kt'), md['A']))], axis=0),
        NT, md['A']))
    p_ak = each(lambda i, bi, gi: jnp.where(strict4, a_all[i][:L, gh * L:], 0.0))
    p_rb = each(lambda i, bi, gi: jnp.where(incl4, a_all[i][L:, :gh * L], 0.0))
    p_rk = each(lambda i, bi, gi: jnp.where(incl4, a_all[i][L:, gh * L:], 0.0))
    eye4 = (t_row == t_col).astype(F32)
    nm = each(lambda i, bi, gi: -jnp.where(strict4, a_all[i][:L, :gh * L], 0.0))
    t_inv = [eye4 + n for n in nm]
    for it in range(n_apply - 1):
        lhs = nm if it == 0 else [jnp.concatenate([n, t], axis=0) for n, t in zip(nm, t_inv)]
        both = each(lambda i, bi, gi: _mm(lhs[i], block_diag(_pre(nm[i], md['SQ'])), NN, md['SQ']))
        if it > 0:
            t_inv = [t + bo[L:] for t, bo in zip(t_inv, both)]
        nm = [bo[:L] for bo in both]
    t_inv = each(lambda i, bi, gi: t_inv[i] + _mm(t_inv[i], block_diag(_pre(nm[i], md['SQ'])), NN, md['SQ']))
    av = each(lambda i, bi, gi: _mm(p_ak[i], v_st[i], NN, md['AV']))
    x = each(lambda i, bi, gi: _mm(
        t_inv[i], jnp.concatenate([stack(_pre(col(bi, gi, 'kg'), md['AP'])), stack(_pre(av[i], md['AP']))], axis=1),
        NN, md['AP']))
    w_m = [xi[:, :gw] for xi in x]
    u0 = [-xi[:, gw:] for xi in x]
    rbw = each(lambda i, bi, gi: _mm(
        p_rb[i], jnp.concatenate([stack(_pre(w_m[i], md['RB'])), stack(_pre(u0[i], md['RB']))], axis=1),
        NN, md['RB']))
    rkv = each(lambda i, bi, gi: _mm(p_rk[i], v_st[i], NN, md['RK']))
    s_old = each(lambda i, bi, gi: s_scr[bi, gi])
    ws = each(lambda i, bi, gi: _mm(
        jnp.concatenate([w_m[i], col(bi, gi, 'rg') - rbw[i][:, :gw]], axis=0), s_old[i], NT, md['WS']))
    u = each(lambda i, bi, gi: u0[i] - ws[i][:L])
    y = each(lambda i, bi, gi: ws[i][L:] + rbw[i][:, gw:] + rkv[i])
    upd = each(lambda i, bi, gi: _mm(
        jnp.concatenate([u[i], vg[i]], axis=0),
        jnp.concatenate([col(bi, gi, 'bh'), col(bi, gi, 'kh')], axis=0), TN, md['UP']))
    for i, (bi, gi) in enumerate(chains):
        s_scr[bi, gi] = s_old[i] * col(bi, gi, 'g_last') + jnp.where(bd_state, upd[i], 0.0)

    inv = 1.0 / hd
    n_ch = len(chains)
    sums = _dot_split(jnp.concatenate(y + each(lambda i, bi, gi: col(bi, gi, 'rkk')), axis=0), seg)
    mean = [sums[i * L:(i + 1) * L] * inv for i in range(n_ch)]
    bonus = [sums[(n_ch + i) * L:(n_ch + i + 1) * L] * vg[i] for i in range(n_ch)]
    yc = [y[i] - mean[i] for i in range(n_ch)]
    sq = _dot_split(jnp.concatenate([z * z for z in yc], axis=0), seg)
    var = [sq[i * L:(i + 1) * L] * inv for i in range(n_ch)]
    for i, (bi, gi) in enumerate(chains):
        sl = slice(gi * gw, (gi + 1) * gw)
        yn = yc[i] * lax.rsqrt(var[i] + GN_EPS) * lnw_ref[:, sl] + lnb_ref[:, sl]
        y_ref[bi, :, sl] = (yn + bonus[i]) * g_ref[bi, :, sl]

    @pl.when(ci == pl.num_programs(1) - 1)
    def _():
        for bi in range(nb):
            for gi in range(n_groups):
                bd = s_scr[bi, gi]
                folded = bd[:, 0:hd]
                for j in range(1, gh):
                    folded = folded + bd[:, j * hd:(j + 1) * hd]
                sout_ref[bi, gi * gh:(gi + 1) * gh] = folded.reshape(gh, hd, hd)


def _rwkv_call(vecs, s0, p, n_seq, seq_len, chunk, nb):
    c = p['c_rwkv']
    n_heads = c // HEAD_DIM
    n_chunks = seq_len // chunk
    assert n_seq % nb == 0 and seq_len % chunk == 0
    vecs = [z.reshape(n_seq, seq_len, c) for z in vecs]
    tok = pl.BlockSpec((nb, chunk, c), lambda bi, ci: (bi, ci, 0))
    st = pl.BlockSpec((nb, n_heads, HEAD_DIM, HEAD_DIM), lambda bi, ci: (bi, 0, 0, 0))
    rowvec = pl.BlockSpec((1, c), lambda bi, ci: (0, 0))
    seg = p['seg'][:GROUP_LANES, :GROUP_LANES]
    y, s_out = pl.pallas_call(
        _rwkv_kernel,
        grid=(n_seq // nb, n_chunks),
        in_specs=[tok] * 7 + [st, rowvec, rowvec, rowvec, pl.BlockSpec(seg.shape, lambda bi, ci: (0, 0))],
        out_specs=[tok, st],
        out_shape=[jax.ShapeDtypeStruct((n_seq, seq_len, c), F32),
                   jax.ShapeDtypeStruct((n_seq, n_heads, HEAD_DIM, HEAD_DIM), F32)],
        scratch_shapes=[pltpu.VMEM((nb, c // GROUP_LANES, GROUP_LANES, GROUP_LANES), F32)],
        compiler_params=pltpu.CompilerParams(dimension_semantics=("arbitrary", "arbitrary"),
                                             vmem_limit_bytes=V7X_VMEM_LIMIT),
        name="rwkv",
    )(*vecs, s0, p['r_k'], p['ln_w'], p['ln_b'], seg)
    return y.reshape(n_seq * seq_len, c), s_out


def _attn_prompt_kernel(q_ref, k_ref, v_ref, o_ref, m_scr, l_scr, acc_scr):
    s_len = q_ref.shape[0]
    band = ATT_BAND
    n_blk = s_len // band

    lane = lax.broadcasted_iota(jnp.int32, (1, LANES), 1)
    head0 = lane < HEAD_DIM
    qi = lax.broadcasted_iota(jnp.int32, (band, 2 * band), 0)
    kj = lax.broadcasted_iota(jnp.int32, (band, 2 * band), 1)
    in_band = (kj >= qi) & (kj <= qi + band)
    in_band2 = jnp.concatenate([in_band, in_band], axis=0)
    kj2 = jnp.concatenate([kj, kj], axis=0)
    scale = HEAD_DIM ** -0.5
    ones = jnp.ones((2 * band, LANES), BF16)

    for ci, (window, dil) in enumerate(DILATED_CONFIGS):
        assert window // dil == band
        per_res = n_blk // dil

        unroll = ATT_UNROLL[dil]
        run = min(unroll, per_res)

        def body(it, carry, ci=ci, dil=dil, per_res=per_res, run=run, unroll=unroll):
            span = band * dil
            tile = lambda start: pl.ds(start, band, stride=dil) if dil > 1 else pl.ds(start, band)
            blocks = []
            tiles = []
            first_dyn = []
            for r in range(unroll // run):
                i0 = it * unroll + r * run
                blk0 = i0 % per_res
                start0 = i0 // per_res + blk0 * span
                base = len(tiles)
                if per_res > run:
                    tiles.append(tile(jnp.maximum(start0 - span, 0)))
                    first_dyn.append(blk0 == 0)
                else:
                    tiles.append(None)
                    first_dyn.append(None)
                for t in range(run):
                    tiles.append(tile(start0 + t * span))
                    blocks.append((tiles[-1], base + t, r if t == 0 else None))
            kt = [None if w is None else k_ref[w, :].astype(BF16) for w in tiles]
            vt = [None if w is None else v_ref[w, :].astype(BF16) for w in tiles]
            prev = lambda ts, i: ts[i + 1] if ts[i] is None else ts[i]
            q = [q_ref[rows, :] * scale for rows, _, _ in blocks]

            def mask_of(first):
                if first is None:
                    return in_band2
                if first_dyn[first] is None:
                    return in_band2 & (kj2 >= band)
                return in_band2 & (kj2 >= jnp.where(first_dyn[first], band, 0))

            s = [jnp.where(mask_of(first), lax.dot_general(
                jnp.concatenate([jnp.where(head0, qj, 0.0), jnp.where(head0, 0.0, qj)], axis=0).astype(BF16),
                jnp.concatenate([prev(kt, i), kt[i + 1]], axis=0), NT, preferred_element_type=F32), NEG_INF)
                 for qj, (_, i, first) in zip(q, blocks)]
            m = [jnp.max(z, axis=-1, keepdims=True) for z in s]
            p = [jnp.exp(z - mx).astype(BF16) for z, mx in zip(s, m)]
            o = [_dot(pj, jnp.concatenate([jnp.concatenate([prev(vt, i), vt[i + 1]], axis=0), ones], axis=1))
                 for pj, (_, i, _) in zip(p, blocks)]
            for j, (rows, _, _) in enumerate(blocks):
                m_scr[ci, rows, :] = jnp.where(head0, m[j][:band], m[j][band:])
                acc_scr[ci, rows, :] = jnp.where(head0, o[j][:band, :LANES], o[j][band:, :LANES])
                l_scr[ci, rows, :] = jnp.where(head0, o[j][:band, LANES:], o[j][band:, LANES:])
            return carry

        lax.fori_loop(0, n_blk // unroll, body, 0)

    rows_per = 256

    def merge(i, carry):
        rows = pl.ds(pl.multiple_of(i * rows_per, rows_per), rows_per)
        ms = [m_scr[ci, rows, :] for ci in range(len(DILATED_CONFIGS))]
        m_all = functools.reduce(jnp.maximum, ms)
        num = jnp.zeros((rows_per, LANES), F32)
        den = jnp.zeros((rows_per, LANES), F32)
        for ci, m_c in enumerate(ms):
            w_c = jnp.exp(m_c - m_all)
            num = num + w_c * acc_scr[ci, rows, :]
            den = den + w_c * l_scr[ci, rows, :]
        o_ref[rows, :] = num / den
        return carry

    lax.fori_loop(0, s_len // rows_per, merge, 0)


def _attn_prompt_call(q, k, v, n_seq, seq_len):
    c = q.shape[1]
    n_pairs = c // LANES
    blk = pl.BlockSpec((seq_len, LANES), lambda bi, hi: (bi, hi))
    return pl.pallas_call(
        _attn_prompt_kernel,
        grid=(n_seq, n_pairs),
        in_specs=[blk, blk, blk],
        out_specs=blk,
        out_shape=jax.ShapeDtypeStruct((n_seq * seq_len, c), F32),
        scratch_shapes=[pltpu.VMEM((len(DILATED_CONFIGS), seq_len, LANES), F32)] * 3,
        compiler_params=pltpu.CompilerParams(dimension_semantics=("arbitrary", "arbitrary"),
                                             vmem_limit_bytes=V7X_VMEM_LIMIT),
        name="attn_prompt",
    )(q, k, v)


def _attn_sample_kernel(q_ref, kn_ref, vn_ref, kc_ref, vc_ref, o_ref, *, n_new):
    hd = HEAD_DIM
    _, n_heads, _, n_buf = kc_ref.shape
    t_pad = kn_ref.shape[1]
    c = q_ref.shape[2]
    q = q_ref[0] * (hd ** -0.5)
    lane_head = lax.broadcasted_iota(jnp.int32, (1, c), 1) // hd
    qs = jnp.concatenate([jnp.where(lane_head == h, q, 0.0) for h in range(n_heads)], axis=0).astype(BF16)
    n_rows = n_heads * t_pad
    t_idx = lax.broadcasted_iota(jnp.int32, (n_rows, 1), 0) % t_pad

    def multiplicity(dist):
        mult = jnp.zeros(dist.shape, F32)
        for window, dil in DILATED_CONFIGS:
            hit = (dist >= 0) & (dist <= window) & (dist % dil == 0)
            mult = mult + jnp.where(hit, 1.0, 0.0)
        return mult

    jc = lax.broadcasted_iota(jnp.int32, (1, n_buf), 1)
    mult_c = multiplicity(n_buf + t_idx - jc)
    jn = lax.broadcasted_iota(jnp.int32, (1, t_pad), 1)
    mult_n = jnp.where(jn < n_new, multiplicity(t_idx - jn), 0.0)

    sc = jnp.concatenate(
        [_dot(qs[h * t_pad:(h + 1) * t_pad, h * hd:(h + 1) * hd], kc_ref[0, h].astype(BF16))
         for h in range(n_heads)], axis=0)
    sn = lax.dot_general(qs, kn_ref[0].astype(BF16), NT, preferred_element_type=F32)
    sc = jnp.where(mult_c > 0.0, sc, NEG_INF)
    sn = jnp.where(mult_n > 0.0, sn, NEG_INF)
    m = jnp.maximum(jnp.max(sc, axis=-1, keepdims=True), jnp.max(sn, axis=-1, keepdims=True))
    pc = (mult_c * jnp.exp(sc - m)).astype(BF16)
    pn = mult_n * jnp.exp(sn - m)
    inv_l = 1.0 / (jnp.sum(pc.astype(F32), axis=-1, keepdims=True) + jnp.sum(pn, axis=-1, keepdims=True))
    o_new = _dot(pn.astype(BF16), vn_ref[0].astype(BF16)) * inv_l
    out = jnp.zeros((t_pad, c), F32)
    for h in range(n_heads):
        out = out + jnp.where(lane_head == h, o_new[h * t_pad:(h + 1) * t_pad, :], 0.0)
    o_buf = [lax.dot_general(pc[h * t_pad:(h + 1) * t_pad, :], vc_ref[0, h].astype(BF16), NT,
                             preferred_element_type=F32) * inv_l[h * t_pad:(h + 1) * t_pad, :]
             for h in range(n_heads)]
    o_ref[0] = out + jnp.concatenate(o_buf, axis=1)


def _attn_sample_call(q, kn, vn, k_buf, v_buf, n_new):
    b, t_pad, c = q.shape
    _, n_buf, n_heads, hd = k_buf.shape
    k_t = jnp.transpose(k_buf, (0, 2, 3, 1))
    v_t = jnp.transpose(v_buf, (0, 2, 3, 1))
    new = pl.BlockSpec((1, t_pad, c), lambda bi: (bi, 0, 0))
    buf = pl.BlockSpec((1, n_heads, hd, n_buf), lambda bi: (bi, 0, 0, 0))
    return pl.pallas_call(
        functools.partial(_attn_sample_kernel, n_new=n_new),
        grid=(b,),
        in_specs=[new, new, new, buf, buf],
        out_specs=new,
        out_shape=jax.ShapeDtypeStruct((b, t_pad, c), F32),
        compiler_params=pltpu.CompilerParams(dimension_semantics=("arbitrary",), vmem_limit_bytes=V7X_VMEM_LIMIT),
        name="attn_sample",
    )(q, kn, vn, k_t, v_t)


def _route_rows(logits, seen):
    lane = lax.broadcasted_iota(jnp.int32, logits.shape, 1)
    lane_f = lane.astype(F32)
    first = lambda hit: jnp.min(jnp.where(hit, lane_f, float(LANES)), axis=-1, keepdims=True)
    is_g = lane < N_GROUPS
    lg = jnp.where(is_g, logits, NEG_INF)
    g_max = jnp.max(lg, axis=-1, keepdims=True)
    g_idx = first(lg == g_max)
    g_w = 1.0 / jnp.sum(jnp.where(is_g, jnp.exp(lg - g_max), 0.0), axis=-1, keepdims=True)
    lo = N_GROUPS + EXPERTS_PER_GROUP * g_idx
    le = jnp.where((lane_f >= lo) & (lane_f < lo + EXPERTS_PER_GROUP), logits, NEG_INF)
    e1 = jnp.max(le, axis=-1, keepdims=True)
    i1 = first(le == e1)
    le2 = jnp.where(lane_f == i1, NEG_INF, le)
    e2 = jnp.max(le2, axis=-1, keepdims=True)
    i2 = first(le2 == e2)
    ex = jnp.exp(e2 - e1)
    gate1 = g_w / (1.0 + ex)
    gate2 = g_w * ex / (1.0 + ex)
    tm = logits.shape[0]
    pick1 = lane_f == i1
    pick2 = lane_f == i2
    picks = jnp.where(pick1 | pick2, 1.0, 0.0)
    earlier = (lax.broadcasted_iota(jnp.int32, (tm, tm), 0) > lax.broadcasted_iota(jnp.int32, (tm, tm), 1))
    before = seen + _dot(earlier.astype(BF16), picks.astype(BF16))
    rank1 = jnp.sum(jnp.where(pick1, before, 0.0), axis=-1, keepdims=True)
    rank2 = jnp.sum(jnp.where(pick2, before, 0.0), axis=-1, keepdims=True)
    out = jnp.where(lane == 0, gate1, jnp.where(lane == 1, gate2, 0.0))
    out = jnp.where(lane == 2, i1 - N_GROUPS, jnp.where(lane == 3, i2 - N_GROUPS, out))
    out = jnp.where(lane == 4, rank1, jnp.where(lane == 5, rank2, out))
    return out, jnp.sum(picks, axis=0, keepdims=True)


def _post_kernel(x_ref, yr_ref, ya_ref, wo_ref, gffn_ref, rw_hi_ref, rw_lo_ref, rb_ref,
                 h_o, hn_o, lg_o, cnt_o, seen_scr):
    @pl.when(pl.program_id(0) == 0)
    def _():
        seen_scr[...] = jnp.zeros_like(seen_scr)

    c = yr_ref.shape[1]
    h = (x_ref[...] + _dot(yr_ref[...].astype(BF16), wo_ref[:c, :]) + _dot(ya_ref[...].astype(BF16), wo_ref[c:, :]))
    ms = jnp.mean(h * h, axis=-1, keepdims=True)
    hn = (h * lax.rsqrt(ms + NORM_EPS)) * gffn_ref[...]
    h_o[...] = h
    bits = pltpu.bitcast(hn.astype(BF16).astype(F32), jnp.uint32)
    half = hn.shape[1] // 2
    hn_o[...] = (bits[:, :half] >> 16) | (bits[:, half:] & jnp.uint32(0xFFFF0000))
    hi = hn.astype(BF16)
    lo = (hn - hi.astype(F32)).astype(BF16)
    logits = (_dot(hi, rw_hi_ref[...]) + _dot(hi, rw_lo_ref[...]) + _dot(lo, rw_hi_ref[...])) + rb_ref[...]
    route, picked = _route_rows(logits, seen_scr[...])
    lg_o[...] = route
    seen_scr[...] = seen_scr[...] + picked
    cnt_o[...] = seen_scr[...]


def _post_call(x2, yr, ya, p, tm):
    t, d = x2.shape
    c = yr.shape[1]
    full = lambda a: pl.BlockSpec(a.shape, lambda i: (0,) * a.ndim)
    tok = lambda w: pl.BlockSpec((tm, w), lambda i: (i, 0))
    weights = [p['wout'], p['gffn'], p['rw_hi'], p['rw_lo'], p['rb']]
    return pl.pallas_call(
        _post_kernel,
        grid=(t // tm,),
        in_specs=[tok(d), tok(c), tok(c)] + [full(w) for w in weights],
        out_specs=[tok(d), tok(d // 2), tok(LANES), pl.BlockSpec((1, LANES), lambda i: (0, 0))],
        out_shape=[jax.ShapeDtypeStruct((t, d), F32), jax.ShapeDtypeStruct((t, d // 2), jnp.uint32),
                   jax.ShapeDtypeStruct((t, LANES), F32), jax.ShapeDtypeStruct((1, LANES), F32)],
        scratch_shapes=[pltpu.VMEM((1, LANES), F32)],
        compiler_params=pltpu.CompilerParams(dimension_semantics=("arbitrary",), vmem_limit_bytes=V7X_VMEM_LIMIT),
        name="post",
    )(x2, yr, ya, *weights)


def _expert_kernel(be_ref, nb_ref, xs_ref, w1_ref, w3_ref, w2_ref, y_ref, w1_s, w3_s, w2_s):
    i = pl.program_id(0)
    live = i < nb_ref[0]

    @pl.when(live & ((i == 0) | (be_ref[i] != be_ref[jnp.maximum(i - 1, 0)])))
    def _():
        w1_s[...] = w1_ref[...].astype(BF16)
        w3_s[...] = w3_ref[...].astype(BF16)
        w2_s[...] = w2_ref[...].astype(BF16)

    @pl.when(live)
    def _():
        packed = xs_ref[...]
        half = packed.shape[1]
        lo = pltpu.bitcast(packed << 16, F32).astype(BF16)
        hi = pltpu.bitcast(packed & jnp.uint32(0xFFFF0000), F32).astype(BF16)
        h1 = _dot(lo, w1_s[:half, :]) + _dot(hi, w1_s[half:, :])
        h3 = _dot(lo, w3_s[:half, :]) + _dot(hi, w3_s[half:, :])
        act = (h1 * _sigmoid(h1)) * h3
        y_ref[...] = _dot(act.astype(BF16), w2_s[...])

    @pl.when(jnp.logical_not(live))
    def _():
        y_ref[...] = jnp.zeros_like(y_ref)


def _expert_call(blk_exp, n_used, xs, p, bm):
    n_slots = xs.shape[0]
    _, d, de = p['w1'].shape
    grid_spec = pltpu.PrefetchScalarGridSpec(
        num_scalar_prefetch=2,
        grid=(n_slots // bm,),
        in_specs=[pl.BlockSpec((bm, d // 2), lambda i, be, nb: (i, 0)),
                  pl.BlockSpec((None, d, de), lambda i, be, nb: (be[i], 0, 0)),
                  pl.BlockSpec((None, d, de), lambda i, be, nb: (be[i], 0, 0)),
                  pl.BlockSpec((None, de, d), lambda i, be, nb: (be[i], 0, 0))],
        out_specs=pl.BlockSpec((bm, d), lambda i, be, nb: (i, 0)),
        scratch_shapes=[pltpu.VMEM((d, de), BF16), pltpu.VMEM((d, de), BF16), pltpu.VMEM((de, d), BF16)],
    )
    return pl.pallas_call(
        _expert_kernel,
        grid_spec=grid_spec,
        out_shape=jax.ShapeDtypeStruct((n_slots, d), F32),
        compiler_params=pltpu.CompilerParams(dimension_semantics=("arbitrary",), vmem_limit_bytes=V7X_VMEM_LIMIT),
        name="experts",
    )(blk_exp, n_used, xs, p['w1'], p['w3'], p['w2'])


def _final_kernel(h_ref, route_ref, y1_ref, y2_ref, gfin_ref, o_ref):
    route = route_ref[...]
    h = h_ref[...] + (route[:, 0:1] * y1_ref[...] + route[:, 1:2] * y2_ref[...])
    ms = jnp.mean(h * h, axis=-1, keepdims=True)
    o_ref[...] = (h * lax.rsqrt(ms + NORM_EPS)) * gfin_ref[...]


def _final_call(h, route, y12, gfin, tm):
    t, d = h.shape
    tok = pl.BlockSpec((tm, d), lambda i: (i, 0))
    routed = lambda a: pl.BlockSpec((None, tm, d), lambda i: (a, i, 0))
    return pl.pallas_call(
        _final_kernel,
        grid=(t // tm,),
        in_specs=[tok, pl.BlockSpec((tm, LANES), lambda i: (i, 0)), routed(0), routed(1),
                  pl.BlockSpec((1, d), lambda i: (0, 0))],
        out_specs=tok,
        out_shape=jax.ShapeDtypeStruct((t, d), F32),
        compiler_params=pltpu.CompilerParams(dimension_semantics=("arbitrary",), vmem_limit_bytes=V7X_VMEM_LIMIT),
        name="final",
    )(h, route, y12, y12, gfin)


def _gather_rows(table, idx):
    info = plsc.get_sparse_core_info()
    nc, ns = info.num_cores, info.num_subcores
    b, d = idx.shape[0], table.shape[1]
    chunk = min(SC_INDEX_LIMIT, SC_GATHER_BYTES // (d * table.dtype.itemsize))
    assert b % (nc * ns * chunk * 2) == 0, "rows must split evenly into chunk pairs per subcore"
    per_w = b // (nc * ns)
    n_chunks = per_w // chunk
    mesh = plsc.VectorSubcoreMesh(core_axis_name="c", subcore_axis_name="s")

    @functools.partial(
        pl.kernel, mesh=mesh, out_type=jax.ShapeDtypeStruct((b, d), table.dtype),
        scratch_types=[pltpu.VMEM((per_w,), jnp.int32), pltpu.VMEM((2, chunk, d), table.dtype),
                       pltpu.SemaphoreType.DMA((2,)), pltpu.SemaphoreType.DMA((2,))])
    def gather(table_hbm, idx_hbm, out_hbm, idx_v, rows_v, fetch_sem, put_sem):
        base = (lax.axis_index("s") * nc + lax.axis_index("c")) * per_w
        pltpu.sync_copy(idx_hbm.at[pl.ds(base, per_w)], idx_v)

        def fetch(c, slot):
            off = pl.multiple_of(c * chunk, chunk)
            return pltpu.make_async_copy(table_hbm.at[idx_v.at[pl.ds(off, chunk)]], rows_v.at[slot],
                                         fetch_sem.at[slot])

        def put(c, slot):
            off = pl.multiple_of(c * chunk, chunk)
            return pltpu.make_async_copy(rows_v.at[slot], out_hbm.at[pl.ds(base + off, chunk)], put_sem.at[slot])

        fetch(0, 0).start()

        @pl.loop(0, n_chunks, step=2)
        def _(c):
            @pl.when(c > 0)
            def _():
                put(c - 1, 1).wait()
            fetch(c + 1, 1).start()
            fetch(c, 0).wait()
            put(c, 0).start()
            fetch(c + 1, 1).wait()
            put(c, 0).wait()

            @pl.when(c + 2 < n_chunks)
            def _():
                fetch(c + 2, 0).start()
            put(c + 1, 1).start()

        put(n_chunks - 1, 1).wait()

    return gather(table, idx)


def _route(route, counts, bm):
    n = route.shape[0]
    eid = route[:, 2:4].astype(jnp.int32)
    rank = route[:, 4:6].astype(jnp.int32)
    m = 2 * n
    experts = jnp.arange(N_EXPERTS, dtype=jnp.int32)
    counts = counts[0, N_GROUPS:N_GROUPS + N_EXPERTS].astype(jnp.int32)
    starts = jnp.cumsum(counts) - counts
    padded = (counts + bm - 1) // bm * bm
    p_ends = jnp.cumsum(padded)
    p_starts = p_ends - padded
    lookup = lambda tbl, e: jnp.sum(jnp.where(e[..., None] == experts, tbl, 0), axis=-1)
    dest = lookup(p_starts, eid) + rank
    bits = max(1, (m - 1).bit_length())
    order = lax.sort(eid.reshape(-1) * (1 << bits) + jnp.arange(m, dtype=jnp.int32)) & ((1 << bits) - 1)
    n_blocks = _round_up(-(-m // bm) + N_EXPERTS, SLOT_MULTIPLE // bm)
    blk_start = jnp.arange(n_blocks, dtype=jnp.int32) * bm
    blk_exp = jnp.minimum(jnp.sum((p_ends[None, :] <= blk_start[:, None]).astype(jnp.int32), axis=1), N_EXPERTS - 1)
    pos = (blk_start - lookup(p_starts, blk_exp))[:, None] + jnp.arange(bm, dtype=jnp.int32)[None, :]
    valid = (pos < lookup(counts, blk_exp)[:, None]).reshape(-1)
    src = jnp.where(valid, (lookup(starts, blk_exp)[:, None] + pos).reshape(-1), 0)
    slot_tok = jnp.where(valid, order.at[src].get(mode='promise_in_bounds') // 2,
                         jnp.arange(valid.shape[0], dtype=jnp.int32) % n)
    n_used = (p_ends[-1] // bm).astype(jnp.int32).reshape(1)
    return slot_tok, dest, blk_exp.astype(jnp.int32), n_used


def _moe_and_final(x2, yr, ya, p, tm, bm):
    h, hn, route, counts = _post_call(x2, yr, ya, p, tm)
    slot_tok, dest, blk_exp, n_used = _route(route, counts, bm)
    xs = _gather_rows(hn, slot_tok)
    yb = _expert_call(blk_exp, n_used, xs, p, bm)
    y12 = _gather_rows(yb, dest.T.reshape(-1)).reshape(2, h.shape[0], h.shape[1])
    return _final_call(h, route, y12, p['gfin'], tm)


def _prep_params(layer, norm_mix_g, w_in, rwkv_mu_rkv, rwkv_mu_wag, rwkv_w0, rwkv_w1, rwkv_w2, rwkv_a0, rwkv_a1,
                 rwkv_a2, rwkv_g1, rwkv_g2, rwkv_k_k, rwkv_k_a, rwkv_r_k, rwkv_ln_w, rwkv_ln_b, w_out, norm_ffn_g,
                 router_group_w, router_group_b, router_expert_w, router_expert_b, expert_w1, expert_w3, expert_w2,
                 norm_final_g):
    d = w_in.shape[1]
    c = rwkv_w0.shape[1]
    row = lambda a: a.reshape(1, -1).astype(F32)
    lowrank = jnp.concatenate([rwkv_w1[layer], rwkv_a1[layer], rwkv_g1[layer]], axis=1)
    mx = rwkv_mu_wag[layer]
    r_w = rwkv_w1.shape[2]
    r_a = rwkv_a1.shape[2]
    r_g = rwkv_g1.shape[2]
    assert r_w + r_a == LANES and r_g == LANES
    mx_cols = jnp.concatenate([jnp.broadcast_to(mx[0][:, None], (d, r_w)), jnp.broadcast_to(mx[1][:, None], (d, r_a)),
                               jnp.broadcast_to(mx[2][:, None], (d, r_g))], axis=1)
    w2a2 = jnp.zeros((LANES, 2 * c), F32)
    w2a2 = w2a2.at[:r_w, :c].set(rwkv_w2[layer]).at[r_w:, c:].set(rwkv_a2[layer])
    head = jnp.arange(c) // HEAD_DIM
    rw = jnp.zeros((d, LANES), F32)
    rw = rw.at[:, :N_GROUPS].set(router_group_w[layer]).at[:, N_GROUPS:N_GROUPS + N_EXPERTS].set(router_expert_w[layer])
    rw_hi = rw.astype(BF16)
    rb = jnp.zeros((1, LANES), F32)
    rb = rb.at[0, :N_GROUPS].set(router_group_b[layer]).at[0, N_GROUPS:N_GROUPS + N_EXPERTS].set(router_expert_b[layer])
    return {
        'c_rwkv': c,
        'gmix': row(norm_mix_g[layer]),
        'win': jnp.concatenate([w_in[layer], lowrank], axis=1).astype(BF16),
        'wdx': (mx_cols * lowrank).astype(BF16),
        'mu_rkv': rwkv_mu_rkv[layer],
        'w0a0': jnp.concatenate([row(rwkv_w0[layer]), row(rwkv_a0[layer])], axis=1),
        'w2a2': w2a2.astype(BF16),
        'g2': rwkv_g2[layer].astype(BF16),
        'k_k': row(rwkv_k_k[layer]),
        'k_a': row(rwkv_k_a[layer]),
        'seg': (head[:, None] == head[None, :]).astype(BF16),
        'r_k': row(rwkv_r_k[layer]),
        'ln_w': row(rwkv_ln_w[layer]),
        'ln_b': row(rwkv_ln_b[layer]),
        'wout': w_out[layer].astype(BF16),
        'gffn': row(norm_ffn_g[layer]),
        'rw_hi': rw_hi,
        'rw_lo': (rw - rw_hi.astype(F32)).astype(BF16),
        'rb': rb,
        'w1': expert_w1[layer],
        'w3': expert_w3[layer],
        'w2': expert_w2[layer],
        'gfin': row(norm_final_g),
    }


def _prompt_group(x, p):
    b, s, d = x.shape
    c = p['c_rwkv']
    x2 = x.reshape(b * s, d)
    tm = PROJ_TILE
    flag = jnp.zeros((b * s, 1), F32)
    keep = min(max(w for w, _ in DILATED_CONFIGS), s)
    assert keep % tm == 0
    r, lw, k, v, kk, bb, g, qa, ka, va, xl, kt, vt = _proj_call(x2, flag, p, s // tm, tm, keep_tiles=keep // tm)
    s0 = jnp.zeros((b, c // HEAD_DIM, HEAD_DIM, HEAD_DIM), F32)
    yr, s_new = _rwkv_call((r, lw, k, v, kk, bb, g), s0, p, b, s, RWKV_CHUNK, 4)
    ya = _attn_prompt_call(qa, ka, va, b, s)
    y = _moe_and_final(x2, yr, ya, p, tm, EXPERT_TILE)
    shift = xl.reshape(b, s // tm, 8, d)[:, -1, 7, :]
    to_cache = lambda z: jnp.transpose(z.reshape(b, c // HEAD_DIM, HEAD_DIM, keep), (0, 3, 1, 2))
    k_keep, v_keep = to_cache(kt), to_cache(vt)
    return y.reshape(b, s, d), s_new, shift, k_keep, v_keep


def _sample_group(x, shift0, s0, k_buf, v_buf, p):
    b, t, d = x.shape
    c = p['c_rwkv']
    n_heads = c // HEAD_DIM
    t_pad = 8
    xc = jnp.concatenate([shift0[:, None, :], x, jnp.zeros((b, t_pad - 1 - t, d), x.dtype)], axis=1)
    flag = jnp.zeros((b, t_pad, 1), F32).at[:, 0].set(1.0)
    outs = _proj_call(xc.reshape(b * t_pad, d), flag.reshape(b * t_pad, 1), p, 1, b * t_pad, xl_rows=b * t_pad)
    xl = outs[10]
    live = (jnp.arange(t_pad) < t)[None, :, None]
    shifted = [jnp.where(live, jnp.roll(o.reshape(b, t_pad, c), -1, axis=1), 0.0) for o in outs[:10]]
    r, lw, k, v, kk, bb, g, qa, ka, va = shifted
    flat = lambda z: z.reshape(b * t_pad, c)
    yr, s_new = _rwkv_call(tuple(flat(z) for z in (r, lw, k, v, kk, bb, g)), s0, p, b, t_pad, t_pad, 8)
    ya = _attn_sample_call(qa, ka, va, k_buf, v_buf, t)
    x_pad = jnp.concatenate([x, jnp.zeros((b, t_pad - t, d), x.dtype)], axis=1).reshape(b * t_pad, d)
    y = _moe_and_final(x_pad, yr, flat(ya), p, b * t_pad // 2, SAMPLE_EXPERT_TILE)
    y = y.reshape(b, t_pad, d)[:, :t]
    shift = xl.reshape(b, t_pad, d)[:, t]
    return (y, s_new, shift, ka[:, :t].reshape(b, t, n_heads, HEAD_DIM), va[:, :t].reshape(b, t, n_heads, HEAD_DIM))


def kernel(x_prompt, x_sample, state_rwkv, state_shift, cache_att_k, cache_att_v, norm_mix_g, w_in, rwkv_mu_rkv, rwkv_mu_wag, rwkv_w0, rwkv_w1, rwkv_w2, rwkv_a0, rwkv_a1, rwkv_a2, rwkv_g1, rwkv_g2, rwkv_k_k, rwkv_k_a, rwkv_r_k, rwkv_ln_w, rwkv_ln_b, w_out, norm_ffn_g, router_group_w, router_group_b, router_expert_w, router_expert_b, expert_w1, expert_w3, expert_w2, norm_final_g):
    assert w_in.shape[0] == 1, "single-layer trunk"
    p = _prep_params(0, norm_mix_g, w_in, rwkv_mu_rkv, rwkv_mu_wag, rwkv_w0, rwkv_w1, rwkv_w2, rwkv_a0, rwkv_a1,
                     rwkv_a2, rwkv_g1, rwkv_g2, rwkv_k_k, rwkv_k_a, rwkv_r_k, rwkv_ln_w, rwkv_ln_b, w_out,
                     norm_ffn_g, router_group_w, router_group_b, router_expert_w, router_expert_b, expert_w1,
                     expert_w3, expert_w2, norm_final_g)
    y_p, rw_p, sh_p, kc_p, vc_p = _prompt_group(x_prompt, p)
    y_s, rw_s, sh_s, kc_s, vc_s = _sample_group(x_sample, state_shift[0], state_rwkv[0], cache_att_k[0],
                                                cache_att_v[0], p)
    return (y_p, y_s, rw_p[None], sh_p[None], kc_p[None], vc_p[None], rw_s[None], sh_s[None], kc_s[None], vc_s[None])
```

```python
import functools
import math

import jax
import jax.numpy as jnp
from jax import lax
from jax.experimental import pallas as pl
from jax.experimental.pallas import tpu as pltpu
from jax.experimental.pallas import tpu_sc as plsc

F32 = jnp.float32
BF16 = jnp.bfloat16

HEAD_DIM = 64
GN_EPS = 64e-5
NORM_EPS = 1e-6
DILATED_CONFIGS = ((128, 1), (512, 4), (2048, 16))
N_GROUPS = 4
EXPERTS_PER_GROUP = 8
N_EXPERTS = N_GROUPS * EXPERTS_PER_GROUP
NEG_INF = -1e30

V7X_VMEM_LIMIT = 56 * 1024 * 1024
LANES = 128

PROJ_TILE = 512
RWKV_CHUNK = 64
ATT_BAND = 128
EXPERT_TILE = 512
ATT_UNROLL = {1: 8, 4: 8, 16: 4}
SC_GATHER_BYTES = 128 * 1024
SC_INDEX_LIMIT = 128
SLOT_MULTIPLE = 4096
SAMPLE_EXPERT_TILE = 128

HIGHEST = lax.Precision.HIGHEST
NN = (((1,), (0,)), ((), ()))
NT = (((1,), (1,)), ((), ()))
TN = (((0,), (0,)), ((), ()))


def _dot(a, b, precision=None):
    return jnp.dot(a, b, preferred_element_type=F32, precision=precision)


def _dot_split(a, b_bf16):
    hi = a.astype(BF16)
    lo = (a - hi.astype(F32)).astype(BF16)
    return _dot(hi, b_bf16) + _dot(lo, b_bf16)


def _mm(a, b, dims, mode):
    if mode == 'f32':
        return lax.dot_general(a, b, dims, precision=HIGHEST, preferred_element_type=F32)
    a_hi = a.astype(BF16)
    b_hi = b.astype(BF16)
    out = lax.dot_general(a_hi, b_hi, dims, preferred_element_type=F32)
    if mode == 'x3':
        a_lo = (a - a_hi.astype(F32)).astype(BF16)
        b_lo = (b - b_hi.astype(F32)).astype(BF16)
        out = out + lax.dot_general(a_hi, b_lo, dims, preferred_element_type=F32)
        out = out + lax.dot_general(a_lo, b_hi, dims, preferred_element_type=F32)
    return out


RWKV_MODES = dict(A='bf16', AV='bf16', SQ='bf16', AP='bf16', RB='bf16', RK='bf16', WS='bf16', UP='bf16')


def _pre(x, mode):
    return x.astype(BF16) if mode == 'bf16' else x


def _round_up(x, k):
    return -(-x // k) * k


def _sigmoid(z):
    return 1.0 / (1.0 + jnp.exp(-z))


def _proj_kernel(x_ref, flag_ref, gmix_ref, win_ref, wdx_ref, mu_ref, w0a0_ref, w2a2_ref, g2_ref,
                 kk_ref, ka_ref, seg_ref,
                 r_o, lw_o, k_o, v_o, kkn_o, b_o, g_o, qa_o, kat_o, vat_o, xl_o, *rest,
                 tiles_per_seq, c_rwkv, keep_tiles):
    xn_carry, pj_carry = rest[-2:]
    i = pl.program_id(0)

    @pl.when(i % tiles_per_seq == 0)
    def _():
        xn_carry[...] = jnp.zeros_like(xn_carry)
        pj_carry[...] = jnp.zeros_like(pj_carry)

    c = c_rwkv
    x = x_ref[...]
    tm = x.shape[0]
    ms = jnp.mean(x * x, axis=-1, keepdims=True)
    xn = (x * lax.rsqrt(ms + NORM_EPS)) * gmix_ref[...]
    xn = jnp.where(flag_ref[...] > 0.0, x, xn)
    row = lax.broadcasted_iota(jnp.int32, (tm, 1), 0)
    xn_prev = jnp.where(row == 0, xn_carry[7:8, :], pltpu.roll(xn, 1, axis=0))
    dx = xn_prev - xn

    proj = _dot(xn.astype(BF16), win_ref[...])
    cur = proj[:, :3 * c]
    prev = jnp.where(row == 0, pj_carry[7:8, :], pltpu.roll(cur, 1, axis=0))
    xn_carry[...] = xn[tm - 8:, :]
    pj_carry[...] = cur[tm - 8:, :]
    xl_rows = xl_o.shape[1]
    xl_o[0] = xn[tm - xl_rows:, :]

    mu = mu_ref[...]
    r = cur[:, :c] + mu[0:1] * (prev[:, :c] - cur[:, :c])
    k = cur[:, c:2 * c] + mu[1:2] * (prev[:, c:2 * c] - cur[:, c:2 * c])
    v = cur[:, 2 * c:3 * c] + mu[2:3] * (prev[:, 2 * c:3 * c] - cur[:, 2 * c:3 * c])

    lr = proj[:, 6 * c:] + _dot(dx.astype(BF16), wdx_ref[...])
    lane = lax.broadcasted_iota(jnp.int32, (1, LANES), 1)
    wa_in = jnp.where(lane < 64, jnp.tanh(lr[:, :LANES]), lr[:, :LANES])
    wa = _dot(wa_in.astype(BF16), w2a2_ref[...]) + w0a0_ref[...]
    z = -wa[:, :c]
    softplus = jnp.maximum(z, 0.0) + jnp.log1p(jnp.exp(-jnp.abs(z)))
    lw = -jnp.exp(-softplus - 0.5)
    a = _sigmoid(wa[:, c:])
    g = _dot(_sigmoid(lr[:, LANES:]).astype(BF16), g2_ref[...])

    kk = k * kk_ref[...]
    ss = _dot_split(kk * kk, seg_ref[...])
    kk = kk * lax.rsqrt(jnp.maximum(ss, 1e-24))

    r_o[...] = r
    lw_o[...] = lw
    k_o[...] = k * (1.0 + (a - 1.0) * ka_ref[...])
    v_o[...] = v
    kkn_o[...] = kk
    b_o[...] = kk * a
    g_o[...] = g
    qa_o[...] = proj[:, 3 * c:4 * c]
    kat_o[...] = proj[:, 4 * c:5 * c]
    vat_o[...] = proj[:, 5 * c:6 * c]
    if keep_tiles:
        kt_o, vt_o = rest[:2]

        @pl.when(i % tiles_per_seq >= tiles_per_seq - keep_tiles)
        def _():
            kt_o[...] = proj[:, 4 * c:5 * c].T
            vt_o[...] = proj[:, 5 * c:6 * c].T


def _proj_call(x2, flag, p, tiles_per_seq, tm, xl_rows=8, keep_tiles=0):
    t, d = x2.shape
    c = p['c_rwkv']
    n_tiles = t // tm
    n_seq = n_tiles // tiles_per_seq
    first = tiles_per_seq - keep_tiles
    kept = pl.BlockSpec((None, c, tm), lambda i: (i // tiles_per_seq, 0, jnp.maximum(i % tiles_per_seq - first, 0)))
    kept_specs = [kept, kept] if keep_tiles else []
    kept_shapes = [jax.ShapeDtypeStruct((n_seq, c, keep_tiles * tm), F32)] * 2 if keep_tiles else []
    full = lambda a: pl.BlockSpec(a.shape, lambda i: (0,) * a.ndim, pipeline_mode=pl.Buffered(1))
    tok = lambda w: pl.BlockSpec((tm, w), lambda i: (i, 0))
    weights = [p['gmix'], p['win'], p['wdx'], p['mu_rkv'], p['w0a0'], p['w2a2'], p['g2'], p['k_k'], p['k_a'], p['seg']]
    outs = pl.pallas_call(
        functools.partial(_proj_kernel, tiles_per_seq=tiles_per_seq, c_rwkv=c, keep_tiles=keep_tiles),
        grid=(n_tiles,),
        in_specs=[tok(d), tok(1)] + [full(w) for w in weights],
        out_specs=[tok(c)] * 10 + [pl.BlockSpec((1, xl_rows, d), lambda i: (i, 0, 0))] + kept_specs,
        out_shape=([jax.ShapeDtypeStruct((t, c), F32)] * 10 + [jax.ShapeDtypeStruct((n_tiles, xl_rows, d), F32)]
                   + kept_shapes),
        scratch_shapes=[pltpu.VMEM((8, d), F32), pltpu.VMEM((8, 3 * c), F32)],
        compiler_params=pltpu.CompilerParams(dimension_semantics=("arbitrary",), vmem_limit_bytes=V7X_VMEM_LIMIT),
        name="proj",
    )(x2, flag, *weights)
    return outs


GROUP_LANES = 256
GROUP_HEADS = GROUP_LANES // HEAD_DIM


def _rwkv_kernel(r_ref, lw_ref, k_ref, v_ref, kk_ref, b_ref, g_ref, s0_ref, rk_ref, lnw_ref, lnb_ref, seg_ref,
                 y_ref, sout_ref, s_scr):
    ci = pl.program_id(1)
    nb, L, c = r_ref.shape
    gw, gh, hd = GROUP_LANES, GROUP_HEADS, HEAD_DIM
    n_groups = c // gw
    md = RWKV_MODES

    lane_head = lax.broadcasted_iota(jnp.int32, (1, gw), 1) // hd
    head_masks = [lane_head == j for j in range(gh)]
    bd_state = (lax.broadcasted_iota(jnp.int32, (gw, gw), 0) // hd) == (lax.broadcasted_iota(jnp.int32, (gw, gw), 1) // hd)
    bd_time = (lax.broadcasted_iota(jnp.int32, (gh * L, gh * L), 0) // L) == (lax.broadcasted_iota(jnp.int32, (gh * L, gh * L), 1) // L)
    t_row = lax.broadcasted_iota(jnp.int32, (L, gh * L), 0)
    t_col = lax.broadcasted_iota(jnp.int32, (L, gh * L), 1) % L
    strict4 = t_row > t_col
    incl4 = t_row >= t_col
    incl = lax.broadcasted_iota(jnp.int32, (L, L), 0) >= lax.broadcasted_iota(jnp.int32, (L, L), 1)

    def stack(x):
        return jnp.concatenate([jnp.where(m, x, jnp.zeros_like(x)) for m in head_masks], axis=0)

    def block_diag(n):
        tiled = jnp.concatenate([n] * gh, axis=0)
        return jnp.where(bd_time, tiled, jnp.zeros_like(tiled))

    @pl.when(ci == 0)
    def _():
        for bi in range(nb):
            for gi in range(n_groups):
                s_in = s0_ref[bi, gi * gh:(gi + 1) * gh].reshape(gw, hd)
                s_scr[bi, gi] = jnp.where(bd_state, jnp.concatenate([s_in] * gh, axis=1), 0.0)

    n_apply = max(1, int(math.log2(L)))
    seg = seg_ref[...]
    pre = []
    for bi in range(nb):
        lw = lw_ref[bi]
        lw_hi = lw.astype(BF16)
        lw_r = lw - lw_hi.astype(F32)
        lw_mid = lw_r.astype(BF16)
        lw_lo = (lw_r - lw_mid.astype(F32)).astype(BF16)
        cs3 = _dot(incl.astype(BF16), jnp.concatenate([lw_hi, lw_mid, lw_lo], axis=1))
        cs = cs3[:, :c] + cs3[:, c:2 * c] + cs3[:, 2 * c:]
        cp = cs - lw
        cm = cs[L // 2 - 1:L // 2, :]
        c_last = cs[L - 1:L, :]
        r, k, v, kk, b = r_ref[bi], k_ref[bi], v_ref[bi], kk_ref[bi], b_ref[bi]
        e_dn = jnp.exp(cm - cs)
        e_l = jnp.exp(c_last - cs)
        pre.append(dict(v=v, rt=r * jnp.exp(cs - cm), kkt=kk * jnp.exp(cp - cm), bt=b * e_dn, kt=k * e_dn,
                        kg=kk * jnp.exp(cp), rg=r * jnp.exp(cs), bh=b * e_l, kh=k * e_l, g_last=jnp.exp(c_last),
                        rkk=r * k * rk_ref[...]))

    chains = [(bi, gi) for bi in range(nb) for gi in range(n_groups)]
    col = lambda bi, gi, name: pre[bi][name][:, gi * gw:(gi + 1) * gw]
    each = lambda fn: [fn(i, bi, gi) for i, (bi, gi) in enumerate(chains)]

    vg = each(lambda i, bi, gi: col(bi, gi, 'v'))
    v_st = each(lambda i, bi, gi: stack(_pre(vg[i], md['AV'])))
    a_all = each(lambda i, bi, gi: _mm(
        jnp.concatenate([col(bi, gi, 'kkt'), col(bi, gi, 'rt')], axis=0),
        jnp.concatenate([stack(_pre(col(bi, gi, 'bt'), md['A'])), stack(_pre(col(bi, gi, 'kt'), md['A']))], axis=0),
        NT, md['A']))
    p_ak = each(lambda i, bi, gi: jnp.where(strict4, a_all[i][:L, gh * L:], 0.0))
    p_rb = each(lambda i, bi, gi: jnp.where(incl4, a_all[i][L:, :gh * L], 0.0))
    p_rk = each(lambda i, bi, gi: jnp.where(incl4, a_all[i][L:, gh * L:], 0.0))
    eye4 = (t_row == t_col).astype(F32)
    nm = each(lambda i, bi, gi: -jnp.where(strict4, a_all[i][:L, :gh * L], 0.0))
    t_inv = [eye4 + n for n in nm]
    for it in range(n_apply - 1):
        lhs = nm if it == 0 else [jnp.concatenate([n, t], axis=0) for n, t in zip(nm, t_inv)]
        both = each(lambda i, bi, gi: _mm(lhs[i], block_diag(_pre(nm[i], md['SQ'])), NN, md['SQ']))
        if it > 0:
            t_inv = [t + bo[L:] for t, bo in zip(t_inv, both)]
        nm = [bo[:L] for bo in both]
    t_inv = each(lambda i, bi, gi: t_inv[i] + _mm(t_inv[i], block_diag(_pre(nm[i], md['SQ'])), NN, md['SQ']))
    av = each(lambda i, bi, gi: _mm(p_ak[i], v_st[i], NN, md['AV']))
    x = each(lambda i, bi, gi: _mm(
        t_inv[i], jnp.concatenate([stack(_pre(col(bi, gi, 'kg'), md['AP'])), stack(_pre(av[i], md['AP']))], axis=1),
        NN, md['AP']))
    w_m = [xi[:, :gw] for xi in x]
    u0 = [-xi[:, gw:] for xi in x]
    rbw = each(lambda i, bi, gi: _mm(
        p_rb[i], jnp.concatenate([stack(_pre(w_m[i], md['RB'])), stack(_pre(u0[i], md['RB']))], axis=1),
        NN, md['RB']))
    rkv = each(lambda i, bi, gi: _mm(p_rk[i], v_st[i], NN, md['RK']))
    s_old = each(lambda i, bi, gi: s_scr[bi, gi])
    ws = each(lambda i, bi, gi: _mm(
        jnp.concatenate([w_m[i], col(bi, gi, 'rg') - rbw[i][:, :gw]], axis=0), s_old[i], NT, md['WS']))
    u = each(lambda i, bi, gi: u0[i] - ws[i][:L])
    y = each(lambda i, bi, gi: ws[i][L:] + rbw[i][:, gw:] + rkv[i])
    upd = each(lambda i, bi, gi: _mm(
        jnp.concatenate([u[i], vg[i]], axis=0),
        jnp.concatenate([col(bi, gi, 'bh'), col(bi, gi, 'kh')], axis=0), TN, md['UP']))
    for i, (bi, gi) in enumerate(chains):
        s_scr[bi, gi] = s_old[i] * col(bi, gi, 'g_last') + jnp.where(bd_state, upd[i], 0.0)

    inv = 1.0 / hd
    n_ch = len(chains)
    sums = _dot_split(jnp.concatenate(y + each(lambda i, bi, gi: col(bi, gi, 'rkk')), axis=0), seg)
    mean = [sums[i * L:(i + 1) * L] * inv for i in range(n_ch)]
    bonus = [sums[(n_ch + i) * L:(n_ch + i + 1) * L] * vg[i] for i in range(n_ch)]
    yc = [y[i] - mean[i] for i in range(n_ch)]
    sq = _dot_split(jnp.concatenate([z * z for z in yc], axis=0), seg)
    var = [sq[i * L:(i + 1) * L] * inv for i in range(n_ch)]
    for i, (bi, gi) in enumerate(chains):
        sl = slice(gi * gw, (gi + 1) * gw)
        yn = yc[i] * lax.rsqrt(var[i] + GN_EPS) * lnw_ref[:, sl] + lnb_ref[:, sl]
        y_ref[bi, :, sl] = (yn + bonus[i]) * g_ref[bi, :, sl]

    @pl.when(ci == pl.num_programs(1) - 1)
    def _():
        for bi in range(nb):
            for gi in range(n_groups):
                bd = s_scr[bi, gi]
                folded = bd[:, 0:hd]
                for j in range(1, gh):
                    folded = folded + bd[:, j * hd:(j + 1) * hd]
                sout_ref[bi, gi * gh:(gi + 1) * gh] = folded.reshape(gh, hd, hd)


def _rwkv_call(vecs, s0, p, n_seq, seq_len, chunk, nb):
    c = p['c_rwkv']
    n_heads = c // HEAD_DIM
    n_chunks = seq_len // chunk
    assert n_seq % nb == 0 and seq_len % chunk == 0
    vecs = [z.reshape(n_seq, seq_len, c) for z in vecs]
    tok = pl.BlockSpec((nb, chunk, c), lambda bi, ci: (bi, ci, 0))
    st = pl.BlockSpec((nb, n_heads, HEAD_DIM, HEAD_DIM), lambda bi, ci: (bi, 0, 0, 0))
    rowvec = pl.BlockSpec((1, c), lambda bi, ci: (0, 0))
    seg = p['seg'][:GROUP_LANES, :GROUP_LANES]
    y, s_out = pl.pallas_call(
        _rwkv_kernel,
        grid=(n_seq // nb, n_chunks),
        in_specs=[tok] * 7 + [st, rowvec, rowvec, rowvec, pl.BlockSpec(seg.shape, lambda bi, ci: (0, 0))],
        out_specs=[tok, st],
        out_shape=[jax.ShapeDtypeStruct((n_seq, seq_len, c), F32),
                   jax.ShapeDtypeStruct((n_seq, n_heads, HEAD_DIM, HEAD_DIM), F32)],
        scratch_shapes=[pltpu.VMEM((nb, c // GROUP_LANES, GROUP_LANES, GROUP_LANES), F32)],
        compiler_params=pltpu.CompilerParams(dimension_semantics=("arbitrary", "arbitrary"),
                                             vmem_limit_bytes=V7X_VMEM_LIMIT),
        name="rwkv",
    )(*vecs, s0, p['r_k'], p['ln_w'], p['ln_b'], seg)
    return y.reshape(n_seq * seq_len, c), s_out


def _attn_prompt_kernel(q_ref, k_ref, v_ref, o_ref, m_scr, l_scr, acc_scr):
    s_len = q_ref.shape[0]
    band = ATT_BAND
    n_blk = s_len // band

    lane = lax.broadcasted_iota(jnp.int32, (1, LANES), 1)
    head0 = lane < HEAD_DIM
    qi = lax.broadcasted_iota(jnp.int32, (band, 2 * band), 0)
    kj = lax.broadcasted_iota(jnp.int32, (band, 2 * band), 1)
    in_band = (kj >= qi) & (kj <= qi + band)
    in_band2 = jnp.concatenate([in_band, in_band], axis=0)
    kj2 = jnp.concatenate([kj, kj], axis=0)
    scale = HEAD_DIM ** -0.5
    ones = jnp.ones((2 * band, LANES), BF16)

    for ci, (window, dil) in enumerate(DILATED_CONFIGS):
        assert window // dil == band
        per_res = n_blk // dil

        unroll = ATT_UNROLL[dil]
        run = min(unroll, per_res)

        def body(it, carry, ci=ci, dil=dil, per_res=per_res, run=run, unroll=unroll):
            span = band * dil
            tile = lambda start: pl.ds(start, band, stride=dil) if dil > 1 else pl.ds(start, band)
            blocks = []
            tiles = []
            first_dyn = []
            for r in range(unroll // run):
                i0 = it * unroll + r * run
                blk0 = i0 % per_res
                start0 = i0 // per_res + blk0 * span
                base = len(tiles)
                if per_res > run:
                    tiles.append(tile(jnp.maximum(start0 - span, 0)))
                    first_dyn.append(blk0 == 0)
                else:
                    tiles.append(None)
                    first_dyn.append(None)
                for t in range(run):
                    tiles.append(tile(start0 + t * span))
                    blocks.append((tiles[-1], base + t, r if t == 0 else None))
            kt = [None if w is None else k_ref[w, :].astype(BF16) for w in tiles]
            vt = [None if w is None else v_ref[w, :].astype(BF16) for w in tiles]
            prev = lambda ts, i: ts[i + 1] if ts[i] is None else ts[i]
            q = [q_ref[rows, :] * scale for rows, _, _ in blocks]

            def mask_of(first):
                if first is None:
                    return in_band2
                if first_dyn[first] is None:
                    return in_band2 & (kj2 >= band)
                return in_band2 & (kj2 >= jnp.where(first_dyn[first], band, 0))

            s = [jnp.where(mask_of(first), lax.dot_general(
                jnp.concatenate([jnp.where(head0, qj, 0.0), jnp.where(head0, 0.0, qj)], axis=0).astype(BF16),
                jnp.concatenate([prev(kt, i), kt[i + 1]], axis=0), NT, preferred_element_type=F32), NEG_INF)
                 for qj, (_, i, first) in zip(q, blocks)]
            m = [jnp.max(z, axis=-1, keepdims=True) for z in s]
            p = [jnp.exp(z - mx).astype(BF16) for z, mx in zip(s, m)]
            o = [_dot(pj, jnp.concatenate([jnp.concatenate([prev(vt, i), vt[i + 1]], axis=0), ones], axis=1))
                 for pj, (_, i, _) in zip(p, blocks)]
            for j, (rows, _, _) in enumerate(blocks):
                m_scr[ci, rows, :] = jnp.where(head0, m[j][:band], m[j][band:])
                acc_scr[ci, rows, :] = jnp.where(head0, o[j][:band, :LANES], o[j][band:, :LANES])
                l_scr[ci, rows, :] = jnp.where(head0, o[j][:band, LANES:], o[j][band:, LANES:])
            return carry

        lax.fori_loop(0, n_blk // unroll, body, 0)

    rows_per = 256

    def merge(i, carry):
        rows = pl.ds(pl.multiple_of(i * rows_per, rows_per), rows_per)
        ms = [m_scr[ci, rows, :] for ci in range(len(DILATED_CONFIGS))]
        m_all = functools.reduce(jnp.maximum, ms)
        num = jnp.zeros((rows_per, LANES), F32)
        den = jnp.zeros((rows_per, LANES), F32)
        for ci, m_c in enumerate(ms):
            w_c = jnp.exp(m_c - m_all)
            num = num + w_c * acc_scr[ci, rows, :]
            den = den + w_c * l_scr[ci, rows, :]
        o_ref[rows, :] = num / den
        return carry

    lax.fori_loop(0, s_len // rows_per, merge, 0)


def _attn_prompt_call(q, k, v, n_seq, seq_len):
    c = q.shape[1]
    n_pairs = c // LANES
    blk = pl.BlockSpec((seq_len, LANES), lambda bi, hi: (bi, hi))
    return pl.pallas_call(
        _attn_prompt_kernel,
        grid=(n_seq, n_pairs),
        in_specs=[blk, blk, blk],
        out_specs=blk,
        out_shape=jax.ShapeDtypeStruct((n_seq * seq_len, c), F32),
        scratch_shapes=[pltpu.VMEM((len(DILATED_CONFIGS), seq_len, LANES), F32)] * 3,
        compiler_params=pltpu.CompilerParams(dimension_semantics=("arbitrary", "arbitrary"),
                                             vmem_limit_bytes=V7X_VMEM_LIMIT),
        name="attn_prompt",
    )(q, k, v)


def _attn_sample_kernel(q_ref, kn_ref, vn_ref, kc_ref, vc_ref, o_ref, *, n_new):
    hd = HEAD_DIM
    _, n_heads, _, n_buf = kc_ref.shape
    t_pad = kn_ref.shape[1]
    c = q_ref.shape[2]
    q = q_ref[0] * (hd ** -0.5)
    lane_head = lax.broadcasted_iota(jnp.int32, (1, c), 1) // hd
    qs = jnp.concatenate([jnp.where(lane_head == h, q, 0.0) for h in range(n_heads)], axis=0).astype(BF16)
    n_rows = n_heads * t_pad
    t_idx = lax.broadcasted_iota(jnp.int32, (n_rows, 1), 0) % t_pad

    def multiplicity(dist):
        mult = jnp.zeros(dist.shape, F32)
        for window, dil in DILATED_CONFIGS:
            hit = (dist >= 0) & (dist <= window) & (dist % dil == 0)
            mult = mult + jnp.where(hit, 1.0, 0.0)
        return mult

    jc = lax.broadcasted_iota(jnp.int32, (1, n_buf), 1)
    mult_c = multiplicity(n_buf + t_idx - jc)
    jn = lax.broadcasted_iota(jnp.int32, (1, t_pad), 1)
    mult_n = jnp.where(jn < n_new, multiplicity(t_idx - jn), 0.0)

    sc = jnp.concatenate(
        [_dot(qs[h * t_pad:(h + 1) * t_pad, h * hd:(h + 1) * hd], kc_ref[0, h].astype(BF16))
         for h in range(n_heads)], axis=0)
    sn = lax.dot_general(qs, kn_ref[0].astype(BF16), NT, preferred_element_type=F32)
    sc = jnp.where(mult_c > 0.0, sc, NEG_INF)
    sn = jnp.where(mult_n > 0.0, sn, NEG_INF)
    m = jnp.maximum(jnp.max(sc, axis=-1, keepdims=True), jnp.max(sn, axis=-1, keepdims=True))
    pc = (mult_c * jnp.exp(sc - m)).astype(BF16)
    pn = mult_n * jnp.exp(sn - m)
    inv_l = 1.0 / (jnp.sum(pc.astype(F32), axis=-1, keepdims=True) + jnp.sum(pn, axis=-1, keepdims=True))
    o_new = _dot(pn.astype(BF16), vn_ref[0].astype(BF16)) * inv_l
    out = jnp.zeros((t_pad, c), F32)
    for h in range(n_heads):
        out = out + jnp.where(lane_head == h, o_new[h * t_pad:(h + 1) * t_pad, :], 0.0)
    o_buf = [lax.dot_general(pc[h * t_pad:(h + 1) * t_pad, :], vc_ref[0, h].astype(BF16), NT,
                             preferred_element_type=F32) * inv_l[h * t_pad:(h + 1) * t_pad, :]
             for h in range(n_heads)]
    o_ref[0] = out + jnp.concatenate(o_buf, axis=1)


def _attn_sample_call(q, kn, vn, k_buf, v_buf, n_new):
    b, t_pad, c = q.shape
    _, n_buf, n_heads, hd = k_buf.shape
    k_t = jnp.transpose(k_buf, (0, 2, 3, 1))
    v_t = jnp.transpose(v_buf, (0, 2, 3, 1))
    new = pl.BlockSpec((1, t_pad, c), lambda bi: (bi, 0, 0))
    buf = pl.BlockSpec((1, n_heads, hd, n_buf), lambda bi: (bi, 0, 0, 0))
    return pl.pallas_call(
        functools.partial(_attn_sample_kernel, n_new=n_new),
        grid=(b,),
        in_specs=[new, new, new, buf, buf],
        out_specs=new,
        out_shape=jax.ShapeDtypeStruct((b, t_pad, c), F32),
        compiler_params=pltpu.CompilerParams(dimension_semantics=("arbitrary",), vmem_limit_bytes=V7X_VMEM_LIMIT),
        name="attn_sample",
    )(q, kn, vn, k_t, v_t)


def _route_rows(logits, seen):
    lane = lax.broadcasted_iota(jnp.int32, logits.shape, 1)
    lane_f = lane.astype(F32)
    first = lambda hit: jnp.min(jnp.where(hit, lane_f, float(LANES)), axis=-1, keepdims=True)
    is_g = lane < N_GROUPS
    lg = jnp.where(is_g, logits, NEG_INF)
    g_max = jnp.max(lg, axis=-1, keepdims=True)
    g_idx = first(lg == g_max)
    g_w = 1.0 / jnp.sum(jnp.where(is_g, jnp.exp(lg - g_max), 0.0), axis=-1, keepdims=True)
    lo = N_GROUPS + EXPERTS_PER_GROUP * g_idx
    le = jnp.where((lane_f >= lo) & (lane_f < lo + EXPERTS_PER_GROUP), logits, NEG_INF)
    e1 = jnp.max(le, axis=-1, keepdims=True)
    i1 = first(le == e1)
    le2 = jnp.where(lane_f == i1, NEG_INF, le)
    e2 = jnp.max(le2, axis=-1, keepdims=True)
    i2 = first(le2 == e2)
    ex = jnp.exp(e2 - e1)
    gate1 = g_w / (1.0 + ex)
    gate2 = g_w * ex / (1.0 + ex)
    tm = logits.shape[0]
    pick1 = lane_f == i1
    pick2 = lane_f == i2
    picks = jnp.where(pick1 | pick2, 1.0, 0.0)
    earlier = (lax.broadcasted_iota(jnp.int32, (tm, tm), 0) > lax.broadcasted_iota(jnp.int32, (tm, tm), 1))
    before = seen + _dot(earlier.astype(BF16), picks.astype(BF16))
    rank1 = jnp.sum(jnp.where(pick1, before, 0.0), axis=-1, keepdims=True)
    rank2 = jnp.sum(jnp.where(pick2, before, 0.0), axis=-1, keepdims=True)
    out = jnp.where(lane == 0, gate1, jnp.where(lane == 1, gate2, 0.0))
    out = jnp.where(lane == 2, i1 - N_GROUPS, jnp.where(lane == 3, i2 - N_GROUPS, out))
    out = jnp.where(lane == 4, rank1, jnp.where(lane == 5, rank2, out))
    return out, jnp.sum(picks, axis=0, keepdims=True)


def _post_kernel(x_ref, yr_ref, ya_ref, wo_ref, gffn_ref, rw_hi_ref, rw_lo_ref, rb_ref,
                 h_o, hn_o, lg_o, cnt_o, seen_scr):
    @pl.when(pl.program_id(0) == 0)
    def _():
        seen_scr[...] = jnp.zeros_like(seen_scr)

    c = yr_ref.shape[1]
    h = (x_ref[...] + _dot(yr_ref[...].astype(BF16), wo_ref[:c, :]) + _dot(ya_ref[...].astype(BF16), wo_ref[c:, :]))
    ms = jnp.mean(h * h, axis=-1, keepdims=True)
    hn = (h * lax.rsqrt(ms + NORM_EPS)) * gffn_ref[...]
    h_o[...] = h
    bits = pltpu.bitcast(hn.astype(BF16).astype(F32), jnp.uint32)
    half = hn.shape[1] // 2
    hn_o[...] = (bits[:, :half] >> 16) | (bits[:, half:] & jnp.uint32(0xFFFF0000))
    hi = hn.astype(BF16)
    lo = (hn - hi.astype(F32)).astype(BF16)
    logits = (_dot(hi, rw_hi_ref[...]) + _dot(hi, rw_lo_ref[...]) + _dot(lo, rw_hi_ref[...])) + rb_ref[...]
    route, picked = _route_rows(logits, seen_scr[...])
    lg_o[...] = route
    seen_scr[...] = seen_scr[...] + picked
    cnt_o[...] = seen_scr[...]


def _post_call(x2, yr, ya, p, tm):
    t, d = x2.shape
    c = yr.shape[1]
    full = lambda a: pl.BlockSpec(a.shape, lambda i: (0,) * a.ndim)
    tok = lambda w: pl.BlockSpec((tm, w), lambda i: (i, 0))
    weights = [p['wout'], p['gffn'], p['rw_hi'], p['rw_lo'], p['rb']]
    return pl.pallas_call(
        _post_kernel,
        grid=(t // tm,),
        in_specs=[tok(d), tok(c), tok(c)] + [full(w) for w in weights],
        out_specs=[tok(d), tok(d // 2), tok(LANES), pl.BlockSpec((1, LANES), lambda i: (0, 0))],
        out_shape=[jax.ShapeDtypeStruct((t, d), F32), jax.ShapeDtypeStruct((t, d // 2), jnp.uint32),
                   jax.ShapeDtypeStruct((t, LANES), F32), jax.ShapeDtypeStruct((1, LANES), F32)],
        scratch_shapes=[pltpu.VMEM((1, LANES), F32)],
        compiler_params=pltpu.CompilerParams(dimension_semantics=("arbitrary",), vmem_limit_bytes=V7X_VMEM_LIMIT),
        name="post",
    )(x2, yr, ya, *weights)


def _expert_kernel(be_ref, nb_ref, xs_ref, w1_ref, w3_ref, w2_ref, y_ref, w1_s, w3_s, w2_s):
    i = pl.program_id(0)
    live = i < nb_ref[0]

    @pl.when(live & ((i == 0) | (be_ref[i] != be_ref[jnp.maximum(i - 1, 0)])))
    def _():
        w1_s[...] = w1_ref[...].astype(BF16)
        w3_s[...] = w3_ref[...].astype(BF16)
        w2_s[...] = w2_ref[...].astype(BF16)

    @pl.when(live)
    def _():
        packed = xs_ref[...]
        half = packed.shape[1]
        lo = pltpu.bitcast(packed << 16, F32).astype(BF16)
        hi = pltpu.bitcast(packed & jnp.uint32(0xFFFF0000), F32).astype(BF16)
        h1 = _dot(lo, w1_s[:half, :]) + _dot(hi, w1_s[half:, :])
        h3 = _dot(lo, w3_s[:half, :]) + _dot(hi, w3_s[half:, :])
        act = (h1 * _sigmoid(h1)) * h3
        y_ref[...] = _dot(act.astype(BF16), w2_s[...])

    @pl.when(jnp.logical_not(live))
    def _():
        y_ref[...] = jnp.zeros_like(y_ref)


def _expert_call(blk_exp, n_used, xs, p, bm):
    n_slots = xs.shape[0]
    _, d, de = p['w1'].shape
    grid_spec = pltpu.PrefetchScalarGridSpec(
        num_scalar_prefetch=2,
        grid=(n_slots // bm,),
        in_specs=[pl.BlockSpec((bm, d // 2), lambda i, be, nb: (i, 0)),
                  pl.BlockSpec((None, d, de), lambda i, be, nb: (be[i], 0, 0)),
                  pl.BlockSpec((None, d, de), lambda i, be, nb: (be[i], 0, 0)),
                  pl.BlockSpec((None, de, d), lambda i, be, nb: (be[i], 0, 0))],
        out_specs=pl.BlockSpec((bm, d), lambda i, be, nb: (i, 0)),
        scratch_shapes=[pltpu.VMEM((d, de), BF16), pltpu.VMEM((d, de), BF16), pltpu.VMEM((de, d), BF16)],
    )
    return pl.pallas_call(
        _expert_kernel,
        grid_spec=grid_spec,
        out_shape=jax.ShapeDtypeStruct((n_slots, d), F32),
        compiler_params=pltpu.CompilerParams(dimension_semantics=("arbitrary",), vmem_limit_bytes=V7X_VMEM_LIMIT),
        name="experts",
    )(blk_exp, n_used, xs, p['w1'], p['w3'], p['w2'])


def _final_kernel(h_ref, route_ref, y1_ref, y2_ref, gfin_ref, o_ref):
    route = route_ref[...]
    h = h_ref[...] + (route[:, 0:1] * y1_ref[...] + route[:, 1:2] * y2_ref[...])
    ms = jnp.mean(h * h, axis=-1, keepdims=True)
    o_ref[...] = (h * lax.rsqrt(ms + NORM_EPS)) * gfin_ref[...]


def _final_call(h, route, y12, gfin, tm):
    t, d = h.shape
    tok = pl.BlockSpec((tm, d), lambda i: (i, 0))
    routed = lambda a: pl.BlockSpec((None, tm, d), lambda i: (a, i, 0))
    return pl.pallas_call(
        _final_kernel,
        grid=(t // tm,),
        in_specs=[tok, pl.BlockSpec((tm, LANES), lambda i: (i, 0)), routed(0), routed(1),
                  pl.BlockSpec((1, d), lambda i: (0, 0))],
        out_specs=tok,
        out_shape=jax.ShapeDtypeStruct((t, d), F32),
        compiler_params=pltpu.CompilerParams(dimension_semantics=("arbitrary",), vmem_limit_bytes=V7X_VMEM_LIMIT),
        name="final",
    )(h, route, y12, y12, gfin)


def _gather_rows(table, idx):
    info = plsc.get_sparse_core_info()
    nc, ns = info.num_cores, info.num_subcores
    b, d = idx.shape[0], table.shape[1]
    chunk = min(SC_INDEX_LIMIT, SC_GATHER_BYTES // (d * table.dtype.itemsize))
    assert b % (nc * ns * chunk * 2) == 0, "rows must split evenly into chunk pairs per subcore"
    per_w = b // (nc * ns)
    n_chunks = per_w // chunk
    mesh = plsc.VectorSubcoreMesh(core_axis_name="c", subcore_axis_name="s")

    @functools.partial(
        pl.kernel, mesh=mesh, out_type=jax.ShapeDtypeStruct((b, d), table.dtype),
        scratch_types=[pltpu.VMEM((per_w,), jnp.int32), pltpu.VMEM((2, chunk, d), table.dtype),
                       pltpu.SemaphoreType.DMA((2,)), pltpu.SemaphoreType.DMA((2,))])
    def gather(table_hbm, idx_hbm, out_hbm, idx_v, rows_v, fetch_sem, put_sem):
        base = (lax.axis_index("s") * nc + lax.axis_index("c")) * per_w
        pltpu.sync_copy(idx_hbm.at[pl.ds(base, per_w)], idx_v)

        def fetch(c, slot):
            off = pl.multiple_of(c * chunk, chunk)
            return pltpu.make_async_copy(table_hbm.at[idx_v.at[pl.ds(off, chunk)]], rows_v.at[slot],
                                         fetch_sem.at[slot])

        def put(c, slot):
            off = pl.multiple_of(c * chunk, chunk)
            return pltpu.make_async_copy(rows_v.at[slot], out_hbm.at[pl.ds(base + off, chunk)], put_sem.at[slot])

        fetch(0, 0).start()

        @pl.loop(0, n_chunks, step=2)
        def _(c):
            @pl.when(c > 0)
            def _():
                put(c - 1, 1).wait()
            fetch(c + 1, 1).start()
            fetch(c, 0).wait()
            put(c, 0).start()
            fetch(c + 1, 1).wait()
            put(c, 0).wait()

            @pl.when(c + 2 < n_chunks)
            def _():
                fetch(c + 2, 0).start()
            put(c + 1, 1).start()

        put(n_chunks - 1, 1).wait()

    return gather(table, idx)


def _route(route, counts, bm):
    n = route.shape[0]
    eid = route[:, 2:4].astype(jnp.int32)
    rank = route[:, 4:6].astype(jnp.int32)
    m = 2 * n
    experts = jnp.arange(N_EXPERTS, dtype=jnp.int32)
    counts = counts[0, N_GROUPS:N_GROUPS + N_EXPERTS].astype(jnp.int32)
    starts = jnp.cumsum(counts) - counts
    padded = (counts + bm - 1) // bm * bm
    p_ends = jnp.cumsum(padded)
    p_starts = p_ends - padded
    lookup = lambda tbl, e: jnp.sum(jnp.where(e[..., None] == experts, tbl, 0), axis=-1)
    dest = lookup(p_starts, eid) + rank
    bits = max(1, (m - 1).bit_length())
    order = lax.sort(eid.reshape(-1) * (1 << bits) + jnp.arange(m, dtype=jnp.int32)) & ((1 << bits) - 1)
    n_blocks = _round_up(-(-m // bm) + N_EXPERTS, SLOT_MULTIPLE // bm)
    blk_start = jnp.arange(n_blocks, dtype=jnp.int32) * bm
    blk_exp = jnp.minimum(jnp.sum((p_ends[None, :] <= blk_start[:, None]).astype(jnp.int32), axis=1), N_EXPERTS - 1)
    pos = (blk_start - lookup(p_starts, blk_exp))[:, None] + jnp.arange(bm, dtype=jnp.int32)[None, :]
    valid = (pos < lookup(counts, blk_exp)[:, None]).reshape(-1)
    src = jnp.where(valid, (lookup(starts, blk_exp)[:, None] + pos).reshape(-1), 0)
    slot_tok = jnp.where(valid, order.at[src].get(mode='promise_in_bounds') // 2,
                         jnp.arange(valid.shape[0], dtype=jnp.int32) % n)
    n_used = (p_ends[-1] // bm).astype(jnp.int32).reshape(1)
    return slot_tok, dest, blk_exp.astype(jnp.int32), n_used


def _moe_dispatch(x2, yr, ya, p, tm, bm):
    h, hn, route, counts = _post_call(x2, yr, ya, p, tm)
    slot_tok, dest, blk_exp, n_used = _route(route, counts, bm)
    return dict(h=h, route=route, dest=dest, blk_exp=blk_exp, n_used=n_used, xs=_gather_rows(hn, slot_tok))


def _moe_combine(ctx, yb, p, tm):
    h = ctx['h']
    y12 = _gather_rows(yb, ctx['dest'].T.reshape(-1)).reshape(2, h.shape[0], h.shape[1])
    return _final_call(h, ctx['route'], y12, p['gfin'], tm)


def _prep_params(layer, norm_mix_g, w_in, rwkv_mu_rkv, rwkv_mu_wag, rwkv_w0, rwkv_w1, rwkv_w2, rwkv_a0, rwkv_a1,
                 rwkv_a2, rwkv_g1, rwkv_g2, rwkv_k_k, rwkv_k_a, rwkv_r_k, rwkv_ln_w, rwkv_ln_b, w_out, norm_ffn_g,
                 router_group_w, router_group_b, router_expert_w, router_expert_b, expert_w1, expert_w3, expert_w2,
                 norm_final_g):
    d = w_in.shape[1]
    c = rwkv_w0.shape[1]
    row = lambda a: a.reshape(1, -1).astype(F32)
    lowrank = jnp.concatenate([rwkv_w1[layer], rwkv_a1[layer], rwkv_g1[layer]], axis=1)
    mx = rwkv_mu_wag[layer]
    r_w = rwkv_w1.shape[2]
    r_a = rwkv_a1.shape[2]
    r_g = rwkv_g1.shape[2]
    assert r_w + r_a == LANES and r_g == LANES
    mx_cols = jnp.concatenate([jnp.broadcast_to(mx[0][:, None], (d, r_w)), jnp.broadcast_to(mx[1][:, None], (d, r_a)),
                               jnp.broadcast_to(mx[2][:, None], (d, r_g))], axis=1)
    w2a2 = jnp.zeros((LANES, 2 * c), F32)
    w2a2 = w2a2.at[:r_w, :c].set(rwkv_w2[layer]).at[r_w:, c:].set(rwkv_a2[layer])
    head = jnp.arange(c) // HEAD_DIM
    rw = jnp.zeros((d, LANES), F32)
    rw = rw.at[:, :N_GROUPS].set(router_group_w[layer]).at[:, N_GROUPS:N_GROUPS + N_EXPERTS].set(router_expert_w[layer])
    rw_hi = rw.astype(BF16)
    rb = jnp.zeros((1, LANES), F32)
    rb = rb.at[0, :N_GROUPS].set(router_group_b[layer]).at[0, N_GROUPS:N_GROUPS + N_EXPERTS].set(router_expert_b[layer])
    return {
        'c_rwkv': c,
        'gmix': row(norm_mix_g[layer]),
        'win': jnp.concatenate([w_in[layer], lowrank], axis=1).astype(BF16),
        'wdx': (mx_cols * lowrank).astype(BF16),
        'mu_rkv': rwkv_mu_rkv[layer],
        'w0a0': jnp.concatenate([row(rwkv_w0[layer]), row(rwkv_a0[layer])], axis=1),
        'w2a2': w2a2.astype(BF16),
        'g2': rwkv_g2[layer].astype(BF16),
        'k_k': row(rwkv_k_k[layer]),
        'k_a': row(rwkv_k_a[layer]),
        'seg': (head[:, None] == head[None, :]).astype(BF16),
        'r_k': row(rwkv_r_k[layer]),
        'ln_w': row(rwkv_ln_w[layer]),
        'ln_b': row(rwkv_ln_b[layer]),
        'wout': w_out[layer].astype(BF16),
        'gffn': row(norm_ffn_g[layer]),
        'rw_hi': rw_hi,
        'rw_lo': (rw - rw_hi.astype(F32)).astype(BF16),
        'rb': rb,
        'w1': expert_w1[layer],
        'w3': expert_w3[layer],
        'w2': expert_w2[layer],
        'gfin': row(norm_final_g),
    }


def _prompt_mix(x, p):
    b, s, d = x.shape
    c = p['c_rwkv']
    x2 = x.reshape(b * s, d)
    tm = PROJ_TILE
    flag = jnp.zeros((b * s, 1), F32)
    keep = min(max(w for w, _ in DILATED_CONFIGS), s)
    assert keep % tm == 0
    r, lw, k, v, kk, bb, g, qa, ka, va, xl, kt, vt = _proj_call(x2, flag, p, s // tm, tm, keep_tiles=keep // tm)
    s0 = jnp.zeros((b, c // HEAD_DIM, HEAD_DIM, HEAD_DIM), F32)
    yr, s_new = _rwkv_call((r, lw, k, v, kk, bb, g), s0, p, b, s, RWKV_CHUNK, 4)
    ya = _attn_prompt_call(qa, ka, va, b, s)
    shift = xl.reshape(b, s // tm, 8, d)[:, -1, 7, :]
    to_cache = lambda z: jnp.transpose(z.reshape(b, c // HEAD_DIM, HEAD_DIM, keep), (0, 3, 1, 2))
    k_keep, v_keep = to_cache(kt), to_cache(vt)
    return (x2, yr, ya), (s_new, shift, k_keep, v_keep)


def _sample_mix(x, shift0, s0, k_buf, v_buf, p):
    b, t, d = x.shape
    c = p['c_rwkv']
    n_heads = c // HEAD_DIM
    t_pad = 8
    xc = jnp.concatenate([shift0[:, None, :], x, jnp.zeros((b, t_pad - 1 - t, d), x.dtype)], axis=1)
    flag = jnp.zeros((b, t_pad, 1), F32).at[:, 0].set(1.0)
    outs = _proj_call(xc.reshape(b * t_pad, d), flag.reshape(b * t_pad, 1), p, 1, b * t_pad, xl_rows=b * t_pad)
    xl = outs[10]
    live = (jnp.arange(t_pad) < t)[None, :, None]
    shifted = [jnp.where(live, jnp.roll(o.reshape(b, t_pad, c), -1, axis=1), 0.0) for o in outs[:10]]
    r, lw, k, v, kk, bb, g, qa, ka, va = shifted
    flat = lambda z: z.reshape(b * t_pad, c)
    yr, s_new = _rwkv_call(tuple(flat(z) for z in (r, lw, k, v, kk, bb, g)), s0, p, b, t_pad, t_pad, 8)
    ya = _attn_sample_call(qa, ka, va, k_buf, v_buf, t)
    x_pad = jnp.concatenate([x, jnp.zeros((b, t_pad - t, d), x.dtype)], axis=1).reshape(b * t_pad, d)
    shift = xl.reshape(b, t_pad, d)[:, t]
    return ((x_pad, yr, flat(ya)),
            (s_new, shift, ka[:, :t].reshape(b, t, n_heads, HEAD_DIM), va[:, :t].reshape(b, t, n_heads, HEAD_DIM)))


def kernel(x_prompt, x_sample, state_rwkv, state_shift, cache_att_k, cache_att_v, norm_mix_g, w_in, rwkv_mu_rkv, rwkv_mu_wag, rwkv_w0, rwkv_w1, rwkv_w2, rwkv_a0, rwkv_a1, rwkv_a2, rwkv_g1, rwkv_g2, rwkv_k_k, rwkv_k_a, rwkv_r_k, rwkv_ln_w, rwkv_ln_b, w_out, norm_ffn_g, router_group_w, router_group_b, router_expert_w, router_expert_b, expert_w1, expert_w3, expert_w2, norm_final_g):
    assert w_in.shape[0] == 1, "single-layer trunk"
    p = _prep_params(0, norm_mix_g, w_in, rwkv_mu_rkv, rwkv_mu_wag, rwkv_w0, rwkv_w1, rwkv_w2, rwkv_a0, rwkv_a1,
                     rwkv_a2, rwkv_g1, rwkv_g2, rwkv_k_k, rwkv_k_a, rwkv_r_k, rwkv_ln_w, rwkv_ln_b, w_out,
                     norm_ffn_g, router_group_w, router_group_b, router_expert_w, router_expert_b, expert_w1,
                     expert_w3, expert_w2, norm_final_g)
    moe_p, (rw_p, sh_p, kc_p, vc_p) = _prompt_mix(x_prompt, p)
    moe_s, (rw_s, sh_s, kc_s, vc_s) = _sample_mix(x_sample, state_shift[0], state_rwkv[0], cache_att_k[0],
                                                   cache_att_v[0], p)
    tm_s = moe_s[0].shape[0] // 2
    ctx_p = _moe_dispatch(*moe_p, p, PROJ_TILE, EXPERT_TILE)
    ctx_s = _moe_dispatch(*moe_s, p, tm_s, SAMPLE_EXPERT_TILE)
    yb_p = _expert_call(ctx_p['blk_exp'], ctx_p['n_used'], ctx_p['xs'], p, EXPERT_TILE)
    xs_s, yb_p = lax.optimization_barrier((ctx_s['xs'], yb_p))
    yb_s = _expert_call(ctx_s['blk_exp'], ctx_s['n_used'], xs_s, p, SAMPLE_EXPERT_TILE)
    y_p = _moe_combine(ctx_p, yb_p, p, PROJ_TILE).reshape(x_prompt.shape)
    y_s = _moe_combine(ctx_s, yb_s, p, tm_s)
    y_s = y_s.reshape(x_sample.shape[0], -1, x_sample.shape[2])[:, :x_sample.shape[1]]
    return (y_p, y_s, rw_p[None], sh_p[None], kc_p[None], vc_p[None], rw_s[None], sh_s[None], kc_s[None], vc_s[None])
```

```python
import functools
import math

import jax
import jax.numpy as jnp
from jax import lax
from jax.experimental import pallas as pl
from jax.experimental.pallas import tpu as pltpu
from jax.experimental.pallas import tpu_sc as plsc

F32 = jnp.float32
BF16 = jnp.bfloat16

HEAD_DIM = 64
GN_EPS = 64e-5
NORM_EPS = 1e-6
DILATED_CONFIGS = ((128, 1), (512, 4), (2048, 16))
N_GROUPS = 4
EXPERTS_PER_GROUP = 8
N_EXPERTS = N_GROUPS * EXPERTS_PER_GROUP
NEG_INF = -1e30

V7X_VMEM_LIMIT = 56 * 1024 * 1024
LANES = 128

PROJ_TILE = 512
RWKV_CHUNK = 64
ATT_BAND = 128
EXPERT_TILE = 512
ATT_UNROLL = {1: 8, 4: 8, 16: 4}
SC_GATHER_BYTES = 128 * 1024
SC_INDEX_LIMIT = 128
SLOT_MULTIPLE = 4096
SAMPLE_EXPERT_TILE = 128

HIGHEST = lax.Precision.HIGHEST
NN = (((1,), (0,)), ((), ()))
NT = (((1,), (1,)), ((), ()))
TN = (((0,), (0,)), ((), ()))


def _dot(a, b, precision=None):
    return jnp.dot(a, b, preferred_element_type=F32, precision=precision)


def _dot_split(a, b_bf16):
    hi = a.astype(BF16)
    lo = (a - hi.astype(F32)).astype(BF16)
    return _dot(hi, b_bf16) + _dot(lo, b_bf16)


def _mm(a, b, dims, mode):
    if mode == 'f32':
        return lax.dot_general(a, b, dims, precision=HIGHEST, preferred_element_type=F32)
    a_hi = a.astype(BF16)
    b_hi = b.astype(BF16)
    out = lax.dot_general(a_hi, b_hi, dims, preferred_element_type=F32)
    if mode == 'x3':
        a_lo = (a - a_hi.astype(F32)).astype(BF16)
        b_lo = (b - b_hi.astype(F32)).astype(BF16)
        out = out + lax.dot_general(a_hi, b_lo, dims, preferred_element_type=F32)
        out = out + lax.dot_general(a_lo, b_hi, dims, preferred_element_type=F32)
    return out


RWKV_MODES = dict(A='bf16', AV='bf16', SQ='bf16', AP='bf16', RB='bf16', RK='bf16', WS='bf16', UP='bf16')


def _pre(x, mode):
    return x.astype(BF16) if mode == 'bf16' else x


def _round_up(x, k):
    return -(-x // k) * k


def _sigmoid(z):
    return 1.0 / (1.0 + jnp.exp(-z))


def _proj_kernel(x_ref, flag_ref, gmix_ref, win_ref, wdx_ref, mu_ref, w0a0_ref, w2a2_ref, g2_ref,
                 kk_ref, ka_ref, seg_ref,
                 r_o, lw_o, k_o, v_o, kkn_o, b_o, g_o, qa_o, kat_o, vat_o, xl_o, *rest,
                 tiles_per_seq, c_rwkv, keep_tiles):
    xn_carry, pj_carry = rest[-2:]
    i = pl.program_id(0)

    @pl.when(i % tiles_per_seq == 0)
    def _():
        xn_carry[...] = jnp.zeros_like(xn_carry)
        pj_carry[...] = jnp.zeros_like(pj_carry)

    c = c_rwkv
    x = x_ref[...]
    tm = x.shape[0]
    ms = jnp.mean(x * x, axis=-1, keepdims=True)
    xn = (x * lax.rsqrt(ms + NORM_EPS)) * gmix_ref[...]
    xn = jnp.where(flag_ref[...] > 0.0, x, xn)
    row = lax.broadcasted_iota(jnp.int32, (tm, 1), 0)
    xn_prev = jnp.where(row == 0, xn_carry[7:8, :], pltpu.roll(xn, 1, axis=0))
    dx = xn_prev - xn

    proj = _dot(xn.astype(BF16), win_ref[...])
    cur = proj[:, :3 * c]
    prev = jnp.where(row == 0, pj_carry[7:8, :], pltpu.roll(cur, 1, axis=0))
    xn_carry[...] = xn[tm - 8:, :]
    pj_carry[...] = cur[tm - 8:, :]
    xl_rows = xl_o.shape[1]
    xl_o[0] = xn[tm - xl_rows:, :]

    mu = mu_ref[...]
    r = cur[:, :c] + mu[0:1] * (prev[:, :c] - cur[:, :c])
    k = cur[:, c:2 * c] + mu[1:2] * (prev[:, c:2 * c] - cur[:, c:2 * c])
    v = cur[:, 2 * c:3 * c] + mu[2:3] * (prev[:, 2 * c:3 * c] - cur[:, 2 * c:3 * c])

    lr = proj[:, 6 * c:] + _dot(dx.astype(BF16), wdx_ref[...])
    lane = lax.broadcasted_iota(jnp.int32, (1, LANES), 1)
    wa_in = jnp.where(lane < 64, jnp.tanh(lr[:, :LANES]), lr[:, :LANES])
    wa = _dot(wa_in.astype(BF16), w2a2_ref[...]) + w0a0_ref[...]
    z = -wa[:, :c]
    softplus = jnp.maximum(z, 0.0) + jnp.log1p(jnp.exp(-jnp.abs(z)))
    lw = -jnp.exp(-softplus - 0.5)
    a = _sigmoid(wa[:, c:])
    g = _dot(_sigmoid(lr[:, LANES:]).astype(BF16), g2_ref[...])

    kk = k * kk_ref[...]
    ss = _dot_split(kk * kk, seg_ref[...])
    kk = kk * lax.rsqrt(jnp.maximum(ss, 1e-24))

    r_o[...] = r
    lw_o[...] = lw
    k_o[...] = k * (1.0 + (a - 1.0) * ka_ref[...])
    v_o[...] = v
    kkn_o[...] = kk
    b_o[...] = kk * a
    g_o[...] = g
    qa_o[...] = proj[:, 3 * c:4 * c]
    kat_o[...] = proj[:, 4 * c:5 * c]
    vat_o[...] = proj[:, 5 * c:6 * c]
    if keep_tiles:
        kt_o, vt_o = rest[:2]

        @pl.when(i % tiles_per_seq >= tiles_per_seq - keep_tiles)
        def _():
            kt_o[...] = proj[:, 4 * c:5 * c].T
            vt_o[...] = proj[:, 5 * c:6 * c].T


def _proj_call(x2, flag, p, tiles_per_seq, tm, xl_rows=8, keep_tiles=0):
    t, d = x2.shape
    c = p['c_rwkv']
    n_tiles = t // tm
    n_seq = n_tiles // tiles_per_seq
    first = tiles_per_seq - keep_tiles
    kept = pl.BlockSpec((None, c, tm), lambda i: (i // tiles_per_seq, 0, jnp.maximum(i % tiles_per_seq - first, 0)))
    kept_specs = [kept, kept] if keep_tiles else []
    kept_shapes = [jax.ShapeDtypeStruct((n_seq, c, keep_tiles * tm), F32)] * 2 if keep_tiles else []
    full = lambda a: pl.BlockSpec(a.shape, lambda i: (0,) * a.ndim, pipeline_mode=pl.Buffered(1))
    tok = lambda w: pl.BlockSpec((tm, w), lambda i: (i, 0))
    weights = [p['gmix'], p['win'], p['wdx'], p['mu_rkv'], p['w0a0'], p['w2a2'], p['g2'], p['k_k'], p['k_a'], p['seg']]
    outs = pl.pallas_call(
        functools.partial(_proj_kernel, tiles_per_seq=tiles_per_seq, c_rwkv=c, keep_tiles=keep_tiles),
        grid=(n_tiles,),
        in_specs=[tok(d), tok(1)] + [full(w) for w in weights],
        out_specs=[tok(c)] * 10 + [pl.BlockSpec((1, xl_rows, d), lambda i: (i, 0, 0))] + kept_specs,
        out_shape=([jax.ShapeDtypeStruct((t, c), F32)] * 10 + [jax.ShapeDtypeStruct((n_tiles, xl_rows, d), F32)]
                   + kept_shapes),
        scratch_shapes=[pltpu.VMEM((8, d), F32), pltpu.VMEM((8, 3 * c), F32)],
        compiler_params=pltpu.CompilerParams(dimension_semantics=("arbitrary",), vmem_limit_bytes=V7X_VMEM_LIMIT),
        name="proj",
    )(x2, flag, *weights)
    return outs


GROUP_LANES = 256
GROUP_HEADS = GROUP_LANES // HEAD_DIM


def _rwkv_kernel(r_ref, lw_ref, k_ref, v_ref, kk_ref, b_ref, g_ref, s0_ref, rk_ref, lnw_ref, lnb_ref, seg_ref,
                 y_ref, sout_ref, s_scr):
    ci = pl.program_id(1)
    nb, L, c = r_ref.shape
    gw, gh, hd = GROUP_LANES, GROUP_HEADS, HEAD_DIM
    n_groups = c // gw
    md = RWKV_MODES

    lane_head = lax.broadcasted_iota(jnp.int32, (1, gw), 1) // hd
    head_masks = [lane_head == j for j in range(gh)]
    bd_state = (lax.broadcasted_iota(jnp.int32, (gw, gw), 0) // hd) == (lax.broadcasted_iota(jnp.int32, (gw, gw), 1) // hd)
    bd_time = (lax.broadcasted_iota(jnp.int32, (gh * L, gh * L), 0) // L) == (lax.broadcasted_iota(jnp.int32, (gh * L, gh * L), 1) // L)
    t_row = lax.broadcasted_iota(jnp.int32, (L, gh * L), 0)
    t_col = lax.broadcasted_iota(jnp.int32, (L, gh * L), 1) % L
    strict4 = t_row > t_col
    incl4 = t_row >= t_col
    incl = lax.broadcasted_iota(jnp.int32, (L, L), 0) >= lax.broadcasted_iota(jnp.int32, (L, L), 1)

    def stack(x):
        return jnp.concatenate([jnp.where(m, x, jnp.zeros_like(x)) for m in head_masks], axis=0)

    def block_diag(n):
        tiled = jnp.concatenate([n] * gh, axis=0)
        return jnp.where(bd_time, tiled, jnp.zeros_like(tiled))

    @pl.when(ci == 0)
    def _():
        for bi in range(nb):
            for gi in range(n_groups):
                s_in = s0_ref[bi, gi * gh:(gi + 1) * gh].reshape(gw, hd)
                s_scr[bi, gi] = jnp.where(bd_state, jnp.concatenate([s_in] * gh, axis=1), 0.0)

    n_apply = max(1, int(math.log2(L)))
    seg = seg_ref[...]
    pre = []
    for bi in range(nb):
        lw = lw_ref[bi]
        lw_hi = lw.astype(BF16)
        lw_r = lw - lw_hi.astype(F32)
        lw_mid = lw_r.astype(BF16)
        lw_lo = (lw_r - lw_mid.astype(F32)).astype(BF16)
        cs3 = _dot(incl.astype(BF16), jnp.concatenate([lw_hi, lw_mid, lw_lo], axis=1))
        cs = cs3[:, :c] + cs3[:, c:2 * c] + cs3[:, 2 * c:]
        cp = cs - lw
        cm = cs[L // 2 - 1:L // 2, :]
        c_last = cs[L - 1:L, :]
        r, k, v, kk, b = r_ref[bi], k_ref[bi], v_ref[bi], kk_ref[bi], b_ref[bi]
        e_dn = jnp.exp(cm - cs)
        e_l = jnp.exp(c_last - cs)
        pre.append(dict(v=v, rt=r * jnp.exp(cs - cm), kkt=kk * jnp.exp(cp - cm), bt=b * e_dn, kt=k * e_dn,
                        kg=kk * jnp.exp(cp), rg=r * jnp.exp(cs), bh=b * e_l, kh=k * e_l, g_last=jnp.exp(c_last),
                        rkk=r * k * rk_ref[...]))

    chains = [(bi, gi) for bi in range(nb) for gi in range(n_groups)]
    col = lambda bi, gi, name: pre[bi][name][:, gi * gw:(gi + 1) * gw]
    each = lambda fn: [fn(i, bi, gi) for i, (bi, gi) in enumerate(chains)]

    vg = each(lambda i, bi, gi: col(bi, gi, 'v'))
    v_st = each(lambda i, bi, gi: stack(_pre(vg[i], md['AV'])))
    a_all = each(lambda i, bi, gi: _mm(
        jnp.concatenate([col(bi, gi, 'kkt'), col(bi, gi, 'rt')], axis=0),
        jnp.concatenate([stack(_pre(col(bi, gi, 'bt'), md['A'])), stack(_pre(col(bi, gi, 'kt'), md['A']))], axis=0),
        NT, md['A']))
    p_ak = each(lambda i, bi, gi: jnp.where(strict4, a_all[i][:L, gh * L:], 0.0))
    p_rb = each(lambda i, bi, gi: jnp.where(incl4, a_all[i][L:, :gh * L], 0.0))
    p_rk = each(lambda i, bi, gi: jnp.where(incl4, a_all[i][L:, gh * L:], 0.0))
    eye4 = (t_row == t_col).astype(F32)
    nm = each(lambda i, bi, gi: -jnp.where(strict4, a_all[i][:L, :gh * L], 0.0))
    t_inv = [eye4 + n for n in nm]
    for it in range(n_apply - 1):
        lhs = nm if it == 0 else [jnp.concatenate([n, t], axis=0) for n, t in zip(nm, t_inv)]
        both = each(lambda i, bi, gi: _mm(lhs[i], block_diag(_pre(nm[i], md['SQ'])), NN, md['SQ']))
        if it > 0:
            t_inv = [t + bo[L:] for t, bo in zip(t_inv, both)]
        nm = [bo[:L] for bo in both]
    t_inv = each(lambda i, bi, gi: t_inv[i] + _mm(t_inv[i], block_diag(_pre(nm[i], md['SQ'])), NN, md['SQ']))
    av = each(lambda i, bi, gi: _mm(p_ak[i], v_st[i], NN, md['AV']))
    x = each(lambda i, bi, gi: _mm(
        t_inv[i], jnp.concatenate([stack(_pre(col(bi, gi, 'kg'), md['AP'])), stack(_pre(av[i], md['AP']))], axis=1),
        NN, md['AP']))
    w_m = [xi[:, :gw] for xi in x]
    u0 = [-xi[:, gw:] for xi in x]
    rbw = each(lambda i, bi, gi: _mm(
        p_rb[i], jnp.concatenate([stack(_pre(w_m[i], md['RB'])), stack(_pre(u0[i], md['RB']))], axis=1),
        NN, md['RB']))
    rkv = each(lambda i, bi, gi: _mm(p_rk[i], v_st[i], NN, md['RK']))
    s_old = each(lambda i, bi, gi: s_scr[bi, gi])
    ws = each(lambda i, bi, gi: _mm(
        jnp.concatenate([w_m[i], col(bi, gi, 'rg') - rbw[i][:, :gw]], axis=0), s_old[i], NT, md['WS']))
    u = each(lambda i, bi, gi: u0[i] - ws[i][:L])
    y = each(lambda i, bi, gi: ws[i][L:] + rbw[i][:, gw:] + rkv[i])
    upd = each(lambda i, bi, gi: _mm(
        jnp.concatenate([u[i], vg[i]], axis=0),
        jnp.concatenate([col(bi, gi, 'bh'), col(bi, gi, 'kh')], axis=0), TN, md['UP']))
    for i, (bi, gi) in enumerate(chains):
        s_scr[bi, gi] = s_old[i] * col(bi, gi, 'g_last') + jnp.where(bd_state, upd[i], 0.0)

    inv = 1.0 / hd
    n_ch = len(chains)
    sums = _dot_split(jnp.concatenate(y + each(lambda i, bi, gi: col(bi, gi, 'rkk')), axis=0), seg)
    mean = [sums[i * L:(i + 1) * L] * inv for i in range(n_ch)]
    bonus = [sums[(n_ch + i) * L:(n_ch + i + 1) * L] * vg[i] for i in range(n_ch)]
    yc = [y[i] - mean[i] for i in range(n_ch)]
    sq = _dot_split(jnp.concatenate([z * z for z in yc], axis=0), seg)
    var = [sq[i * L:(i + 1) * L] * inv for i in range(n_ch)]
    for i, (bi, gi) in enumerate(chains):
        sl = slice(gi * gw, (gi + 1) * gw)
        yn = yc[i] * lax.rsqrt(var[i] + GN_EPS) * lnw_ref[:, sl] + lnb_ref[:, sl]
        y_ref[bi, :, sl] = (yn + bonus[i]) * g_ref[bi, :, sl]

    @pl.when(ci == pl.num_programs(1) - 1)
    def _():
        for bi in range(nb):
            for gi in range(n_groups):
                bd = s_scr[bi, gi]
                folded = bd[:, 0:hd]
                for j in range(1, gh):
                    folded = folded + bd[:, j * hd:(j + 1) * hd]
                sout_ref[bi, gi * gh:(gi + 1) * gh] = folded.reshape(gh, hd, hd)


def _rwkv_call(vecs, s0, p, n_seq, seq_len, chunk, nb):
    c = p['c_rwkv']
    n_heads = c // HEAD_DIM
    n_chunks = seq_len // chunk
    assert n_seq % nb == 0 and seq_len % chunk == 0
    vecs = [z.reshape(n_seq, seq_len, c) for z in vecs]
    tok = pl.BlockSpec((nb, chunk, c), lambda bi, ci: (bi, ci, 0))
    st = pl.BlockSpec((nb, n_heads, HEAD_DIM, HEAD_DIM), lambda bi, ci: (bi, 0, 0, 0))
    rowvec = pl.BlockSpec((1, c), lambda bi, ci: (0, 0))
    seg = p['seg'][:GROUP_LANES, :GROUP_LANES]
    y, s_out = pl.pallas_call(
        _rwkv_kernel,
        grid=(n_seq // nb, n_chunks),
        in_specs=[tok] * 7 + [st, rowvec, rowvec, rowvec, pl.BlockSpec(seg.shape, lambda bi, ci: (0, 0))],
        out_specs=[tok, st],
        out_shape=[jax.ShapeDtypeStruct((n_seq, seq_len, c), F32),
                   jax.ShapeDtypeStruct((n_seq, n_heads, HEAD_DIM, HEAD_DIM), F32)],
        scratch_shapes=[pltpu.VMEM((nb, c // GROUP_LANES, GROUP_LANES, GROUP_LANES), F32)],
        compiler_params=pltpu.CompilerParams(dimension_semantics=("arbitrary", "arbitrary"),
                                             vmem_limit_bytes=V7X_VMEM_LIMIT),
        name="rwkv",
    )(*vecs, s0, p['r_k'], p['ln_w'], p['ln_b'], seg)
    return y.reshape(n_seq * seq_len, c), s_out


def _attn_prompt_kernel(q_ref, k_ref, v_ref, o_ref, m_scr, l_scr, acc_scr):
    s_len = q_ref.shape[0]
    band = ATT_BAND
    n_blk = s_len // band

    lane = lax.broadcasted_iota(jnp.int32, (1, LANES), 1)
    head0 = lane < HEAD_DIM
    qi = lax.broadcasted_iota(jnp.int32, (band, 2 * band), 0)
    kj = lax.broadcasted_iota(jnp.int32, (band, 2 * band), 1)
    in_band = (kj >= qi) & (kj <= qi + band)
    in_band2 = jnp.concatenate([in_band, in_band], axis=0)
    kj2 = jnp.concatenate([kj, kj], axis=0)
    scale = HEAD_DIM ** -0.5
    ones = jnp.ones((2 * band, LANES), BF16)

    for ci, (window, dil) in enumerate(DILATED_CONFIGS):
        assert window // dil == band
        per_res = n_blk // dil

        unroll = ATT_UNROLL[dil]
        run = min(unroll, per_res)

        def body(it, carry, ci=ci, dil=dil, per_res=per_res, run=run, unroll=unroll):
            span = band * dil
            tile = lambda start: pl.ds(start, band, stride=dil) if dil > 1 else pl.ds(start, band)
            blocks = []
            tiles = []
            first_dyn = []
            for r in range(unroll // run):
                i0 = it * unroll + r * run
                blk0 = i0 % per_res
                start0 = i0 // per_res + blk0 * span
                base = len(tiles)
                if per_res > run:
                    tiles.append(tile(jnp.maximum(start0 - span, 0)))
                    first_dyn.append(blk0 == 0)
                else:
                    tiles.append(None)
                    first_dyn.append(None)
                for t in range(run):
                    tiles.append(tile(start0 + t * span))
                    blocks.append((tiles[-1], base + t, r if t == 0 else None))
            kt = [None if w is None else k_ref[w, :].astype(BF16) for w in tiles]
            vt = [None if w is None else v_ref[w, :].astype(BF16) for w in tiles]
            prev = lambda ts, i: ts[i + 1] if ts[i] is None else ts[i]
            q = [q_ref[rows, :] * scale for rows, _, _ in blocks]

            def mask_of(first):
                if first is None:
                    return in_band2
                if first_dyn[first] is None:
                    return in_band2 & (kj2 >= band)
                return in_band2 & (kj2 >= jnp.where(first_dyn[first], band, 0))

            s = [jnp.where(mask_of(first), lax.dot_general(
                jnp.concatenate([jnp.where(head0, qj, 0.0), jnp.where(head0, 0.0, qj)], axis=0).astype(BF16),
                jnp.concatenate([prev(kt, i), kt[i + 1]], axis=0), NT, preferred_element_type=F32), NEG_INF)
                 for qj, (_, i, first) in zip(q, blocks)]
            m = [jnp.max(z, axis=-1, keepdims=True) for z in s]
            p = [jnp.exp(z - mx).astype(BF16) for z, mx in zip(s, m)]
            o = [_dot(pj, jnp.concatenate([jnp.concatenate([prev(vt, i), vt[i + 1]], axis=0), ones], axis=1))
                 for pj, (_, i, _) in zip(p, blocks)]
            for j, (rows, _, _) in enumerate(blocks):
                m_scr[ci, rows, :] = jnp.where(head0, m[j][:band], m[j][band:])
                acc_scr[ci, rows, :] = jnp.where(head0, o[j][:band, :LANES], o[j][band:, :LANES])
                l_scr[ci, rows, :] = jnp.where(head0, o[j][:band, LANES:], o[j][band:, LANES:])
            return carry

        lax.fori_loop(0, n_blk // unroll, body, 0)

    rows_per = 256

    def merge(i, carry):
        rows = pl.ds(pl.multiple_of(i * rows_per, rows_per), rows_per)
        ms = [m_scr[ci, rows, :] for ci in range(len(DILATED_CONFIGS))]
        m_all = functools.reduce(jnp.maximum, ms)
        num = jnp.zeros((rows_per, LANES), F32)
        den = jnp.zeros((rows_per, LANES), F32)
        for ci, m_c in enumerate(ms):
            w_c = jnp.exp(m_c - m_all)
            num = num + w_c * acc_scr[ci, rows, :]
            den = den + w_c * l_scr[ci, rows, :]
        o_ref[rows, :] = num / den
        return carry

    lax.fori_loop(0, s_len // rows_per, merge, 0)


def _attn_prompt_call(q, k, v, n_seq, seq_len):
    c = q.shape[1]
    n_pairs = c // LANES
    blk = pl.BlockSpec((seq_len, LANES), lambda bi, hi: (bi, hi))
    return pl.pallas_call(
        _attn_prompt_kernel,
        grid=(n_seq, n_pairs),
        in_specs=[blk, blk, blk],
        out_specs=blk,
        out_shape=jax.ShapeDtypeStruct((n_seq * seq_len, c), F32),
        scratch_shapes=[pltpu.VMEM((len(DILATED_CONFIGS), seq_len, LANES), F32)] * 3,
        compiler_params=pltpu.CompilerParams(dimension_semantics=("arbitrary", "arbitrary"),
                                             vmem_limit_bytes=V7X_VMEM_LIMIT),
        name="attn_prompt",
    )(q, k, v)


def _attn_sample_kernel(q_ref, kn_ref, vn_ref, kc_ref, vc_ref, o_ref, *, n_new):
    hd = HEAD_DIM
    _, n_heads, _, n_buf = kc_ref.shape
    t_pad = kn_ref.shape[1]
    c = q_ref.shape[2]
    q = q_ref[0] * (hd ** -0.5)
    lane_head = lax.broadcasted_iota(jnp.int32, (1, c), 1) // hd
    qs = jnp.concatenate([jnp.where(lane_head == h, q, 0.0) for h in range(n_heads)], axis=0).astype(BF16)
    n_rows = n_heads * t_pad
    t_idx = lax.broadcasted_iota(jnp.int32, (n_rows, 1), 0) % t_pad

    def multiplicity(dist):
        mult = jnp.zeros(dist.shape, F32)
        for window, dil in DILATED_CONFIGS:
            hit = (dist >= 0) & (dist <= window) & (dist % dil == 0)
            mult = mult + jnp.where(hit, 1.0, 0.0)
        return mult

    jc = lax.broadcasted_iota(jnp.int32, (1, n_buf), 1)
    mult_c = multiplicity(n_buf + t_idx - jc)
    jn = lax.broadcasted_iota(jnp.int32, (1, t_pad), 1)
    mult_n = jnp.where(jn < n_new, multiplicity(t_idx - jn), 0.0)

    sc = jnp.concatenate(
        [_dot(qs[h * t_pad:(h + 1) * t_pad, h * hd:(h + 1) * hd], kc_ref[0, h].astype(BF16))
         for h in range(n_heads)], axis=0)
    sn = lax.dot_general(qs, kn_ref[0].astype(BF16), NT, preferred_element_type=F32)
    sc = jnp.where(mult_c > 0.0, sc, NEG_INF)
    sn = jnp.where(mult_n > 0.0, sn, NEG_INF)
    m = jnp.maximum(jnp.max(sc, axis=-1, keepdims=True), jnp.max(sn, axis=-1, keepdims=True))
    pc = (mult_c * jnp.exp(sc - m)).astype(BF16)
    pn = mult_n * jnp.exp(sn - m)
    inv_l = 1.0 / (jnp.sum(pc.astype(F32), axis=-1, keepdims=True) + jnp.sum(pn, axis=-1, keepdims=True))
    o_new = _dot(pn.astype(BF16), vn_ref[0].astype(BF16)) * inv_l
    out = jnp.zeros((t_pad, c), F32)
    for h in range(n_heads):
        out = out + jnp.where(lane_head == h, o_new[h * t_pad:(h + 1) * t_pad, :], 0.0)
    o_buf = [lax.dot_general(pc[h * t_pad:(h + 1) * t_pad, :], vc_ref[0, h].astype(BF16), NT,
                             preferred_element_type=F32) * inv_l[h * t_pad:(h + 1) * t_pad, :]
             for h in range(n_heads)]
    o_ref[0] = out + jnp.concatenate(o_buf, axis=1)


def _attn_sample_call(q, kn, vn, k_buf, v_buf, n_new):
    b, t_pad, c = q.shape
    _, n_buf, n_heads, hd = k_buf.shape
    k_t = jnp.transpose(k_buf, (0, 2, 3, 1))
    v_t = jnp.transpose(v_buf, (0, 2, 3, 1))
    new = pl.BlockSpec((1, t_pad, c), lambda bi: (bi, 0, 0))
    buf = pl.BlockSpec((1, n_heads, hd, n_buf), lambda bi: (bi, 0, 0, 0))
    return pl.pallas_call(
        functools.partial(_attn_sample_kernel, n_new=n_new),
        grid=(b,),
        in_specs=[new, new, new, buf, buf],
        out_specs=new,
        out_shape=jax.ShapeDtypeStruct((b, t_pad, c), F32),
        compiler_params=pltpu.CompilerParams(dimension_semantics=("arbitrary",), vmem_limit_bytes=V7X_VMEM_LIMIT),
        name="attn_sample",
    )(q, kn, vn, k_t, v_t)


def _route_rows(logits, seen):
    lane = lax.broadcasted_iota(jnp.int32, logits.shape, 1)
    lane_f = lane.astype(F32)
    first = lambda hit: jnp.min(jnp.where(hit, lane_f, float(LANES)), axis=-1, keepdims=True)
    is_g = lane < N_GROUPS
    lg = jnp.where(is_g, logits, NEG_INF)
    g_max = jnp.max(lg, axis=-1, keepdims=True)
    g_idx = first(lg == g_max)
    g_w = 1.0 / jnp.sum(jnp.where(is_g, jnp.exp(lg - g_max), 0.0), axis=-1, keepdims=True)
    lo = N_GROUPS + EXPERTS_PER_GROUP * g_idx
    le = jnp.where((lane_f >= lo) & (lane_f < lo + EXPERTS_PER_GROUP), logits, NEG_INF)
    e1 = jnp.max(le, axis=-1, keepdims=True)
    i1 = first(le == e1)
    le2 = jnp.where(lane_f == i1, NEG_INF, le)
    e2 = jnp.max(le2, axis=-1, keepdims=True)
    i2 = first(le2 == e2)
    ex = jnp.exp(e2 - e1)
    gate1 = g_w / (1.0 + ex)
    gate2 = g_w * ex / (1.0 + ex)
    tm = logits.shape[0]
    pick1 = lane_f == i1
    pick2 = lane_f == i2
    picks = jnp.where(pick1 | pick2, 1.0, 0.0)
    earlier = (lax.broadcasted_iota(jnp.int32, (tm, tm), 0) > lax.broadcasted_iota(jnp.int32, (tm, tm), 1))
    before = seen + _dot(earlier.astype(BF16), picks.astype(BF16))
    rank1 = jnp.sum(jnp.where(pick1, before, 0.0), axis=-1, keepdims=True)
    rank2 = jnp.sum(jnp.where(pick2, before, 0.0), axis=-1, keepdims=True)
    out = jnp.where(lane == 0, gate1, jnp.where(lane == 1, gate2, 0.0))
    out = jnp.where(lane == 2, i1 - N_GROUPS, jnp.where(lane == 3, i2 - N_GROUPS, out))
    out = jnp.where(lane == 4, rank1, jnp.where(lane == 5, rank2, out))
    return out, jnp.sum(picks, axis=0, keepdims=True)


def _post_kernel(x_ref, yr_ref, ya_ref, wo_ref, gffn_ref, rw_cat_ref, rb_ref,
                 h_o, hn_o, lg_o, cnt_o, seen_scr):
    @pl.when(pl.program_id(0) == 0)
    def _():
        seen_scr[...] = jnp.zeros_like(seen_scr)

    c = yr_ref.shape[1]
    h = (x_ref[...] + _dot(yr_ref[...].astype(BF16), wo_ref[:c, :]) + _dot(ya_ref[...].astype(BF16), wo_ref[c:, :]))
    ms = jnp.mean(h * h, axis=-1, keepdims=True)
    hn = (h * lax.rsqrt(ms + NORM_EPS)) * gffn_ref[...]
    h_o[...] = h
    bits = pltpu.bitcast(hn.astype(BF16).astype(F32), jnp.uint32)
    half = hn.shape[1] // 2
    hn_o[...] = (bits[:, :half] >> 16) | (bits[:, half:] & jnp.uint32(0xFFFF0000))
    hi = hn.astype(BF16)
    lo = (hn - hi.astype(F32)).astype(BF16)
    both = _dot(hi, rw_cat_ref[...])
    logits = (both[:, :LANES] + both[:, LANES:] + _dot(lo, rw_cat_ref[:, :LANES])) + rb_ref[...]
    route, picked = _route_rows(logits, seen_scr[...])
    lg_o[...] = route
    seen_scr[...] = seen_scr[...] + picked
    cnt_o[...] = seen_scr[...]


def _post_call(x2, yr, ya, p, tm):
    t, d = x2.shape
    c = yr.shape[1]
    full = lambda a: pl.BlockSpec(a.shape, lambda i: (0,) * a.ndim)
    tok = lambda w: pl.BlockSpec((tm, w), lambda i: (i, 0))
    weights = [p['wout'], p['gffn'], p['rw_cat'], p['rb']]
    return pl.pallas_call(
        _post_kernel,
        grid=(t // tm,),
        in_specs=[tok(d), tok(c), tok(c)] + [full(w) for w in weights],
        out_specs=[tok(d), tok(d // 2), tok(LANES), pl.BlockSpec((1, LANES), lambda i: (0, 0))],
        out_shape=[jax.ShapeDtypeStruct((t, d), F32), jax.ShapeDtypeStruct((t, d // 2), jnp.uint32),
                   jax.ShapeDtypeStruct((t, LANES), F32), jax.ShapeDtypeStruct((1, LANES), F32)],
        scratch_shapes=[pltpu.VMEM((1, LANES), F32)],
        compiler_params=pltpu.CompilerParams(dimension_semantics=("arbitrary",), vmem_limit_bytes=V7X_VMEM_LIMIT),
        name="post",
    )(x2, yr, ya, *weights)


def _expert_kernel(be_ref, nb_ref, xs_ref, w1_ref, w3_ref, w2_ref, y_ref, w1_s, w3_s, w2_s):
    i = pl.program_id(0)
    live = i < nb_ref[0]

    @pl.when(live & ((i == 0) | (be_ref[i] != be_ref[jnp.maximum(i - 1, 0)])))
    def _():
        w1_s[...] = w1_ref[...].astype(BF16)
        w3_s[...] = w3_ref[...].astype(BF16)
        w2_s[...] = w2_ref[...].astype(BF16)

    @pl.when(live)
    def _():
        packed = xs_ref[...]
        half = packed.shape[1]
        lo = pltpu.bitcast(packed << 16, F32).astype(BF16)
        hi = pltpu.bitcast(packed & jnp.uint32(0xFFFF0000), F32).astype(BF16)
        h1 = _dot(lo, w1_s[:half, :]) + _dot(hi, w1_s[half:, :])
        h3 = _dot(lo, w3_s[:half, :]) + _dot(hi, w3_s[half:, :])
        act = (h1 * _sigmoid(h1)) * h3
        y_ref[...] = _dot(act.astype(BF16), w2_s[...])

    @pl.when(jnp.logical_not(live))
    def _():
        y_ref[...] = jnp.zeros_like(y_ref)


def _expert_call(blk_exp, n_used, xs, p, bm):
    n_slots = xs.shape[0]
    _, d, de = p['w1'].shape
    grid_spec = pltpu.PrefetchScalarGridSpec(
        num_scalar_prefetch=2,
        grid=(n_slots // bm,),
        in_specs=[pl.BlockSpec((bm, d // 2), lambda i, be, nb: (i, 0)),
                  pl.BlockSpec((None, d, de), lambda i, be, nb: (be[i], 0, 0)),
                  pl.BlockSpec((None, d, de), lambda i, be, nb: (be[i], 0, 0)),
                  pl.BlockSpec((None, de, d), lambda i, be, nb: (be[i], 0, 0))],
        out_specs=pl.BlockSpec((bm, d), lambda i, be, nb: (i, 0)),
        scratch_shapes=[pltpu.VMEM((d, de), BF16), pltpu.VMEM((d, de), BF16), pltpu.VMEM((de, d), BF16)],
    )
    return pl.pallas_call(
        _expert_kernel,
        grid_spec=grid_spec,
        out_shape=jax.ShapeDtypeStruct((n_slots, d), F32),
        compiler_params=pltpu.CompilerParams(dimension_semantics=("arbitrary",), vmem_limit_bytes=V7X_VMEM_LIMIT),
        name="experts",
    )(blk_exp, n_used, xs, p['w1'], p['w3'], p['w2'])


def _final_kernel(h_ref, route_ref, y1_ref, y2_ref, gfin_ref, o_ref):
    route = route_ref[...]
    h = h_ref[...] + (route[:, 0:1] * y1_ref[...] + route[:, 1:2] * y2_ref[...])
    ms = jnp.mean(h * h, axis=-1, keepdims=True)
    o_ref[...] = (h * lax.rsqrt(ms + NORM_EPS)) * gfin_ref[...]


def _final_call(h, route, y12, gfin, tm):
    t, d = h.shape
    tok = pl.BlockSpec((tm, d), lambda i: (i, 0))
    routed = lambda a: pl.BlockSpec((None, tm, d), lambda i: (a, i, 0))
    return pl.pallas_call(
        _final_kernel,
        grid=(t // tm,),
        in_specs=[tok, pl.BlockSpec((tm, LANES), lambda i: (i, 0)), routed(0), routed(1),
                  pl.BlockSpec((1, d), lambda i: (0, 0))],
        out_specs=tok,
        out_shape=jax.ShapeDtypeStruct((t, d), F32),
        compiler_params=pltpu.CompilerParams(dimension_semantics=("arbitrary",), vmem_limit_bytes=V7X_VMEM_LIMIT),
        name="final",
    )(h, route, y12, y12, gfin)


def _gather_rows(table, idx):
    info = plsc.get_sparse_core_info()
    nc, ns = info.num_cores, info.num_subcores
    b, d = idx.shape[0], table.shape[1]
    chunk = min(SC_INDEX_LIMIT, SC_GATHER_BYTES // (d * table.dtype.itemsize))
    assert b % (nc * ns * chunk * 2) == 0, "rows must split evenly into chunk pairs per subcore"
    per_w = b // (nc * ns)
    n_chunks = per_w // chunk
    mesh = plsc.VectorSubcoreMesh(core_axis_name="c", subcore_axis_name="s")

    @functools.partial(
        pl.kernel, mesh=mesh, out_type=jax.ShapeDtypeStruct((b, d), table.dtype),
        scratch_types=[pltpu.VMEM((per_w,), jnp.int32), pltpu.VMEM((2, chunk, d), table.dtype),
                       pltpu.SemaphoreType.DMA((2,)), pltpu.SemaphoreType.DMA((2,))])
    def gather(table_hbm, idx_hbm, out_hbm, idx_v, rows_v, fetch_sem, put_sem):
        base = (lax.axis_index("s") * nc + lax.axis_index("c")) * per_w
        pltpu.sync_copy(idx_hbm.at[pl.ds(base, per_w)], idx_v)

        def fetch(c, slot):
            off = pl.multiple_of(c * chunk, chunk)
            return pltpu.make_async_copy(table_hbm.at[idx_v.at[pl.ds(off, chunk)]], rows_v.at[slot],
                                         fetch_sem.at[slot])

        def put(c, slot):
            off = pl.multiple_of(c * chunk, chunk)
            return pltpu.make_async_copy(rows_v.at[slot], out_hbm.at[pl.ds(base + off, chunk)], put_sem.at[slot])

        fetch(0, 0).start()

        @pl.loop(0, n_chunks, step=2)
        def _(c):
            @pl.when(c > 0)
            def _():
                put(c - 1, 1).wait()
            fetch(c + 1, 1).start()
            fetch(c, 0).wait()
            put(c, 0).start()
            fetch(c + 1, 1).wait()
            put(c, 0).wait()

            @pl.when(c + 2 < n_chunks)
            def _():
                fetch(c + 2, 0).start()
            put(c + 1, 1).start()

        put(n_chunks - 1, 1).wait()

    return gather(table, idx)


def _route(route, counts, bm):
    n = route.shape[0]
    eid = route[:, 2:4].astype(jnp.int32)
    rank = route[:, 4:6].astype(jnp.int32)
    m = 2 * n
    experts = jnp.arange(N_EXPERTS, dtype=jnp.int32)
    counts = counts[0, N_GROUPS:N_GROUPS + N_EXPERTS].astype(jnp.int32)
    starts = jnp.cumsum(counts) - counts
    padded = (counts + bm - 1) // bm * bm
    p_ends = jnp.cumsum(padded)
    p_starts = p_ends - padded
    lookup = lambda tbl, e: jnp.sum(jnp.where(e[..., None] == experts, tbl, 0), axis=-1)
    dest = lookup(p_starts, eid) + rank
    bits = max(1, (m - 1).bit_length())
    order = lax.sort(eid.reshape(-1) * (1 << bits) + jnp.arange(m, dtype=jnp.int32)) & ((1 << bits) - 1)
    n_blocks = _round_up(-(-m // bm) + N_EXPERTS, SLOT_MULTIPLE // bm)
    blk_start = jnp.arange(n_blocks, dtype=jnp.int32) * bm
    blk_exp = jnp.minimum(jnp.sum((p_ends[None, :] <= blk_start[:, None]).astype(jnp.int32), axis=1), N_EXPERTS - 1)
    pos = (blk_start - lookup(p_starts, blk_exp))[:, None] + jnp.arange(bm, dtype=jnp.int32)[None, :]
    valid = (pos < lookup(counts, blk_exp)[:, None]).reshape(-1)
    src = jnp.where(valid, (lookup(starts, blk_exp)[:, None] + pos).reshape(-1), 0)
    slot_tok = jnp.where(valid, order.at[src].get(mode='promise_in_bounds') // 2,
                         jnp.arange(valid.shape[0], dtype=jnp.int32) % n)
    n_used = (p_ends[-1] // bm).astype(jnp.int32).reshape(1)
    return slot_tok, dest, blk_exp.astype(jnp.int32), n_used


def _moe_dispatch(x2, yr, ya, p, tm, bm):
    h, hn, route, counts = _post_call(x2, yr, ya, p, tm)
    slot_tok, dest, blk_exp, n_used = _route(route, counts, bm)
    return dict(h=h, route=route, dest=dest, blk_exp=blk_exp, n_used=n_used, xs=_gather_rows(hn, slot_tok))


def _moe_combine(ctx, yb, p, tm):
    h = ctx['h']
    y12 = _gather_rows(yb, ctx['dest'].T.reshape(-1)).reshape(2, h.shape[0], h.shape[1])
    return _final_call(h, ctx['route'], y12, p['gfin'], tm)


def _prep_params(layer, norm_mix_g, w_in, rwkv_mu_rkv, rwkv_mu_wag, rwkv_w0, rwkv_w1, rwkv_w2, rwkv_a0, rwkv_a1,
                 rwkv_a2, rwkv_g1, rwkv_g2, rwkv_k_k, rwkv_k_a, rwkv_r_k, rwkv_ln_w, rwkv_ln_b, w_out, norm_ffn_g,
                 router_group_w, router_group_b, router_expert_w, router_expert_b, expert_w1, expert_w3, expert_w2,
                 norm_final_g):
    d = w_in.shape[1]
    c = rwkv_w0.shape[1]
    row = lambda a: a.reshape(1, -1).astype(F32)
    lowrank = jnp.concatenate([rwkv_w1[layer], rwkv_a1[layer], rwkv_g1[layer]], axis=1)
    mx = rwkv_mu_wag[layer]
    r_w = rwkv_w1.shape[2]
    r_a = rwkv_a1.shape[2]
    r_g = rwkv_g1.shape[2]
    assert r_w + r_a == LANES and r_g == LANES
    mx_cols = jnp.concatenate([jnp.broadcast_to(mx[0][:, None], (d, r_w)), jnp.broadcast_to(mx[1][:, None], (d, r_a)),
                               jnp.broadcast_to(mx[2][:, None], (d, r_g))], axis=1)
    w2a2 = jnp.zeros((LANES, 2 * c), F32)
    w2a2 = w2a2.at[:r_w, :c].set(rwkv_w2[layer]).at[r_w:, c:].set(rwkv_a2[layer])
    head = jnp.arange(c) // HEAD_DIM
    rw = jnp.zeros((d, LANES), F32)
    rw = rw.at[:, :N_GROUPS].set(router_group_w[layer]).at[:, N_GROUPS:N_GROUPS + N_EXPERTS].set(router_expert_w[layer])
    rw_hi = rw.astype(BF16)
    rb = jnp.zeros((1, LANES), F32)
    rb = rb.at[0, :N_GROUPS].set(router_group_b[layer]).at[0, N_GROUPS:N_GROUPS + N_EXPERTS].set(router_expert_b[layer])
    return {
        'c_rwkv': c,
        'gmix': row(norm_mix_g[layer]),
        'win': jnp.concatenate([w_in[layer], lowrank], axis=1).astype(BF16),
        'wdx': (mx_cols * lowrank).astype(BF16),
        'mu_rkv': rwkv_mu_rkv[layer],
        'w0a0': jnp.concatenate([row(rwkv_w0[layer]), row(rwkv_a0[layer])], axis=1),
        'w2a2': w2a2.astype(BF16),
        'g2': rwkv_g2[layer].astype(BF16),
        'k_k': row(rwkv_k_k[layer]),
        'k_a': row(rwkv_k_a[layer]),
        'seg': (head[:, None] == head[None, :]).astype(BF16),
        'r_k': row(rwkv_r_k[layer]),
        'ln_w': row(rwkv_ln_w[layer]),
        'ln_b': row(rwkv_ln_b[layer]),
        'wout': w_out[layer].astype(BF16),
        'gffn': row(norm_ffn_g[layer]),
        'rw_cat': jnp.concatenate([rw_hi, (rw - rw_hi.astype(F32)).astype(BF16)], axis=1),
        'rb': rb,
        'w1': expert_w1[layer],
        'w3': expert_w3[layer],
        'w2': expert_w2[layer],
        'gfin': row(norm_final_g),
    }


def _prompt_mix(x, p):
    b, s, d = x.shape
    c = p['c_rwkv']
    x2 = x.reshape(b * s, d)
    tm = PROJ_TILE
    flag = jnp.zeros((b * s, 1), F32)
    keep = min(max(w for w, _ in DILATED_CONFIGS), s)
    assert keep % tm == 0
    r, lw, k, v, kk, bb, g, qa, ka, va, xl, kt, vt = _proj_call(x2, flag, p, s // tm, tm, keep_tiles=keep // tm)
    s0 = jnp.zeros((b, c // HEAD_DIM, HEAD_DIM, HEAD_DIM), F32)
    yr, s_new = _rwkv_call((r, lw, k, v, kk, bb, g), s0, p, b, s, RWKV_CHUNK, 8)
    ya = _attn_prompt_call(qa, ka, va, b, s)
    shift = xl.reshape(b, s // tm, 8, d)[:, -1, 7, :]
    to_cache = lambda z: jnp.transpose(z.reshape(b, c // HEAD_DIM, HEAD_DIM, keep), (0, 3, 1, 2))
    k_keep, v_keep = to_cache(kt), to_cache(vt)
    return (x2, yr, ya), (s_new, shift, k_keep, v_keep)


def _sample_mix(x, shift0, s0, k_buf, v_buf, p):
    b, t, d = x.shape
    c = p['c_rwkv']
    n_heads = c // HEAD_DIM
    t_pad = 8
    xc = jnp.concatenate([shift0[:, None, :], x, jnp.zeros((b, t_pad - 1 - t, d), x.dtype)], axis=1)
    flag = jnp.zeros((b, t_pad, 1), F32).at[:, 0].set(1.0)
    outs = _proj_call(xc.reshape(b * t_pad, d), flag.reshape(b * t_pad, 1), p, 1, b * t_pad, xl_rows=b * t_pad)
    xl = outs[10]
    live = (jnp.arange(t_pad) < t)[None, :, None]
    shifted = [jnp.where(live, jnp.roll(o.reshape(b, t_pad, c), -1, axis=1), 0.0) for o in outs[:10]]
    r, lw, k, v, kk, bb, g, qa, ka, va = shifted
    flat = lambda z: z.reshape(b * t_pad, c)
    yr, s_new = _rwkv_call(tuple(flat(z) for z in (r, lw, k, v, kk, bb, g)), s0, p, b, t_pad, t_pad, 8)
    ya = _attn_sample_call(qa, ka, va, k_buf, v_buf, t)
    x_pad = jnp.concatenate([x, jnp.zeros((b, t_pad - t, d), x.dtype)], axis=1).reshape(b * t_pad, d)
    shift = xl.reshape(b, t_pad, d)[:, t]
    return ((x_pad, yr, flat(ya)),
            (s_new, shift, ka[:, :t].reshape(b, t, n_heads, HEAD_DIM), va[:, :t].reshape(b, t, n_heads, HEAD_DIM)))


def kernel(x_prompt, x_sample, state_rwkv, state_shift, cache_att_k, cache_att_v, norm_mix_g, w_in, rwkv_mu_rkv, rwkv_mu_wag, rwkv_w0, rwkv_w1, rwkv_w2, rwkv_a0, rwkv_a1, rwkv_a2, rwkv_g1, rwkv_g2, rwkv_k_k, rwkv_k_a, rwkv_r_k, rwkv_ln_w, rwkv_ln_b, w_out, norm_ffn_g, router_group_w, router_group_b, router_expert_w, router_expert_b, expert_w1, expert_w3, expert_w2, norm_final_g):
    assert w_in.shape[0] == 1, "single-layer trunk"
    p = _prep_params(0, norm_mix_g, w_in, rwkv_mu_rkv, rwkv_mu_wag, rwkv_w0, rwkv_w1, rwkv_w2, rwkv_a0, rwkv_a1,
                     rwkv_a2, rwkv_g1, rwkv_g2, rwkv_k_k, rwkv_k_a, rwkv_r_k, rwkv_ln_w, rwkv_ln_b, w_out,
                     norm_ffn_g, router_group_w, router_group_b, router_expert_w, router_expert_b, expert_w1,
                     expert_w3, expert_w2, norm_final_g)
    moe_p, (rw_p, sh_p, kc_p, vc_p) = _prompt_mix(x_prompt, p)
    moe_s, (rw_s, sh_s, kc_s, vc_s) = _sample_mix(x_sample, state_shift[0], state_rwkv[0], cache_att_k[0],
                                                   cache_att_v[0], p)
    tm_s = moe_s[0].shape[0] // 2
    ctx_p = _moe_dispatch(*moe_p, p, PROJ_TILE, EXPERT_TILE)
    ctx_s = _moe_dispatch(*moe_s, p, tm_s, SAMPLE_EXPERT_TILE)
    yb_p = _expert_call(ctx_p['blk_exp'], ctx_p['n_used'], ctx_p['xs'], p, EXPERT_TILE)
    xs_s, yb_p = lax.optimization_barrier((ctx_s['xs'], yb_p))
    yb_s = _expert_call(ctx_s['blk_exp'], ctx_s['n_used'], xs_s, p, SAMPLE_EXPERT_TILE)
    y_p = _moe_combine(ctx_p, yb_p, p, PROJ_TILE).reshape(x_prompt.shape)
    y_s = _moe_combine(ctx_s, yb_s, p, tm_s)
    y_s = y_s.reshape(x_sample.shape[0], -1, x_sample.shape[2])[:, :x_sample.shape[1]]
    return (y_p, y_s, rw_p[None], sh_p[None], kc_p[None], vc_p[None], rw_s[None], sh_s[None], kc_s[None], vc_s[None])
```

```python
import functools
import math

import jax
import jax.numpy as jnp
from jax import lax
from jax.experimental import pallas as pl
from jax.experimental.pallas import tpu as pltpu
from jax.experimental.pallas import tpu_sc as plsc

F32 = jnp.float32
BF16 = jnp.bfloat16

HEAD_DIM = 64
GN_EPS = 64e-5
NORM_EPS = 1e-6
DILATED_CONFIGS = ((128, 1), (512, 4), (2048, 16))
N_GROUPS = 4
EXPERTS_PER_GROUP = 8
N_EXPERTS = N_GROUPS * EXPERTS_PER_GROUP
NEG_INF = -1e30

V7X_VMEM_LIMIT = 56 * 1024 * 1024
LANES = 128

PROJ_TILE = 512
RWKV_CHUNK = 64
ATT_BAND = 128
EXPERT_TILE = 512
ATT_UNROLL = {1: 8, 4: 8, 16: 4}
SC_GATHER_BYTES = 128 * 1024
SC_INDEX_LIMIT = 128
SLOT_MULTIPLE = 4096
SAMPLE_EXPERT_TILE = 128

HIGHEST = lax.Precision.HIGHEST
NN = (((1,), (0,)), ((), ()))
NT = (((1,), (1,)), ((), ()))
TN = (((0,), (0,)), ((), ()))


def _dot(a, b, precision=None):
    return jnp.dot(a, b, preferred_element_type=F32, precision=precision)


def _dot_split(a, b_bf16):
    hi = a.astype(BF16)
    lo = (a - hi.astype(F32)).astype(BF16)
    return _dot(hi, b_bf16) + _dot(lo, b_bf16)


def _mm(a, b, dims, mode):
    if mode == 'f32':
        return lax.dot_general(a, b, dims, precision=HIGHEST, preferred_element_type=F32)
    a_hi = a.astype(BF16)
    b_hi = b.astype(BF16)
    out = lax.dot_general(a_hi, b_hi, dims, preferred_element_type=F32)
    if mode == 'x3':
        a_lo = (a - a_hi.astype(F32)).astype(BF16)
        b_lo = (b - b_hi.astype(F32)).astype(BF16)
        out = out + lax.dot_general(a_hi, b_lo, dims, preferred_element_type=F32)
        out = out + lax.dot_general(a_lo, b_hi, dims, preferred_element_type=F32)
    return out


RWKV_MODES = dict(A='bf16', AV='bf16', SQ='bf16', AP='bf16', RB='bf16', RK='bf16', WS='bf16', UP='bf16')


def _pre(x, mode):
    return x.astype(BF16) if mode == 'bf16' else x


def _round_up(x, k):
    return -(-x // k) * k


def _sigmoid(z):
    return 1.0 / (1.0 + jnp.exp(-z))


def _proj_kernel(x_ref, flag_ref, gmix_ref, win_ref, wdx_ref, mu_ref, w0a0_ref, w2a2_ref, g2_ref,
                 kk_ref, ka_ref, seg_ref,
                 r_o, lw_o, k_o, v_o, kkn_o, b_o, g_o, qa_o, kat_o, vat_o, xl_o, *rest,
                 tiles_per_seq, c_rwkv, keep_tiles):
    xn_carry, pj_carry = rest[-2:]
    i = pl.program_id(0)

    @pl.when(i % tiles_per_seq == 0)
    def _():
        xn_carry[...] = jnp.zeros_like(xn_carry)
        pj_carry[...] = jnp.zeros_like(pj_carry)

    c = c_rwkv
    x = x_ref[...]
    tm = x.shape[0]
    ms = jnp.mean(x * x, axis=-1, keepdims=True)
    xn = (x * lax.rsqrt(ms + NORM_EPS)) * gmix_ref[...]
    xn = jnp.where(flag_ref[...] > 0.0, x, xn)
    row = lax.broadcasted_iota(jnp.int32, (tm, 1), 0)
    xn_prev = jnp.where(row == 0, xn_carry[7:8, :], pltpu.roll(xn, 1, axis=0))
    dx = xn_prev - xn

    proj = _dot(xn.astype(BF16), win_ref[...])
    cur = proj[:, :3 * c]
    prev = jnp.where(row == 0, pj_carry[7:8, :], pltpu.roll(cur, 1, axis=0))
    xn_carry[...] = xn[tm - 8:, :]
    pj_carry[...] = cur[tm - 8:, :]
    xl_rows = xl_o.shape[1]
    xl_o[0] = xn[tm - xl_rows:, :]

    mu = mu_ref[...]
    r = cur[:, :c] + mu[0:1] * (prev[:, :c] - cur[:, :c])
    k = cur[:, c:2 * c] + mu[1:2] * (prev[:, c:2 * c] - cur[:, c:2 * c])
    v = cur[:, 2 * c:3 * c] + mu[2:3] * (prev[:, 2 * c:3 * c] - cur[:, 2 * c:3 * c])

    lr = proj[:, 6 * c:] + _dot(dx.astype(BF16), wdx_ref[...])
    lane = lax.broadcasted_iota(jnp.int32, (1, LANES), 1)
    wa_in = jnp.where(lane < 64, jnp.tanh(lr[:, :LANES]), lr[:, :LANES])
    wa = _dot(wa_in.astype(BF16), w2a2_ref[...]) + w0a0_ref[...]
    z = -wa[:, :c]
    softplus = jnp.maximum(z, 0.0) + jnp.log1p(jnp.exp(-jnp.abs(z)))
    lw = -jnp.exp(-softplus - 0.5)
    a = _sigmoid(wa[:, c:])
    g = _dot(_sigmoid(lr[:, LANES:]).astype(BF16), g2_ref[...])

    kk = k * kk_ref[...]
    sq = kk * kk
    gw = seg_ref.shape[0]
    ss = jnp.concatenate([_dot_split(sq[:, j:j + gw], seg_ref[...]) for j in range(0, c, gw)], axis=1)
    kk = kk * lax.rsqrt(jnp.maximum(ss, 1e-24))

    r_o[...] = r
    lw_o[...] = lw
    k_o[...] = k * (1.0 + (a - 1.0) * ka_ref[...])
    v_o[...] = v
    kkn_o[...] = kk
    b_o[...] = kk * a
    g_o[...] = g
    qa_o[...] = proj[:, 3 * c:4 * c]
    kat_o[...] = proj[:, 4 * c:5 * c]
    vat_o[...] = proj[:, 5 * c:6 * c]
    if keep_tiles:
        kt_o, vt_o = rest[:2]

        @pl.when(i % tiles_per_seq >= tiles_per_seq - keep_tiles)
        def _():
            kt_o[...] = proj[:, 4 * c:5 * c].T
            vt_o[...] = proj[:, 5 * c:6 * c].T


def _proj_call(x2, flag, p, tiles_per_seq, tm, xl_rows=8, keep_tiles=0):
    t, d = x2.shape
    c = p['c_rwkv']
    n_tiles = t // tm
    n_seq = n_tiles // tiles_per_seq
    first = tiles_per_seq - keep_tiles
    kept = pl.BlockSpec((None, c, tm), lambda i: (i // tiles_per_seq, 0, jnp.maximum(i % tiles_per_seq - first, 0)))
    kept_specs = [kept, kept] if keep_tiles else []
    kept_shapes = [jax.ShapeDtypeStruct((n_seq, c, keep_tiles * tm), F32)] * 2 if keep_tiles else []
    full = lambda a: pl.BlockSpec(a.shape, lambda i: (0,) * a.ndim, pipeline_mode=pl.Buffered(1))
    tok = lambda w: pl.BlockSpec((tm, w), lambda i: (i, 0))
    weights = [p['gmix'], p['win'], p['wdx'], p['mu_rkv'], p['w0a0'], p['w2a2'], p['g2'], p['k_k'], p['k_a'],
               p['seg'][:GROUP_LANES, :GROUP_LANES]]
    outs = pl.pallas_call(
        functools.partial(_proj_kernel, tiles_per_seq=tiles_per_seq, c_rwkv=c, keep_tiles=keep_tiles),
        grid=(n_tiles,),
        in_specs=[tok(d), tok(1)] + [full(w) for w in weights],
        out_specs=[tok(c)] * 10 + [pl.BlockSpec((1, xl_rows, d), lambda i: (i, 0, 0))] + kept_specs,
        out_shape=([jax.ShapeDtypeStruct((t, c), F32)] * 10 + [jax.ShapeDtypeStruct((n_tiles, xl_rows, d), F32)]
                   + kept_shapes),
        scratch_shapes=[pltpu.VMEM((8, d), F32), pltpu.VMEM((8, 3 * c), F32)],
        compiler_params=pltpu.CompilerParams(dimension_semantics=("arbitrary",), vmem_limit_bytes=V7X_VMEM_LIMIT),
        name="proj",
    )(x2, flag, *weights)
    return outs


GROUP_LANES = 256
GROUP_HEADS = GROUP_LANES // HEAD_DIM


def _rwkv_kernel(r_ref, lw_ref, k_ref, v_ref, kk_ref, b_ref, g_ref, s0_ref, rk_ref, lnw_ref, lnb_ref, seg_ref,
                 y_ref, sout_ref, s_scr):
    ci = pl.program_id(1)
    nb, L, c = r_ref.shape
    gw, gh, hd = GROUP_LANES, GROUP_HEADS, HEAD_DIM
    n_groups = c // gw
    md = RWKV_MODES

    lane_head = lax.broadcasted_iota(jnp.int32, (1, gw), 1) // hd
    head_masks = [lane_head == j for j in range(gh)]
    bd_state = (lax.broadcasted_iota(jnp.int32, (gw, gw), 0) // hd) == (lax.broadcasted_iota(jnp.int32, (gw, gw), 1) // hd)
    bd_time = (lax.broadcasted_iota(jnp.int32, (gh * L, gh * L), 0) // L) == (lax.broadcasted_iota(jnp.int32, (gh * L, gh * L), 1) // L)
    t_row = lax.broadcasted_iota(jnp.int32, (L, gh * L), 0)
    t_col = lax.broadcasted_iota(jnp.int32, (L, gh * L), 1) % L
    strict4 = t_row > t_col
    incl4 = t_row >= t_col
    incl = lax.broadcasted_iota(jnp.int32, (L, L), 0) >= lax.broadcasted_iota(jnp.int32, (L, L), 1)

    def stack(x):
        return jnp.concatenate([jnp.where(m, x, jnp.zeros_like(x)) for m in head_masks], axis=0)

    def block_diag(n):
        tiled = jnp.concatenate([n] * gh, axis=0)
        return jnp.where(bd_time, tiled, jnp.zeros_like(tiled))

    @pl.when(ci == 0)
    def _():
        for bi in range(nb):
            for gi in range(n_groups):
                s_in = s0_ref[bi, gi * gh:(gi + 1) * gh].reshape(gw, hd)
                s_scr[bi, gi] = jnp.where(bd_state, jnp.concatenate([s_in] * gh, axis=1), 0.0)

    n_apply = max(1, int(math.log2(L)))
    seg = seg_ref[...]
    pre = []
    for bi in range(nb):
        lw = lw_ref[bi]
        lw_hi = lw.astype(BF16)
        lw_r = lw - lw_hi.astype(F32)
        lw_mid = lw_r.astype(BF16)
        lw_lo = (lw_r - lw_mid.astype(F32)).astype(BF16)
        cs3 = _dot(incl.astype(BF16), jnp.concatenate([lw_hi, lw_mid, lw_lo], axis=1))
        cs = cs3[:, :c] + cs3[:, c:2 * c] + cs3[:, 2 * c:]
        cp = cs - lw
        cm = cs[L // 2 - 1:L // 2, :]
        c_last = cs[L - 1:L, :]
        r, k, v, kk, b = r_ref[bi], k_ref[bi], v_ref[bi], kk_ref[bi], b_ref[bi]
        e_dn = jnp.exp(cm - cs)
        e_l = jnp.exp(c_last - cs)
        pre.append(dict(v=v, rt=r * jnp.exp(cs - cm), kkt=kk * jnp.exp(cp - cm), bt=b * e_dn, kt=k * e_dn,
                        kg=kk * jnp.exp(cp), rg=r * jnp.exp(cs), bh=b * e_l, kh=k * e_l, g_last=jnp.exp(c_last),
                        rkk=r * k * rk_ref[...]))

    chains = [(bi, gi) for bi in range(nb) for gi in range(n_groups)]
    col = lambda bi, gi, name: pre[bi][name][:, gi * gw:(gi + 1) * gw]
    each = lambda fn: [fn(i, bi, gi) for i, (bi, gi) in enumerate(chains)]

    vg = each(lambda i, bi, gi: col(bi, gi, 'v'))
    v_st = each(lambda i, bi, gi: stack(_pre(vg[i], md['AV'])))
    a_all = each(lambda i, bi, gi: _mm(
        jnp.concatenate([col(bi, gi, 'kkt'), col(bi, gi, 'rt')], axis=0),
        jnp.concatenate([stack(_pre(col(bi, gi, 'bt'), md['A'])), stack(_pre(col(bi, gi, 'kt'), md['A']))], axis=0),
        NT, md['A']))
    p_ak = each(lambda i, bi, gi: jnp.where(strict4, a_all[i][:L, gh * L:], 0.0))
    p_rb = each(lambda i, bi, gi: jnp.where(incl4, a_all[i][L:, :gh * L], 0.0))
    p_rk = each(lambda i, bi, gi: jnp.where(incl4, a_all[i][L:, gh * L:], 0.0))
    eye4 = (t_row == t_col).astype(F32)
    nm = each(lambda i, bi, gi: -jnp.where(strict4, a_all[i][:L, :gh * L], 0.0))
    t_inv = [eye4 + n for n in nm]
    for it in range(n_apply - 1):
        lhs = nm if it == 0 else [jnp.concatenate([n, t], axis=0) for n, t in zip(nm, t_inv)]
        both = each(lambda i, bi, gi: _mm(lhs[i], block_diag(_pre(nm[i], md['SQ'])), NN, md['SQ']))
        if it > 0:
            t_inv = [t + bo[L:] for t, bo in zip(t_inv, both)]
        nm = [bo[:L] for bo in both]
    t_inv = each(lambda i, bi, gi: t_inv[i] + _mm(t_inv[i], block_diag(_pre(nm[i], md['SQ'])), NN, md['SQ']))
    av = each(lambda i, bi, gi: _mm(p_ak[i], v_st[i], NN, md['AV']))
    x = each(lambda i, bi, gi: _mm(
        t_inv[i], jnp.concatenate([stack(_pre(col(bi, gi, 'kg'), md['AP'])), stack(_pre(av[i], md['AP']))], axis=1),
        NN, md['AP']))
    w_m = [xi[:, :gw] for xi in x]
    u0 = [-xi[:, gw:] for xi in x]
    rbw = each(lambda i, bi, gi: _mm(
        p_rb[i], jnp.concatenate([stack(_pre(w_m[i], md['RB'])), stack(_pre(u0[i], md['RB']))], axis=1),
        NN, md['RB']))
    rkv = each(lambda i, bi, gi: _mm(p_rk[i], v_st[i], NN, md['RK']))
    s_old = each(lambda i, bi, gi: s_scr[bi, gi])
    ws = each(lambda i, bi, gi: _mm(
        jnp.concatenate([w_m[i], col(bi, gi, 'rg') - rbw[i][:, :gw]], axis=0), s_old[i], NT, md['WS']))
    u = each(lambda i, bi, gi: u0[i] - ws[i][:L])
    y = each(lambda i, bi, gi: ws[i][L:] + rbw[i][:, gw:] + rkv[i])
    upd = each(lambda i, bi, gi: _mm(
        jnp.concatenate([u[i], vg[i]], axis=0),
        jnp.concatenate([col(bi, gi, 'bh'), col(bi, gi, 'kh')], axis=0), TN, md['UP']))
    for i, (bi, gi) in enumerate(chains):
        s_scr[bi, gi] = s_old[i] * col(bi, gi, 'g_last') + jnp.where(bd_state, upd[i], 0.0)

    inv = 1.0 / hd
    n_ch = len(chains)
    sums = _dot_split(jnp.concatenate(y + each(lambda i, bi, gi: col(bi, gi, 'rkk')), axis=0), seg)
    mean = [sums[i * L:(i + 1) * L] * inv for i in range(n_ch)]
    bonus = [sums[(n_ch + i) * L:(n_ch + i + 1) * L] * vg[i] for i in range(n_ch)]
    yc = [y[i] - mean[i] for i in range(n_ch)]
    sq = _dot_split(jnp.concatenate([z * z for z in yc], axis=0), seg)
    var = [sq[i * L:(i + 1) * L] * inv for i in range(n_ch)]
    for i, (bi, gi) in enumerate(chains):
        sl = slice(gi * gw, (gi + 1) * gw)
        yn = yc[i] * lax.rsqrt(var[i] + GN_EPS) * lnw_ref[:, sl] + lnb_ref[:, sl]
        y_ref[bi, :, sl] = (yn + bonus[i]) * g_ref[bi, :, sl]

    @pl.when(ci == pl.num_programs(1) - 1)
    def _():
        for bi in range(nb):
            for gi in range(n_groups):
                bd = s_scr[bi, gi]
                folded = bd[:, 0:hd]
                for j in range(1, gh):
                    folded = folded + bd[:, j * hd:(j + 1) * hd]
                sout_ref[bi, gi * gh:(gi + 1) * gh] = folded.reshape(gh, hd, hd)


def _rwkv_call(vecs, s0, p, n_seq, seq_len, chunk, nb):
    c = p['c_rwkv']
    n_heads = c // HEAD_DIM
    n_chunks = seq_len // chunk
    assert n_seq % nb == 0 and seq_len % chunk == 0
    vecs = [z.reshape(n_seq, seq_len, c) for z in vecs]
    tok = pl.BlockSpec((nb, chunk, c), lambda bi, ci: (bi, ci, 0))
    st = pl.BlockSpec((nb, n_heads, HEAD_DIM, HEAD_DIM), lambda bi, ci: (bi, 0, 0, 0))
    rowvec = pl.BlockSpec((1, c), lambda bi, ci: (0, 0))
    seg = p['seg'][:GROUP_LANES, :GROUP_LANES]
    y, s_out = pl.pallas_call(
        _rwkv_kernel,
        grid=(n_seq // nb, n_chunks),
        in_specs=[tok] * 7 + [st, rowvec, rowvec, rowvec, pl.BlockSpec(seg.shape, lambda bi, ci: (0, 0))],
        out_specs=[tok, st],
        out_shape=[jax.ShapeDtypeStruct((n_seq, seq_len, c), F32),
                   jax.ShapeDtypeStruct((n_seq, n_heads, HEAD_DIM, HEAD_DIM), F32)],
        scratch_shapes=[pltpu.VMEM((nb, c // GROUP_LANES, GROUP_LANES, GROUP_LANES), F32)],
        compiler_params=pltpu.CompilerParams(dimension_semantics=("arbitrary", "arbitrary"),
                                             vmem_limit_bytes=V7X_VMEM_LIMIT),
        name="rwkv",
    )(*vecs, s0, p['r_k'], p['ln_w'], p['ln_b'], seg)
    return y.reshape(n_seq * seq_len, c), s_out


def _attn_prompt_kernel(q_ref, k_ref, v_ref, o_ref, m_scr, l_scr, acc_scr):
    s_len = q_ref.shape[0]
    band = ATT_BAND
    n_blk = s_len // band

    lane = lax.broadcasted_iota(jnp.int32, (1, LANES), 1)
    head0 = lane < HEAD_DIM
    qi = lax.broadcasted_iota(jnp.int32, (band, 2 * band), 0)
    kj = lax.broadcasted_iota(jnp.int32, (band, 2 * band), 1)
    in_band = (kj >= qi) & (kj <= qi + band)
    in_band2 = jnp.concatenate([in_band, in_band], axis=0)
    kj2 = jnp.concatenate([kj, kj], axis=0)
    scale = HEAD_DIM ** -0.5
    ones = jnp.ones((2 * band, LANES), BF16)

    for ci, (window, dil) in enumerate(DILATED_CONFIGS):
        assert window // dil == band
        per_res = n_blk // dil

        unroll = ATT_UNROLL[dil]
        run = min(unroll, per_res)

        def body(it, carry, ci=ci, dil=dil, per_res=per_res, run=run, unroll=unroll):
            span = band * dil
            tile = lambda start: pl.ds(start, band, stride=dil) if dil > 1 else pl.ds(start, band)
            blocks = []
            tiles = []
            first_dyn = []
            for r in range(unroll // run):
                i0 = it * unroll + r * run
                blk0 = i0 % per_res
                start0 = i0 // per_res + blk0 * span
                base = len(tiles)
                if per_res > run:
                    tiles.append(tile(jnp.maximum(start0 - span, 0)))
                    first_dyn.append(blk0 == 0)
                else:
                    tiles.append(None)
                    first_dyn.append(None)
                for t in range(run):
                    tiles.append(tile(start0 + t * span))
                    blocks.append((tiles[-1], base + t, r if t == 0 else None))
            kt = [None if w is None else k_ref[w, :].astype(BF16) for w in tiles]
            vt = [None if w is None else v_ref[w, :].astype(BF16) for w in tiles]
            prev = lambda ts, i: ts[i + 1] if ts[i] is None else ts[i]
            q = [q_ref[rows, :] * scale for rows, _, _ in blocks]

            def mask_of(first):
                if first is None:
                    return in_band2
                if first_dyn[first] is None:
                    return in_band2 & (kj2 >= band)
                return in_band2 & (kj2 >= jnp.where(first_dyn[first], band, 0))

            s = [jnp.where(mask_of(first), lax.dot_general(
                jnp.concatenate([jnp.where(head0, qj, 0.0), jnp.where(head0, 0.0, qj)], axis=0).astype(BF16),
                jnp.concatenate([prev(kt, i), kt[i + 1]], axis=0), NT, preferred_element_type=F32), NEG_INF)
                 for qj, (_, i, first) in zip(q, blocks)]
            m = [jnp.max(z, axis=-1, keepdims=True) for z in s]
            p = [jnp.exp(z - mx).astype(BF16) for z, mx in zip(s, m)]
            o = [_dot(pj, jnp.concatenate([jnp.concatenate([prev(vt, i), vt[i + 1]], axis=0), ones], axis=1))
                 for pj, (_, i, _) in zip(p, blocks)]
            for j, (rows, _, _) in enumerate(blocks):
                m_scr[ci, rows, :] = jnp.where(head0, m[j][:band], m[j][band:])
                acc_scr[ci, rows, :] = jnp.where(head0, o[j][:band, :LANES], o[j][band:, :LANES])
                l_scr[ci, rows, :] = jnp.where(head0, o[j][:band, LANES:], o[j][band:, LANES:])
            return carry

        lax.fori_loop(0, n_blk // unroll, body, 0)

    rows_per = 256

    def merge(i, carry):
        rows = pl.ds(pl.multiple_of(i * rows_per, rows_per), rows_per)
        ms = [m_scr[ci, rows, :] for ci in range(len(DILATED_CONFIGS))]
        m_all = functools.reduce(jnp.maximum, ms)
        num = jnp.zeros((rows_per, LANES), F32)
        den = jnp.zeros((rows_per, LANES), F32)
        for ci, m_c in enumerate(ms):
            w_c = jnp.exp(m_c - m_all)
            num = num + w_c * acc_scr[ci, rows, :]
            den = den + w_c * l_scr[ci, rows, :]
        o_ref[rows, :] = num / den
        return carry

    lax.fori_loop(0, s_len // rows_per, merge, 0)


def _attn_prompt_call(q, k, v, n_seq, seq_len):
    c = q.shape[1]
    n_pairs = c // LANES
    blk = pl.BlockSpec((seq_len, LANES), lambda bi, hi: (bi, hi))
    return pl.pallas_call(
        _attn_prompt_kernel,
        grid=(n_seq, n_pairs),
        in_specs=[blk, blk, blk],
        out_specs=blk,
        out_shape=jax.ShapeDtypeStruct((n_seq * seq_len, c), F32),
        scratch_shapes=[pltpu.VMEM((len(DILATED_CONFIGS), seq_len, LANES), F32)] * 3,
        compiler_params=pltpu.CompilerParams(dimension_semantics=("arbitrary", "arbitrary"),
                                             vmem_limit_bytes=V7X_VMEM_LIMIT),
        name="attn_prompt",
    )(q, k, v)


def _attn_sample_kernel(q_ref, kn_ref, vn_ref, kc_ref, vc_ref, o_ref, *, n_new):
    hd = HEAD_DIM
    _, n_heads, _, n_buf = kc_ref.shape
    t_pad = kn_ref.shape[1]
    c = q_ref.shape[2]
    q = q_ref[0] * (hd ** -0.5)
    lane_head = lax.broadcasted_iota(jnp.int32, (1, c), 1) // hd
    qs = jnp.concatenate([jnp.where(lane_head == h, q, 0.0) for h in range(n_heads)], axis=0).astype(BF16)
    n_rows = n_heads * t_pad
    t_idx = lax.broadcasted_iota(jnp.int32, (n_rows, 1), 0) % t_pad

    def multiplicity(dist):
        mult = jnp.zeros(dist.shape, F32)
        for window, dil in DILATED_CONFIGS:
            hit = (dist >= 0) & (dist <= window) & (dist % dil == 0)
            mult = mult + jnp.where(hit, 1.0, 0.0)
        return mult

    jc = lax.broadcasted_iota(jnp.int32, (1, n_buf), 1)
    mult_c = multiplicity(n_buf + t_idx - jc)
    jn = lax.broadcasted_iota(jnp.int32, (1, t_pad), 1)
    mult_n = jnp.where(jn < n_new, multiplicity(t_idx - jn), 0.0)

    sc = jnp.concatenate(
        [_dot(qs[h * t_pad:(h + 1) * t_pad, h * hd:(h + 1) * hd], kc_ref[0, h].astype(BF16))
         for h in range(n_heads)], axis=0)
    sn = lax.dot_general(qs, kn_ref[0].astype(BF16), NT, preferred_element_type=F32)
    sc = jnp.where(mult_c > 0.0, sc, NEG_INF)
    sn = jnp.where(mult_n > 0.0, sn, NEG_INF)
    m = jnp.maximum(jnp.max(sc, axis=-1, keepdims=True), jnp.max(sn, axis=-1, keepdims=True))
    pc = (mult_c * jnp.exp(sc - m)).astype(BF16)
    pn = mult_n * jnp.exp(sn - m)
    inv_l = 1.0 / (jnp.sum(pc.astype(F32), axis=-1, keepdims=True) + jnp.sum(pn, axis=-1, keepdims=True))
    o_new = _dot(pn.astype(BF16), vn_ref[0].astype(BF16)) * inv_l
    out = jnp.zeros((t_pad, c), F32)
    for h in range(n_heads):
        out = out + jnp.where(lane_head == h, o_new[h * t_pad:(h + 1) * t_pad, :], 0.0)
    o_buf = [lax.dot_general(pc[h * t_pad:(h + 1) * t_pad, :], vc_ref[0, h].astype(BF16), NT,
                             preferred_element_type=F32) * inv_l[h * t_pad:(h + 1) * t_pad, :]
             for h in range(n_heads)]
    o_ref[0] = out + jnp.concatenate(o_buf, axis=1)


def _attn_sample_call(q, kn, vn, k_buf, v_buf, n_new):
    b, t_pad, c = q.shape
    _, n_buf, n_heads, hd = k_buf.shape
    k_t = jnp.transpose(k_buf, (0, 2, 3, 1))
    v_t = jnp.transpose(v_buf, (0, 2, 3, 1))
    new = pl.BlockSpec((1, t_pad, c), lambda bi: (bi, 0, 0))
    buf = pl.BlockSpec((1, n_heads, hd, n_buf), lambda bi: (bi, 0, 0, 0))
    return pl.pallas_call(
        functools.partial(_attn_sample_kernel, n_new=n_new),
        grid=(b,),
        in_specs=[new, new, new, buf, buf],
        out_specs=new,
        out_shape=jax.ShapeDtypeStruct((b, t_pad, c), F32),
        compiler_params=pltpu.CompilerParams(dimension_semantics=("arbitrary",), vmem_limit_bytes=V7X_VMEM_LIMIT),
        name="attn_sample",
    )(q, kn, vn, k_t, v_t)


def _route_rows(logits, seen):
    lane = lax.broadcasted_iota(jnp.int32, logits.shape, 1)
    lane_f = lane.astype(F32)
    first = lambda hit: jnp.min(jnp.where(hit, lane_f, float(LANES)), axis=-1, keepdims=True)
    is_g = lane < N_GROUPS
    lg = jnp.where(is_g, logits, NEG_INF)
    g_max = jnp.max(lg, axis=-1, keepdims=True)
    g_idx = first(lg == g_max)
    g_w = 1.0 / jnp.sum(jnp.where(is_g, jnp.exp(lg - g_max), 0.0), axis=-1, keepdims=True)
    lo = N_GROUPS + EXPERTS_PER_GROUP * g_idx
    le = jnp.where((lane_f >= lo) & (lane_f < lo + EXPERTS_PER_GROUP), logits, NEG_INF)
    e1 = jnp.max(le, axis=-1, keepdims=True)
    i1 = first(le == e1)
    le2 = jnp.where(lane_f == i1, NEG_INF, le)
    e2 = jnp.max(le2, axis=-1, keepdims=True)
    i2 = first(le2 == e2)
    ex = jnp.exp(e2 - e1)
    gate1 = g_w / (1.0 + ex)
    gate2 = g_w * ex / (1.0 + ex)
    tm = logits.shape[0]
    pick1 = lane_f == i1
    pick2 = lane_f == i2
    picks = jnp.where(pick1 | pick2, 1.0, 0.0)
    earlier = (lax.broadcasted_iota(jnp.int32, (tm, tm), 0) > lax.broadcasted_iota(jnp.int32, (tm, tm), 1))
    before = seen + _dot(earlier.astype(BF16), picks.astype(BF16))
    rank1 = jnp.sum(jnp.where(pick1, before, 0.0), axis=-1, keepdims=True)
    rank2 = jnp.sum(jnp.where(pick2, before, 0.0), axis=-1, keepdims=True)
    out = jnp.where(lane == 0, gate1, jnp.where(lane == 1, gate2, 0.0))
    out = jnp.where(lane == 2, i1 - N_GROUPS, jnp.where(lane == 3, i2 - N_GROUPS, out))
    out = jnp.where(lane == 4, rank1, jnp.where(lane == 5, rank2, out))
    return out, jnp.sum(picks, axis=0, keepdims=True)


def _post_kernel(x_ref, yr_ref, ya_ref, wo_ref, gffn_ref, rw_cat_ref, rb_ref,
                 h_o, hn_o, lg_o, cnt_o, seen_scr):
    @pl.when(pl.program_id(0) == 0)
    def _():
        seen_scr[...] = jnp.zeros_like(seen_scr)

    c = yr_ref.shape[1]
    h = (x_ref[...] + _dot(yr_ref[...].astype(BF16), wo_ref[:c, :]) + _dot(ya_ref[...].astype(BF16), wo_ref[c:, :]))
    ms = jnp.mean(h * h, axis=-1, keepdims=True)
    hn = (h * lax.rsqrt(ms + NORM_EPS)) * gffn_ref[...]
    h_o[...] = h
    bits = pltpu.bitcast(hn.astype(BF16).astype(F32), jnp.uint32)
    half = hn.shape[1] // 2
    hn_o[...] = (bits[:, :half] >> 16) | (bits[:, half:] & jnp.uint32(0xFFFF0000))
    hi = hn.astype(BF16)
    lo = (hn - hi.astype(F32)).astype(BF16)
    both = _dot(hi, rw_cat_ref[...])
    logits = (both[:, :LANES] + both[:, LANES:] + _dot(lo, rw_cat_ref[:, :LANES])) + rb_ref[...]
    route, picked = _route_rows(logits, seen_scr[...])
    lg_o[...] = route
    seen_scr[...] = seen_scr[...] + picked
    cnt_o[...] = seen_scr[...]


def _post_call(x2, yr, ya, p, tm):
    t, d = x2.shape
    c = yr.shape[1]
    full = lambda a: pl.BlockSpec(a.shape, lambda i: (0,) * a.ndim)
    tok = lambda w: pl.BlockSpec((tm, w), lambda i: (i, 0))
    weights = [p['wout'], p['gffn'], p['rw_cat'], p['rb']]
    return pl.pallas_call(
        _post_kernel,
        grid=(t // tm,),
        in_specs=[tok(d), tok(c), tok(c)] + [full(w) for w in weights],
        out_specs=[tok(d), tok(d // 2), tok(LANES), pl.BlockSpec((1, LANES), lambda i: (0, 0))],
        out_shape=[jax.ShapeDtypeStruct((t, d), F32), jax.ShapeDtypeStruct((t, d // 2), jnp.uint32),
                   jax.ShapeDtypeStruct((t, LANES), F32), jax.ShapeDtypeStruct((1, LANES), F32)],
        scratch_shapes=[pltpu.VMEM((1, LANES), F32)],
        compiler_params=pltpu.CompilerParams(dimension_semantics=("arbitrary",), vmem_limit_bytes=V7X_VMEM_LIMIT),
        name="post",
    )(x2, yr, ya, *weights)


def _expert_kernel(be_ref, nb_ref, xs_ref, w1_ref, w3_ref, w2_ref, y_ref, w1_s, w3_s, w2_s):
    i = pl.program_id(0)
    live = i < nb_ref[0]

    @pl.when(live & ((i == 0) | (be_ref[i] != be_ref[jnp.maximum(i - 1, 0)])))
    def _():
        w1_s[...] = w1_ref[...].astype(BF16)
        w3_s[...] = w3_ref[...].astype(BF16)
        w2_s[...] = w2_ref[...].astype(BF16)

    @pl.when(live)
    def _():
        packed = xs_ref[...]
        half = packed.shape[1]
        lo = pltpu.bitcast(packed << 16, F32).astype(BF16)
        hi = pltpu.bitcast(packed & jnp.uint32(0xFFFF0000), F32).astype(BF16)
        h1 = _dot(lo, w1_s[:half, :]) + _dot(hi, w1_s[half:, :])
        h3 = _dot(lo, w3_s[:half, :]) + _dot(hi, w3_s[half:, :])
        act = (h1 * _sigmoid(h1)) * h3
        y_ref[...] = _dot(act.astype(BF16), w2_s[...])

    @pl.when(jnp.logical_not(live))
    def _():
        y_ref[...] = jnp.zeros_like(y_ref)


def _expert_call(blk_exp, n_used, xs, p, bm):
    n_slots = xs.shape[0]
    _, d, de = p['w1'].shape
    grid_spec = pltpu.PrefetchScalarGridSpec(
        num_scalar_prefetch=2,
        grid=(n_slots // bm,),
        in_specs=[pl.BlockSpec((bm, d // 2), lambda i, be, nb: (i, 0)),
                  pl.BlockSpec((None, d, de), lambda i, be, nb: (be[i], 0, 0)),
                  pl.BlockSpec((None, d, de), lambda i, be, nb: (be[i], 0, 0)),
                  pl.BlockSpec((None, de, d), lambda i, be, nb: (be[i], 0, 0))],
        out_specs=pl.BlockSpec((bm, d), lambda i, be, nb: (i, 0)),
        scratch_shapes=[pltpu.VMEM((d, de), BF16), pltpu.VMEM((d, de), BF16), pltpu.VMEM((de, d), BF16)],
    )
    return pl.pallas_call(
        _expert_kernel,
        grid_spec=grid_spec,
        out_shape=jax.ShapeDtypeStruct((n_slots, d), F32),
        compiler_params=pltpu.CompilerParams(dimension_semantics=("arbitrary",), vmem_limit_bytes=V7X_VMEM_LIMIT),
        name="experts",
    )(blk_exp, n_used, xs, p['w1'], p['w3'], p['w2'])


def _final_kernel(h_ref, route_ref, y1_ref, y2_ref, gfin_ref, o_ref):
    route = route_ref[...]
    h = h_ref[...] + (route[:, 0:1] * y1_ref[...] + route[:, 1:2] * y2_ref[...])
    ms = jnp.mean(h * h, axis=-1, keepdims=True)
    o_ref[...] = (h * lax.rsqrt(ms + NORM_EPS)) * gfin_ref[...]


def _final_call(h, route, y12, gfin, tm):
    t, d = h.shape
    tok = pl.BlockSpec((tm, d), lambda i: (i, 0))
    routed = lambda a: pl.BlockSpec((None, tm, d), lambda i: (a, i, 0))
    return pl.pallas_call(
        _final_kernel,
        grid=(t // tm,),
        in_specs=[tok, pl.BlockSpec((tm, LANES), lambda i: (i, 0)), routed(0), routed(1),
                  pl.BlockSpec((1, d), lambda i: (0, 0))],
        out_specs=tok,
        out_shape=jax.ShapeDtypeStruct((t, d), F32),
        compiler_params=pltpu.CompilerParams(dimension_semantics=("arbitrary",), vmem_limit_bytes=V7X_VMEM_LIMIT),
        name="final",
    )(h, route, y12, y12, gfin)


def _gather_rows(table, idx):
    info = plsc.get_sparse_core_info()
    nc, ns = info.num_cores, info.num_subcores
    b, d = idx.shape[0], table.shape[1]
    chunk = min(SC_INDEX_LIMIT, SC_GATHER_BYTES // (d * table.dtype.itemsize))
    assert b % (nc * ns * chunk * 2) == 0, "rows must split evenly into chunk pairs per subcore"
    per_w = b // (nc * ns)
    n_chunks = per_w // chunk
    mesh = plsc.VectorSubcoreMesh(core_axis_name="c", subcore_axis_name="s")

    @functools.partial(
        pl.kernel, mesh=mesh, out_type=jax.ShapeDtypeStruct((b, d), table.dtype),
        scratch_types=[pltpu.VMEM((per_w,), jnp.int32), pltpu.VMEM((2, chunk, d), table.dtype),
                       pltpu.SemaphoreType.DMA((2,)), pltpu.SemaphoreType.DMA((2,))])
    def gather(table_hbm, idx_hbm, out_hbm, idx_v, rows_v, fetch_sem, put_sem):
        base = (lax.axis_index("s") * nc + lax.axis_index("c")) * per_w
        pltpu.sync_copy(idx_hbm.at[pl.ds(base, per_w)], idx_v)

        def fetch(c, slot):
            off = pl.multiple_of(c * chunk, chunk)
            return pltpu.make_async_copy(table_hbm.at[idx_v.at[pl.ds(off, chunk)]], rows_v.at[slot],
                                         fetch_sem.at[slot])

        def put(c, slot):
            off = pl.multiple_of(c * chunk, chunk)
            return pltpu.make_async_copy(rows_v.at[slot], out_hbm.at[pl.ds(base + off, chunk)], put_sem.at[slot])

        fetch(0, 0).start()

        @pl.loop(0, n_chunks, step=2)
        def _(c):
            @pl.when(c > 0)
            def _():
                put(c - 1, 1).wait()
            fetch(c + 1, 1).start()
            fetch(c, 0).wait()
            put(c, 0).start()
            fetch(c + 1, 1).wait()
            put(c, 0).wait()

            @pl.when(c + 2 < n_chunks)
            def _():
                fetch(c + 2, 0).start()
            put(c + 1, 1).start()

        put(n_chunks - 1, 1).wait()

    return gather(table, idx)


def _route(route, counts, bm):
    n = route.shape[0]
    eid = route[:, 2:4].astype(jnp.int32)
    rank = route[:, 4:6].astype(jnp.int32)
    m = 2 * n
    experts = jnp.arange(N_EXPERTS, dtype=jnp.int32)
    counts = counts[0, N_GROUPS:N_GROUPS + N_EXPERTS].astype(jnp.int32)
    starts = jnp.cumsum(counts) - counts
    padded = (counts + bm - 1) // bm * bm
    p_ends = jnp.cumsum(padded)
    p_starts = p_ends - padded
    lookup = lambda tbl, e: jnp.sum(jnp.where(e[..., None] == experts, tbl, 0), axis=-1)
    dest = lookup(p_starts, eid) + rank
    bits = max(1, (m - 1).bit_length())
    order = lax.sort(eid.reshape(-1) * (1 << bits) + jnp.arange(m, dtype=jnp.int32)) & ((1 << bits) - 1)
    n_blocks = _round_up(-(-m // bm) + N_EXPERTS, SLOT_MULTIPLE // bm)
    blk_start = jnp.arange(n_blocks, dtype=jnp.int32) * bm
    blk_exp = jnp.minimum(jnp.sum((p_ends[None, :] <= blk_start[:, None]).astype(jnp.int32), axis=1), N_EXPERTS - 1)
    pos = (blk_start - lookup(p_starts, blk_exp))[:, None] + jnp.arange(bm, dtype=jnp.int32)[None, :]
    valid = (pos < lookup(counts, blk_exp)[:, None]).reshape(-1)
    src = jnp.where(valid, (lookup(starts, blk_exp)[:, None] + pos).reshape(-1), 0)
    slot_tok = jnp.where(valid, order.at[src].get(mode='promise_in_bounds') // 2,
                         jnp.arange(valid.shape[0], dtype=jnp.int32) % n)
    n_used = (p_ends[-1] // bm).astype(jnp.int32).reshape(1)
    return slot_tok, dest, blk_exp.astype(jnp.int32), n_used


def _moe_dispatch(x2, yr, ya, p, tm, bm):
    h, hn, route, counts = _post_call(x2, yr, ya, p, tm)
    slot_tok, dest, blk_exp, n_used = _route(route, counts, bm)
    return dict(h=h, route=route, dest=dest, blk_exp=blk_exp, n_used=n_used, xs=_gather_rows(hn, slot_tok))


def _moe_combine(ctx, yb, p, tm):
    h = ctx['h']
    y12 = _gather_rows(yb, ctx['dest'].T.reshape(-1)).reshape(2, h.shape[0], h.shape[1])
    return _final_call(h, ctx['route'], y12, p['gfin'], tm)


def _prep_params(layer, norm_mix_g, w_in, rwkv_mu_rkv, rwkv_mu_wag, rwkv_w0, rwkv_w1, rwkv_w2, rwkv_a0, rwkv_a1,
                 rwkv_a2, rwkv_g1, rwkv_g2, rwkv_k_k, rwkv_k_a, rwkv_r_k, rwkv_ln_w, rwkv_ln_b, w_out, norm_ffn_g,
                 router_group_w, router_group_b, router_expert_w, router_expert_b, expert_w1, expert_w3, expert_w2,
                 norm_final_g):
    d = w_in.shape[1]
    c = rwkv_w0.shape[1]
    row = lambda a: a.reshape(1, -1).astype(F32)
    lowrank = jnp.concatenate([rwkv_w1[layer], rwkv_a1[layer], rwkv_g1[layer]], axis=1)
    mx = rwkv_mu_wag[layer]
    r_w = rwkv_w1.shape[2]
    r_a = rwkv_a1.shape[2]
    r_g = rwkv_g1.shape[2]
    assert r_w + r_a == LANES and r_g == LANES
    mx_cols = jnp.concatenate([jnp.broadcast_to(mx[0][:, None], (d, r_w)), jnp.broadcast_to(mx[1][:, None], (d, r_a)),
                               jnp.broadcast_to(mx[2][:, None], (d, r_g))], axis=1)
    w2a2 = jnp.zeros((LANES, 2 * c), F32)
    w2a2 = w2a2.at[:r_w, :c].set(rwkv_w2[layer]).at[r_w:, c:].set(rwkv_a2[layer])
    head = jnp.arange(c) // HEAD_DIM
    rw = jnp.zeros((d, LANES), F32)
    rw = rw.at[:, :N_GROUPS].set(router_group_w[layer]).at[:, N_GROUPS:N_GROUPS + N_EXPERTS].set(router_expert_w[layer])
    rw_hi = rw.astype(BF16)
    rb = jnp.zeros((1, LANES), F32)
    rb = rb.at[0, :N_GROUPS].set(router_group_b[layer]).at[0, N_GROUPS:N_GROUPS + N_EXPERTS].set(router_expert_b[layer])
    return {
        'c_rwkv': c,
        'gmix': row(norm_mix_g[layer]),
        'win': jnp.concatenate([w_in[layer], lowrank], axis=1).astype(BF16),
        'wdx': (mx_cols * lowrank).astype(BF16),
        'mu_rkv': rwkv_mu_rkv[layer],
        'w0a0': jnp.concatenate([row(rwkv_w0[layer]), row(rwkv_a0[layer])], axis=1),
        'w2a2': w2a2.astype(BF16),
        'g2': rwkv_g2[layer].astype(BF16),
        'k_k': row(rwkv_k_k[layer]),
        'k_a': row(rwkv_k_a[layer]),
        'seg': (head[:, None] == head[None, :]).astype(BF16),
        'r_k': row(rwkv_r_k[layer]),
        'ln_w': row(rwkv_ln_w[layer]),
        'ln_b': row(rwkv_ln_b[layer]),
        'wout': w_out[layer].astype(BF16),
        'gffn': row(norm_ffn_g[layer]),
        'rw_cat': jnp.concatenate([rw_hi, (rw - rw_hi.astype(F32)).astype(BF16)], axis=1),
        'rb': rb,
        'w1': expert_w1[layer],
        'w3': expert_w3[layer],
        'w2': expert_w2[layer],
        'gfin': row(norm_final_g),
    }


def _prompt_mix(x, p):
    b, s, d = x.shape
    c = p['c_rwkv']
    x2 = x.reshape(b * s, d)
    tm = PROJ_TILE
    flag = jnp.zeros((b * s, 1), F32)
    keep = min(max(w for w, _ in DILATED_CONFIGS), s)
    assert keep % tm == 0
    r, lw, k, v, kk, bb, g, qa, ka, va, xl, kt, vt = _proj_call(x2, flag, p, s // tm, tm, keep_tiles=keep // tm)
    s0 = jnp.zeros((b, c // HEAD_DIM, HEAD_DIM, HEAD_DIM), F32)
    yr, s_new = _rwkv_call((r, lw, k, v, kk, bb, g), s0, p, b, s, RWKV_CHUNK, 8)
    ya = _attn_prompt_call(qa, ka, va, b, s)
    shift = xl.reshape(b, s // tm, 8, d)[:, -1, 7, :]
    to_cache = lambda z: jnp.transpose(z.reshape(b, c // HEAD_DIM, HEAD_DIM, keep), (0, 3, 1, 2))
    k_keep, v_keep = to_cache(kt), to_cache(vt)
    return (x2, yr, ya), (s_new, shift, k_keep, v_keep)


def _sample_mix(x, shift0, s0, k_buf, v_buf, p, after):
    b, t, d = x.shape
    c = p['c_rwkv']
    n_heads = c // HEAD_DIM
    t_pad = 8
    xc = jnp.concatenate([shift0[:, None, :], x, jnp.zeros((b, t_pad - 1 - t, d), x.dtype)], axis=1)
    flag = jnp.zeros((b, t_pad, 1), F32).at[:, 0].set(1.0)
    outs = _proj_call(xc.reshape(b * t_pad, d), flag.reshape(b * t_pad, 1), p, 1, b * t_pad, xl_rows=b * t_pad)
    xl = outs[10]
    live = (jnp.arange(t_pad) < t)[None, :, None]
    shifted = [jnp.where(live, jnp.roll(o.reshape(b, t_pad, c), -1, axis=1), 0.0) for o in outs[:10]]
    r, lw, k, v, kk, bb, g, qa, ka, va = shifted
    flat = lambda z: z.reshape(b * t_pad, c)
    yr, s_new = _rwkv_call(tuple(flat(z) for z in (r, lw, k, v, kk, bb, g)), s0, p, b, t_pad, t_pad, 8)
    qa_held, _ = lax.optimization_barrier((qa, after))
    ya = _attn_sample_call(qa_held, ka, va, k_buf, v_buf, t)
    x_pad = jnp.concatenate([x, jnp.zeros((b, t_pad - t, d), x.dtype)], axis=1).reshape(b * t_pad, d)
    shift = xl.reshape(b, t_pad, d)[:, t]
    return ((x_pad, yr, flat(ya)),
            (s_new, shift, ka[:, :t].reshape(b, t, n_heads, HEAD_DIM), va[:, :t].reshape(b, t, n_heads, HEAD_DIM)))


def kernel(x_prompt, x_sample, state_rwkv, state_shift, cache_att_k, cache_att_v, norm_mix_g, w_in, rwkv_mu_rkv, rwkv_mu_wag, rwkv_w0, rwkv_w1, rwkv_w2, rwkv_a0, rwkv_a1, rwkv_a2, rwkv_g1, rwkv_g2, rwkv_k_k, rwkv_k_a, rwkv_r_k, rwkv_ln_w, rwkv_ln_b, w_out, norm_ffn_g, router_group_w, router_group_b, router_expert_w, router_expert_b, expert_w1, expert_w3, expert_w2, norm_final_g):
    assert w_in.shape[0] == 1, "single-layer trunk"
    p = _prep_params(0, norm_mix_g, w_in, rwkv_mu_rkv, rwkv_mu_wag, rwkv_w0, rwkv_w1, rwkv_w2, rwkv_a0, rwkv_a1,
                     rwkv_a2, rwkv_g1, rwkv_g2, rwkv_k_k, rwkv_k_a, rwkv_r_k, rwkv_ln_w, rwkv_ln_b, w_out,
                     norm_ffn_g, router_group_w, router_group_b, router_expert_w, router_expert_b, expert_w1,
                     expert_w3, expert_w2, norm_final_g)
    moe_p, (rw_p, sh_p, kc_p, vc_p) = _prompt_mix(x_prompt, p)
    ctx_p = _moe_dispatch(*moe_p, p, PROJ_TILE, EXPERT_TILE)
    yb_p = _expert_call(ctx_p['blk_exp'], ctx_p['n_used'], ctx_p['xs'], p, EXPERT_TILE)
    y12_p = _gather_rows(yb_p, ctx_p['dest'].T.reshape(-1))
    moe_s, (rw_s, sh_s, kc_s, vc_s) = _sample_mix(x_sample, state_shift[0], state_rwkv[0], cache_att_k[0],
                                                   cache_att_v[0], p, after=yb_p)
    tm_s = moe_s[0].shape[0] // 2
    ctx_s = _moe_dispatch(*moe_s, p, tm_s, SAMPLE_EXPERT_TILE)
    yb_s = _expert_call(ctx_s['blk_exp'], ctx_s['n_used'], ctx_s['xs'], p, SAMPLE_EXPERT_TILE)
    h_p = ctx_p['h']
    y_p = _final_call(h_p, ctx_p['route'], y12_p.reshape(2, *h_p.shape), p['gfin'], PROJ_TILE).reshape(x_prompt.shape)
    y_s = _moe_combine(ctx_s, yb_s, p, tm_s)
    y_s = y_s.reshape(x_sample.shape[0], -1, x_sample.shape[2])[:, :x_sample.shape[1]]
    return (y_p, y_s, rw_p[None], sh_p[None], kc_p[None], vc_p[None], rw_s[None], sh_s[None], kc_s[None], vc_s[None])
```

```python
import functools
import math

import jax
import jax.numpy as jnp
from jax import lax
from jax.experimental import pallas as pl
from jax.experimental.pallas import tpu as pltpu
from jax.experimental.pallas import tpu_sc as plsc

F32 = jnp.float32
BF16 = jnp.bfloat16

HEAD_DIM = 64
GN_EPS = 64e-5
NORM_EPS = 1e-6
DILATED_CONFIGS = ((128, 1), (512, 4), (2048, 16))
N_GROUPS = 4
EXPERTS_PER_GROUP = 8
N_EXPERTS = N_GROUPS * EXPERTS_PER_GROUP
NEG_INF = -1e30

V7X_VMEM_LIMIT = 56 * 1024 * 1024
LANES = 128

PROJ_TILE = 512
RWKV_CHUNK = 64
ATT_BAND = 128
EXPERT_TILE = 512
ATT_UNROLL = {1: 8, 4: 8, 16: 4}
SC_GATHER_BYTES = 128 * 1024
SC_INDEX_LIMIT = 128
SLOT_MULTIPLE = 4096
SAMPLE_EXPERT_TILE = 128

NN = (((1,), (0,)), ((), ()))
NT = (((1,), (1,)), ((), ()))
TN = (((0,), (0,)), ((), ()))


def _dot(a, b):
    return jnp.dot(a, b, preferred_element_type=F32)


def _dot_split(a, b_bf16):
    hi = a.astype(BF16)
    lo = (a - hi.astype(F32)).astype(BF16)
    return _dot(hi, b_bf16) + _dot(lo, b_bf16)


def _mm(a, b, dims):
    return lax.dot_general(a.astype(BF16), b.astype(BF16), dims, preferred_element_type=F32)


def _round_up(x, k):
    return -(-x // k) * k


def _sigmoid(z):
    return 1.0 / (1.0 + jnp.exp(-z))


def _proj_kernel(x_ref, flag_ref, gmix_ref, win_ref, wdx_ref, mu_ref, w0a0_ref, w2a2_ref, g2_ref,
                 kk_ref, ka_ref, seg_ref,
                 r_o, lw_o, k_o, v_o, kkn_o, b_o, g_o, qa_o, kat_o, vat_o, xl_o, *rest,
                 tiles_per_seq, c_rwkv, keep_tiles):
    xn_carry, pj_carry = rest[-2:]
    i = pl.program_id(0)

    @pl.when(i % tiles_per_seq == 0)
    def _():
        xn_carry[...] = jnp.zeros_like(xn_carry)
        pj_carry[...] = jnp.zeros_like(pj_carry)

    c = c_rwkv
    x = x_ref[...]
    tm = x.shape[0]
    ms = jnp.mean(x * x, axis=-1, keepdims=True)
    xn = (x * lax.rsqrt(ms + NORM_EPS)) * gmix_ref[...]
    xn = jnp.where(flag_ref[...] > 0.0, x, xn)
    row = lax.broadcasted_iota(jnp.int32, (tm, 1), 0)
    xn_prev = jnp.where(row == 0, xn_carry[7:8, :], pltpu.roll(xn, 1, axis=0))
    dx = xn_prev - xn

    proj = _dot(xn.astype(BF16), win_ref[...])
    cur = proj[:, :3 * c]
    prev = jnp.where(row == 0, pj_carry[7:8, :], pltpu.roll(cur, 1, axis=0))
    xn_carry[...] = xn[tm - 8:, :]
    pj_carry[...] = cur[tm - 8:, :]
    xl_rows = xl_o.shape[1]
    xl_o[0] = xn[tm - xl_rows:, :]

    mu = mu_ref[...]
    r = cur[:, :c] + mu[0:1] * (prev[:, :c] - cur[:, :c])
    k = cur[:, c:2 * c] + mu[1:2] * (prev[:, c:2 * c] - cur[:, c:2 * c])
    v = cur[:, 2 * c:3 * c] + mu[2:3] * (prev[:, 2 * c:3 * c] - cur[:, 2 * c:3 * c])

    lr = proj[:, 6 * c:] + _dot(dx.astype(BF16), wdx_ref[...])
    lane = lax.broadcasted_iota(jnp.int32, (1, LANES), 1)
    wa_in = jnp.where(lane < 64, jnp.tanh(lr[:, :LANES]), lr[:, :LANES])
    wa = _dot(wa_in.astype(BF16), w2a2_ref[...]) + w0a0_ref[...]
    z = -wa[:, :c]
    softplus = jnp.maximum(z, 0.0) + jnp.log1p(jnp.exp(-jnp.abs(z)))
    lw = -jnp.exp(-softplus - 0.5)
    a = _sigmoid(wa[:, c:])
    g = _dot(_sigmoid(lr[:, LANES:]).astype(BF16), g2_ref[...])

    kk = k * kk_ref[...]
    sq = kk * kk
    gw = seg_ref.shape[0]
    ss = jnp.concatenate([_dot_split(sq[:, j:j + gw], seg_ref[...]) for j in range(0, c, gw)], axis=1)
    kk = kk * lax.rsqrt(jnp.maximum(ss, 1e-24))

    r_o[...] = r
    lw_o[...] = lw
    k_o[...] = k * (1.0 + (a - 1.0) * ka_ref[...])
    v_o[...] = v
    kkn_o[...] = kk
    b_o[...] = kk * a
    g_o[...] = g
    qa_o[...] = proj[:, 3 * c:4 * c]
    kat_o[...] = proj[:, 4 * c:5 * c]
    vat_o[...] = proj[:, 5 * c:6 * c]
    if keep_tiles:
        kt_o, vt_o = rest[:2]

        @pl.when(i % tiles_per_seq >= tiles_per_seq - keep_tiles)
        def _():
            kt_o[...] = proj[:, 4 * c:5 * c].T
            vt_o[...] = proj[:, 5 * c:6 * c].T


def _proj_call(x2, flag, p, tiles_per_seq, tm, xl_rows=8, keep_tiles=0):
    t, d = x2.shape
    c = p['c_rwkv']
    n_tiles = t // tm
    n_seq = n_tiles // tiles_per_seq
    first = tiles_per_seq - keep_tiles
    kept = pl.BlockSpec((None, c, tm), lambda i: (i // tiles_per_seq, 0, jnp.maximum(i % tiles_per_seq - first, 0)))
    kept_specs = [kept, kept] if keep_tiles else []
    kept_shapes = [jax.ShapeDtypeStruct((n_seq, c, keep_tiles * tm), F32)] * 2 if keep_tiles else []
    full = lambda a: pl.BlockSpec(a.shape, lambda i: (0,) * a.ndim, pipeline_mode=pl.Buffered(1))
    tok = lambda w: pl.BlockSpec((tm, w), lambda i: (i, 0))
    weights = [p['gmix'], p['win'], p['wdx'], p['mu_rkv'], p['w0a0'], p['w2a2'], p['g2'], p['k_k'], p['k_a'],
               p['seg'][:GROUP_LANES, :GROUP_LANES]]
    outs = pl.pallas_call(
        functools.partial(_proj_kernel, tiles_per_seq=tiles_per_seq, c_rwkv=c, keep_tiles=keep_tiles),
        grid=(n_tiles,),
        in_specs=[tok(d), tok(1)] + [full(w) for w in weights],
        out_specs=[tok(c)] * 10 + [pl.BlockSpec((1, xl_rows, d), lambda i: (i, 0, 0))] + kept_specs,
        out_shape=([jax.ShapeDtypeStruct((t, c), F32)] * 10 + [jax.ShapeDtypeStruct((n_tiles, xl_rows, d), F32)]
                   + kept_shapes),
        scratch_shapes=[pltpu.VMEM((8, d), F32), pltpu.VMEM((8, 3 * c), F32)],
        compiler_params=pltpu.CompilerParams(dimension_semantics=("arbitrary",), vmem_limit_bytes=V7X_VMEM_LIMIT),
        name="proj",
    )(x2, flag, *weights)
    return outs


GROUP_LANES = 256
GROUP_HEADS = GROUP_LANES // HEAD_DIM


def _rwkv_kernel(r_ref, lw_ref, k_ref, v_ref, kk_ref, b_ref, g_ref, s0_ref, rk_ref, lnw_ref, lnb_ref, seg_ref,
                 y_ref, sout_ref, s_scr):
    ci = pl.program_id(1)
    nb, L, c = r_ref.shape
    gw, gh, hd = GROUP_LANES, GROUP_HEADS, HEAD_DIM
    n_groups = c // gw

    lane_head = lax.broadcasted_iota(jnp.int32, (1, gw), 1) // hd
    head_masks = [lane_head == j for j in range(gh)]
    bd_state = (lax.broadcasted_iota(jnp.int32, (gw, gw), 0) // hd) == (lax.broadcasted_iota(jnp.int32, (gw, gw), 1) // hd)
    bd_time = (lax.broadcasted_iota(jnp.int32, (gh * L, gh * L), 0) // L) == (lax.broadcasted_iota(jnp.int32, (gh * L, gh * L), 1) // L)
    t_row = lax.broadcasted_iota(jnp.int32, (L, gh * L), 0)
    t_col = lax.broadcasted_iota(jnp.int32, (L, gh * L), 1) % L
    strict4 = t_row > t_col
    incl4 = t_row >= t_col
    incl = lax.broadcasted_iota(jnp.int32, (L, L), 0) >= lax.broadcasted_iota(jnp.int32, (L, L), 1)

    def stack(x):
        return jnp.concatenate([jnp.where(m, x, jnp.zeros_like(x)) for m in head_masks], axis=0)

    def block_diag(n):
        tiled = jnp.concatenate([n] * gh, axis=0)
        return jnp.where(bd_time, tiled, jnp.zeros_like(tiled))

    @pl.when(ci == 0)
    def _():
        for bi in range(nb):
            for gi in range(n_groups):
                s_in = s0_ref[bi, gi * gh:(gi + 1) * gh].reshape(gw, hd)
                s_scr[bi, gi] = jnp.where(bd_state, jnp.concatenate([s_in] * gh, axis=1), 0.0)

    n_apply = max(1, int(math.log2(L)))
    seg = seg_ref[...]
    pre = []
    for bi in range(nb):
        lw = lw_ref[bi]
        lw_hi = lw.astype(BF16)
        lw_r = lw - lw_hi.astype(F32)
        lw_mid = lw_r.astype(BF16)
        lw_lo = (lw_r - lw_mid.astype(F32)).astype(BF16)
        cs3 = _dot(incl.astype(BF16), jnp.concatenate([lw_hi, lw_mid, lw_lo], axis=1))
        cs = cs3[:, :c] + cs3[:, c:2 * c] + cs3[:, 2 * c:]
        cp = cs - lw
        cm = cs[L // 2 - 1:L // 2, :]
        c_last = cs[L - 1:L, :]
        r, k, v, kk, b = r_ref[bi], k_ref[bi], v_ref[bi], kk_ref[bi], b_ref[bi]
        e_dn = jnp.exp(cm - cs)
        e_l = jnp.exp(c_last - cs)
        pre.append(dict(v=v, rt=r * jnp.exp(cs - cm), kkt=kk * jnp.exp(cp - cm), bt=b * e_dn, kt=k * e_dn,
                        kg=kk * jnp.exp(cp), rg=r * jnp.exp(cs), bh=b * e_l, kh=k * e_l, g_last=jnp.exp(c_last),
                        rkk=r * k * rk_ref[...]))

    chains = [(bi, gi) for bi in range(nb) for gi in range(n_groups)]
    col = lambda bi, gi, name: pre[bi][name][:, gi * gw:(gi + 1) * gw]
    each = lambda fn: [fn(i, bi, gi) for i, (bi, gi) in enumerate(chains)]

    vg = each(lambda i, bi, gi: col(bi, gi, 'v'))
    v_st = each(lambda i, bi, gi: stack(vg[i].astype(BF16)))
    a_all = each(lambda i, bi, gi: _mm(
        jnp.concatenate([col(bi, gi, 'kkt'), col(bi, gi, 'rt')], axis=0),
        jnp.concatenate([stack(col(bi, gi, 'bt').astype(BF16)), stack(col(bi, gi, 'kt').astype(BF16))], axis=0),
        NT))
    p_ak = each(lambda i, bi, gi: jnp.where(strict4, a_all[i][:L, gh * L:], 0.0))
    p_rb = each(lambda i, bi, gi: jnp.where(incl4, a_all[i][L:, :gh * L], 0.0))
    p_rk = each(lambda i, bi, gi: jnp.where(incl4, a_all[i][L:, gh * L:], 0.0))
    eye4 = (t_row == t_col).astype(F32)
    nm = each(lambda i, bi, gi: -jnp.where(strict4, a_all[i][:L, :gh * L], 0.0))
    t_inv = [eye4 + n for n in nm]
    for it in range(n_apply - 1):
        lhs = nm if it == 0 else [jnp.concatenate([n, t], axis=0) for n, t in zip(nm, t_inv)]
        both = each(lambda i, bi, gi: _mm(lhs[i], block_diag(nm[i].astype(BF16)), NN))
        if it > 0:
            t_inv = [t + bo[L:] for t, bo in zip(t_inv, both)]
        nm = [bo[:L] for bo in both]
    t_inv = each(lambda i, bi, gi: t_inv[i] + _mm(t_inv[i], block_diag(nm[i].astype(BF16)), NN))
    av = each(lambda i, bi, gi: _mm(p_ak[i], v_st[i], NN))
    x = each(lambda i, bi, gi: _mm(
        t_inv[i], jnp.concatenate([stack(col(bi, gi, 'kg').astype(BF16)), stack(av[i].astype(BF16))], axis=1),
        NN))
    w_m = [xi[:, :gw] for xi in x]
    u0 = [-xi[:, gw:] for xi in x]
    rbw = each(lambda i, bi, gi: _mm(
        p_rb[i], jnp.concatenate([stack(w_m[i].astype(BF16)), stack(u0[i].astype(BF16))], axis=1),
        NN))
    rkv = each(lambda i, bi, gi: _mm(p_rk[i], v_st[i], NN))
    s_old = each(lambda i, bi, gi: s_scr[bi, gi])
    ws = each(lambda i, bi, gi: _mm(
        jnp.concatenate([w_m[i], col(bi, gi, 'rg') - rbw[i][:, :gw]], axis=0), s_old[i], NT))
    u = each(lambda i, bi, gi: u0[i] - ws[i][:L])
    y = each(lambda i, bi, gi: ws[i][L:] + rbw[i][:, gw:] + rkv[i])
    upd = each(lambda i, bi, gi: _mm(
        jnp.concatenate([u[i], vg[i]], axis=0),
        jnp.concatenate([col(bi, gi, 'bh'), col(bi, gi, 'kh')], axis=0), TN))
    for i, (bi, gi) in enumerate(chains):
        s_scr[bi, gi] = s_old[i] * col(bi, gi, 'g_last') + jnp.where(bd_state, upd[i], 0.0)

    inv = 1.0 / hd
    n_ch = len(chains)
    sums = _dot_split(jnp.concatenate(y + each(lambda i, bi, gi: col(bi, gi, 'rkk')), axis=0), seg)
    mean = [sums[i * L:(i + 1) * L] * inv for i in range(n_ch)]
    bonus = [sums[(n_ch + i) * L:(n_ch + i + 1) * L] * vg[i] for i in range(n_ch)]
    yc = [y[i] - mean[i] for i in range(n_ch)]
    sq = _dot_split(jnp.concatenate([z * z for z in yc], axis=0), seg)
    var = [sq[i * L:(i + 1) * L] * inv for i in range(n_ch)]
    for i, (bi, gi) in enumerate(chains):
        sl = slice(gi * gw, (gi + 1) * gw)
        yn = yc[i] * lax.rsqrt(var[i] + GN_EPS) * lnw_ref[:, sl] + lnb_ref[:, sl]
        y_ref[bi, :, sl] = (yn + bonus[i]) * g_ref[bi, :, sl]

    @pl.when(ci == pl.num_programs(1) - 1)
    def _():
        for bi in range(nb):
            for gi in range(n_groups):
                bd = s_scr[bi, gi]
                folded = bd[:, 0:hd]
                for j in range(1, gh):
                    folded = folded + bd[:, j * hd:(j + 1) * hd]
                sout_ref[bi, gi * gh:(gi + 1) * gh] = folded.reshape(gh, hd, hd)


def _rwkv_call(vecs, s0, p, n_seq, seq_len, chunk, nb):
    c = p['c_rwkv']
    n_heads = c // HEAD_DIM
    n_chunks = seq_len // chunk
    assert n_seq % nb == 0 and seq_len % chunk == 0
    vecs = [z.reshape(n_seq, seq_len, c) for z in vecs]
    tok = pl.BlockSpec((nb, chunk, c), lambda bi, ci: (bi, ci, 0))
    st = pl.BlockSpec((nb, n_heads, HEAD_DIM, HEAD_DIM), lambda bi, ci: (bi, 0, 0, 0))
    rowvec = pl.BlockSpec((1, c), lambda bi, ci: (0, 0))
    seg = p['seg'][:GROUP_LANES, :GROUP_LANES]
    y, s_out = pl.pallas_call(
        _rwkv_kernel,
        grid=(n_seq // nb, n_chunks),
        in_specs=[tok] * 7 + [st, rowvec, rowvec, rowvec, pl.BlockSpec(seg.shape, lambda bi, ci: (0, 0))],
        out_specs=[tok, st],
        out_shape=[jax.ShapeDtypeStruct((n_seq, seq_len, c), F32),
                   jax.ShapeDtypeStruct((n_seq, n_heads, HEAD_DIM, HEAD_DIM), F32)],
        scratch_shapes=[pltpu.VMEM((nb, c // GROUP_LANES, GROUP_LANES, GROUP_LANES), F32)],
        compiler_params=pltpu.CompilerParams(dimension_semantics=("arbitrary", "arbitrary"),
                                             vmem_limit_bytes=V7X_VMEM_LIMIT),
        name="rwkv",
    )(*vecs, s0, p['r_k'], p['ln_w'], p['ln_b'], seg)
    return y.reshape(n_seq * seq_len, c), s_out


def _attn_prompt_kernel(q_ref, k_ref, v_ref, o_ref, m_scr, l_scr, acc_scr):
    s_len = q_ref.shape[0]
    band = ATT_BAND
    n_blk = s_len // band

    lane = lax.broadcasted_iota(jnp.int32, (1, LANES), 1)
    head0 = lane < HEAD_DIM
    qi = lax.broadcasted_iota(jnp.int32, (band, 2 * band), 0)
    kj = lax.broadcasted_iota(jnp.int32, (band, 2 * band), 1)
    in_band = (kj >= qi) & (kj <= qi + band)
    in_band2 = jnp.concatenate([in_band, in_band], axis=0)
    kj2 = jnp.concatenate([kj, kj], axis=0)
    scale = HEAD_DIM ** -0.5
    ones = jnp.ones((2 * band, LANES), BF16)

    for ci, (window, dil) in enumerate(DILATED_CONFIGS):
        assert window // dil == band
        per_res = n_blk // dil

        unroll = ATT_UNROLL[dil]
        run = min(unroll, per_res)

        def body(it, carry, ci=ci, dil=dil, per_res=per_res, run=run, unroll=unroll):
            span = band * dil
            tile = lambda start: pl.ds(start, band, stride=dil) if dil > 1 else pl.ds(start, band)
            blocks = []
            tiles = []
            first_dyn = []
            for r in range(unroll // run):
                i0 = it * unroll + r * run
                blk0 = i0 % per_res
                start0 = i0 // per_res + blk0 * span
                base = len(tiles)
                if per_res > run:
                    tiles.append(tile(jnp.maximum(start0 - span, 0)))
                    first_dyn.append(blk0 == 0)
                else:
                    tiles.append(None)
                    first_dyn.append(None)
                for t in range(run):
                    tiles.append(tile(start0 + t * span))
                    blocks.append((tiles[-1], base + t, r if t == 0 else None))
            kt = [None if w is None else k_ref[w, :].astype(BF16) for w in tiles]
            vt = [None if w is None else v_ref[w, :].astype(BF16) for w in tiles]
            prev = lambda ts, i: ts[i + 1] if ts[i] is None else ts[i]
            q = [q_ref[rows, :] * scale for rows, _, _ in blocks]

            def mask_of(first):
                if first is None:
                    return in_band2
                if first_dyn[first] is None:
                    return in_band2 & (kj2 >= band)
                return in_band2 & (kj2 >= jnp.where(first_dyn[first], band, 0))

            s = [jnp.where(mask_of(first), lax.dot_general(
                jnp.concatenate([jnp.where(head0, qj, 0.0), jnp.where(head0, 0.0, qj)], axis=0).astype(BF16),
                jnp.concatenate([prev(kt, i), kt[i + 1]], axis=0), NT, preferred_element_type=F32), NEG_INF)
                 for qj, (_, i, first) in zip(q, blocks)]
            m = [jnp.max(z, axis=-1, keepdims=True) for z in s]
            p = [jnp.exp(z - mx).astype(BF16) for z, mx in zip(s, m)]
            o = [_dot(pj, jnp.concatenate([jnp.concatenate([prev(vt, i), vt[i + 1]], axis=0), ones], axis=1))
                 for pj, (_, i, _) in zip(p, blocks)]
            for j, (rows, _, _) in enumerate(blocks):
                m_scr[ci, rows, :] = jnp.where(head0, m[j][:band], m[j][band:])
                acc_scr[ci, rows, :] = jnp.where(head0, o[j][:band, :LANES], o[j][band:, :LANES])
                l_scr[ci, rows, :] = jnp.where(head0, o[j][:band, LANES:], o[j][band:, LANES:])
            return carry

        lax.fori_loop(0, n_blk // unroll, body, 0)

    rows_per = 256

    def merge(i, carry):
        rows = pl.ds(pl.multiple_of(i * rows_per, rows_per), rows_per)
        ms = [m_scr[ci, rows, :] for ci in range(len(DILATED_CONFIGS))]
        m_all = functools.reduce(jnp.maximum, ms)
        num = jnp.zeros((rows_per, LANES), F32)
        den = jnp.zeros((rows_per, LANES), F32)
        for ci, m_c in enumerate(ms):
            w_c = jnp.exp(m_c - m_all)
            num = num + w_c * acc_scr[ci, rows, :]
            den = den + w_c * l_scr[ci, rows, :]
        o_ref[rows, :] = num / den
        return carry

    lax.fori_loop(0, s_len // rows_per, merge, 0)


def _attn_prompt_call(q, k, v, n_seq, seq_len):
    c = q.shape[1]
    n_pairs = c // LANES
    blk = pl.BlockSpec((seq_len, LANES), lambda bi, hi: (bi, hi))
    return pl.pallas_call(
        _attn_prompt_kernel,
        grid=(n_seq, n_pairs),
        in_specs=[blk, blk, blk],
        out_specs=blk,
        out_shape=jax.ShapeDtypeStruct((n_seq * seq_len, c), F32),
        scratch_shapes=[pltpu.VMEM((len(DILATED_CONFIGS), seq_len, LANES), F32)] * 3,
        compiler_params=pltpu.CompilerParams(dimension_semantics=("arbitrary", "arbitrary"),
                                             vmem_limit_bytes=V7X_VMEM_LIMIT),
        name="attn_prompt",
    )(q, k, v)


def _attn_sample_kernel(q_ref, kn_ref, vn_ref, kc_ref, vc_ref, o_ref, *, n_new):
    hd = HEAD_DIM
    _, n_heads, _, n_buf = kc_ref.shape
    t_pad = kn_ref.shape[1]
    c = q_ref.shape[2]
    q = q_ref[0] * (hd ** -0.5)
    lane_head = lax.broadcasted_iota(jnp.int32, (1, c), 1) // hd
    qs = jnp.concatenate([jnp.where(lane_head == h, q, 0.0) for h in range(n_heads)], axis=0).astype(BF16)
    n_rows = n_heads * t_pad
    t_idx = lax.broadcasted_iota(jnp.int32, (n_rows, 1), 0) % t_pad

    def multiplicity(dist):
        mult = jnp.zeros(dist.shape, F32)
        for window, dil in DILATED_CONFIGS:
            hit = (dist >= 0) & (dist <= window) & (dist % dil == 0)
            mult = mult + jnp.where(hit, 1.0, 0.0)
        return mult

    jc = lax.broadcasted_iota(jnp.int32, (1, n_buf), 1)
    mult_c = multiplicity(n_buf + t_idx - jc)
    jn = lax.broadcasted_iota(jnp.int32, (1, t_pad), 1)
    mult_n = jnp.where(jn < n_new, multiplicity(t_idx - jn), 0.0)

    sc = jnp.concatenate(
        [_dot(qs[h * t_pad:(h + 1) * t_pad, h * hd:(h + 1) * hd], kc_ref[0, h].astype(BF16))
         for h in range(n_heads)], axis=0)
    sn = lax.dot_general(qs, kn_ref[0].astype(BF16), NT, preferred_element_type=F32)
    sc = jnp.where(mult_c > 0.0, sc, NEG_INF)
    sn = jnp.where(mult_n > 0.0, sn, NEG_INF)
    m = jnp.maximum(jnp.max(sc, axis=-1, keepdims=True), jnp.max(sn, axis=-1, keepdims=True))
    pc = (mult_c * jnp.exp(sc - m)).astype(BF16)
    pn = mult_n * jnp.exp(sn - m)
    inv_l = 1.0 / (jnp.sum(pc.astype(F32), axis=-1, keepdims=True) + jnp.sum(pn, axis=-1, keepdims=True))
    o_new = _dot(pn.astype(BF16), vn_ref[0].astype(BF16)) * inv_l
    out = jnp.zeros((t_pad, c), F32)
    for h in range(n_heads):
        out = out + jnp.where(lane_head == h, o_new[h * t_pad:(h + 1) * t_pad, :], 0.0)
    o_buf = [lax.dot_general(pc[h * t_pad:(h + 1) * t_pad, :], vc_ref[0, h].astype(BF16), NT,
                             preferred_element_type=F32) * inv_l[h * t_pad:(h + 1) * t_pad, :]
             for h in range(n_heads)]
    o_ref[0] = out + jnp.concatenate(o_buf, axis=1)


def _attn_sample_call(q, kn, vn, k_buf, v_buf, n_new):
    b, t_pad, c = q.shape
    _, n_buf, n_heads, hd = k_buf.shape
    k_t = jnp.transpose(k_buf, (0, 2, 3, 1))
    v_t = jnp.transpose(v_buf, (0, 2, 3, 1))
    new = pl.BlockSpec((1, t_pad, c), lambda bi: (bi, 0, 0))
    buf = pl.BlockSpec((1, n_heads, hd, n_buf), lambda bi: (bi, 0, 0, 0))
    return pl.pallas_call(
        functools.partial(_attn_sample_kernel, n_new=n_new),
        grid=(b,),
        in_specs=[new, new, new, buf, buf],
        out_specs=new,
        out_shape=jax.ShapeDtypeStruct((b, t_pad, c), F32),
        compiler_params=pltpu.CompilerParams(dimension_semantics=("arbitrary",), vmem_limit_bytes=V7X_VMEM_LIMIT),
        name="attn_sample",
    )(q, kn, vn, k_t, v_t)


def _route_rows(logits, seen):
    lane = lax.broadcasted_iota(jnp.int32, logits.shape, 1)
    lane_f = lane.astype(F32)
    first = lambda hit: jnp.min(jnp.where(hit, lane_f, float(LANES)), axis=-1, keepdims=True)
    is_g = lane < N_GROUPS
    lg = jnp.where(is_g, logits, NEG_INF)
    g_max = jnp.max(lg, axis=-1, keepdims=True)
    g_idx = first(lg == g_max)
    g_w = 1.0 / jnp.sum(jnp.where(is_g, jnp.exp(lg - g_max), 0.0), axis=-1, keepdims=True)
    lo = N_GROUPS + EXPERTS_PER_GROUP * g_idx
    le = jnp.where((lane_f >= lo) & (lane_f < lo + EXPERTS_PER_GROUP), logits, NEG_INF)
    e1 = jnp.max(le, axis=-1, keepdims=True)
    i1 = first(le == e1)
    le2 = jnp.where(lane_f == i1, NEG_INF, le)
    e2 = jnp.max(le2, axis=-1, keepdims=True)
    i2 = first(le2 == e2)
    ex = jnp.exp(e2 - e1)
    gate1 = g_w / (1.0 + ex)
    gate2 = g_w * ex / (1.0 + ex)
    tm = logits.shape[0]
    pick1 = lane_f == i1
    pick2 = lane_f == i2
    picks = jnp.where(pick1 | pick2, 1.0, 0.0)
    earlier = (lax.broadcasted_iota(jnp.int32, (tm, tm), 0) > lax.broadcasted_iota(jnp.int32, (tm, tm), 1))
    before = seen + _dot(earlier.astype(BF16), picks.astype(BF16))
    rank1 = jnp.sum(jnp.where(pick1, before, 0.0), axis=-1, keepdims=True)
    rank2 = jnp.sum(jnp.where(pick2, before, 0.0), axis=-1, keepdims=True)
    out = jnp.where(lane == 0, gate1, jnp.where(lane == 1, gate2, 0.0))
    out = jnp.where(lane == 2, i1 - N_GROUPS, jnp.where(lane == 3, i2 - N_GROUPS, out))
    out = jnp.where(lane == 4, rank1, jnp.where(lane == 5, rank2, out))
    return out, jnp.sum(picks, axis=0, keepdims=True)


def _post_kernel(x_ref, yr_ref, ya_ref, wo_ref, gffn_ref, rw_cat_ref, rb_ref,
                 h_o, hn_o, lg_o, cnt_o, seen_scr):
    @pl.when(pl.program_id(0) == 0)
    def _():
        seen_scr[...] = jnp.zeros_like(seen_scr)

    c = yr_ref.shape[1]
    h = (x_ref[...] + _dot(yr_ref[...].astype(BF16), wo_ref[:c, :]) + _dot(ya_ref[...].astype(BF16), wo_ref[c:, :]))
    ms = jnp.mean(h * h, axis=-1, keepdims=True)
    hn = (h * lax.rsqrt(ms + NORM_EPS)) * gffn_ref[...]
    h_o[...] = h
    bits = pltpu.bitcast(hn.astype(BF16).astype(F32), jnp.uint32)
    half = hn.shape[1] // 2
    hn_o[...] = (bits[:, :half] >> 16) | (bits[:, half:] & jnp.uint32(0xFFFF0000))
    hi = hn.astype(BF16)
    lo = (hn - hi.astype(F32)).astype(BF16)
    both = _dot(hi, rw_cat_ref[...])
    logits = (both[:, :LANES] + both[:, LANES:] + _dot(lo, rw_cat_ref[:, :LANES])) + rb_ref[...]
    route, picked = _route_rows(logits, seen_scr[...])
    lg_o[...] = route
    seen_scr[...] = seen_scr[...] + picked
    cnt_o[...] = seen_scr[...]


def _post_call(x2, yr, ya, p, tm):
    t, d = x2.shape
    c = yr.shape[1]
    full = lambda a: pl.BlockSpec(a.shape, lambda i: (0,) * a.ndim)
    tok = lambda w: pl.BlockSpec((tm, w), lambda i: (i, 0))
    weights = [p['wout'], p['gffn'], p['rw_cat'], p['rb']]
    return pl.pallas_call(
        _post_kernel,
        grid=(t // tm,),
        in_specs=[tok(d), tok(c), tok(c)] + [full(w) for w in weights],
        out_specs=[tok(d), tok(d // 2), tok(LANES), pl.BlockSpec((1, LANES), lambda i: (0, 0))],
        out_shape=[jax.ShapeDtypeStruct((t, d), F32), jax.ShapeDtypeStruct((t, d // 2), jnp.uint32),
                   jax.ShapeDtypeStruct((t, LANES), F32), jax.ShapeDtypeStruct((1, LANES), F32)],
        scratch_shapes=[pltpu.VMEM((1, LANES), F32)],
        compiler_params=pltpu.CompilerParams(dimension_semantics=("arbitrary",), vmem_limit_bytes=V7X_VMEM_LIMIT),
        name="post",
    )(x2, yr, ya, *weights)


def _expert_kernel(be_ref, nb_ref, xs_ref, w1_ref, w3_ref, w2_ref, y_ref, w1_s, w3_s, w2_s):
    i = pl.program_id(0)
    live = i < nb_ref[0]

    @pl.when(live & ((i == 0) | (be_ref[i] != be_ref[jnp.maximum(i - 1, 0)])))
    def _():
        w1_s[...] = w1_ref[...].astype(BF16)
        w3_s[...] = w3_ref[...].astype(BF16)
        w2_s[...] = w2_ref[...].astype(BF16)

    @pl.when(live)
    def _():
        packed = xs_ref[...]
        half = packed.shape[1]
        lo = pltpu.bitcast(packed << 16, F32).astype(BF16)
        hi = pltpu.bitcast(packed & jnp.uint32(0xFFFF0000), F32).astype(BF16)
        h1 = _dot(lo, w1_s[:half, :]) + _dot(hi, w1_s[half:, :])
        h3 = _dot(lo, w3_s[:half, :]) + _dot(hi, w3_s[half:, :])
        act = (h1 * _sigmoid(h1)) * h3
        y_ref[...] = _dot(act.astype(BF16), w2_s[...])

    @pl.when(jnp.logical_not(live))
    def _():
        y_ref[...] = jnp.zeros_like(y_ref)


def _expert_call(blk_exp, n_used, xs, p, bm):
    n_slots = xs.shape[0]
    _, d, de = p['w1'].shape
    grid_spec = pltpu.PrefetchScalarGridSpec(
        num_scalar_prefetch=2,
        grid=(n_slots // bm,),
        in_specs=[pl.BlockSpec((bm, d // 2), lambda i, be, nb: (i, 0)),
                  pl.BlockSpec((None, d, de), lambda i, be, nb: (be[i], 0, 0)),
                  pl.BlockSpec((None, d, de), lambda i, be, nb: (be[i], 0, 0)),
                  pl.BlockSpec((None, de, d), lambda i, be, nb: (be[i], 0, 0))],
        out_specs=pl.BlockSpec((bm, d), lambda i, be, nb: (i, 0)),
        scratch_shapes=[pltpu.VMEM((d, de), BF16), pltpu.VMEM((d, de), BF16), pltpu.VMEM((de, d), BF16)],
    )
    return pl.pallas_call(
        _expert_kernel,
        grid_spec=grid_spec,
        out_shape=jax.ShapeDtypeStruct((n_slots, d), F32),
        compiler_params=pltpu.CompilerParams(dimension_semantics=("arbitrary",), vmem_limit_bytes=V7X_VMEM_LIMIT),
        name="experts",
    )(blk_exp, n_used, xs, p['w1'], p['w3'], p['w2'])


def _final_kernel(h_ref, route_ref, y1_ref, y2_ref, gfin_ref, o_ref):
    route = route_ref[...]
    h = h_ref[...] + (route[:, 0:1] * y1_ref[...] + route[:, 1:2] * y2_ref[...])
    ms = jnp.mean(h * h, axis=-1, keepdims=True)
    o_ref[...] = (h * lax.rsqrt(ms + NORM_EPS)) * gfin_ref[...]


def _final_call(h, route, y12, gfin, tm):
    t, d = h.shape
    tok = pl.BlockSpec((tm, d), lambda i: (i, 0))
    routed = lambda a: pl.BlockSpec((None, tm, d), lambda i: (a, i, 0))
    return pl.pallas_call(
        _final_kernel,
        grid=(t // tm,),
        in_specs=[tok, pl.BlockSpec((tm, LANES), lambda i: (i, 0)), routed(0), routed(1),
                  pl.BlockSpec((1, d), lambda i: (0, 0))],
        out_specs=tok,
        out_shape=jax.ShapeDtypeStruct((t, d), F32),
        compiler_params=pltpu.CompilerParams(dimension_semantics=("arbitrary",), vmem_limit_bytes=V7X_VMEM_LIMIT),
        name="final",
    )(h, route, y12, y12, gfin)


def _gather_rows(table, idx):
    info = plsc.get_sparse_core_info()
    nc, ns = info.num_cores, info.num_subcores
    b, d = idx.shape[0], table.shape[1]
    chunk = min(SC_INDEX_LIMIT, SC_GATHER_BYTES // (d * table.dtype.itemsize))
    assert b % (nc * ns * chunk * 2) == 0, "rows must split evenly into chunk pairs per subcore"
    per_w = b // (nc * ns)
    n_chunks = per_w // chunk
    mesh = plsc.VectorSubcoreMesh(core_axis_name="c", subcore_axis_name="s")

    @functools.partial(
        pl.kernel, mesh=mesh, out_type=jax.ShapeDtypeStruct((b, d), table.dtype),
        scratch_types=[pltpu.VMEM((per_w,), jnp.int32), pltpu.VMEM((2, chunk, d), table.dtype),
                       pltpu.SemaphoreType.DMA((2,)), pltpu.SemaphoreType.DMA((2,))])
    def gather(table_hbm, idx_hbm, out_hbm, idx_v, rows_v, fetch_sem, put_sem):
        base = (lax.axis_index("s") * nc + lax.axis_index("c")) * per_w
        pltpu.sync_copy(idx_hbm.at[pl.ds(base, per_w)], idx_v)

        def fetch(c, slot):
            off = pl.multiple_of(c * chunk, chunk)
            return pltpu.make_async_copy(table_hbm.at[idx_v.at[pl.ds(off, chunk)]], rows_v.at[slot],
                                         fetch_sem.at[slot])

        def put(c, slot):
            off = pl.multiple_of(c * chunk, chunk)
            return pltpu.make_async_copy(rows_v.at[slot], out_hbm.at[pl.ds(base + off, chunk)], put_sem.at[slot])

        fetch(0, 0).start()

        @pl.loop(0, n_chunks, step=2)
        def _(c):
            @pl.when(c > 0)
            def _():
                put(c - 1, 1).wait()
            fetch(c + 1, 1).start()
            fetch(c, 0).wait()
            put(c, 0).start()
            fetch(c + 1, 1).wait()
            put(c, 0).wait()

            @pl.when(c + 2 < n_chunks)
            def _():
                fetch(c + 2, 0).start()
            put(c + 1, 1).start()

        put(n_chunks - 1, 1).wait()

    return gather(table, idx)


def _route(route, counts, bm):
    n = route.shape[0]
    eid = route[:, 2:4].astype(jnp.int32)
    rank = route[:, 4:6].astype(jnp.int32)
    m = 2 * n
    experts = jnp.arange(N_EXPERTS, dtype=jnp.int32)
    counts = counts[0, N_GROUPS:N_GROUPS + N_EXPERTS].astype(jnp.int32)
    starts = jnp.cumsum(counts) - counts
    padded = (counts + bm - 1) // bm * bm
    p_ends = jnp.cumsum(padded)
    p_starts = p_ends - padded
    lookup = lambda tbl, e: jnp.sum(jnp.where(e[..., None] == experts, tbl, 0), axis=-1)
    dest = lookup(p_starts, eid.T) + rank.T
    bits = max(1, (m - 1).bit_length())
    order = lax.sort(eid.reshape(-1) * (1 << bits) + jnp.arange(m, dtype=jnp.int32)) & ((1 << bits) - 1)
    n_blocks = _round_up(-(-m // bm) + N_EXPERTS, SLOT_MULTIPLE // bm)
    blk_start = jnp.arange(n_blocks, dtype=jnp.int32) * bm
    blk_exp = jnp.minimum(jnp.sum((p_ends[None, :] <= blk_start[:, None]).astype(jnp.int32), axis=1), N_EXPERTS - 1)
    in_expert = blk_start - lookup(p_starts, blk_exp)
    valid = (in_expert[:, None] + jnp.arange(bm, dtype=jnp.int32)[None, :]) < lookup(counts, blk_exp)[:, None]
    first = jnp.clip(lookup(starts, blk_exp) + in_expert, 0, m)
    padded_order = jnp.concatenate([order, jnp.zeros((bm,), jnp.int32)])
    window = jax.vmap(lambda lo: lax.dynamic_slice(padded_order, (lo,), (bm,)))(first)
    slot_tok = jnp.where(valid, window // 2, jnp.arange(n_blocks * bm, dtype=jnp.int32).reshape(n_blocks, bm) % n)
    slot_tok = slot_tok.reshape(-1)
    n_used = (p_ends[-1] // bm).astype(jnp.int32).reshape(1)
    return slot_tok, dest, blk_exp.astype(jnp.int32), n_used


def _moe_dispatch(x2, yr, ya, p, tm, bm):
    h, hn, route, counts = _post_call(x2, yr, ya, p, tm)
    slot_tok, dest, blk_exp, n_used = _route(route, counts, bm)
    return dict(h=h, route=route, dest=dest, blk_exp=blk_exp, n_used=n_used, xs=_gather_rows(hn, slot_tok))


def _moe_combine(ctx, yb, p, tm):
    h = ctx['h']
    y12 = _gather_rows(yb, ctx['dest'].reshape(-1)).reshape(2, h.shape[0], h.shape[1])
    return _final_call(h, ctx['route'], y12, p['gfin'], tm)


def _prep_params(layer, norm_mix_g, w_in, rwkv_mu_rkv, rwkv_mu_wag, rwkv_w0, rwkv_w1, rwkv_w2, rwkv_a0, rwkv_a1,
                 rwkv_a2, rwkv_g1, rwkv_g2, rwkv_k_k, rwkv_k_a, rwkv_r_k, rwkv_ln_w, rwkv_ln_b, w_out, norm_ffn_g,
                 router_group_w, router_group_b, router_expert_w, router_expert_b, expert_w1, expert_w3, expert_w2,
                 norm_final_g):
    d = w_in.shape[1]
    c = rwkv_w0.shape[1]
    row = lambda a: a.reshape(1, -1).astype(F32)
    lowrank = jnp.concatenate([rwkv_w1[layer], rwkv_a1[layer], rwkv_g1[layer]], axis=1)
    mx = rwkv_mu_wag[layer]
    r_w = rwkv_w1.shape[2]
    r_a = rwkv_a1.shape[2]
    r_g = rwkv_g1.shape[2]
    assert r_w + r_a == LANES and r_g == LANES
    mx_cols = jnp.concatenate([jnp.broadcast_to(mx[0][:, None], (d, r_w)), jnp.broadcast_to(mx[1][:, None], (d, r_a)),
                               jnp.broadcast_to(mx[2][:, None], (d, r_g))], axis=1)
    w2a2 = jnp.zeros((LANES, 2 * c), F32)
    w2a2 = w2a2.at[:r_w, :c].set(rwkv_w2[layer]).at[r_w:, c:].set(rwkv_a2[layer])
    head = jnp.arange(c) // HEAD_DIM
    rw = jnp.zeros((d, LANES), F32)
    rw = rw.at[:, :N_GROUPS].set(router_group_w[layer]).at[:, N_GROUPS:N_GROUPS + N_EXPERTS].set(router_expert_w[layer])
    rw_hi = rw.astype(BF16)
    rb = jnp.zeros((1, LANES), F32)
    rb = rb.at[0, :N_GROUPS].set(router_group_b[layer]).at[0, N_GROUPS:N_GROUPS + N_EXPERTS].set(router_expert_b[layer])
    return {
        'c_rwkv': c,
        'gmix': row(norm_mix_g[layer]),
        'win': jnp.concatenate([w_in[layer], lowrank], axis=1).astype(BF16),
        'wdx': (mx_cols * lowrank).astype(BF16),
        'mu_rkv': rwkv_mu_rkv[layer],
        'w0a0': jnp.concatenate([row(rwkv_w0[layer]), row(rwkv_a0[layer])], axis=1),
        'w2a2': w2a2.astype(BF16),
        'g2': rwkv_g2[layer].astype(BF16),
        'k_k': row(rwkv_k_k[layer]),
        'k_a': row(rwkv_k_a[layer]),
        'seg': (head[:, None] == head[None, :]).astype(BF16),
        'r_k': row(rwkv_r_k[layer]),
        'ln_w': row(rwkv_ln_w[layer]),
        'ln_b': row(rwkv_ln_b[layer]),
        'wout': w_out[layer].astype(BF16),
        'gffn': row(norm_ffn_g[layer]),
        'rw_cat': jnp.concatenate([rw_hi, (rw - rw_hi.astype(F32)).astype(BF16)], axis=1),
        'rb': rb,
        'w1': expert_w1[layer],
        'w3': expert_w3[layer],
        'w2': expert_w2[layer],
        'gfin': row(norm_final_g),
    }


def _prompt_mix(x, p):
    b, s, d = x.shape
    c = p['c_rwkv']
    x2 = x.reshape(b * s, d)
    tm = PROJ_TILE
    flag = jnp.zeros((b * s, 1), F32)
    keep = min(max(w for w, _ in DILATED_CONFIGS), s)
    assert keep % tm == 0
    r, lw, k, v, kk, bb, g, qa, ka, va, xl, kt, vt = _proj_call(x2, flag, p, s // tm, tm, keep_tiles=keep // tm)
    s0 = jnp.zeros((b, c // HEAD_DIM, HEAD_DIM, HEAD_DIM), F32)
    yr, s_new = _rwkv_call((r, lw, k, v, kk, bb, g), s0, p, b, s, RWKV_CHUNK, 8)
    ya = _attn_prompt_call(qa, ka, va, b, s)
    shift = xl.reshape(b, s // tm, 8, d)[:, -1, 7, :]
    to_cache = lambda z: jnp.transpose(z.reshape(b, c // HEAD_DIM, HEAD_DIM, keep), (0, 3, 1, 2))
    k_keep, v_keep = to_cache(kt), to_cache(vt)
    return (x2, yr, ya), (s_new, shift, k_keep, v_keep)


def _sample_mix(x, shift0, s0, k_buf, v_buf, p, after):
    b, t, d = x.shape
    c = p['c_rwkv']
    n_heads = c // HEAD_DIM
    t_pad = 8
    xc = jnp.concatenate([shift0[:, None, :], x, jnp.zeros((b, t_pad - 1 - t, d), x.dtype)], axis=1)
    flag = jnp.zeros((b, t_pad, 1), F32).at[:, 0].set(1.0)
    outs = _proj_call(xc.reshape(b * t_pad, d), flag.reshape(b * t_pad, 1), p, 1, b * t_pad, xl_rows=b * t_pad)
    xl = outs[10]
    live = (jnp.arange(t_pad) < t)[None, :, None]
    shifted = [jnp.where(live, jnp.roll(o.reshape(b, t_pad, c), -1, axis=1), 0.0) for o in outs[:10]]
    r, lw, k, v, kk, bb, g, qa, ka, va = shifted
    flat = lambda z: z.reshape(b * t_pad, c)
    yr, s_new = _rwkv_call(tuple(flat(z) for z in (r, lw, k, v, kk, bb, g)), s0, p, b, t_pad, t_pad, 8)
    qa_held, _ = lax.optimization_barrier((qa, after))
    ya = _attn_sample_call(qa_held, ka, va, k_buf, v_buf, t)
    x_pad = jnp.concatenate([x, jnp.zeros((b, t_pad - t, d), x.dtype)], axis=1).reshape(b * t_pad, d)
    shift = xl.reshape(b, t_pad, d)[:, t]
    return ((x_pad, yr, flat(ya)),
            (s_new, shift, ka[:, :t].reshape(b, t, n_heads, HEAD_DIM), va[:, :t].reshape(b, t, n_heads, HEAD_DIM)))


def kernel(x_prompt, x_sample, state_rwkv, state_shift, cache_att_k, cache_att_v, norm_mix_g, w_in, rwkv_mu_rkv, rwkv_mu_wag, rwkv_w0, rwkv_w1, rwkv_w2, rwkv_a0, rwkv_a1, rwkv_a2, rwkv_g1, rwkv_g2, rwkv_k_k, rwkv_k_a, rwkv_r_k, rwkv_ln_w, rwkv_ln_b, w_out, norm_ffn_g, router_group_w, router_group_b, router_expert_w, router_expert_b, expert_w1, expert_w3, expert_w2, norm_final_g):
    assert w_in.shape[0] == 1, "single-layer trunk"
    p = _prep_params(0, norm_mix_g, w_in, rwkv_mu_rkv, rwkv_mu_wag, rwkv_w0, rwkv_w1, rwkv_w2, rwkv_a0, rwkv_a1,
                     rwkv_a2, rwkv_g1, rwkv_g2, rwkv_k_k, rwkv_k_a, rwkv_r_k, rwkv_ln_w, rwkv_ln_b, w_out,
                     norm_ffn_g, router_group_w, router_group_b, router_expert_w, router_expert_b, expert_w1,
                     expert_w3, expert_w2, norm_final_g)
    moe_p, (rw_p, sh_p, kc_p, vc_p) = _prompt_mix(x_prompt, p)
    ctx_p = _moe_dispatch(*moe_p, p, PROJ_TILE, EXPERT_TILE)
    yb_p = _expert_call(ctx_p['blk_exp'], ctx_p['n_used'], ctx_p['xs'], p, EXPERT_TILE)
    y12_p = _gather_rows(yb_p, ctx_p['dest'].reshape(-1))
    moe_s, (rw_s, sh_s, kc_s, vc_s) = _sample_mix(x_sample, state_shift[0], state_rwkv[0], cache_att_k[0],
                                                   cache_att_v[0], p, after=yb_p)
    tm_s = moe_s[0].shape[0] // 2
    ctx_s = _moe_dispatch(*moe_s, p, tm_s, SAMPLE_EXPERT_TILE)
    yb_s = _expert_call(ctx_s['blk_exp'], ctx_s['n_used'], ctx_s['xs'], p, SAMPLE_EXPERT_TILE)
    h_p = ctx_p['h']
    y_p = _final_call(h_p, ctx_p['route'], y12_p.reshape(2, *h_p.shape), p['gfin'], PROJ_TILE).reshape(x_prompt.shape)
    y_s = _moe_combine(ctx_s, yb_s, p, tm_s)
    y_s = y_s.reshape(x_sample.shape[0], -1, x_sample.shape[2])[:, :x_sample.shape[1]]
    return (y_p, y_s, rw_p[None], sh_p[None], kc_p[None], vc_p[None], rw_s[None], sh_s[None], kc_s[None], vc_s[None])
```

```python
import functools
import math

import jax
import jax.numpy as jnp
from jax import lax
from jax.experimental import pallas as pl
from jax.experimental.pallas import tpu as pltpu
from jax.experimental.pallas import tpu_sc as plsc

F32 = jnp.float32
BF16 = jnp.bfloat16

HEAD_DIM = 64
GN_EPS = 64e-5
NORM_EPS = 1e-6
DILATED_CONFIGS = ((128, 1), (512, 4), (2048, 16))
N_GROUPS = 4
EXPERTS_PER_GROUP = 8
N_EXPERTS = N_GROUPS * EXPERTS_PER_GROUP
NEG_INF = -1e30

V7X_VMEM_LIMIT = 56 * 1024 * 1024
LANES = 128

PROJ_TILE = 512
RWKV_CHUNK = 64
ATT_BAND = 128
EXPERT_TILE = 512
ATT_UNROLL = {1: 8, 4: 8, 16: 4}
SC_GATHER_BYTES = 128 * 1024
SC_INDEX_LIMIT = 128
SLOT_MULTIPLE = 4096
SAMPLE_EXPERT_TILE = 128

NN = (((1,), (0,)), ((), ()))
NT = (((1,), (1,)), ((), ()))
TN = (((0,), (0,)), ((), ()))


def _dot(a, b):
    return jnp.dot(a, b, preferred_element_type=F32)


def _dot_split(a, b_bf16):
    hi = a.astype(BF16)
    lo = (a - hi.astype(F32)).astype(BF16)
    return _dot(hi, b_bf16) + _dot(lo, b_bf16)


def _mm(a, b, dims):
    return lax.dot_general(a.astype(BF16), b.astype(BF16), dims, preferred_element_type=F32)


def _round_up(x, k):
    return -(-x // k) * k


def _sigmoid(z):
    return 1.0 / (1.0 + jnp.exp(-z))


def _proj_kernel(x_ref, flag_ref, gmix_ref, win_ref, wdx_ref, mu_ref, w0a0_ref, w2a2_ref, g2_ref,
                 kk_ref, ka_ref, seg_ref,
                 r_o, lw_o, k_o, v_o, kkn_o, b_o, g_o, qa_o, kat_o, vat_o, xl_o, *rest,
                 tiles_per_seq, c_rwkv, keep_tiles):
    xn_carry, pj_carry = rest[-2:]
    i = pl.program_id(0)

    @pl.when(i % tiles_per_seq == 0)
    def _():
        xn_carry[...] = jnp.zeros_like(xn_carry)
        pj_carry[...] = jnp.zeros_like(pj_carry)

    c = c_rwkv
    x = x_ref[...]
    tm = x.shape[0]
    ms = jnp.mean(x * x, axis=-1, keepdims=True)
    xn = (x * lax.rsqrt(ms + NORM_EPS)) * gmix_ref[...]
    xn = jnp.where(flag_ref[...] > 0.0, x, xn)
    row = lax.broadcasted_iota(jnp.int32, (tm, 1), 0)
    xn_prev = jnp.where(row == 0, xn_carry[7:8, :], pltpu.roll(xn, 1, axis=0))
    dx = xn_prev - xn

    proj = _dot(xn.astype(BF16), win_ref[...])
    cur = proj[:, :3 * c]
    prev = jnp.where(row == 0, pj_carry[7:8, :], pltpu.roll(cur, 1, axis=0))
    xn_carry[...] = xn[tm - 8:, :]
    pj_carry[...] = cur[tm - 8:, :]
    xl_rows = xl_o.shape[1]
    xl_o[0] = xn[tm - xl_rows:, :]

    mu = mu_ref[...]
    r = cur[:, :c] + mu[0:1] * (prev[:, :c] - cur[:, :c])
    k = cur[:, c:2 * c] + mu[1:2] * (prev[:, c:2 * c] - cur[:, c:2 * c])
    v = cur[:, 2 * c:3 * c] + mu[2:3] * (prev[:, 2 * c:3 * c] - cur[:, 2 * c:3 * c])

    lr = proj[:, 6 * c:] + _dot(dx.astype(BF16), wdx_ref[...])
    lane = lax.broadcasted_iota(jnp.int32, (1, LANES), 1)
    wa_in = jnp.where(lane < 64, jnp.tanh(lr[:, :LANES]), lr[:, :LANES])
    wa = _dot(wa_in.astype(BF16), w2a2_ref[...]) + w0a0_ref[...]
    z = -wa[:, :c]
    softplus = jnp.maximum(z, 0.0) + jnp.log1p(jnp.exp(-jnp.abs(z)))
    lw = -jnp.exp(-softplus - 0.5)
    a = _sigmoid(wa[:, c:])
    g = _dot(_sigmoid(lr[:, LANES:]).astype(BF16), g2_ref[...])

    kk = k * kk_ref[...]
    sq = kk * kk
    gw = seg_ref.shape[0]
    ss = jnp.concatenate([_dot_split(sq[:, j:j + gw], seg_ref[...]) for j in range(0, c, gw)], axis=1)
    kk = kk * lax.rsqrt(jnp.maximum(ss, 1e-24))

    r_o[...] = r
    lw_o[...] = lw
    k_o[...] = k * (1.0 + (a - 1.0) * ka_ref[...])
    v_o[...] = v
    kkn_o[...] = kk
    b_o[...] = kk * a
    g_o[...] = g
    qa_o[...] = proj[:, 3 * c:4 * c]
    kat_o[...] = proj[:, 4 * c:5 * c]
    vat_o[...] = proj[:, 5 * c:6 * c]
    if keep_tiles:
        kt_o, vt_o = rest[:2]

        @pl.when(i % tiles_per_seq >= tiles_per_seq - keep_tiles)
        def _():
            kt_o[...] = proj[:, 4 * c:5 * c].T
            vt_o[...] = proj[:, 5 * c:6 * c].T


def _proj_call(x2, flag, p, tiles_per_seq, tm, xl_rows=8, keep_tiles=0):
    t, d = x2.shape
    c = p['c_rwkv']
    n_tiles = t // tm
    n_seq = n_tiles // tiles_per_seq
    first = tiles_per_seq - keep_tiles
    kept = pl.BlockSpec((None, c, tm), lambda i: (i // tiles_per_seq, 0, jnp.maximum(i % tiles_per_seq - first, 0)))
    kept_specs = [kept, kept] if keep_tiles else []
    kept_shapes = [jax.ShapeDtypeStruct((n_seq, c, keep_tiles * tm), F32)] * 2 if keep_tiles else []
    full = lambda a: pl.BlockSpec(a.shape, lambda i: (0,) * a.ndim, pipeline_mode=pl.Buffered(1))
    tok = lambda w: pl.BlockSpec((tm, w), lambda i: (i, 0))
    weights = [p['gmix'], p['win'], p['wdx'], p['mu_rkv'], p['w0a0'], p['w2a2'], p['g2'], p['k_k'], p['k_a'],
               p['seg'][:GROUP_LANES, :GROUP_LANES]]
    outs = pl.pallas_call(
        functools.partial(_proj_kernel, tiles_per_seq=tiles_per_seq, c_rwkv=c, keep_tiles=keep_tiles),
        grid=(n_tiles,),
        in_specs=[tok(d), tok(1)] + [full(w) for w in weights],
        out_specs=[tok(c)] * 10 + [pl.BlockSpec((1, xl_rows, d), lambda i: (i, 0, 0))] + kept_specs,
        out_shape=([jax.ShapeDtypeStruct((t, c), F32)] * 10 + [jax.ShapeDtypeStruct((n_tiles, xl_rows, d), F32)]
                   + kept_shapes),
        scratch_shapes=[pltpu.VMEM((8, d), F32), pltpu.VMEM((8, 3 * c), F32)],
        compiler_params=pltpu.CompilerParams(dimension_semantics=("arbitrary",), vmem_limit_bytes=V7X_VMEM_LIMIT),
        name="proj",
    )(x2, flag, *weights)
    return outs


GROUP_LANES = 256
GROUP_HEADS = GROUP_LANES // HEAD_DIM


def _rwkv_kernel(r_ref, lw_ref, k_ref, v_ref, kk_ref, b_ref, g_ref, s0_ref, rk_ref, lnw_ref, lnb_ref, seg_ref,
                 y_ref, sout_ref, s_scr):
    ci = pl.program_id(1)
    nb, L, c = r_ref.shape
    gw, gh, hd = GROUP_LANES, GROUP_HEADS, HEAD_DIM
    n_groups = c // gw

    lane_head = lax.broadcasted_iota(jnp.int32, (1, gw), 1) // hd
    head_masks = [lane_head == j for j in range(gh)]
    bd_state = (lax.broadcasted_iota(jnp.int32, (gw, gw), 0) // hd) == (lax.broadcasted_iota(jnp.int32, (gw, gw), 1) // hd)
    bd_time = (lax.broadcasted_iota(jnp.int32, (gh * L, gh * L), 0) // L) == (lax.broadcasted_iota(jnp.int32, (gh * L, gh * L), 1) // L)
    t_row = lax.broadcasted_iota(jnp.int32, (L, gh * L), 0)
    t_col = lax.broadcasted_iota(jnp.int32, (L, gh * L), 1) % L
    strict4 = t_row > t_col
    incl4 = t_row >= t_col
    incl = lax.broadcasted_iota(jnp.int32, (L, L), 0) >= lax.broadcasted_iota(jnp.int32, (L, L), 1)

    def stack(x):
        return jnp.concatenate([jnp.where(m, x, jnp.zeros_like(x)) for m in head_masks], axis=0)

    def block_diag(n):
        tiled = jnp.concatenate([n] * gh, axis=0)
        return jnp.where(bd_time, tiled, jnp.zeros_like(tiled))

    @pl.when(ci == 0)
    def _():
        for bi in range(nb):
            for gi in range(n_groups):
                s_in = s0_ref[bi, gi * gh:(gi + 1) * gh].reshape(gw, hd)
                s_scr[bi, gi] = jnp.where(bd_state, jnp.concatenate([s_in] * gh, axis=1), 0.0)

    n_apply = max(1, int(math.log2(L)))
    seg = seg_ref[...]
    pre = []
    for bi in range(nb):
        lw = lw_ref[bi]
        lw_hi = lw.astype(BF16)
        lw_r = lw - lw_hi.astype(F32)
        lw_mid = lw_r.astype(BF16)
        lw_lo = (lw_r - lw_mid.astype(F32)).astype(BF16)
        cs3 = _dot(incl.astype(BF16), jnp.concatenate([lw_hi, lw_mid, lw_lo], axis=1))
        cs = cs3[:, :c] + cs3[:, c:2 * c] + cs3[:, 2 * c:]
        cp = cs - lw
        cm = cs[L // 2 - 1:L // 2, :]
        c_last = cs[L - 1:L, :]
        r, k, v, kk, b = r_ref[bi], k_ref[bi], v_ref[bi], kk_ref[bi], b_ref[bi]
        e_dn = jnp.exp(cm - cs)
        e_l = jnp.exp(c_last - cs)
        pre.append(dict(v=v, rt=r * jnp.exp(cs - cm), kkt=kk * jnp.exp(cp - cm), bt=b * e_dn, kt=k * e_dn,
                        kg=kk * jnp.exp(cp), rg=r * jnp.exp(cs), bh=b * e_l, kh=k * e_l, g_last=jnp.exp(c_last),
                        rkk=r * k * rk_ref[...]))

    chains = [(bi, gi) for bi in range(nb) for gi in range(n_groups)]
    col = lambda bi, gi, name: pre[bi][name][:, gi * gw:(gi + 1) * gw]
    each = lambda fn: [fn(i, bi, gi) for i, (bi, gi) in enumerate(chains)]

    vg = each(lambda i, bi, gi: col(bi, gi, 'v'))
    v_st = each(lambda i, bi, gi: stack(vg[i].astype(BF16)))
    a_all = each(lambda i, bi, gi: _mm(
        jnp.concatenate([col(bi, gi, 'kkt'), col(bi, gi, 'rt')], axis=0),
        jnp.concatenate([stack(col(bi, gi, 'bt').astype(BF16)), stack(col(bi, gi, 'kt').astype(BF16))], axis=0),
        NT))
    p_ak = each(lambda i, bi, gi: jnp.where(strict4, a_all[i][:L, gh * L:], 0.0))
    p_rb = each(lambda i, bi, gi: jnp.where(incl4, a_all[i][L:, :gh * L], 0.0))
    p_rk = each(lambda i, bi, gi: jnp.where(incl4, a_all[i][L:, gh * L:], 0.0))
    eye4 = (t_row == t_col).astype(F32)
    nm = each(lambda i, bi, gi: -jnp.where(strict4, a_all[i][:L, :gh * L], 0.0))
    t_inv = [eye4 + n for n in nm]
    for it in range(n_apply - 1):
        lhs = nm if it == 0 else [jnp.concatenate([n, t], axis=0) for n, t in zip(nm, t_inv)]
        both = each(lambda i, bi, gi: _mm(lhs[i], block_diag(nm[i].astype(BF16)), NN))
        if it > 0:
            t_inv = [t + bo[L:] for t, bo in zip(t_inv, both)]
        nm = [bo[:L] for bo in both]
    t_inv = each(lambda i, bi, gi: t_inv[i] + _mm(t_inv[i], block_diag(nm[i].astype(BF16)), NN))
    av = each(lambda i, bi, gi: _mm(p_ak[i], v_st[i], NN))
    x = each(lambda i, bi, gi: _mm(
        t_inv[i], jnp.concatenate([stack(col(bi, gi, 'kg').astype(BF16)), stack(av[i].astype(BF16))], axis=1),
        NN))
    w_m = [xi[:, :gw] for xi in x]
    u0 = [-xi[:, gw:] for xi in x]
    rbw = each(lambda i, bi, gi: _mm(
        p_rb[i], jnp.concatenate([stack(w_m[i].astype(BF16)), stack(u0[i].astype(BF16))], axis=1),
        NN))
    rkv = each(lambda i, bi, gi: _mm(p_rk[i], v_st[i], NN))
    s_old = each(lambda i, bi, gi: s_scr[bi, gi])
    ws = each(lambda i, bi, gi: _mm(
        jnp.concatenate([w_m[i], col(bi, gi, 'rg') - rbw[i][:, :gw]], axis=0), s_old[i], NT))
    u = each(lambda i, bi, gi: u0[i] - ws[i][:L])
    y = each(lambda i, bi, gi: ws[i][L:] + rbw[i][:, gw:] + rkv[i])
    upd = each(lambda i, bi, gi: _mm(
        jnp.concatenate([u[i], vg[i]], axis=0),
        jnp.concatenate([col(bi, gi, 'bh'), col(bi, gi, 'kh')], axis=0), TN))
    for i, (bi, gi) in enumerate(chains):
        s_scr[bi, gi] = s_old[i] * col(bi, gi, 'g_last') + jnp.where(bd_state, upd[i], 0.0)

    inv = 1.0 / hd
    n_ch = len(chains)
    sums = _dot_split(jnp.concatenate(y + each(lambda i, bi, gi: col(bi, gi, 'rkk')), axis=0), seg)
    mean = [sums[i * L:(i + 1) * L] * inv for i in range(n_ch)]
    bonus = [sums[(n_ch + i) * L:(n_ch + i + 1) * L] * vg[i] for i in range(n_ch)]
    yc = [y[i] - mean[i] for i in range(n_ch)]
    sq = _dot_split(jnp.concatenate([z * z for z in yc], axis=0), seg)
    var = [sq[i * L:(i + 1) * L] * inv for i in range(n_ch)]
    for i, (bi, gi) in enumerate(chains):
        sl = slice(gi * gw, (gi + 1) * gw)
        yn = yc[i] * lax.rsqrt(var[i] + GN_EPS) * lnw_ref[:, sl] + lnb_ref[:, sl]
        y_ref[bi, :, sl] = (yn + bonus[i]) * g_ref[bi, :, sl]

    @pl.when(ci == pl.num_programs(1) - 1)
    def _():
        for bi in range(nb):
            for gi in range(n_groups):
                bd = s_scr[bi, gi]
                folded = bd[:, 0:hd]
                for j in range(1, gh):
                    folded = folded + bd[:, j * hd:(j + 1) * hd]
                sout_ref[bi, gi * gh:(gi + 1) * gh] = folded.reshape(gh, hd, hd)


def _rwkv_call(vecs, s0, p, n_seq, seq_len, chunk, nb):
    c = p['c_rwkv']
    n_heads = c // HEAD_DIM
    n_chunks = seq_len // chunk
    assert n_seq % nb == 0 and seq_len % chunk == 0
    vecs = [z.reshape(n_seq, seq_len, c) for z in vecs]
    tok = pl.BlockSpec((nb, chunk, c), lambda bi, ci: (bi, ci, 0))
    st = pl.BlockSpec((nb, n_heads, HEAD_DIM, HEAD_DIM), lambda bi, ci: (bi, 0, 0, 0))
    rowvec = pl.BlockSpec((1, c), lambda bi, ci: (0, 0))
    seg = p['seg'][:GROUP_LANES, :GROUP_LANES]
    y, s_out = pl.pallas_call(
        _rwkv_kernel,
        grid=(n_seq // nb, n_chunks),
        in_specs=[tok] * 7 + [st, rowvec, rowvec, rowvec, pl.BlockSpec(seg.shape, lambda bi, ci: (0, 0))],
        out_specs=[tok, st],
        out_shape=[jax.ShapeDtypeStruct((n_seq, seq_len, c), F32),
                   jax.ShapeDtypeStruct((n_seq, n_heads, HEAD_DIM, HEAD_DIM), F32)],
        scratch_shapes=[pltpu.VMEM((nb, c // GROUP_LANES, GROUP_LANES, GROUP_LANES), F32)],
        compiler_params=pltpu.CompilerParams(dimension_semantics=("arbitrary", "arbitrary"),
                                             vmem_limit_bytes=V7X_VMEM_LIMIT),
        name="rwkv",
    )(*vecs, s0, p['r_k'], p['ln_w'], p['ln_b'], seg)
    return y.reshape(n_seq * seq_len, c), s_out


def _attn_prompt_kernel(q_ref, k_ref, v_ref, o_ref, m_scr, l_scr, acc_scr):
    s_len = q_ref.shape[0]
    band = ATT_BAND
    n_blk = s_len // band

    lane = lax.broadcasted_iota(jnp.int32, (1, LANES), 1)
    head0 = lane < HEAD_DIM
    qi = lax.broadcasted_iota(jnp.int32, (band, 2 * band), 0)
    kj = lax.broadcasted_iota(jnp.int32, (band, 2 * band), 1)
    in_band = (kj >= qi) & (kj <= qi + band)
    in_band2 = jnp.concatenate([in_band, in_band], axis=0)
    kj2 = jnp.concatenate([kj, kj], axis=0)
    scale = HEAD_DIM ** -0.5
    ones = jnp.ones((2 * band, LANES), BF16)

    for ci, (window, dil) in enumerate(DILATED_CONFIGS):
        assert window // dil == band
        per_res = n_blk // dil

        unroll = ATT_UNROLL[dil]
        run = min(unroll, per_res)

        def body(it, carry, ci=ci, dil=dil, per_res=per_res, run=run, unroll=unroll):
            span = band * dil
            tile = lambda start: pl.ds(start, band, stride=dil) if dil > 1 else pl.ds(start, band)
            blocks = []
            tiles = []
            first_dyn = []
            for r in range(unroll // run):
                i0 = it * unroll + r * run
                blk0 = i0 % per_res
                start0 = i0 // per_res + blk0 * span
                base = len(tiles)
                if per_res > run:
                    tiles.append(tile(jnp.maximum(start0 - span, 0)))
                    first_dyn.append(blk0 == 0)
                else:
                    tiles.append(None)
                    first_dyn.append(None)
                for t in range(run):
                    tiles.append(tile(start0 + t * span))
                    blocks.append((tiles[-1], base + t, r if t == 0 else None))
            kt = [None if w is None else k_ref[w, :].astype(BF16) for w in tiles]
            vt = [None if w is None else v_ref[w, :].astype(BF16) for w in tiles]
            prev = lambda ts, i: ts[i + 1] if ts[i] is None else ts[i]
            q = [q_ref[rows, :] * scale for rows, _, _ in blocks]

            def mask_of(first):
                if first is None:
                    return in_band2
                if first_dyn[first] is None:
                    return in_band2 & (kj2 >= band)
                return in_band2 & (kj2 >= jnp.where(first_dyn[first], band, 0))

            s = [jnp.where(mask_of(first), lax.dot_general(
                jnp.concatenate([jnp.where(head0, qj, 0.0), jnp.where(head0, 0.0, qj)], axis=0).astype(BF16),
                jnp.concatenate([prev(kt, i), kt[i + 1]], axis=0), NT, preferred_element_type=F32), NEG_INF)
                 for qj, (_, i, first) in zip(q, blocks)]
            m = [jnp.max(z, axis=-1, keepdims=True) for z in s]
            p = [jnp.exp(z - mx).astype(BF16) for z, mx in zip(s, m)]
            o = [_dot(pj, jnp.concatenate([jnp.concatenate([prev(vt, i), vt[i + 1]], axis=0), ones], axis=1))
                 for pj, (_, i, _) in zip(p, blocks)]
            for j, (rows, _, _) in enumerate(blocks):
                m_scr[ci, rows, :] = jnp.where(head0, m[j][:band], m[j][band:])
                acc_scr[ci, rows, :] = jnp.where(head0, o[j][:band, :LANES], o[j][band:, :LANES])
                l_scr[ci, rows, :] = jnp.where(head0, o[j][:band, LANES:], o[j][band:, LANES:])
            return carry

        lax.fori_loop(0, n_blk // unroll, body, 0)

    rows_per = 256

    def merge(i, carry):
        rows = pl.ds(pl.multiple_of(i * rows_per, rows_per), rows_per)
        ms = [m_scr[ci, rows, :] for ci in range(len(DILATED_CONFIGS))]
        m_all = functools.reduce(jnp.maximum, ms)
        num = jnp.zeros((rows_per, LANES), F32)
        den = jnp.zeros((rows_per, LANES), F32)
        for ci, m_c in enumerate(ms):
            w_c = jnp.exp(m_c - m_all)
            num = num + w_c * acc_scr[ci, rows, :]
            den = den + w_c * l_scr[ci, rows, :]
        o_ref[rows, :] = num / den
        return carry

    lax.fori_loop(0, s_len // rows_per, merge, 0)


def _attn_prompt_call(q, k, v, n_seq, seq_len):
    c = q.shape[1]
    n_pairs = c // LANES
    blk = pl.BlockSpec((seq_len, LANES), lambda bi, hi: (bi, hi))
    return pl.pallas_call(
        _attn_prompt_kernel,
        grid=(n_seq, n_pairs),
        in_specs=[blk, blk, blk],
        out_specs=blk,
        out_shape=jax.ShapeDtypeStruct((n_seq * seq_len, c), F32),
        scratch_shapes=[pltpu.VMEM((len(DILATED_CONFIGS), seq_len, LANES), F32)] * 3,
        compiler_params=pltpu.CompilerParams(dimension_semantics=("arbitrary", "arbitrary"),
                                             vmem_limit_bytes=V7X_VMEM_LIMIT),
        name="attn_prompt",
    )(q, k, v)


def _attn_sample_kernel(q_ref, kn_ref, vn_ref, kc_ref, vc_ref, o_ref, *, n_new):
    hd = HEAD_DIM
    _, n_heads, _, n_buf = kc_ref.shape
    t_pad = kn_ref.shape[1]
    c = q_ref.shape[2]
    q = q_ref[0] * (hd ** -0.5)
    lane_head = lax.broadcasted_iota(jnp.int32, (1, c), 1) // hd
    qs = jnp.concatenate([jnp.where(lane_head == h, q, 0.0) for h in range(n_heads)], axis=0).astype(BF16)
    n_rows = n_heads * t_pad
    t_idx = lax.broadcasted_iota(jnp.int32, (n_rows, 1), 0) % t_pad

    def multiplicity(dist):
        mult = jnp.zeros(dist.shape, F32)
        for window, dil in DILATED_CONFIGS:
            hit = (dist >= 0) & (dist <= window) & (dist % dil == 0)
            mult = mult + jnp.where(hit, 1.0, 0.0)
        return mult

    jc = lax.broadcasted_iota(jnp.int32, (1, n_buf), 1)
    mult_c = multiplicity(n_buf + t_idx - jc)
    jn = lax.broadcasted_iota(jnp.int32, (1, t_pad), 1)
    mult_n = jnp.where(jn < n_new, multiplicity(t_idx - jn), 0.0)

    sc = jnp.concatenate(
        [_dot(qs[h * t_pad:(h + 1) * t_pad, h * hd:(h + 1) * hd], kc_ref[0, h].astype(BF16))
         for h in range(n_heads)], axis=0)
    sn = lax.dot_general(qs, kn_ref[0].astype(BF16), NT, preferred_element_type=F32)
    sc = jnp.where(mult_c > 0.0, sc, NEG_INF)
    sn = jnp.where(mult_n > 0.0, sn, NEG_INF)
    m = jnp.maximum(jnp.max(sc, axis=-1, keepdims=True), jnp.max(sn, axis=-1, keepdims=True))
    pc = (mult_c * jnp.exp(sc - m)).astype(BF16)
    pn = mult_n * jnp.exp(sn - m)
    inv_l = 1.0 / (jnp.sum(pc.astype(F32), axis=-1, keepdims=True) + jnp.sum(pn, axis=-1, keepdims=True))
    o_new = _dot(pn.astype(BF16), vn_ref[0].astype(BF16)) * inv_l
    out = jnp.zeros((t_pad, c), F32)
    for h in range(n_heads):
        out = out + jnp.where(lane_head == h, o_new[h * t_pad:(h + 1) * t_pad, :], 0.0)
    o_buf = [lax.dot_general(pc[h * t_pad:(h + 1) * t_pad, :], vc_ref[0, h].astype(BF16), NT,
                             preferred_element_type=F32) * inv_l[h * t_pad:(h + 1) * t_pad, :]
             for h in range(n_heads)]
    o_ref[0] = out + jnp.concatenate(o_buf, axis=1)


def _attn_sample_call(q, kn, vn, k_buf, v_buf, n_new):
    b, t_pad, c = q.shape
    _, n_buf, n_heads, hd = k_buf.shape
    k_t = jnp.transpose(k_buf, (0, 2, 3, 1))
    v_t = jnp.transpose(v_buf, (0, 2, 3, 1))
    new = pl.BlockSpec((1, t_pad, c), lambda bi: (bi, 0, 0))
    buf = pl.BlockSpec((1, n_heads, hd, n_buf), lambda bi: (bi, 0, 0, 0))
    return pl.pallas_call(
        functools.partial(_attn_sample_kernel, n_new=n_new),
        grid=(b,),
        in_specs=[new, new, new, buf, buf],
        out_specs=new,
        out_shape=jax.ShapeDtypeStruct((b, t_pad, c), F32),
        compiler_params=pltpu.CompilerParams(dimension_semantics=("arbitrary",), vmem_limit_bytes=V7X_VMEM_LIMIT),
        name="attn_sample",
    )(q, kn, vn, k_t, v_t)


def _route_rows(logits, seen):
    lane = lax.broadcasted_iota(jnp.int32, logits.shape, 1)
    lane_f = lane.astype(F32)
    first = lambda hit: jnp.min(jnp.where(hit, lane_f, float(LANES)), axis=-1, keepdims=True)
    is_g = lane < N_GROUPS
    lg = jnp.where(is_g, logits, NEG_INF)
    g_max = jnp.max(lg, axis=-1, keepdims=True)
    g_idx = first(lg == g_max)
    g_w = 1.0 / jnp.sum(jnp.where(is_g, jnp.exp(lg - g_max), 0.0), axis=-1, keepdims=True)
    lo = N_GROUPS + EXPERTS_PER_GROUP * g_idx
    le = jnp.where((lane_f >= lo) & (lane_f < lo + EXPERTS_PER_GROUP), logits, NEG_INF)
    e1 = jnp.max(le, axis=-1, keepdims=True)
    i1 = first(le == e1)
    le2 = jnp.where(lane_f == i1, NEG_INF, le)
    e2 = jnp.max(le2, axis=-1, keepdims=True)
    i2 = first(le2 == e2)
    ex = jnp.exp(e2 - e1)
    gate1 = g_w / (1.0 + ex)
    gate2 = g_w * ex / (1.0 + ex)
    tm = logits.shape[0]
    pick1 = lane_f == i1
    pick2 = lane_f == i2
    picks = jnp.where(pick1 | pick2, 1.0, 0.0)
    earlier = (lax.broadcasted_iota(jnp.int32, (tm, tm), 0) > lax.broadcasted_iota(jnp.int32, (tm, tm), 1))
    before = seen + _dot(earlier.astype(BF16), picks.astype(BF16))
    rank1 = jnp.sum(jnp.where(pick1, before, 0.0), axis=-1, keepdims=True)
    rank2 = jnp.sum(jnp.where(pick2, before, 0.0), axis=-1, keepdims=True)
    out = jnp.where(lane == 0, gate1, jnp.where(lane == 1, gate2, 0.0))
    out = jnp.where(lane == 2, i1 - N_GROUPS, jnp.where(lane == 3, i2 - N_GROUPS, out))
    out = jnp.where(lane == 4, rank1, jnp.where(lane == 5, rank2, out))
    return out, jnp.sum(picks, axis=0, keepdims=True)


def _post_kernel(x_ref, yr_ref, ya_ref, wo_ref, gffn_ref, rw_cat_ref, rb_ref,
                 h_o, hn_o, lg_o, cnt_o, seen_scr):
    @pl.when(pl.program_id(0) == 0)
    def _():
        seen_scr[...] = jnp.zeros_like(seen_scr)

    c = yr_ref.shape[1]
    h = (x_ref[...] + _dot(yr_ref[...].astype(BF16), wo_ref[:c, :]) + _dot(ya_ref[...].astype(BF16), wo_ref[c:, :]))
    ms = jnp.mean(h * h, axis=-1, keepdims=True)
    hn = (h * lax.rsqrt(ms + NORM_EPS)) * gffn_ref[...]
    h_o[...] = h
    bits = pltpu.bitcast(hn.astype(BF16).astype(F32), jnp.uint32)
    half = hn.shape[1] // 2
    hn_o[...] = (bits[:, :half] >> 16) | (bits[:, half:] & jnp.uint32(0xFFFF0000))
    hi = hn.astype(BF16)
    lo = (hn - hi.astype(F32)).astype(BF16)
    both = _dot(hi, rw_cat_ref[...])
    logits = (both[:, :LANES] + both[:, LANES:] + _dot(lo, rw_cat_ref[:, :LANES])) + rb_ref[...]
    route, picked = _route_rows(logits, seen_scr[...])
    lg_o[...] = route
    seen_scr[...] = seen_scr[...] + picked
    cnt_o[...] = seen_scr[...]


def _post_call(x2, yr, ya, p, tm):
    t, d = x2.shape
    c = yr.shape[1]
    full = lambda a: pl.BlockSpec(a.shape, lambda i: (0,) * a.ndim)
    tok = lambda w: pl.BlockSpec((tm, w), lambda i: (i, 0))
    weights = [p['wout'], p['gffn'], p['rw_cat'], p['rb']]
    return pl.pallas_call(
        _post_kernel,
        grid=(t // tm,),
        in_specs=[tok(d), tok(c), tok(c)] + [full(w) for w in weights],
        out_specs=[tok(d), tok(d // 2), tok(LANES), pl.BlockSpec((1, LANES), lambda i: (0, 0))],
        out_shape=[jax.ShapeDtypeStruct((t, d), F32), jax.ShapeDtypeStruct((t, d // 2), jnp.uint32),
                   jax.ShapeDtypeStruct((t, LANES), F32), jax.ShapeDtypeStruct((1, LANES), F32)],
        scratch_shapes=[pltpu.VMEM((1, LANES), F32)],
        compiler_params=pltpu.CompilerParams(dimension_semantics=("arbitrary",), vmem_limit_bytes=V7X_VMEM_LIMIT),
        name="post",
    )(x2, yr, ya, *weights)


def _expert_kernel(ve_ref, vb_ref, lo_ref, hi_ref, nv_ref, xs_ref, w1_ref, w3_ref, w2_ref, y_ref, w1_s, w3_s, w2_s):
    i = pl.program_id(0)
    live = i < nv_ref[0]
    prev = jnp.maximum(i - 1, 0)

    @pl.when(live & ((i == 0) | (ve_ref[i] != ve_ref[prev])))
    def _():
        w1_s[...] = w1_ref[...].astype(BF16)
        w3_s[...] = w3_ref[...].astype(BF16)
        w2_s[...] = w2_ref[...].astype(BF16)

    @pl.when(live)
    def _():
        packed = xs_ref[...]
        half = packed.shape[1]
        lo = pltpu.bitcast(packed << 16, F32).astype(BF16)
        hi = pltpu.bitcast(packed & jnp.uint32(0xFFFF0000), F32).astype(BF16)
        h1 = _dot(lo, w1_s[:half, :]) + _dot(hi, w1_s[half:, :])
        h3 = _dot(lo, w3_s[:half, :]) + _dot(hi, w3_s[half:, :])
        act = (h1 * _sigmoid(h1)) * h3
        y = _dot(act.astype(BF16), w2_s[...])
        row = lax.broadcasted_iota(jnp.int32, (y.shape[0], 1), 0)
        mine = (row >= lo_ref[i]) & (row < hi_ref[i])
        revisit = (i > 0) & (vb_ref[i] == vb_ref[prev])

        @pl.when(revisit)
        def _():
            y_ref[...] = jnp.where(mine, y, y_ref[...])

        @pl.when(jnp.logical_not(revisit))
        def _():
            y_ref[...] = jnp.where(mine, y, 0.0)


def _expert_call(visits, xs, p, bm, n_rows):
    vis_exp, vis_blk, vis_lo, vis_hi, n_vis = visits
    _, d, de = p['w1'].shape
    grid_spec = pltpu.PrefetchScalarGridSpec(
        num_scalar_prefetch=5,
        grid=(vis_exp.shape[0],),
        in_specs=[pl.BlockSpec((bm, d // 2), lambda i, ve, vb, lo, hi, nv: (vb[i], 0)),
                  pl.BlockSpec((None, d, de), lambda i, ve, vb, lo, hi, nv: (ve[i], 0, 0)),
                  pl.BlockSpec((None, d, de), lambda i, ve, vb, lo, hi, nv: (ve[i], 0, 0)),
                  pl.BlockSpec((None, de, d), lambda i, ve, vb, lo, hi, nv: (ve[i], 0, 0))],
        out_specs=pl.BlockSpec((bm, d), lambda i, ve, vb, lo, hi, nv: (vb[i], 0)),
        scratch_shapes=[pltpu.VMEM((d, de), BF16), pltpu.VMEM((d, de), BF16), pltpu.VMEM((de, d), BF16)],
    )
    return pl.pallas_call(
        _expert_kernel,
        grid_spec=grid_spec,
        out_shape=jax.ShapeDtypeStruct((n_rows, d), F32),
        compiler_params=pltpu.CompilerParams(dimension_semantics=("arbitrary",), vmem_limit_bytes=V7X_VMEM_LIMIT),
        name="experts",
    )(vis_exp, vis_blk, vis_lo, vis_hi, n_vis, xs, p['w1'], p['w3'], p['w2'])


def _final_kernel(h_ref, route_ref, y1_ref, y2_ref, gfin_ref, o_ref):
    route = route_ref[...]
    h = h_ref[...] + (route[:, 0:1] * y1_ref[...] + route[:, 1:2] * y2_ref[...])
    ms = jnp.mean(h * h, axis=-1, keepdims=True)
    o_ref[...] = (h * lax.rsqrt(ms + NORM_EPS)) * gfin_ref[...]


def _final_call(h, route, y12, gfin, tm):
    t, d = h.shape
    tok = pl.BlockSpec((tm, d), lambda i: (i, 0))
    routed = lambda a: pl.BlockSpec((None, tm, d), lambda i: (a, i, 0))
    return pl.pallas_call(
        _final_kernel,
        grid=(t // tm,),
        in_specs=[tok, pl.BlockSpec((tm, LANES), lambda i: (i, 0)), routed(0), routed(1),
                  pl.BlockSpec((1, d), lambda i: (0, 0))],
        out_specs=tok,
        out_shape=jax.ShapeDtypeStruct((t, d), F32),
        compiler_params=pltpu.CompilerParams(dimension_semantics=("arbitrary",), vmem_limit_bytes=V7X_VMEM_LIMIT),
        name="final",
    )(h, route, y12, y12, gfin)


def _gather_rows(table, idx):
    info = plsc.get_sparse_core_info()
    nc, ns = info.num_cores, info.num_subcores
    b, d = idx.shape[0], table.shape[1]
    chunk = min(SC_INDEX_LIMIT, SC_GATHER_BYTES // (d * table.dtype.itemsize))
    assert b % (nc * ns * chunk * 2) == 0, "rows must split evenly into chunk pairs per subcore"
    per_w = b // (nc * ns)
    n_chunks = per_w // chunk
    mesh = plsc.VectorSubcoreMesh(core_axis_name="c", subcore_axis_name="s")

    @functools.partial(
        pl.kernel, mesh=mesh, out_type=jax.ShapeDtypeStruct((b, d), table.dtype),
        scratch_types=[pltpu.VMEM((per_w,), jnp.int32), pltpu.VMEM((2, chunk, d), table.dtype),
                       pltpu.SemaphoreType.DMA((2,)), pltpu.SemaphoreType.DMA((2,))])
    def gather(table_hbm, idx_hbm, out_hbm, idx_v, rows_v, fetch_sem, put_sem):
        base = (lax.axis_index("s") * nc + lax.axis_index("c")) * per_w
        pltpu.sync_copy(idx_hbm.at[pl.ds(base, per_w)], idx_v)

        def fetch(c, slot):
            off = pl.multiple_of(c * chunk, chunk)
            return pltpu.make_async_copy(table_hbm.at[idx_v.at[pl.ds(off, chunk)]], rows_v.at[slot],
                                         fetch_sem.at[slot])

        def put(c, slot):
            off = pl.multiple_of(c * chunk, chunk)
            return pltpu.make_async_copy(rows_v.at[slot], out_hbm.at[pl.ds(base + off, chunk)], put_sem.at[slot])

        fetch(0, 0).start()

        @pl.loop(0, n_chunks, step=2)
        def _(c):
            @pl.when(c > 0)
            def _():
                put(c - 1, 1).wait()
            fetch(c + 1, 1).start()
            fetch(c, 0).wait()
            put(c, 0).start()
            fetch(c + 1, 1).wait()
            put(c, 0).wait()

            @pl.when(c + 2 < n_chunks)
            def _():
                fetch(c + 2, 0).start()
            put(c + 1, 1).start()

        put(n_chunks - 1, 1).wait()

    return gather(table, idx)


def _route(route, counts, bm):
    n = route.shape[0]
    eid = route[:, 2:4].astype(jnp.int32)
    rank = route[:, 4:6].astype(jnp.int32)
    m = 2 * n
    assert m % bm == 0
    experts = jnp.arange(N_EXPERTS, dtype=jnp.int32)
    counts = counts[0, N_GROUPS:N_GROUPS + N_EXPERTS].astype(jnp.int32)
    ends = jnp.cumsum(counts)
    starts = ends - counts
    lookup = lambda tbl, e: jnp.sum(jnp.where(e[..., None] == experts, tbl, 0), axis=-1)
    dest = lookup(starts, eid.T) + rank.T
    bits = max(1, (m - 1).bit_length())
    order = lax.sort(eid.reshape(-1) * (1 << bits) + jnp.arange(m, dtype=jnp.int32)) & ((1 << bits) - 1)
    pad = _round_up(m, SLOT_MULTIPLE) - m
    row_tok = jnp.concatenate([order // 2, jnp.arange(pad, dtype=jnp.int32) % n])
    blk_lo = starts // bm
    n_touch = jnp.where(counts > 0, (ends - 1) // bm - blk_lo + 1, 0)
    v_end = jnp.cumsum(n_touch)
    v = jnp.arange(m // bm + N_EXPERTS, dtype=jnp.int32)
    vis_exp = jnp.minimum(jnp.sum((v_end[None, :] <= v[:, None]).astype(jnp.int32), axis=1), N_EXPERTS - 1)
    vis_blk = jnp.minimum(lookup(blk_lo, vis_exp) + v - lookup(v_end - n_touch, vis_exp), m // bm - 1)
    vis_lo = jnp.clip(lookup(starts, vis_exp) - vis_blk * bm, 0, bm)
    vis_hi = jnp.clip(lookup(ends, vis_exp) - vis_blk * bm, 0, bm)
    n_vis = v_end[-1:].astype(jnp.int32)
    return row_tok, dest, (vis_exp.astype(jnp.int32), vis_blk.astype(jnp.int32), vis_lo.astype(jnp.int32),
                           vis_hi.astype(jnp.int32), n_vis)


def _moe_dispatch(x2, yr, ya, p, tm, bm):
    h, hn, route, counts = _post_call(x2, yr, ya, p, tm)
    row_tok, dest, visits = _route(route, counts, bm)
    return dict(h=h, route=route, dest=dest, visits=visits, xs=_gather_rows(hn, row_tok))


def _moe_combine(ctx, yb, p, tm):
    h = ctx['h']
    y12 = _gather_rows(yb, ctx['dest'].reshape(-1)).reshape(2, h.shape[0], h.shape[1])
    return _final_call(h, ctx['route'], y12, p['gfin'], tm)


def _prep_params(layer, norm_mix_g, w_in, rwkv_mu_rkv, rwkv_mu_wag, rwkv_w0, rwkv_w1, rwkv_w2, rwkv_a0, rwkv_a1,
                 rwkv_a2, rwkv_g1, rwkv_g2, rwkv_k_k, rwkv_k_a, rwkv_r_k, rwkv_ln_w, rwkv_ln_b, w_out, norm_ffn_g,
                 router_group_w, router_group_b, router_expert_w, router_expert_b, expert_w1, expert_w3, expert_w2,
                 norm_final_g):
    d = w_in.shape[1]
    c = rwkv_w0.shape[1]
    row = lambda a: a.reshape(1, -1).astype(F32)
    lowrank = jnp.concatenate([rwkv_w1[layer], rwkv_a1[layer], rwkv_g1[layer]], axis=1)
    mx = rwkv_mu_wag[layer]
    r_w = rwkv_w1.shape[2]
    r_a = rwkv_a1.shape[2]
    r_g = rwkv_g1.shape[2]
    assert r_w + r_a == LANES and r_g == LANES
    mx_cols = jnp.concatenate([jnp.broadcast_to(mx[0][:, None], (d, r_w)), jnp.broadcast_to(mx[1][:, None], (d, r_a)),
                               jnp.broadcast_to(mx[2][:, None], (d, r_g))], axis=1)
    w2a2 = jnp.zeros((LANES, 2 * c), F32)
    w2a2 = w2a2.at[:r_w, :c].set(rwkv_w2[layer]).at[r_w:, c:].set(rwkv_a2[layer])
    head = jnp.arange(c) // HEAD_DIM
    rw = jnp.zeros((d, LANES), F32)
    rw = rw.at[:, :N_GROUPS].set(router_group_w[layer]).at[:, N_GROUPS:N_GROUPS + N_EXPERTS].set(router_expert_w[layer])
    rw_hi = rw.astype(BF16)
    rb = jnp.zeros((1, LANES), F32)
    rb = rb.at[0, :N_GROUPS].set(router_group_b[layer]).at[0, N_GROUPS:N_GROUPS + N_EXPERTS].set(router_expert_b[layer])
    return {
        'c_rwkv': c,
        'gmix': row(norm_mix_g[layer]),
        'win': jnp.concatenate([w_in[layer], lowrank], axis=1).astype(BF16),
        'wdx': (mx_cols * lowrank).astype(BF16),
        'mu_rkv': rwkv_mu_rkv[layer],
        'w0a0': jnp.concatenate([row(rwkv_w0[layer]), row(rwkv_a0[layer])], axis=1),
        'w2a2': w2a2.astype(BF16),
        'g2': rwkv_g2[layer].astype(BF16),
        'k_k': row(rwkv_k_k[layer]),
        'k_a': row(rwkv_k_a[layer]),
        'seg': (head[:, None] == head[None, :]).astype(BF16),
        'r_k': row(rwkv_r_k[layer]),
        'ln_w': row(rwkv_ln_w[layer]),
        'ln_b': row(rwkv_ln_b[layer]),
        'wout': w_out[layer].astype(BF16),
        'gffn': row(norm_ffn_g[layer]),
        'rw_cat': jnp.concatenate([rw_hi, (rw - rw_hi.astype(F32)).astype(BF16)], axis=1),
        'rb': rb,
        'w1': expert_w1[layer],
        'w3': expert_w3[layer],
        'w2': expert_w2[layer],
        'gfin': row(norm_final_g),
    }


def _prompt_mix(x, p):
    b, s, d = x.shape
    c = p['c_rwkv']
    x2 = x.reshape(b * s, d)
    tm = PROJ_TILE
    flag = jnp.zeros((b * s, 1), F32)
    keep = min(max(w for w, _ in DILATED_CONFIGS), s)
    assert keep % tm == 0
    r, lw, k, v, kk, bb, g, qa, ka, va, xl, kt, vt = _proj_call(x2, flag, p, s // tm, tm, keep_tiles=keep // tm)
    s0 = jnp.zeros((b, c // HEAD_DIM, HEAD_DIM, HEAD_DIM), F32)
    yr, s_new = _rwkv_call((r, lw, k, v, kk, bb, g), s0, p, b, s, RWKV_CHUNK, 8)
    ya = _attn_prompt_call(qa, ka, va, b, s)
    shift = xl.reshape(b, s // tm, 8, d)[:, -1, 7, :]
    to_cache = lambda z: jnp.transpose(z.reshape(b, c // HEAD_DIM, HEAD_DIM, keep), (0, 3, 1, 2))
    k_keep, v_keep = to_cache(kt), to_cache(vt)
    return (x2, yr, ya), (s_new, shift, k_keep, v_keep)


def _sample_mix(x, shift0, s0, k_buf, v_buf, p, after):
    b, t, d = x.shape
    c = p['c_rwkv']
    n_heads = c // HEAD_DIM
    t_pad = 8
    xc = jnp.concatenate([shift0[:, None, :], x, jnp.zeros((b, t_pad - 1 - t, d), x.dtype)], axis=1)
    flag = jnp.zeros((b, t_pad, 1), F32).at[:, 0].set(1.0)
    outs = _proj_call(xc.reshape(b * t_pad, d), flag.reshape(b * t_pad, 1), p, 1, b * t_pad, xl_rows=b * t_pad)
    xl = outs[10]
    live = (jnp.arange(t_pad) < t)[None, :, None]
    shifted = [jnp.where(live, jnp.roll(o.reshape(b, t_pad, c), -1, axis=1), 0.0) for o in outs[:10]]
    r, lw, k, v, kk, bb, g, qa, ka, va = shifted
    flat = lambda z: z.reshape(b * t_pad, c)
    yr, s_new = _rwkv_call(tuple(flat(z) for z in (r, lw, k, v, kk, bb, g)), s0, p, b, t_pad, t_pad, 8)
    qa_held, _ = lax.optimization_barrier((qa, after))
    ya = _attn_sample_call(qa_held, ka, va, k_buf, v_buf, t)
    x_pad = jnp.concatenate([x, jnp.zeros((b, t_pad - t, d), x.dtype)], axis=1).reshape(b * t_pad, d)
    shift = xl.reshape(b, t_pad, d)[:, t]
    return ((x_pad, yr, flat(ya)),
            (s_new, shift, ka[:, :t].reshape(b, t, n_heads, HEAD_DIM), va[:, :t].reshape(b, t, n_heads, HEAD_DIM)))


def kernel(x_prompt, x_sample, state_rwkv, state_shift, cache_att_k, cache_att_v, norm_mix_g, w_in, rwkv_mu_rkv, rwkv_mu_wag, rwkv_w0, rwkv_w1, rwkv_w2, rwkv_a0, rwkv_a1, rwkv_a2, rwkv_g1, rwkv_g2, rwkv_k_k, rwkv_k_a, rwkv_r_k, rwkv_ln_w, rwkv_ln_b, w_out, norm_ffn_g, router_group_w, router_group_b, router_expert_w, router_expert_b, expert_w1, expert_w3, expert_w2, norm_final_g):
    assert w_in.shape[0] == 1, "single-layer trunk"
    p = _prep_params(0, norm_mix_g, w_in, rwkv_mu_rkv, rwkv_mu_wag, rwkv_w0, rwkv_w1, rwkv_w2, rwkv_a0, rwkv_a1,
                     rwkv_a2, rwkv_g1, rwkv_g2, rwkv_k_k, rwkv_k_a, rwkv_r_k, rwkv_ln_w, rwkv_ln_b, w_out,
                     norm_ffn_g, router_group_w, router_group_b, router_expert_w, router_expert_b, expert_w1,
                     expert_w3, expert_w2, norm_final_g)
    moe_p, (rw_p, sh_p, kc_p, vc_p) = _prompt_mix(x_prompt, p)
    ctx_p = _moe_dispatch(*moe_p, p, PROJ_TILE, EXPERT_TILE)
    yb_p = _expert_call(ctx_p['visits'], ctx_p['xs'], p, EXPERT_TILE, 2 * ctx_p['h'].shape[0])
    y12_p = _gather_rows(yb_p, ctx_p['dest'].reshape(-1))
    moe_s, (rw_s, sh_s, kc_s, vc_s) = _sample_mix(x_sample, state_shift[0], state_rwkv[0], cache_att_k[0],
                                                   cache_att_v[0], p, after=yb_p)
    tm_s = moe_s[0].shape[0] // 2
    ctx_s = _moe_dispatch(*moe_s, p, tm_s, SAMPLE_EXPERT_TILE)
    yb_s = _expert_call(ctx_s['visits'], ctx_s['xs'], p, SAMPLE_EXPERT_TILE, 2 * ctx_s['h'].shape[0])
    h_p = ctx_p['h']
    y_p = _final_call(h_p, ctx_p['route'], y12_p.reshape(2, *h_p.shape), p['gfin'], PROJ_TILE).reshape(x_prompt.shape)
    y_s = _moe_combine(ctx_s, yb_s, p, tm_s)
    y_s = y_s.reshape(x_sample.shape[0], -1, x_sample.shape[2])[:, :x_sample.shape[1]]
    return (y_p, y_s, rw_p[None], sh_p[None], kc_p[None], vc_p[None], rw_s[None], sh_s[None], kc_s[None], vc_s[None])
```

```python
import functools
import math

import jax
import jax.numpy as jnp
from jax import lax
from jax.experimental import pallas as pl
from jax.experimental.pallas import tpu as pltpu
from jax.experimental.pallas import tpu_sc as plsc

F32 = jnp.float32
BF16 = jnp.bfloat16

HEAD_DIM = 64
GN_EPS = 64e-5
NORM_EPS = 1e-6
DILATED_CONFIGS = ((128, 1), (512, 4), (2048, 16))
N_GROUPS = 4
EXPERTS_PER_GROUP = 8
N_EXPERTS = N_GROUPS * EXPERTS_PER_GROUP
NEG_INF = -1e30

V7X_VMEM_LIMIT = 56 * 1024 * 1024
LANES = 128

PROJ_TILE = 512
RWKV_CHUNK = 64
ATT_BAND = 128
EXPERT_TILE = 512
ATT_UNROLL = {1: 8, 4: 8, 16: 4}
SC_GATHER_BYTES = 128 * 1024
SC_INDEX_LIMIT = 128
SLOT_MULTIPLE = 4096
SAMPLE_EXPERT_TILE = 128

NN = (((1,), (0,)), ((), ()))
NT = (((1,), (1,)), ((), ()))
TN = (((0,), (0,)), ((), ()))


def _dot(a, b):
    return jnp.dot(a, b, preferred_element_type=F32)


def _dot_split(a, b_bf16):
    hi = a.astype(BF16)
    lo = (a - hi.astype(F32)).astype(BF16)
    return _dot(hi, b_bf16) + _dot(lo, b_bf16)


def _mm(a, b, dims):
    return lax.dot_general(a.astype(BF16), b.astype(BF16), dims, preferred_element_type=F32)


def _round_up(x, k):
    return -(-x // k) * k


def _sigmoid(z):
    return 1.0 / (1.0 + jnp.exp(-z))


def _proj_kernel(x_ref, flag_ref, gmix_ref, win_ref, wdx_ref, mu_ref, w0a0_ref, w2a2_ref, g2_ref,
                 kk_ref, ka_ref, seg_ref,
                 r_o, lw_o, k_o, v_o, kkn_o, b_o, g_o, qa_o, kat_o, vat_o, xl_o, *rest,
                 tiles_per_seq, c_rwkv, keep_tiles):
    xn_carry, pj_carry = rest[-2:]
    i = pl.program_id(0)

    @pl.when(i % tiles_per_seq == 0)
    def _():
        xn_carry[...] = jnp.zeros_like(xn_carry)
        pj_carry[...] = jnp.zeros_like(pj_carry)

    c = c_rwkv
    x = x_ref[...]
    tm = x.shape[0]
    ms = jnp.mean(x * x, axis=-1, keepdims=True)
    xn = (x * lax.rsqrt(ms + NORM_EPS)) * gmix_ref[...]
    xn = jnp.where(flag_ref[...] > 0.0, x, xn)
    row = lax.broadcasted_iota(jnp.int32, (tm, 1), 0)
    xn_prev = jnp.where(row == 0, xn_carry[7:8, :], pltpu.roll(xn, 1, axis=0))
    dx = xn_prev - xn

    proj = _dot(xn.astype(BF16), win_ref[...])
    cur = proj[:, :3 * c]
    prev = jnp.where(row == 0, pj_carry[7:8, :], pltpu.roll(cur, 1, axis=0))
    xn_carry[...] = xn[tm - 8:, :]
    pj_carry[...] = cur[tm - 8:, :]
    xl_rows = xl_o.shape[1]
    xl_o[0] = xn[tm - xl_rows:, :]

    mu = mu_ref[...]
    r = cur[:, :c] + mu[0:1] * (prev[:, :c] - cur[:, :c])
    k = cur[:, c:2 * c] + mu[1:2] * (prev[:, c:2 * c] - cur[:, c:2 * c])
    v = cur[:, 2 * c:3 * c] + mu[2:3] * (prev[:, 2 * c:3 * c] - cur[:, 2 * c:3 * c])

    lr = proj[:, 6 * c:] + _dot(dx.astype(BF16), wdx_ref[...])
    lane = lax.broadcasted_iota(jnp.int32, (1, LANES), 1)
    wa_in = jnp.where(lane < 64, jnp.tanh(lr[:, :LANES]), lr[:, :LANES])
    wa = _dot(wa_in.astype(BF16), w2a2_ref[...]) + w0a0_ref[...]
    z = -wa[:, :c]
    softplus = jnp.maximum(z, 0.0) + jnp.log1p(jnp.exp(-jnp.abs(z)))
    lw = -jnp.exp(-softplus - 0.5)
    a = _sigmoid(wa[:, c:])
    g = _dot(_sigmoid(lr[:, LANES:]).astype(BF16), g2_ref[...])

    kk = k * kk_ref[...]
    sq = kk * kk
    gw = seg_ref.shape[0]
    ss = jnp.concatenate([_dot_split(sq[:, j:j + gw], seg_ref[...]) for j in range(0, c, gw)], axis=1)
    kk = kk * lax.rsqrt(jnp.maximum(ss, 1e-24))

    r_o[...] = r
    lw_o[...] = lw
    k_o[...] = k * (1.0 + (a - 1.0) * ka_ref[...])
    v_o[...] = v
    kkn_o[...] = kk
    b_o[...] = kk * a
    g_o[...] = g
    qa_o[...] = proj[:, 3 * c:4 * c]
    kat_o[...] = proj[:, 4 * c:5 * c]
    vat_o[...] = proj[:, 5 * c:6 * c]
    if keep_tiles:
        kt_o, vt_o = rest[:2]

        @pl.when(i % tiles_per_seq >= tiles_per_seq - keep_tiles)
        def _():
            kt_o[...] = proj[:, 4 * c:5 * c].T
            vt_o[...] = proj[:, 5 * c:6 * c].T


def _proj_call(x2, flag, p, tiles_per_seq, tm, xl_rows=8, keep_tiles=0):
    t, d = x2.shape
    c = p['c_rwkv']
    n_tiles = t // tm
    n_seq = n_tiles // tiles_per_seq
    first = tiles_per_seq - keep_tiles
    kept = pl.BlockSpec((None, c, tm), lambda i: (i // tiles_per_seq, 0, jnp.maximum(i % tiles_per_seq - first, 0)))
    kept_specs = [kept, kept] if keep_tiles else []
    kept_shapes = [jax.ShapeDtypeStruct((n_seq, c, keep_tiles * tm), F32)] * 2 if keep_tiles else []
    full = lambda a: pl.BlockSpec(a.shape, lambda i: (0,) * a.ndim, pipeline_mode=pl.Buffered(1))
    tok = lambda w: pl.BlockSpec((tm, w), lambda i: (i, 0))
    weights = [p['gmix'], p['win'], p['wdx'], p['mu_rkv'], p['w0a0'], p['w2a2'], p['g2'], p['k_k'], p['k_a'],
               p['seg'][:GROUP_LANES, :GROUP_LANES]]
    outs = pl.pallas_call(
        functools.partial(_proj_kernel, tiles_per_seq=tiles_per_seq, c_rwkv=c, keep_tiles=keep_tiles),
        grid=(n_tiles,),
        in_specs=[tok(d), tok(1)] + [full(w) for w in weights],
        out_specs=[tok(c)] * 10 + [pl.BlockSpec((1, xl_rows, d), lambda i: (i, 0, 0))] + kept_specs,
        out_shape=([jax.ShapeDtypeStruct((t, c), F32)] * 10 + [jax.ShapeDtypeStruct((n_tiles, xl_rows, d), F32)]
                   + kept_shapes),
        scratch_shapes=[pltpu.VMEM((8, d), F32), pltpu.VMEM((8, 3 * c), F32)],
        compiler_params=pltpu.CompilerParams(dimension_semantics=("arbitrary",), vmem_limit_bytes=V7X_VMEM_LIMIT),
        name="proj",
    )(x2, flag, *weights)
    return outs


GROUP_LANES = 256
GROUP_HEADS = GROUP_LANES // HEAD_DIM


def _rwkv_kernel(r_ref, lw_ref, k_ref, v_ref, kk_ref, b_ref, g_ref, s0_ref, rk_ref, lnw_ref, lnb_ref, seg_ref,
                 y_ref, sout_ref, s_scr):
    ci = pl.program_id(1)
    nb, L, c = r_ref.shape
    gw, gh, hd = GROUP_LANES, GROUP_HEADS, HEAD_DIM
    n_groups = c // gw

    lane_head = lax.broadcasted_iota(jnp.int32, (1, gw), 1) // hd
    head_masks = [lane_head == j for j in range(gh)]
    bd_state = (lax.broadcasted_iota(jnp.int32, (gw, gw), 0) // hd) == (lax.broadcasted_iota(jnp.int32, (gw, gw), 1) // hd)
    bd_time = (lax.broadcasted_iota(jnp.int32, (gh * L, gh * L), 0) // L) == (lax.broadcasted_iota(jnp.int32, (gh * L, gh * L), 1) // L)
    t_row = lax.broadcasted_iota(jnp.int32, (L, gh * L), 0)
    t_col = lax.broadcasted_iota(jnp.int32, (L, gh * L), 1) % L
    strict4 = t_row > t_col
    incl4 = t_row >= t_col
    incl = lax.broadcasted_iota(jnp.int32, (L, L), 0) >= lax.broadcasted_iota(jnp.int32, (L, L), 1)

    def stack(x):
        return jnp.concatenate([jnp.where(m, x, jnp.zeros_like(x)) for m in head_masks], axis=0)

    def block_diag(n):
        tiled = jnp.concatenate([n] * gh, axis=0)
        return jnp.where(bd_time, tiled, jnp.zeros_like(tiled))

    @pl.when(ci == 0)
    def _():
        for bi in range(nb):
            for gi in range(n_groups):
                s_in = s0_ref[bi, gi * gh:(gi + 1) * gh].reshape(gw, hd)
                s_scr[bi, gi] = jnp.where(bd_state, jnp.concatenate([s_in] * gh, axis=1), 0.0)

    n_apply = max(1, int(math.log2(L)))
    seg = seg_ref[...]
    pre = []
    for bi in range(nb):
        lw = lw_ref[bi]
        lw_hi = lw.astype(BF16)
        lw_r = lw - lw_hi.astype(F32)
        lw_mid = lw_r.astype(BF16)
        lw_lo = (lw_r - lw_mid.astype(F32)).astype(BF16)
        cs3 = _dot(incl.astype(BF16), jnp.concatenate([lw_hi, lw_mid, lw_lo], axis=1))
        cs = cs3[:, :c] + cs3[:, c:2 * c] + cs3[:, 2 * c:]
        cp = cs - lw
        cm = cs[L // 2 - 1:L // 2, :]
        c_last = cs[L - 1:L, :]
        r, k, v, kk, b = r_ref[bi], k_ref[bi], v_ref[bi], kk_ref[bi], b_ref[bi]
        e_dn = jnp.exp(cm - cs)
        e_l = jnp.exp(c_last - cs)
        pre.append(dict(v=v, rt=r * jnp.exp(cs - cm), kkt=kk * jnp.exp(cp - cm), bt=b * e_dn, kt=k * e_dn,
                        kg=kk * jnp.exp(cp), rg=r * jnp.exp(cs), bh=b * e_l, kh=k * e_l, g_last=jnp.exp(c_last),
                        rkk=r * k * rk_ref[...]))

    chains = [(bi, gi) for bi in range(nb) for gi in range(n_groups)]
    col = lambda bi, gi, name: pre[bi][name][:, gi * gw:(gi + 1) * gw]
    each = lambda fn: [fn(i, bi, gi) for i, (bi, gi) in enumerate(chains)]

    vg = each(lambda i, bi, gi: col(bi, gi, 'v'))
    v_st = each(lambda i, bi, gi: stack(vg[i].astype(BF16)))
    a_all = each(lambda i, bi, gi: _mm(
        jnp.concatenate([col(bi, gi, 'kkt'), col(bi, gi, 'rt')], axis=0),
        jnp.concatenate([stack(col(bi, gi, 'bt').astype(BF16)), stack(col(bi, gi, 'kt').astype(BF16))], axis=0),
        NT))
    p_ak = each(lambda i, bi, gi: jnp.where(strict4, a_all[i][:L, gh * L:], 0.0))
    p_rb = each(lambda i, bi, gi: jnp.where(incl4, a_all[i][L:, :gh * L], 0.0))
    p_rk = each(lambda i, bi, gi: jnp.where(incl4, a_all[i][L:, gh * L:], 0.0))
    eye4 = (t_row == t_col).astype(F32)
    nm = each(lambda i, bi, gi: -jnp.where(strict4, a_all[i][:L, :gh * L], 0.0))
    t_inv = [eye4 + n for n in nm]
    for it in range(n_apply - 1):
        lhs = nm if it == 0 else [jnp.concatenate([n, t], axis=0) for n, t in zip(nm, t_inv)]
        both = each(lambda i, bi, gi: _mm(lhs[i], block_diag(nm[i].astype(BF16)), NN))
        if it > 0:
            t_inv = [t + bo[L:] for t, bo in zip(t_inv, both)]
        nm = [bo[:L] for bo in both]
    t_inv = each(lambda i, bi, gi: t_inv[i] + _mm(t_inv[i], block_diag(nm[i].astype(BF16)), NN))
    av = each(lambda i, bi, gi: _mm(p_ak[i], v_st[i], NN))
    x = each(lambda i, bi, gi: _mm(
        t_inv[i], jnp.concatenate([stack(col(bi, gi, 'kg').astype(BF16)), stack(av[i].astype(BF16))], axis=1),
        NN))
    w_m = [xi[:, :gw] for xi in x]
    u0 = [-xi[:, gw:] for xi in x]
    rbw = each(lambda i, bi, gi: _mm(
        p_rb[i], jnp.concatenate([stack(w_m[i].astype(BF16)), stack(u0[i].astype(BF16))], axis=1),
        NN))
    rkv = each(lambda i, bi, gi: _mm(p_rk[i], v_st[i], NN))
    s_old = each(lambda i, bi, gi: s_scr[bi, gi])
    ws = each(lambda i, bi, gi: _mm(
        jnp.concatenate([w_m[i], col(bi, gi, 'rg') - rbw[i][:, :gw]], axis=0), s_old[i], NT))
    u = each(lambda i, bi, gi: u0[i] - ws[i][:L])
    y = each(lambda i, bi, gi: ws[i][L:] + rbw[i][:, gw:] + rkv[i])
    upd = each(lambda i, bi, gi: _mm(
        jnp.concatenate([u[i], vg[i]], axis=0),
        jnp.concatenate([col(bi, gi, 'bh'), col(bi, gi, 'kh')], axis=0), TN))
    for i, (bi, gi) in enumerate(chains):
        s_scr[bi, gi] = s_old[i] * col(bi, gi, 'g_last') + jnp.where(bd_state, upd[i], 0.0)

    inv = 1.0 / hd
    n_ch = len(chains)
    sums = _dot_split(jnp.concatenate(y + each(lambda i, bi, gi: col(bi, gi, 'rkk')), axis=0), seg)
    mean = [sums[i * L:(i + 1) * L] * inv for i in range(n_ch)]
    bonus = [sums[(n_ch + i) * L:(n_ch + i + 1) * L] * vg[i] for i in range(n_ch)]
    yc = [y[i] - mean[i] for i in range(n_ch)]
    sq = _dot_split(jnp.concatenate([z * z for z in yc], axis=0), seg)
    var = [sq[i * L:(i + 1) * L] * inv for i in range(n_ch)]
    for i, (bi, gi) in enumerate(chains):
        sl = slice(gi * gw, (gi + 1) * gw)
        yn = yc[i] * lax.rsqrt(var[i] + GN_EPS) * lnw_ref[:, sl] + lnb_ref[:, sl]
        y_ref[bi, :, sl] = (yn + bonus[i]) * g_ref[bi, :, sl]

    @pl.when(ci == pl.num_programs(1) - 1)
    def _():
        for bi in range(nb):
            for gi in range(n_groups):
                bd = s_scr[bi, gi]
                folded = bd[:, 0:hd]
                for j in range(1, gh):
                    folded = folded + bd[:, j * hd:(j + 1) * hd]
                sout_ref[bi, gi * gh:(gi + 1) * gh] = folded.reshape(gh, hd, hd)


def _rwkv_call(vecs, s0, p, n_seq, seq_len, chunk, nb):
    c = p['c_rwkv']
    n_heads = c // HEAD_DIM
    n_chunks = seq_len // chunk
    assert n_seq % nb == 0 and seq_len % chunk == 0
    vecs = [z.reshape(n_seq, seq_len, c) for z in vecs]
    tok = pl.BlockSpec((nb, chunk, c), lambda bi, ci: (bi, ci, 0))
    st = pl.BlockSpec((nb, n_heads, HEAD_DIM, HEAD_DIM), lambda bi, ci: (bi, 0, 0, 0))
    rowvec = pl.BlockSpec((1, c), lambda bi, ci: (0, 0))
    seg = p['seg'][:GROUP_LANES, :GROUP_LANES]
    y, s_out = pl.pallas_call(
        _rwkv_kernel,
        grid=(n_seq // nb, n_chunks),
        in_specs=[tok] * 7 + [st, rowvec, rowvec, rowvec, pl.BlockSpec(seg.shape, lambda bi, ci: (0, 0))],
        out_specs=[tok, st],
        out_shape=[jax.ShapeDtypeStruct((n_seq, seq_len, c), F32),
                   jax.ShapeDtypeStruct((n_seq, n_heads, HEAD_DIM, HEAD_DIM), F32)],
        scratch_shapes=[pltpu.VMEM((nb, c // GROUP_LANES, GROUP_LANES, GROUP_LANES), F32)],
        compiler_params=pltpu.CompilerParams(dimension_semantics=("arbitrary", "arbitrary"),
                                             vmem_limit_bytes=V7X_VMEM_LIMIT),
        name="rwkv",
    )(*vecs, s0, p['r_k'], p['ln_w'], p['ln_b'], seg)
    return y.reshape(n_seq * seq_len, c), s_out


def _attn_prompt_kernel(q_ref, k_ref, v_ref, o_ref, m_scr, l_scr, acc_scr):
    s_len = q_ref.shape[0]
    band = ATT_BAND
    n_blk = s_len // band

    lane = lax.broadcasted_iota(jnp.int32, (1, LANES), 1)
    head0 = lane < HEAD_DIM
    qi = lax.broadcasted_iota(jnp.int32, (band, 2 * band), 0)
    kj = lax.broadcasted_iota(jnp.int32, (band, 2 * band), 1)
    in_band = (kj >= qi) & (kj <= qi + band)
    in_band2 = jnp.concatenate([in_band, in_band], axis=0)
    kj2 = jnp.concatenate([kj, kj], axis=0)
    scale = HEAD_DIM ** -0.5
    ones = jnp.ones((2 * band, LANES), BF16)

    for ci, (window, dil) in enumerate(DILATED_CONFIGS):
        assert window // dil == band
        per_res = n_blk // dil

        unroll = ATT_UNROLL[dil]
        run = min(unroll, per_res)

        def body(it, carry, ci=ci, dil=dil, per_res=per_res, run=run, unroll=unroll):
            span = band * dil
            tile = lambda start: pl.ds(start, band, stride=dil) if dil > 1 else pl.ds(start, band)
            blocks = []
            tiles = []
            first_dyn = []
            for r in range(unroll // run):
                i0 = it * unroll + r * run
                blk0 = i0 % per_res
                start0 = i0 // per_res + blk0 * span
                base = len(tiles)
                if per_res > run:
                    tiles.append(tile(jnp.maximum(start0 - span, 0)))
                    first_dyn.append(blk0 == 0)
                else:
                    tiles.append(None)
                    first_dyn.append(None)
                for t in range(run):
                    tiles.append(tile(start0 + t * span))
                    blocks.append((tiles[-1], base + t, r if t == 0 else None))
            kt = [None if w is None else k_ref[w, :].astype(BF16) for w in tiles]
            vt = [None if w is None else v_ref[w, :].astype(BF16) for w in tiles]
            prev = lambda ts, i: ts[i + 1] if ts[i] is None else ts[i]
            q = [q_ref[rows, :] * scale for rows, _, _ in blocks]

            def mask_of(first):
                if first is None:
                    return in_band2
                if first_dyn[first] is None:
                    return in_band2 & (kj2 >= band)
                return in_band2 & (kj2 >= jnp.where(first_dyn[first], band, 0))

            s = [jnp.where(mask_of(first), lax.dot_general(
                jnp.concatenate([jnp.where(head0, qj, 0.0), jnp.where(head0, 0.0, qj)], axis=0).astype(BF16),
                jnp.concatenate([prev(kt, i), kt[i + 1]], axis=0), NT, preferred_element_type=F32), NEG_INF)
                 for qj, (_, i, first) in zip(q, blocks)]
            m = [jnp.max(z, axis=-1, keepdims=True) for z in s]
            p = [jnp.exp(z - mx).astype(BF16) for z, mx in zip(s, m)]
            o = [_dot(pj, jnp.concatenate([jnp.concatenate([prev(vt, i), vt[i + 1]], axis=0), ones], axis=1))
                 for pj, (_, i, _) in zip(p, blocks)]
            for j, (rows, _, _) in enumerate(blocks):
                m_scr[ci, rows, :] = jnp.where(head0, m[j][:band], m[j][band:])
                acc_scr[ci, rows, :] = jnp.where(head0, o[j][:band, :LANES], o[j][band:, :LANES])
                l_scr[ci, rows, :] = jnp.where(head0, o[j][:band, LANES:], o[j][band:, LANES:])
            return carry

        lax.fori_loop(0, n_blk // unroll, body, 0)

    rows_per = 256

    def merge(i, carry):
        rows = pl.ds(pl.multiple_of(i * rows_per, rows_per), rows_per)
        ms = [m_scr[ci, rows, :] for ci in range(len(DILATED_CONFIGS))]
        m_all = functools.reduce(jnp.maximum, ms)
        num = jnp.zeros((rows_per, LANES), F32)
        den = jnp.zeros((rows_per, LANES), F32)
        for ci, m_c in enumerate(ms):
            w_c = jnp.exp(m_c - m_all)
            num = num + w_c * acc_scr[ci, rows, :]
            den = den + w_c * l_scr[ci, rows, :]
        o_ref[rows, :] = num / den
        return carry

    lax.fori_loop(0, s_len // rows_per, merge, 0)


def _attn_prompt_call(q, k, v, n_seq, seq_len):
    c = q.shape[1]
    n_pairs = c // LANES
    blk = pl.BlockSpec((seq_len, LANES), lambda bi, hi: (bi, hi))
    return pl.pallas_call(
        _attn_prompt_kernel,
        grid=(n_seq, n_pairs),
        in_specs=[blk, blk, blk],
        out_specs=blk,
        out_shape=jax.ShapeDtypeStruct((n_seq * seq_len, c), F32),
        scratch_shapes=[pltpu.VMEM((len(DILATED_CONFIGS), seq_len, LANES), F32)] * 3,
        compiler_params=pltpu.CompilerParams(dimension_semantics=("arbitrary", "arbitrary"),
                                             vmem_limit_bytes=V7X_VMEM_LIMIT),
        name="attn_prompt",
    )(q, k, v)


def _attn_sample_kernel(q_ref, kn_ref, vn_ref, kc_ref, vc_ref, o_ref, *, n_new):
    hd = HEAD_DIM
    _, n_heads, _, n_buf = kc_ref.shape
    t_pad = kn_ref.shape[1]
    c = q_ref.shape[2]
    q = q_ref[0] * (hd ** -0.5)
    lane_head = lax.broadcasted_iota(jnp.int32, (1, c), 1) // hd
    qs = jnp.concatenate([jnp.where(lane_head == h, q, 0.0) for h in range(n_heads)], axis=0).astype(BF16)
    n_rows = n_heads * t_pad
    t_idx = lax.broadcasted_iota(jnp.int32, (n_rows, 1), 0) % t_pad

    def multiplicity(dist):
        mult = jnp.zeros(dist.shape, F32)
        for window, dil in DILATED_CONFIGS:
            hit = (dist >= 0) & (dist <= window) & (dist % dil == 0)
            mult = mult + jnp.where(hit, 1.0, 0.0)
        return mult

    jc = lax.broadcasted_iota(jnp.int32, (1, n_buf), 1)
    mult_c = multiplicity(n_buf + t_idx - jc)
    jn = lax.broadcasted_iota(jnp.int32, (1, t_pad), 1)
    mult_n = jnp.where(jn < n_new, multiplicity(t_idx - jn), 0.0)

    sc = jnp.concatenate(
        [_dot(qs[h * t_pad:(h + 1) * t_pad, h * hd:(h + 1) * hd], kc_ref[0, h].astype(BF16))
         for h in range(n_heads)], axis=0)
    sn = lax.dot_general(qs, kn_ref[0].astype(BF16), NT, preferred_element_type=F32)
    sc = jnp.where(mult_c > 0.0, sc, NEG_INF)
    sn = jnp.where(mult_n > 0.0, sn, NEG_INF)
    m = jnp.maximum(jnp.max(sc, axis=-1, keepdims=True), jnp.max(sn, axis=-1, keepdims=True))
    pc = (mult_c * jnp.exp(sc - m)).astype(BF16)
    pn = mult_n * jnp.exp(sn - m)
    inv_l = 1.0 / (jnp.sum(pc.astype(F32), axis=-1, keepdims=True) + jnp.sum(pn, axis=-1, keepdims=True))
    o_new = _dot(pn.astype(BF16), vn_ref[0].astype(BF16)) * inv_l
    out = jnp.zeros((t_pad, c), F32)
    for h in range(n_heads):
        out = out + jnp.where(lane_head == h, o_new[h * t_pad:(h + 1) * t_pad, :], 0.0)
    o_buf = [lax.dot_general(pc[h * t_pad:(h + 1) * t_pad, :], vc_ref[0, h].astype(BF16), NT,
                             preferred_element_type=F32) * inv_l[h * t_pad:(h + 1) * t_pad, :]
             for h in range(n_heads)]
    o_ref[0] = out + jnp.concatenate(o_buf, axis=1)


def _attn_sample_call(q, kn, vn, k_buf, v_buf, n_new):
    b, t_pad, c = q.shape
    _, n_buf, n_heads, hd = k_buf.shape
    k_t = jnp.transpose(k_buf, (0, 2, 3, 1))
    v_t = jnp.transpose(v_buf, (0, 2, 3, 1))
    new = pl.BlockSpec((1, t_pad, c), lambda bi: (bi, 0, 0))
    buf = pl.BlockSpec((1, n_heads, hd, n_buf), lambda bi: (bi, 0, 0, 0))
    return pl.pallas_call(
        functools.partial(_attn_sample_kernel, n_new=n_new),
        grid=(b,),
        in_specs=[new, new, new, buf, buf],
        out_specs=new,
        out_shape=jax.ShapeDtypeStruct((b, t_pad, c), F32),
        cost_estimate=pl.CostEstimate(flops=8 * b * t_pad * n_buf * c, transcendentals=b * n_heads * t_pad * n_buf,
                                      bytes_accessed=2 * 4 * b * n_buf * c + 4 * 4 * b * t_pad * c),
        compiler_params=pltpu.CompilerParams(dimension_semantics=("arbitrary",), vmem_limit_bytes=V7X_VMEM_LIMIT),
        name="attn_sample",
    )(q, kn, vn, k_t, v_t)


def _route_rows(logits, seen):
    lane = lax.broadcasted_iota(jnp.int32, logits.shape, 1)
    lane_f = lane.astype(F32)
    first = lambda hit: jnp.min(jnp.where(hit, lane_f, float(LANES)), axis=-1, keepdims=True)
    is_g = lane < N_GROUPS
    lg = jnp.where(is_g, logits, NEG_INF)
    g_max = jnp.max(lg, axis=-1, keepdims=True)
    g_idx = first(lg == g_max)
    g_w = 1.0 / jnp.sum(jnp.where(is_g, jnp.exp(lg - g_max), 0.0), axis=-1, keepdims=True)
    lo = N_GROUPS + EXPERTS_PER_GROUP * g_idx
    le = jnp.where((lane_f >= lo) & (lane_f < lo + EXPERTS_PER_GROUP), logits, NEG_INF)
    e1 = jnp.max(le, axis=-1, keepdims=True)
    i1 = first(le == e1)
    le2 = jnp.where(lane_f == i1, NEG_INF, le)
    e2 = jnp.max(le2, axis=-1, keepdims=True)
    i2 = first(le2 == e2)
    ex = jnp.exp(e2 - e1)
    gate1 = g_w / (1.0 + ex)
    gate2 = g_w * ex / (1.0 + ex)
    tm = logits.shape[0]
    pick1 = lane_f == i1
    pick2 = lane_f == i2
    picks = jnp.where(pick1 | pick2, 1.0, 0.0)
    earlier = (lax.broadcasted_iota(jnp.int32, (tm, tm), 0) > lax.broadcasted_iota(jnp.int32, (tm, tm), 1))
    before = seen + _dot(earlier.astype(BF16), picks.astype(BF16))
    rank1 = jnp.sum(jnp.where(pick1, before, 0.0), axis=-1, keepdims=True)
    rank2 = jnp.sum(jnp.where(pick2, before, 0.0), axis=-1, keepdims=True)
    out = jnp.where(lane == 0, gate1, jnp.where(lane == 1, gate2, 0.0))
    out = jnp.where(lane == 2, i1 - N_GROUPS, jnp.where(lane == 3, i2 - N_GROUPS, out))
    out = jnp.where(lane == 4, rank1, jnp.where(lane == 5, rank2, out))
    return out, jnp.sum(picks, axis=0, keepdims=True)


def _post_kernel(x_ref, yr_ref, ya_ref, wo_ref, gffn_ref, rw_cat_ref, rb_ref,
                 h_o, hn_o, lg_o, cnt_o, seen_scr):
    @pl.when(pl.program_id(0) == 0)
    def _():
        seen_scr[...] = jnp.zeros_like(seen_scr)

    c = yr_ref.shape[1]
    h = (x_ref[...] + _dot(yr_ref[...].astype(BF16), wo_ref[:c, :]) + _dot(ya_ref[...].astype(BF16), wo_ref[c:, :]))
    ms = jnp.mean(h * h, axis=-1, keepdims=True)
    hn = (h * lax.rsqrt(ms + NORM_EPS)) * gffn_ref[...]
    h_o[...] = h
    bits = pltpu.bitcast(hn.astype(BF16).astype(F32), jnp.uint32)
    half = hn.shape[1] // 2
    hn_o[...] = (bits[:, :half] >> 16) | (bits[:, half:] & jnp.uint32(0xFFFF0000))
    hi = hn.astype(BF16)
    lo = (hn - hi.astype(F32)).astype(BF16)
    both = _dot(hi, rw_cat_ref[...])
    logits = (both[:, :LANES] + both[:, LANES:] + _dot(lo, rw_cat_ref[:, :LANES])) + rb_ref[...]
    route, picked = _route_rows(logits, seen_scr[...])
    lg_o[...] = route
    seen_scr[...] = seen_scr[...] + picked
    cnt_o[...] = seen_scr[...]


def _post_call(x2, yr, ya, p, tm):
    t, d = x2.shape
    c = yr.shape[1]
    full = lambda a: pl.BlockSpec(a.shape, lambda i: (0,) * a.ndim)
    tok = lambda w: pl.BlockSpec((tm, w), lambda i: (i, 0))
    weights = [p['wout'], p['gffn'], p['rw_cat'], p['rb']]
    return pl.pallas_call(
        _post_kernel,
        grid=(t // tm,),
        in_specs=[tok(d), tok(c), tok(c)] + [full(w) for w in weights],
        out_specs=[tok(d), tok(d // 2), tok(LANES), pl.BlockSpec((1, LANES), lambda i: (0, 0))],
        out_shape=[jax.ShapeDtypeStruct((t, d), F32), jax.ShapeDtypeStruct((t, d // 2), jnp.uint32),
                   jax.ShapeDtypeStruct((t, LANES), F32), jax.ShapeDtypeStruct((1, LANES), F32)],
        scratch_shapes=[pltpu.VMEM((1, LANES), F32)],
        compiler_params=pltpu.CompilerParams(dimension_semantics=("arbitrary",), vmem_limit_bytes=V7X_VMEM_LIMIT),
        name="post",
    )(x2, yr, ya, *weights)


def _expert_kernel(ve_ref, vb_ref, lo_ref, hi_ref, nv_ref, xs_ref, w1_ref, w3_ref, w2_ref, y_ref, w1_s, w3_s, w2_s):
    i = pl.program_id(0)
    live = i < nv_ref[0]
    prev = jnp.maximum(i - 1, 0)

    @pl.when(live & ((i == 0) | (ve_ref[i] != ve_ref[prev])))
    def _():
        w1_s[...] = w1_ref[...].astype(BF16)
        w3_s[...] = w3_ref[...].astype(BF16)
        w2_s[...] = w2_ref[...].astype(BF16)

    @pl.when(live)
    def _():
        packed = xs_ref[...]
        half = packed.shape[1]
        lo = pltpu.bitcast(packed << 16, F32).astype(BF16)
        hi = pltpu.bitcast(packed & jnp.uint32(0xFFFF0000), F32).astype(BF16)
        h1 = _dot(lo, w1_s[:half, :]) + _dot(hi, w1_s[half:, :])
        h3 = _dot(lo, w3_s[:half, :]) + _dot(hi, w3_s[half:, :])
        act = (h1 * _sigmoid(h1)) * h3
        y = _dot(act.astype(BF16), w2_s[...])
        row = lax.broadcasted_iota(jnp.int32, (y.shape[0], 1), 0)
        mine = (row >= lo_ref[i]) & (row < hi_ref[i])
        revisit = (i > 0) & (vb_ref[i] == vb_ref[prev])

        @pl.when(revisit)
        def _():
            y_ref[...] = jnp.where(mine, y, y_ref[...])

        @pl.when(jnp.logical_not(revisit))
        def _():
            y_ref[...] = jnp.where(mine, y, 0.0)


def _expert_call(visits, xs, p, bm, n_rows):
    vis_exp, vis_blk, vis_lo, vis_hi, n_vis = visits
    _, d, de = p['w1'].shape
    grid_spec = pltpu.PrefetchScalarGridSpec(
        num_scalar_prefetch=5,
        grid=(vis_exp.shape[0],),
        in_specs=[pl.BlockSpec((bm, d // 2), lambda i, ve, vb, lo, hi, nv: (vb[i], 0)),
                  pl.BlockSpec((None, d, de), lambda i, ve, vb, lo, hi, nv: (ve[i], 0, 0)),
                  pl.BlockSpec((None, d, de), lambda i, ve, vb, lo, hi, nv: (ve[i], 0, 0)),
                  pl.BlockSpec((None, de, d), lambda i, ve, vb, lo, hi, nv: (ve[i], 0, 0))],
        out_specs=pl.BlockSpec((bm, d), lambda i, ve, vb, lo, hi, nv: (vb[i], 0)),
        scratch_shapes=[pltpu.VMEM((d, de), BF16), pltpu.VMEM((d, de), BF16), pltpu.VMEM((de, d), BF16)],
    )
    return pl.pallas_call(
        _expert_kernel,
        grid_spec=grid_spec,
        out_shape=jax.ShapeDtypeStruct((n_rows, d), F32),
        cost_estimate=pl.CostEstimate(flops=6 * n_rows * d * de, transcendentals=n_rows * de,
                                      bytes_accessed=2 * n_rows * d + 4 * n_rows * d + 3 * 4 * N_EXPERTS * d * de),
        compiler_params=pltpu.CompilerParams(dimension_semantics=("arbitrary",), vmem_limit_bytes=V7X_VMEM_LIMIT),
        name="experts",
    )(vis_exp, vis_blk, vis_lo, vis_hi, n_vis, xs, p['w1'], p['w3'], p['w2'])


def _final_kernel(h_ref, route_ref, y1_ref, y2_ref, gfin_ref, o_ref):
    route = route_ref[...]
    h = h_ref[...] + (route[:, 0:1] * y1_ref[...] + route[:, 1:2] * y2_ref[...])
    ms = jnp.mean(h * h, axis=-1, keepdims=True)
    o_ref[...] = (h * lax.rsqrt(ms + NORM_EPS)) * gfin_ref[...]


def _final_call(h, route, y12, gfin, tm):
    t, d = h.shape
    tok = pl.BlockSpec((tm, d), lambda i: (i, 0))
    routed = lambda a: pl.BlockSpec((None, tm, d), lambda i: (a, i, 0))
    return pl.pallas_call(
        _final_kernel,
        grid=(t // tm,),
        in_specs=[tok, pl.BlockSpec((tm, LANES), lambda i: (i, 0)), routed(0), routed(1),
                  pl.BlockSpec((1, d), lambda i: (0, 0))],
        out_specs=tok,
        out_shape=jax.ShapeDtypeStruct((t, d), F32),
        cost_estimate=pl.CostEstimate(flops=8 * t * d, transcendentals=t, bytes_accessed=4 * 4 * t * d),
        compiler_params=pltpu.CompilerParams(dimension_semantics=("arbitrary",), vmem_limit_bytes=V7X_VMEM_LIMIT),
        name="final",
    )(h, route, y12, y12, gfin)


def _gather_rows(table, idx):
    info = plsc.get_sparse_core_info()
    nc, ns = info.num_cores, info.num_subcores
    b, d = idx.shape[0], table.shape[1]
    chunk = min(SC_INDEX_LIMIT, SC_GATHER_BYTES // (d * table.dtype.itemsize))
    assert b % (nc * ns * chunk * 2) == 0, "rows must split evenly into chunk pairs per subcore"
    per_w = b // (nc * ns)
    n_chunks = per_w // chunk
    mesh = plsc.VectorSubcoreMesh(core_axis_name="c", subcore_axis_name="s")

    @functools.partial(
        pl.kernel, mesh=mesh, out_type=jax.ShapeDtypeStruct((b, d), table.dtype),
        cost_estimate=pl.CostEstimate(flops=0, transcendentals=0, bytes_accessed=2 * b * d * table.dtype.itemsize),
        scratch_types=[pltpu.VMEM((per_w,), jnp.int32), pltpu.VMEM((2, chunk, d), table.dtype),
                       pltpu.SemaphoreType.DMA((2,)), pltpu.SemaphoreType.DMA((2,))])
    def gather(table_hbm, idx_hbm, out_hbm, idx_v, rows_v, fetch_sem, put_sem):
        base = (lax.axis_index("s") * nc + lax.axis_index("c")) * per_w
        pltpu.sync_copy(idx_hbm.at[pl.ds(base, per_w)], idx_v)

        def fetch(c, slot):
            off = pl.multiple_of(c * chunk, chunk)
            return pltpu.make_async_copy(table_hbm.at[idx_v.at[pl.ds(off, chunk)]], rows_v.at[slot],
                                         fetch_sem.at[slot])

        def put(c, slot):
            off = pl.multiple_of(c * chunk, chunk)
            return pltpu.make_async_copy(rows_v.at[slot], out_hbm.at[pl.ds(base + off, chunk)], put_sem.at[slot])

        fetch(0, 0).start()

        @pl.loop(0, n_chunks, step=2)
        def _(c):
            @pl.when(c > 0)
            def _():
                put(c - 1, 1).wait()
            fetch(c + 1, 1).start()
            fetch(c, 0).wait()
            put(c, 0).start()
            fetch(c + 1, 1).wait()
            put(c, 0).wait()

            @pl.when(c + 2 < n_chunks)
            def _():
                fetch(c + 2, 0).start()
            put(c + 1, 1).start()

        put(n_chunks - 1, 1).wait()

    return gather(table, idx)


def _route(route, counts, bm):
    n = route.shape[0]
    eid = route[:, 2:4].astype(jnp.int32)
    rank = route[:, 4:6].astype(jnp.int32)
    m = 2 * n
    assert m % bm == 0
    experts = jnp.arange(N_EXPERTS, dtype=jnp.int32)
    counts = counts[0, N_GROUPS:N_GROUPS + N_EXPERTS].astype(jnp.int32)
    ends = jnp.cumsum(counts)
    starts = ends - counts
    lookup = lambda tbl, e: jnp.sum(jnp.where(e[..., None] == experts, tbl, 0), axis=-1)
    dest = lookup(starts, eid.T) + rank.T
    bits = max(1, (m - 1).bit_length())
    order = lax.sort(eid.reshape(-1) * (1 << bits) + jnp.arange(m, dtype=jnp.int32)) & ((1 << bits) - 1)
    pad = _round_up(m, SLOT_MULTIPLE) - m
    row_tok = jnp.concatenate([order // 2, jnp.arange(pad, dtype=jnp.int32) % n])
    blk_lo = starts // bm
    n_touch = jnp.where(counts > 0, (ends - 1) // bm - blk_lo + 1, 0)
    v_end = jnp.cumsum(n_touch)
    v = jnp.arange(m // bm + N_EXPERTS, dtype=jnp.int32)
    vis_exp = jnp.minimum(jnp.sum((v_end[None, :] <= v[:, None]).astype(jnp.int32), axis=1), N_EXPERTS - 1)
    vis_blk = jnp.minimum(lookup(blk_lo, vis_exp) + v - lookup(v_end - n_touch, vis_exp), m // bm - 1)
    vis_lo = jnp.clip(lookup(starts, vis_exp) - vis_blk * bm, 0, bm)
    vis_hi = jnp.clip(lookup(ends, vis_exp) - vis_blk * bm, 0, bm)
    n_vis = v_end[-1:].astype(jnp.int32)
    return row_tok, dest, (vis_exp.astype(jnp.int32), vis_blk.astype(jnp.int32), vis_lo.astype(jnp.int32),
                           vis_hi.astype(jnp.int32), n_vis)


def _moe_dispatch(x2, yr, ya, p, tm, bm):
    h, hn, route, counts = _post_call(x2, yr, ya, p, tm)
    row_tok, dest, visits = _route(route, counts, bm)
    return dict(h=h, route=route, dest=dest, visits=visits, xs=_gather_rows(hn, row_tok))


def _moe_combine(ctx, yb, p, tm):
    h = ctx['h']
    y12 = _gather_rows(yb, ctx['dest'].reshape(-1)).reshape(2, h.shape[0], h.shape[1])
    return _final_call(h, ctx['route'], y12, p['gfin'], tm)


def _prep_params(layer, norm_mix_g, w_in, rwkv_mu_rkv, rwkv_mu_wag, rwkv_w0, rwkv_w1, rwkv_w2, rwkv_a0, rwkv_a1,
                 rwkv_a2, rwkv_g1, rwkv_g2, rwkv_k_k, rwkv_k_a, rwkv_r_k, rwkv_ln_w, rwkv_ln_b, w_out, norm_ffn_g,
                 router_group_w, router_group_b, router_expert_w, router_expert_b, expert_w1, expert_w3, expert_w2,
                 norm_final_g):
    d = w_in.shape[1]
    c = rwkv_w0.shape[1]
    row = lambda a: a.reshape(1, -1).astype(F32)
    lowrank = jnp.concatenate([rwkv_w1[layer], rwkv_a1[layer], rwkv_g1[layer]], axis=1)
    mx = rwkv_mu_wag[layer]
    r_w = rwkv_w1.shape[2]
    r_a = rwkv_a1.shape[2]
    r_g = rwkv_g1.shape[2]
    assert r_w + r_a == LANES and r_g == LANES
    mx_cols = jnp.concatenate([jnp.broadcast_to(mx[0][:, None], (d, r_w)), jnp.broadcast_to(mx[1][:, None], (d, r_a)),
                               jnp.broadcast_to(mx[2][:, None], (d, r_g))], axis=1)
    w2a2 = jnp.zeros((LANES, 2 * c), F32)
    w2a2 = w2a2.at[:r_w, :c].set(rwkv_w2[layer]).at[r_w:, c:].set(rwkv_a2[layer])
    head = jnp.arange(c) // HEAD_DIM
    rw = jnp.zeros((d, LANES), F32)
    rw = rw.at[:, :N_GROUPS].set(router_group_w[layer]).at[:, N_GROUPS:N_GROUPS + N_EXPERTS].set(router_expert_w[layer])
    rw_hi = rw.astype(BF16)
    rb = jnp.zeros((1, LANES), F32)
    rb = rb.at[0, :N_GROUPS].set(router_group_b[layer]).at[0, N_GROUPS:N_GROUPS + N_EXPERTS].set(router_expert_b[layer])
    return {
        'c_rwkv': c,
        'gmix': row(norm_mix_g[layer]),
        'win': jnp.concatenate([w_in[layer], lowrank], axis=1).astype(BF16),
        'wdx': (mx_cols * lowrank).astype(BF16),
        'mu_rkv': rwkv_mu_rkv[layer],
        'w0a0': jnp.concatenate([row(rwkv_w0[layer]), row(rwkv_a0[layer])], axis=1),
        'w2a2': w2a2.astype(BF16),
        'g2': rwkv_g2[layer].astype(BF16),
        'k_k': row(rwkv_k_k[layer]),
        'k_a': row(rwkv_k_a[layer]),
        'seg': (head[:, None] == head[None, :]).astype(BF16),
        'r_k': row(rwkv_r_k[layer]),
        'ln_w': row(rwkv_ln_w[layer]),
        'ln_b': row(rwkv_ln_b[layer]),
        'wout': w_out[layer].astype(BF16),
        'gffn': row(norm_ffn_g[layer]),
        'rw_cat': jnp.concatenate([rw_hi, (rw - rw_hi.astype(F32)).astype(BF16)], axis=1),
        'rb': rb,
        'w1': expert_w1[layer],
        'w3': expert_w3[layer],
        'w2': expert_w2[layer],
        'gfin': row(norm_final_g),
    }


def _prompt_mix(x, p):
    b, s, d = x.shape
    c = p['c_rwkv']
    x2 = x.reshape(b * s, d)
    tm = PROJ_TILE
    flag = jnp.zeros((b * s, 1), F32)
    keep = min(max(w for w, _ in DILATED_CONFIGS), s)
    assert keep % tm == 0
    r, lw, k, v, kk, bb, g, qa, ka, va, xl, kt, vt = _proj_call(x2, flag, p, s // tm, tm, keep_tiles=keep // tm)
    s0 = jnp.zeros((b, c // HEAD_DIM, HEAD_DIM, HEAD_DIM), F32)
    yr, s_new = _rwkv_call((r, lw, k, v, kk, bb, g), s0, p, b, s, RWKV_CHUNK, 8)
    ya = _attn_prompt_call(qa, ka, va, b, s)
    shift = xl.reshape(b, s // tm, 8, d)[:, -1, 7, :]
    to_cache = lambda z: jnp.transpose(z.reshape(b, c // HEAD_DIM, HEAD_DIM, keep), (0, 3, 1, 2))
    k_keep, v_keep = to_cache(kt), to_cache(vt)
    return (x2, yr, ya), (s_new, shift, k_keep, v_keep)


def _sample_mix(x, shift0, s0, k_buf, v_buf, p, after):
    b, t, d = x.shape
    c = p['c_rwkv']
    n_heads = c // HEAD_DIM
    t_pad = 8
    xc = jnp.concatenate([shift0[:, None, :], x, jnp.zeros((b, t_pad - 1 - t, d), x.dtype)], axis=1)
    flag = jnp.zeros((b, t_pad, 1), F32).at[:, 0].set(1.0)
    outs = _proj_call(xc.reshape(b * t_pad, d), flag.reshape(b * t_pad, 1), p, 1, b * t_pad, xl_rows=b * t_pad)
    xl = outs[10]
    live = (jnp.arange(t_pad) < t)[None, :, None]
    shifted = [jnp.where(live, jnp.roll(o.reshape(b, t_pad, c), -1, axis=1), 0.0) for o in outs[:10]]
    r, lw, k, v, kk, bb, g, qa, ka, va = shifted
    flat = lambda z: z.reshape(b * t_pad, c)
    yr, s_new = _rwkv_call(tuple(flat(z) for z in (r, lw, k, v, kk, bb, g)), s0, p, b, t_pad, t_pad, 8)
    qa_held, _ = lax.optimization_barrier((qa, after))
    ya = _attn_sample_call(qa_held, ka, va, k_buf, v_buf, t)
    x_pad = jnp.concatenate([x, jnp.zeros((b, t_pad - t, d), x.dtype)], axis=1).reshape(b * t_pad, d)
    shift = xl.reshape(b, t_pad, d)[:, t]
    return ((x_pad, yr, flat(ya)),
            (s_new, shift, ka[:, :t].reshape(b, t, n_heads, HEAD_DIM), va[:, :t].reshape(b, t, n_heads, HEAD_DIM)))


def kernel(x_prompt, x_sample, state_rwkv, state_shift, cache_att_k, cache_att_v, norm_mix_g, w_in, rwkv_mu_rkv, rwkv_mu_wag, rwkv_w0, rwkv_w1, rwkv_w2, rwkv_a0, rwkv_a1, rwkv_a2, rwkv_g1, rwkv_g2, rwkv_k_k, rwkv_k_a, rwkv_r_k, rwkv_ln_w, rwkv_ln_b, w_out, norm_ffn_g, router_group_w, router_group_b, router_expert_w, router_expert_b, expert_w1, expert_w3, expert_w2, norm_final_g):
    assert w_in.shape[0] == 1, "single-layer trunk"
    p = _prep_params(0, norm_mix_g, w_in, rwkv_mu_rkv, rwkv_mu_wag, rwkv_w0, rwkv_w1, rwkv_w2, rwkv_a0, rwkv_a1,
                     rwkv_a2, rwkv_g1, rwkv_g2, rwkv_k_k, rwkv_k_a, rwkv_r_k, rwkv_ln_w, rwkv_ln_b, w_out,
                     norm_ffn_g, router_group_w, router_group_b, router_expert_w, router_expert_b, expert_w1,
                     expert_w3, expert_w2, norm_final_g)
    moe_p, (rw_p, sh_p, kc_p, vc_p) = _prompt_mix(x_prompt, p)
    ctx_p = _moe_dispatch(*moe_p, p, PROJ_TILE, EXPERT_TILE)
    yb_p = _expert_call(ctx_p['visits'], ctx_p['xs'], p, EXPERT_TILE, 2 * ctx_p['h'].shape[0])
    y12_p = _gather_rows(yb_p, ctx_p['dest'].reshape(-1))
    moe_s, (rw_s, sh_s, kc_s, vc_s) = _sample_mix(x_sample, state_shift[0], state_rwkv[0], cache_att_k[0],
                                                   cache_att_v[0], p, after=yb_p)
    tm_s = moe_s[0].shape[0] // 2
    ctx_s = _moe_dispatch(*moe_s, p, tm_s, SAMPLE_EXPERT_TILE)
    yb_s = _expert_call(ctx_s['visits'], ctx_s['xs'], p, SAMPLE_EXPERT_TILE, 2 * ctx_s['h'].shape[0])
    h_p = ctx_p['h']
    y_p = _final_call(h_p, ctx_p['route'], y12_p.reshape(2, *h_p.shape), p['gfin'], PROJ_TILE).reshape(x_prompt.shape)
    y_s = _moe_combine(ctx_s, yb_s, p, tm_s)
    y_s = y_s.reshape(x_sample.shape[0], -1, x_sample.shape[2])[:, :x_sample.shape[1]]
    return (y_p, y_s, rw_p[None], sh_p[None], kc_p[None], vc_p[None], rw_s[None], sh_s[None], kc_s[None], vc_s[None])
```

```python
import functools
import math

import jax
import jax.numpy as jnp
from jax import lax
from jax.experimental import pallas as pl
from jax.experimental.pallas import tpu as pltpu
from jax.experimental.pallas import tpu_sc as plsc

F32 = jnp.float32
BF16 = jnp.bfloat16

HEAD_DIM = 64
GN_EPS = 64e-5
NORM_EPS = 1e-6
DILATED_CONFIGS = ((128, 1), (512, 4), (2048, 16))
N_GROUPS = 4
EXPERTS_PER_GROUP = 8
N_EXPERTS = N_GROUPS * EXPERTS_PER_GROUP
NEG_INF = -1e30

V7X_VMEM_LIMIT = 56 * 1024 * 1024
LANES = 128

PROJ_TILE = 512
RWKV_CHUNK = 64
ATT_BAND = 128
EXPERT_TILE = 512
ATT_UNROLL = {1: 8, 4: 8, 16: 4}
SC_GATHER_BYTES = 128 * 1024
SC_INDEX_LIMIT = 128
SLOT_MULTIPLE = 4096
SAMPLE_EXPERT_TILE = 128

NN = (((1,), (0,)), ((), ()))
NT = (((1,), (1,)), ((), ()))
TN = (((0,), (0,)), ((), ()))


def _dot(a, b):
    return jnp.dot(a, b, preferred_element_type=F32)


def _dot_split(a, b_bf16):
    hi = a.astype(BF16)
    lo = (a - hi.astype(F32)).astype(BF16)
    return _dot(hi, b_bf16) + _dot(lo, b_bf16)


def _mm(a, b, dims):
    return lax.dot_general(a.astype(BF16), b.astype(BF16), dims, preferred_element_type=F32)


def _round_up(x, k):
    return -(-x // k) * k


def _sigmoid(z):
    return 1.0 / (1.0 + jnp.exp(-z))


def _proj_kernel(x_ref, flag_ref, gmix_ref, win_ref, wdx_ref, mu_ref, w0a0_ref, w2a2_ref, g2_ref,
                 kk_ref, ka_ref, seg_ref,
                 r_o, lw_o, k_o, v_o, kkn_o, b_o, g_o, qa_o, kat_o, vat_o, xl_o, *rest,
                 tiles_per_seq, c_rwkv, keep_tiles):
    xn_carry, pj_carry = rest[-2:]
    i = pl.program_id(0)

    @pl.when(i % tiles_per_seq == 0)
    def _():
        xn_carry[...] = jnp.zeros_like(xn_carry)
        pj_carry[...] = jnp.zeros_like(pj_carry)

    c = c_rwkv
    x = x_ref[...]
    tm = x.shape[0]
    ms = jnp.mean(x * x, axis=-1, keepdims=True)
    xn = (x * lax.rsqrt(ms + NORM_EPS)) * gmix_ref[...]
    xn = jnp.where(flag_ref[...] > 0.0, x, xn)
    row = lax.broadcasted_iota(jnp.int32, (tm, 1), 0)
    xn_prev = jnp.where(row == 0, xn_carry[7:8, :], pltpu.roll(xn, 1, axis=0))
    dx = xn_prev - xn

    proj = _dot(xn.astype(BF16), win_ref[...])
    cur = proj[:, :3 * c]
    prev = jnp.where(row == 0, pj_carry[7:8, :], pltpu.roll(cur, 1, axis=0))
    xn_carry[...] = xn[tm - 8:, :]
    pj_carry[...] = cur[tm - 8:, :]
    xl_rows = xl_o.shape[1]
    xl_o[0] = xn[tm - xl_rows:, :]

    mu = mu_ref[...]
    r = cur[:, :c] + mu[0:1] * (prev[:, :c] - cur[:, :c])
    k = cur[:, c:2 * c] + mu[1:2] * (prev[:, c:2 * c] - cur[:, c:2 * c])
    v = cur[:, 2 * c:3 * c] + mu[2:3] * (prev[:, 2 * c:3 * c] - cur[:, 2 * c:3 * c])

    lr = proj[:, 6 * c:] + _dot(dx.astype(BF16), wdx_ref[...])
    lane = lax.broadcasted_iota(jnp.int32, (1, LANES), 1)
    wa_in = jnp.where(lane < 64, jnp.tanh(lr[:, :LANES]), lr[:, :LANES])
    wa = _dot(wa_in.astype(BF16), w2a2_ref[...]) + w0a0_ref[...]
    z = -wa[:, :c]
    softplus = jnp.maximum(z, 0.0) + jnp.log1p(jnp.exp(-jnp.abs(z)))
    lw = -jnp.exp(-softplus - 0.5)
    a = _sigmoid(wa[:, c:])
    g = _dot(_sigmoid(lr[:, LANES:]).astype(BF16), g2_ref[...])

    kk = k * kk_ref[...]
    sq = kk * kk
    gw = seg_ref.shape[0]
    ss = jnp.concatenate([_dot_split(sq[:, j:j + gw], seg_ref[...]) for j in range(0, c, gw)], axis=1)
    kk = kk * lax.rsqrt(jnp.maximum(ss, 1e-24))

    r_o[...] = r
    lw_o[...] = lw
    k_o[...] = k * (1.0 + (a - 1.0) * ka_ref[...])
    v_o[...] = v
    kkn_o[...] = kk
    b_o[...] = kk * a
    g_o[...] = g
    qa_o[...] = proj[:, 3 * c:4 * c]
    kat_o[...] = proj[:, 4 * c:5 * c]
    vat_o[...] = proj[:, 5 * c:6 * c]
    if keep_tiles:
        kt_o, vt_o = rest[:2]

        @pl.when(i % tiles_per_seq >= tiles_per_seq - keep_tiles)
        def _():
            kt_o[...] = proj[:, 4 * c:5 * c].T
            vt_o[...] = proj[:, 5 * c:6 * c].T


def _proj_call(x2, flag, p, tiles_per_seq, tm, xl_rows=8, keep_tiles=0):
    t, d = x2.shape
    c = p['c_rwkv']
    n_tiles = t // tm
    n_seq = n_tiles // tiles_per_seq
    first = tiles_per_seq - keep_tiles
    kept = pl.BlockSpec((None, c, tm), lambda i: (i // tiles_per_seq, 0, jnp.maximum(i % tiles_per_seq - first, 0)))
    kept_specs = [kept, kept] if keep_tiles else []
    kept_shapes = [jax.ShapeDtypeStruct((n_seq, c, keep_tiles * tm), F32)] * 2 if keep_tiles else []
    full = lambda a: pl.BlockSpec(a.shape, lambda i: (0,) * a.ndim, pipeline_mode=pl.Buffered(1))
    tok = lambda w: pl.BlockSpec((tm, w), lambda i: (i, 0))
    weights = [p['gmix'], p['win'], p['wdx'], p['mu_rkv'], p['w0a0'], p['w2a2'], p['g2'], p['k_k'], p['k_a'],
               p['seg'][:GROUP_LANES, :GROUP_LANES]]
    outs = pl.pallas_call(
        functools.partial(_proj_kernel, tiles_per_seq=tiles_per_seq, c_rwkv=c, keep_tiles=keep_tiles),
        grid=(n_tiles,),
        in_specs=[tok(d), tok(1)] + [full(w) for w in weights],
        out_specs=[tok(c)] * 10 + [pl.BlockSpec((1, xl_rows, d), lambda i: (i, 0, 0))] + kept_specs,
        out_shape=([jax.ShapeDtypeStruct((t, c), F32)] * 10 + [jax.ShapeDtypeStruct((n_tiles, xl_rows, d), F32)]
                   + kept_shapes),
        scratch_shapes=[pltpu.VMEM((8, d), F32), pltpu.VMEM((8, 3 * c), F32)],
        compiler_params=pltpu.CompilerParams(dimension_semantics=("arbitrary",), vmem_limit_bytes=V7X_VMEM_LIMIT),
        name="proj",
    )(x2, flag, *weights)
    return outs


GROUP_LANES = 256
GROUP_HEADS = GROUP_LANES // HEAD_DIM


def _rwkv_kernel(r_ref, lw_ref, k_ref, v_ref, kk_ref, b_ref, g_ref, s0_ref, rk_ref, lnw_ref, lnb_ref, seg_ref,
                 y_ref, sout_ref, s_scr):
    ci = pl.program_id(1)
    nb, L, c = r_ref.shape
    gw, gh, hd = GROUP_LANES, GROUP_HEADS, HEAD_DIM
    n_groups = c // gw

    lane_head = lax.broadcasted_iota(jnp.int32, (1, gw), 1) // hd
    head_masks = [lane_head == j for j in range(gh)]
    bd_state = (lax.broadcasted_iota(jnp.int32, (gw, gw), 0) // hd) == (lax.broadcasted_iota(jnp.int32, (gw, gw), 1) // hd)
    bd_time = (lax.broadcasted_iota(jnp.int32, (gh * L, gh * L), 0) // L) == (lax.broadcasted_iota(jnp.int32, (gh * L, gh * L), 1) // L)
    t_row = lax.broadcasted_iota(jnp.int32, (L, gh * L), 0)
    t_col = lax.broadcasted_iota(jnp.int32, (L, gh * L), 1) % L
    strict4 = t_row > t_col
    incl4 = t_row >= t_col
    incl = lax.broadcasted_iota(jnp.int32, (L, L), 0) >= lax.broadcasted_iota(jnp.int32, (L, L), 1)

    def stack(x):
        return jnp.concatenate([jnp.where(m, x, jnp.zeros_like(x)) for m in head_masks], axis=0)

    def block_diag(n):
        tiled = jnp.concatenate([n] * gh, axis=0)
        return jnp.where(bd_time, tiled, jnp.zeros_like(tiled))

    @pl.when(ci == 0)
    def _():
        for bi in range(nb):
            for gi in range(n_groups):
                s_in = s0_ref[bi, gi * gh:(gi + 1) * gh].reshape(gw, hd)
                s_scr[bi, gi] = jnp.where(bd_state, jnp.concatenate([s_in] * gh, axis=1), 0.0)

    n_apply = max(1, int(math.log2(L)))
    seg = seg_ref[...]
    pre = []
    for bi in range(nb):
        lw = lw_ref[bi]
        lw_hi = lw.astype(BF16)
        lw_r = lw - lw_hi.astype(F32)
        lw_mid = lw_r.astype(BF16)
        lw_lo = (lw_r - lw_mid.astype(F32)).astype(BF16)
        cs3 = _dot(incl.astype(BF16), jnp.concatenate([lw_hi, lw_mid, lw_lo], axis=1))
        cs = cs3[:, :c] + cs3[:, c:2 * c] + cs3[:, 2 * c:]
        cp = cs - lw
        cm = cs[L // 2 - 1:L // 2, :]
        c_last = cs[L - 1:L, :]
        r, k, v, kk, b = r_ref[bi], k_ref[bi], v_ref[bi], kk_ref[bi], b_ref[bi]
        e_dn = jnp.exp(cm - cs)
        e_l = jnp.exp(c_last - cs)
        pre.append(dict(v=v, rt=r * jnp.exp(cs - cm), kkt=kk * jnp.exp(cp - cm), bt=b * e_dn, kt=k * e_dn,
                        kg=kk * jnp.exp(cp), rg=r * jnp.exp(cs), bh=b * e_l, kh=k * e_l, g_last=jnp.exp(c_last),
                        rkk=r * k * rk_ref[...]))

    chains = [(bi, gi) for bi in range(nb) for gi in range(n_groups)]
    col = lambda bi, gi, name: pre[bi][name][:, gi * gw:(gi + 1) * gw]
    each = lambda fn: [fn(i, bi, gi) for i, (bi, gi) in enumerate(chains)]

    vg = each(lambda i, bi, gi: col(bi, gi, 'v'))
    v_st = each(lambda i, bi, gi: stack(vg[i].astype(BF16)))
    a_all = each(lambda i, bi, gi: _mm(
        jnp.concatenate([col(bi, gi, 'kkt'), col(bi, gi, 'rt')], axis=0),
        jnp.concatenate([stack(col(bi, gi, 'bt').astype(BF16)), stack(col(bi, gi, 'kt').astype(BF16))], axis=0),
        NT))
    p_ak = each(lambda i, bi, gi: jnp.where(strict4, a_all[i][:L, gh * L:], 0.0))
    p_rb = each(lambda i, bi, gi: jnp.where(incl4, a_all[i][L:, :gh * L], 0.0))
    p_rk = each(lambda i, bi, gi: jnp.where(incl4, a_all[i][L:, gh * L:], 0.0))
    eye4 = (t_row == t_col).astype(F32)
    nm = each(lambda i, bi, gi: -jnp.where(strict4, a_all[i][:L, :gh * L], 0.0))
    t_inv = [eye4 + n for n in nm]
    for it in range(n_apply - 1):
        lhs = nm if it == 0 else [jnp.concatenate([n, t], axis=0) for n, t in zip(nm, t_inv)]
        both = each(lambda i, bi, gi: _mm(lhs[i], block_diag(nm[i].astype(BF16)), NN))
        if it > 0:
            t_inv = [t + bo[L:] for t, bo in zip(t_inv, both)]
        nm = [bo[:L] for bo in both]
    t_inv = each(lambda i, bi, gi: t_inv[i] + _mm(t_inv[i], block_diag(nm[i].astype(BF16)), NN))
    av = each(lambda i, bi, gi: _mm(p_ak[i], v_st[i], NN))
    x = each(lambda i, bi, gi: _mm(
        t_inv[i], jnp.concatenate([stack(col(bi, gi, 'kg').astype(BF16)), stack(av[i].astype(BF16))], axis=1),
        NN))
    w_m = [xi[:, :gw] for xi in x]
    u0 = [-xi[:, gw:] for xi in x]
    rbw = each(lambda i, bi, gi: _mm(
        p_rb[i], jnp.concatenate([stack(w_m[i].astype(BF16)), stack(u0[i].astype(BF16))], axis=1),
        NN))
    rkv = each(lambda i, bi, gi: _mm(p_rk[i], v_st[i], NN))
    s_old = each(lambda i, bi, gi: s_scr[bi, gi])
    ws = each(lambda i, bi, gi: _mm(
        jnp.concatenate([w_m[i], col(bi, gi, 'rg') - rbw[i][:, :gw]], axis=0), s_old[i], NT))
    u = each(lambda i, bi, gi: u0[i] - ws[i][:L])
    y = each(lambda i, bi, gi: ws[i][L:] + rbw[i][:, gw:] + rkv[i])
    upd = each(lambda i, bi, gi: _mm(
        jnp.concatenate([u[i], vg[i]], axis=0),
        jnp.concatenate([col(bi, gi, 'bh'), col(bi, gi, 'kh')], axis=0), TN))
    for i, (bi, gi) in enumerate(chains):
        s_scr[bi, gi] = s_old[i] * col(bi, gi, 'g_last') + jnp.where(bd_state, upd[i], 0.0)

    inv = 1.0 / hd
    n_ch = len(chains)
    sums = _dot_split(jnp.concatenate(y + each(lambda i, bi, gi: col(bi, gi, 'rkk')), axis=0), seg)
    mean = [sums[i * L:(i + 1) * L] * inv for i in range(n_ch)]
    bonus = [sums[(n_ch + i) * L:(n_ch + i + 1) * L] * vg[i] for i in range(n_ch)]
    yc = [y[i] - mean[i] for i in range(n_ch)]
    sq = _dot_split(jnp.concatenate([z * z for z in yc], axis=0), seg)
    var = [sq[i * L:(i + 1) * L] * inv for i in range(n_ch)]
    for i, (bi, gi) in enumerate(chains):
        sl = slice(gi * gw, (gi + 1) * gw)
        yn = yc[i] * lax.rsqrt(var[i] + GN_EPS) * lnw_ref[:, sl] + lnb_ref[:, sl]
        y_ref[bi, :, sl] = (yn + bonus[i]) * g_ref[bi, :, sl]

    @pl.when(ci == pl.num_programs(1) - 1)
    def _():
        for bi in range(nb):
            for gi in range(n_groups):
                bd = s_scr[bi, gi]
                folded = bd[:, 0:hd]
                for j in range(1, gh):
                    folded = folded + bd[:, j * hd:(j + 1) * hd]
                sout_ref[bi, gi * gh:(gi + 1) * gh] = folded.reshape(gh, hd, hd)


def _rwkv_call(vecs, s0, p, n_seq, seq_len, chunk, nb):
    c = p['c_rwkv']
    n_heads = c // HEAD_DIM
    n_chunks = seq_len // chunk
    assert n_seq % nb == 0 and seq_len % chunk == 0
    vecs = [z.reshape(n_seq, seq_len, c) for z in vecs]
    tok = pl.BlockSpec((nb, chunk, c), lambda bi, ci: (bi, ci, 0))
    st = pl.BlockSpec((nb, n_heads, HEAD_DIM, HEAD_DIM), lambda bi, ci: (bi, 0, 0, 0))
    rowvec = pl.BlockSpec((1, c), lambda bi, ci: (0, 0))
    seg = p['seg'][:GROUP_LANES, :GROUP_LANES]
    y, s_out = pl.pallas_call(
        _rwkv_kernel,
        grid=(n_seq // nb, n_chunks),
        in_specs=[tok] * 7 + [st, rowvec, rowvec, rowvec, pl.BlockSpec(seg.shape, lambda bi, ci: (0, 0))],
        out_specs=[tok, st],
        out_shape=[jax.ShapeDtypeStruct((n_seq, seq_len, c), F32),
                   jax.ShapeDtypeStruct((n_seq, n_heads, HEAD_DIM, HEAD_DIM), F32)],
        scratch_shapes=[pltpu.VMEM((nb, c // GROUP_LANES, GROUP_LANES, GROUP_LANES), F32)],
        compiler_params=pltpu.CompilerParams(dimension_semantics=("arbitrary", "arbitrary"),
                                             vmem_limit_bytes=V7X_VMEM_LIMIT),
        name="rwkv",
    )(*vecs, s0, p['r_k'], p['ln_w'], p['ln_b'], seg)
    return y.reshape(n_seq * seq_len, c), s_out


def _attn_prompt_kernel(q_ref, k_ref, v_ref, o_ref, m_scr, l_scr, acc_scr):
    s_len = q_ref.shape[0]
    band = ATT_BAND
    n_blk = s_len // band

    lane = lax.broadcasted_iota(jnp.int32, (1, LANES), 1)
    head0 = lane < HEAD_DIM
    qi = lax.broadcasted_iota(jnp.int32, (band, 2 * band), 0)
    kj = lax.broadcasted_iota(jnp.int32, (band, 2 * band), 1)
    in_band = (kj >= qi) & (kj <= qi + band)
    in_band2 = jnp.concatenate([in_band, in_band], axis=0)
    kj2 = jnp.concatenate([kj, kj], axis=0)
    scale = HEAD_DIM ** -0.5
    ones = jnp.ones((2 * band, LANES), BF16)

    for ci, (window, dil) in enumerate(DILATED_CONFIGS):
        assert window // dil == band
        per_res = n_blk // dil

        unroll = ATT_UNROLL[dil]
        run = min(unroll, per_res)

        def body(it, carry, ci=ci, dil=dil, per_res=per_res, run=run, unroll=unroll):
            span = band * dil
            tile = lambda start: pl.ds(start, band, stride=dil) if dil > 1 else pl.ds(start, band)
            blocks = []
            tiles = []
            first_dyn = []
            for r in range(unroll // run):
                i0 = it * unroll + r * run
                blk0 = i0 % per_res
                start0 = i0 // per_res + blk0 * span
                base = len(tiles)
                if per_res > run:
                    tiles.append(tile(jnp.maximum(start0 - span, 0)))
                    first_dyn.append(blk0 == 0)
                else:
                    tiles.append(None)
                    first_dyn.append(None)
                for t in range(run):
                    tiles.append(tile(start0 + t * span))
                    blocks.append((tiles[-1], base + t, r if t == 0 else None))
            kt = [None if w is None else k_ref[w, :].astype(BF16) for w in tiles]
            vt = [None if w is None else v_ref[w, :].astype(BF16) for w in tiles]
            prev = lambda ts, i: ts[i + 1] if ts[i] is None else ts[i]
            q = [q_ref[rows, :] * scale for rows, _, _ in blocks]

            def mask_of(first):
                if first is None:
                    return in_band2
                if first_dyn[first] is None:
                    return in_band2 & (kj2 >= band)
                return in_band2 & (kj2 >= jnp.where(first_dyn[first], band, 0))

            s = [jnp.where(mask_of(first), lax.dot_general(
                jnp.concatenate([jnp.where(head0, qj, 0.0), jnp.where(head0, 0.0, qj)], axis=0).astype(BF16),
                jnp.concatenate([prev(kt, i), kt[i + 1]], axis=0), NT, preferred_element_type=F32), NEG_INF)
                 for qj, (_, i, first) in zip(q, blocks)]
            m = [jnp.max(z, axis=-1, keepdims=True) for z in s]
            p = [jnp.exp(z - mx).astype(BF16) for z, mx in zip(s, m)]
            o = [_dot(pj, jnp.concatenate([jnp.concatenate([prev(vt, i), vt[i + 1]], axis=0), ones], axis=1))
                 for pj, (_, i, _) in zip(p, blocks)]
            for j, (rows, _, _) in enumerate(blocks):
                m_scr[ci, rows, :] = jnp.where(head0, m[j][:band], m[j][band:])
                acc_scr[ci, rows, :] = jnp.where(head0, o[j][:band, :LANES], o[j][band:, :LANES])
                l_scr[ci, rows, :] = jnp.where(head0, o[j][:band, LANES:], o[j][band:, LANES:])
            return carry

        lax.fori_loop(0, n_blk // unroll, body, 0)

    rows_per = 256

    def merge(i, carry):
        rows = pl.ds(pl.multiple_of(i * rows_per, rows_per), rows_per)
        ms = [m_scr[ci, rows, :] for ci in range(len(DILATED_CONFIGS))]
        m_all = functools.reduce(jnp.maximum, ms)
        num = jnp.zeros((rows_per, LANES), F32)
        den = jnp.zeros((rows_per, LANES), F32)
        for ci, m_c in enumerate(ms):
            w_c = jnp.exp(m_c - m_all)
            num = num + w_c * acc_scr[ci, rows, :]
            den = den + w_c * l_scr[ci, rows, :]
        o_ref[rows, :] = num / den
        return carry

    lax.fori_loop(0, s_len // rows_per, merge, 0)


def _attn_prompt_call(q, k, v, n_seq, seq_len):
    c = q.shape[1]
    n_pairs = c // LANES
    blk = pl.BlockSpec((seq_len, LANES), lambda bi, hi: (bi, hi))
    return pl.pallas_call(
        _attn_prompt_kernel,
        grid=(n_seq, n_pairs),
        in_specs=[blk, blk, blk],
        out_specs=blk,
        out_shape=jax.ShapeDtypeStruct((n_seq * seq_len, c), F32),
        scratch_shapes=[pltpu.VMEM((len(DILATED_CONFIGS), seq_len, LANES), F32)] * 3,
        compiler_params=pltpu.CompilerParams(dimension_semantics=("arbitrary", "arbitrary"),
                                             vmem_limit_bytes=V7X_VMEM_LIMIT),
        name="attn_prompt",
    )(q, k, v)


def _attn_sample_kernel(q_ref, kn_ref, vn_ref, kc_ref, vc_ref, o_ref, *, n_new):
    hd = HEAD_DIM
    _, n_heads, _, n_buf = kc_ref.shape
    t_pad = kn_ref.shape[1]
    c = q_ref.shape[2]
    q = q_ref[0] * (hd ** -0.5)
    lane_head = lax.broadcasted_iota(jnp.int32, (1, c), 1) // hd
    qs = jnp.concatenate([jnp.where(lane_head == h, q, 0.0) for h in range(n_heads)], axis=0).astype(BF16)
    n_rows = n_heads * t_pad
    t_idx = lax.broadcasted_iota(jnp.int32, (n_rows, 1), 0) % t_pad

    def multiplicity(dist):
        mult = jnp.zeros(dist.shape, F32)
        for window, dil in DILATED_CONFIGS:
            hit = (dist >= 0) & (dist <= window) & (dist % dil == 0)
            mult = mult + jnp.where(hit, 1.0, 0.0)
        return mult

    jc = lax.broadcasted_iota(jnp.int32, (1, n_buf), 1)
    mult_c = multiplicity(n_buf + t_idx - jc)
    jn = lax.broadcasted_iota(jnp.int32, (1, t_pad), 1)
    mult_n = jnp.where(jn < n_new, multiplicity(t_idx - jn), 0.0)

    sc = jnp.concatenate(
        [_dot(qs[h * t_pad:(h + 1) * t_pad, h * hd:(h + 1) * hd], kc_ref[0, h].astype(BF16))
         for h in range(n_heads)], axis=0)
    sn = lax.dot_general(qs, kn_ref[0].astype(BF16), NT, preferred_element_type=F32)
    sc = jnp.where(mult_c > 0.0, sc, NEG_INF)
    sn = jnp.where(mult_n > 0.0, sn, NEG_INF)
    m = jnp.maximum(jnp.max(sc, axis=-1, keepdims=True), jnp.max(sn, axis=-1, keepdims=True))
    pc = (mult_c * jnp.exp(sc - m)).astype(BF16)
    pn = mult_n * jnp.exp(sn - m)
    inv_l = 1.0 / (jnp.sum(pc.astype(F32), axis=-1, keepdims=True) + jnp.sum(pn, axis=-1, keepdims=True))
    o_new = _dot(pn.astype(BF16), vn_ref[0].astype(BF16)) * inv_l
    out = jnp.zeros((t_pad, c), F32)
    for h in range(n_heads):
        out = out + jnp.where(lane_head == h, o_new[h * t_pad:(h + 1) * t_pad, :], 0.0)
    o_buf = [lax.dot_general(pc[h * t_pad:(h + 1) * t_pad, :], vc_ref[0, h].astype(BF16), NT,
                             preferred_element_type=F32) * inv_l[h * t_pad:(h + 1) * t_pad, :]
             for h in range(n_heads)]
    o_ref[0] = out + jnp.concatenate(o_buf, axis=1)


def _attn_sample_call(q, kn, vn, k_buf, v_buf, n_new):
    b, t_pad, c = q.shape
    _, n_buf, n_heads, hd = k_buf.shape
    k_t = jnp.transpose(k_buf, (0, 2, 3, 1))
    v_t = jnp.transpose(v_buf, (0, 2, 3, 1))
    new = pl.BlockSpec((1, t_pad, c), lambda bi: (bi, 0, 0))
    buf = pl.BlockSpec((1, n_heads, hd, n_buf), lambda bi: (bi, 0, 0, 0))
    return pl.pallas_call(
        functools.partial(_attn_sample_kernel, n_new=n_new),
        grid=(b,),
        in_specs=[new, new, new, buf, buf],
        out_specs=new,
        out_shape=jax.ShapeDtypeStruct((b, t_pad, c), F32),
        compiler_params=pltpu.CompilerParams(dimension_semantics=("arbitrary",), vmem_limit_bytes=V7X_VMEM_LIMIT),
        name="attn_sample",
    )(q, kn, vn, k_t, v_t)


def _route_rows(logits, seen):
    lane = lax.broadcasted_iota(jnp.int32, logits.shape, 1)
    lane_f = lane.astype(F32)
    first = lambda hit: jnp.min(jnp.where(hit, lane_f, float(LANES)), axis=-1, keepdims=True)
    is_g = lane < N_GROUPS
    lg = jnp.where(is_g, logits, NEG_INF)
    g_max = jnp.max(lg, axis=-1, keepdims=True)
    g_idx = first(lg == g_max)
    g_w = 1.0 / jnp.sum(jnp.where(is_g, jnp.exp(lg - g_max), 0.0), axis=-1, keepdims=True)
    lo = N_GROUPS + EXPERTS_PER_GROUP * g_idx
    le = jnp.where((lane_f >= lo) & (lane_f < lo + EXPERTS_PER_GROUP), logits, NEG_INF)
    e1 = jnp.max(le, axis=-1, keepdims=True)
    i1 = first(le == e1)
    le2 = jnp.where(lane_f == i1, NEG_INF, le)
    e2 = jnp.max(le2, axis=-1, keepdims=True)
    i2 = first(le2 == e2)
    ex = jnp.exp(e2 - e1)
    gate1 = g_w / (1.0 + ex)
    gate2 = g_w * ex / (1.0 + ex)
    tm = logits.shape[0]
    pick1 = lane_f == i1
    pick2 = lane_f == i2
    picks = jnp.where(pick1 | pick2, 1.0, 0.0)
    earlier = (lax.broadcasted_iota(jnp.int32, (tm, tm), 0) > lax.broadcasted_iota(jnp.int32, (tm, tm), 1))
    before = seen + _dot(earlier.astype(BF16), picks.astype(BF16))
    rank1 = jnp.sum(jnp.where(pick1, before, 0.0), axis=-1, keepdims=True)
    rank2 = jnp.sum(jnp.where(pick2, before, 0.0), axis=-1, keepdims=True)
    out = jnp.where(lane == 0, gate1, jnp.where(lane == 1, gate2, 0.0))
    out = jnp.where(lane == 2, i1 - N_GROUPS, jnp.where(lane == 3, i2 - N_GROUPS, out))
    out = jnp.where(lane == 4, rank1, jnp.where(lane == 5, rank2, out))
    return out, jnp.sum(picks, axis=0, keepdims=True)


def _post_kernel(x_ref, yr_ref, ya_ref, wo_ref, gffn_ref, rw_cat_ref, rb_ref,
                 h_o, hn_o, lg_o, cnt_o, seen_scr):
    @pl.when(pl.program_id(0) == 0)
    def _():
        seen_scr[...] = jnp.zeros_like(seen_scr)

    c = yr_ref.shape[1]
    h = (x_ref[...] + _dot(yr_ref[...].astype(BF16), wo_ref[:c, :]) + _dot(ya_ref[...].astype(BF16), wo_ref[c:, :]))
    ms = jnp.mean(h * h, axis=-1, keepdims=True)
    hn = (h * lax.rsqrt(ms + NORM_EPS)) * gffn_ref[...]
    h_o[...] = h
    bits = pltpu.bitcast(hn.astype(BF16).astype(F32), jnp.uint32)
    half = hn.shape[1] // 2
    hn_o[...] = (bits[:, :half] >> 16) | (bits[:, half:] & jnp.uint32(0xFFFF0000))
    hi = hn.astype(BF16)
    lo = (hn - hi.astype(F32)).astype(BF16)
    both = _dot(hi, rw_cat_ref[...])
    logits = (both[:, :LANES] + both[:, LANES:] + _dot(lo, rw_cat_ref[:, :LANES])) + rb_ref[...]
    route, picked = _route_rows(logits, seen_scr[...])
    lg_o[...] = route
    seen_scr[...] = seen_scr[...] + picked
    cnt_o[...] = seen_scr[...]


def _post_call(x2, yr, ya, p, tm):
    t, d = x2.shape
    c = yr.shape[1]
    full = lambda a: pl.BlockSpec(a.shape, lambda i: (0,) * a.ndim)
    tok = lambda w: pl.BlockSpec((tm, w), lambda i: (i, 0))
    weights = [p['wout'], p['gffn'], p['rw_cat'], p['rb']]
    return pl.pallas_call(
        _post_kernel,
        grid=(t // tm,),
        in_specs=[tok(d), tok(c), tok(c)] + [full(w) for w in weights],
        out_specs=[tok(d), tok(d // 2), tok(LANES), pl.BlockSpec((1, LANES), lambda i: (0, 0))],
        out_shape=[jax.ShapeDtypeStruct((t, d), F32), jax.ShapeDtypeStruct((t, d // 2), jnp.uint32),
                   jax.ShapeDtypeStruct((t, LANES), F32), jax.ShapeDtypeStruct((1, LANES), F32)],
        scratch_shapes=[pltpu.VMEM((1, LANES), F32)],
        compiler_params=pltpu.CompilerParams(dimension_semantics=("arbitrary",), vmem_limit_bytes=V7X_VMEM_LIMIT),
        name="post",
    )(x2, yr, ya, *weights)


def _expert_kernel(ve_ref, vb_ref, lo_ref, hi_ref, nv_ref, xs_ref, w1_ref, w3_ref, w2_ref, y_ref, w1_s, w3_s, w2_s):
    i = pl.program_id(0)
    live = i < nv_ref[0]
    prev = jnp.maximum(i - 1, 0)

    @pl.when(live & ((i == 0) | (ve_ref[i] != ve_ref[prev])))
    def _():
        w1_s[...] = w1_ref[...].astype(BF16)
        w3_s[...] = w3_ref[...].astype(BF16)
        w2_s[...] = w2_ref[...].astype(BF16)

    @pl.when(live)
    def _():
        packed = xs_ref[...]
        half = packed.shape[1]
        lo = pltpu.bitcast(packed << 16, F32).astype(BF16)
        hi = pltpu.bitcast(packed & jnp.uint32(0xFFFF0000), F32).astype(BF16)
        h1 = _dot(lo, w1_s[:half, :]) + _dot(hi, w1_s[half:, :])
        h3 = _dot(lo, w3_s[:half, :]) + _dot(hi, w3_s[half:, :])
        act = (h1 * _sigmoid(h1)) * h3
        y = _dot(act.astype(BF16), w2_s[...])
        row = lax.broadcasted_iota(jnp.int32, (y.shape[0], 1), 0)
        mine = (row >= lo_ref[i]) & (row < hi_ref[i])
        revisit = (i > 0) & (vb_ref[i] == vb_ref[prev])

        @pl.when(revisit)
        def _():
            y_ref[...] = jnp.where(mine, y, y_ref[...])

        @pl.when(jnp.logical_not(revisit))
        def _():
            y_ref[...] = jnp.where(mine, y, 0.0)


def _expert_call(visits, xs, p, bm, n_rows):
    vis_exp, vis_blk, vis_lo, vis_hi, n_vis = visits
    _, d, de = p['w1'].shape
    grid_spec = pltpu.PrefetchScalarGridSpec(
        num_scalar_prefetch=5,
        grid=(vis_exp.shape[0],),
        in_specs=[pl.BlockSpec((bm, d // 2), lambda i, ve, vb, lo, hi, nv: (vb[i], 0)),
                  pl.BlockSpec((None, d, de), lambda i, ve, vb, lo, hi, nv: (ve[i], 0, 0)),
                  pl.BlockSpec((None, d, de), lambda i, ve, vb, lo, hi, nv: (ve[i], 0, 0)),
                  pl.BlockSpec((None, de, d), lambda i, ve, vb, lo, hi, nv: (ve[i], 0, 0))],
        out_specs=pl.BlockSpec((bm, d), lambda i, ve, vb, lo, hi, nv: (vb[i], 0)),
        scratch_shapes=[pltpu.VMEM((d, de), BF16), pltpu.VMEM((d, de), BF16), pltpu.VMEM((de, d), BF16)],
    )
    return pl.pallas_call(
        _expert_kernel,
        grid_spec=grid_spec,
        out_shape=jax.ShapeDtypeStruct((n_rows, d), F32),
        compiler_params=pltpu.CompilerParams(dimension_semantics=("arbitrary",), vmem_limit_bytes=V7X_VMEM_LIMIT),
        name="experts",
    )(vis_exp, vis_blk, vis_lo, vis_hi, n_vis, xs, p['w1'], p['w3'], p['w2'])


def _final_kernel(h_ref, route_ref, y1_ref, y2_ref, gfin_ref, o_ref):
    route = route_ref[...]
    h = h_ref[...] + (route[:, 0:1] * y1_ref[...] + route[:, 1:2] * y2_ref[...])
    ms = jnp.mean(h * h, axis=-1, keepdims=True)
    o_ref[...] = (h * lax.rsqrt(ms + NORM_EPS)) * gfin_ref[...]


def _final_call(h, route, y12, gfin, tm):
    t, d = h.shape
    tok = pl.BlockSpec((tm, d), lambda i: (i, 0))
    routed = lambda a: pl.BlockSpec((None, tm, d), lambda i: (a, i, 0))
    return pl.pallas_call(
        _final_kernel,
        grid=(t // tm,),
        in_specs=[tok, pl.BlockSpec((tm, LANES), lambda i: (i, 0)), routed(0), routed(1),
                  pl.BlockSpec((1, d), lambda i: (0, 0))],
        out_specs=tok,
        out_shape=jax.ShapeDtypeStruct((t, d), F32),
        compiler_params=pltpu.CompilerParams(dimension_semantics=("arbitrary",), vmem_limit_bytes=V7X_VMEM_LIMIT),
        name="final",
    )(h, route, y12, y12, gfin)


def _gather_rows(table, idx):
    info = plsc.get_sparse_core_info()
    nc, ns = info.num_cores, info.num_subcores
    b, d = idx.shape[0], table.shape[1]
    chunk = min(SC_INDEX_LIMIT, SC_GATHER_BYTES // (d * table.dtype.itemsize))
    assert b % (nc * ns * chunk * 2) == 0, "rows must split evenly into chunk pairs per subcore"
    per_w = b // (nc * ns)
    n_chunks = per_w // chunk
    mesh = plsc.VectorSubcoreMesh(core_axis_name="c", subcore_axis_name="s")

    @functools.partial(
        pl.kernel, mesh=mesh, out_type=jax.ShapeDtypeStruct((b, d), table.dtype),
        scratch_types=[pltpu.VMEM((per_w,), jnp.int32), pltpu.VMEM((2, chunk, d), table.dtype),
                       pltpu.SemaphoreType.DMA((2,)), pltpu.SemaphoreType.DMA((2,))])
    def gather(table_hbm, idx_hbm, out_hbm, idx_v, rows_v, fetch_sem, put_sem):
        base = (lax.axis_index("s") * nc + lax.axis_index("c")) * per_w
        pltpu.sync_copy(idx_hbm.at[pl.ds(base, per_w)], idx_v)

        def fetch(c, slot):
            off = pl.multiple_of(c * chunk, chunk)
            return pltpu.make_async_copy(table_hbm.at[idx_v.at[pl.ds(off, chunk)]], rows_v.at[slot],
                                         fetch_sem.at[slot])

        def put(c, slot):
            off = pl.multiple_of(c * chunk, chunk)
            return pltpu.make_async_copy(rows_v.at[slot], out_hbm.at[pl.ds(base + off, chunk)], put_sem.at[slot])

        fetch(0, 0).start()

        @pl.loop(0, n_chunks, step=2)
        def _(c):
            @pl.when(c > 0)
            def _():
                put(c - 1, 1).wait()
            fetch(c + 1, 1).start()
            fetch(c, 0).wait()
            put(c, 0).start()
            fetch(c + 1, 1).wait()
            put(c, 0).wait()

            @pl.when(c + 2 < n_chunks)
            def _():
                fetch(c + 2, 0).start()
            put(c + 1, 1).start()

        put(n_chunks - 1, 1).wait()

    return gather(table, idx)


def _scatter_rows(rows, dest, n_out):
    n, d = rows.shape
    window = SC_INDEX_LIMIT
    halves = rows.reshape(2 * n, d // 2)
    idx = (2 * dest[:, :, None] + jnp.arange(2, dtype=jnp.int32)).reshape(2, 1, 2 * n)
    assert (2 * n) % window == 0
    mesh = plsc.VectorSubcoreMesh(core_axis_name="c", subcore_axis_name="s")

    @functools.partial(pl.kernel, mesh=mesh, out_type=jax.ShapeDtypeStruct((2 * n_out, d // 2), rows.dtype))
    def scatter(rows_hbm, idx0_hbm, idx1_hbm, out_hbm):
        def body(rows_v, idx0_v, idx1_v):
            pltpu.sync_copy(rows_v, out_hbm.at[idx0_v.at[0]])
            pltpu.sync_copy(rows_v, out_hbm.at[idx1_v.at[0]])

        idx_spec = pl.BlockSpec((1, window), lambda i: (0, i))
        pltpu.emit_pipeline(
            body, grid=(2 * n // window,),
            in_specs=[pl.BlockSpec((window, d // 2), lambda i: (i, 0)), idx_spec, idx_spec],
            out_specs=[], core_axis_name=("c", "s"), dimension_semantics=(pltpu.PARALLEL,),
        )(rows_hbm, idx0_hbm, idx1_hbm)

    return scatter(halves, idx[0], idx[1]).reshape(n_out, d)


def _route(route, counts, bm, by_scatter):
    n = route.shape[0]
    eid = route[:, 2:4].astype(jnp.int32)
    rank = route[:, 4:6].astype(jnp.int32)
    m = 2 * n
    assert m % bm == 0
    experts = jnp.arange(N_EXPERTS, dtype=jnp.int32)
    counts = counts[0, N_GROUPS:N_GROUPS + N_EXPERTS].astype(jnp.int32)
    ends = jnp.cumsum(counts)
    starts = ends - counts
    lookup = lambda tbl, e: jnp.sum(jnp.where(e[..., None] == experts, tbl, 0), axis=-1)
    dest = lookup(starts, eid.T) + rank.T
    row_tok = None
    if not by_scatter:
        bits = max(1, (m - 1).bit_length())
        order = lax.sort(eid.reshape(-1) * (1 << bits) + jnp.arange(m, dtype=jnp.int32)) & ((1 << bits) - 1)
        pad = _round_up(m, SLOT_MULTIPLE) - m
        row_tok = jnp.concatenate([order // 2, jnp.arange(pad, dtype=jnp.int32) % n])
    blk_lo = starts // bm
    n_touch = jnp.where(counts > 0, (ends - 1) // bm - blk_lo + 1, 0)
    v_end = jnp.cumsum(n_touch)
    v = jnp.arange(m // bm + N_EXPERTS, dtype=jnp.int32)
    vis_exp = jnp.minimum(jnp.sum((v_end[None, :] <= v[:, None]).astype(jnp.int32), axis=1), N_EXPERTS - 1)
    vis_blk = jnp.minimum(lookup(blk_lo, vis_exp) + v - lookup(v_end - n_touch, vis_exp), m // bm - 1)
    vis_lo = jnp.clip(lookup(starts, vis_exp) - vis_blk * bm, 0, bm)
    vis_hi = jnp.clip(lookup(ends, vis_exp) - vis_blk * bm, 0, bm)
    n_vis = v_end[-1:].astype(jnp.int32)
    return row_tok, dest, (vis_exp.astype(jnp.int32), vis_blk.astype(jnp.int32), vis_lo.astype(jnp.int32),
                           vis_hi.astype(jnp.int32), n_vis)


def _moe_dispatch(x2, yr, ya, p, tm, bm, by_scatter):
    h, hn, route, counts = _post_call(x2, yr, ya, p, tm)
    row_tok, dest, visits = _route(route, counts, bm, by_scatter)
    xs = _scatter_rows(hn, dest, 2 * hn.shape[0]) if by_scatter else _gather_rows(hn, row_tok)
    return dict(h=h, route=route, dest=dest, visits=visits, xs=xs)


def _moe_combine(ctx, yb, p, tm):
    h = ctx['h']
    y12 = _gather_rows(yb, ctx['dest'].reshape(-1)).reshape(2, h.shape[0], h.shape[1])
    return _final_call(h, ctx['route'], y12, p['gfin'], tm)


def _prep_params(layer, norm_mix_g, w_in, rwkv_mu_rkv, rwkv_mu_wag, rwkv_w0, rwkv_w1, rwkv_w2, rwkv_a0, rwkv_a1,
                 rwkv_a2, rwkv_g1, rwkv_g2, rwkv_k_k, rwkv_k_a, rwkv_r_k, rwkv_ln_w, rwkv_ln_b, w_out, norm_ffn_g,
                 router_group_w, router_group_b, router_expert_w, router_expert_b, expert_w1, expert_w3, expert_w2,
                 norm_final_g):
    d = w_in.shape[1]
    c = rwkv_w0.shape[1]
    row = lambda a: a.reshape(1, -1).astype(F32)
    lowrank = jnp.concatenate([rwkv_w1[layer], rwkv_a1[layer], rwkv_g1[layer]], axis=1)
    mx = rwkv_mu_wag[layer]
    r_w = rwkv_w1.shape[2]
    r_a = rwkv_a1.shape[2]
    r_g = rwkv_g1.shape[2]
    assert r_w + r_a == LANES and r_g == LANES
    mx_cols = jnp.concatenate([jnp.broadcast_to(mx[0][:, None], (d, r_w)), jnp.broadcast_to(mx[1][:, None], (d, r_a)),
                               jnp.broadcast_to(mx[2][:, None], (d, r_g))], axis=1)
    w2a2 = jnp.zeros((LANES, 2 * c), F32)
    w2a2 = w2a2.at[:r_w, :c].set(rwkv_w2[layer]).at[r_w:, c:].set(rwkv_a2[layer])
    head = jnp.arange(c) // HEAD_DIM
    rw = jnp.zeros((d, LANES), F32)
    rw = rw.at[:, :N_GROUPS].set(router_group_w[layer]).at[:, N_GROUPS:N_GROUPS + N_EXPERTS].set(router_expert_w[layer])
    rw_hi = rw.astype(BF16)
    rb = jnp.zeros((1, LANES), F32)
    rb = rb.at[0, :N_GROUPS].set(router_group_b[layer]).at[0, N_GROUPS:N_GROUPS + N_EXPERTS].set(router_expert_b[layer])
    return {
        'c_rwkv': c,
        'gmix': row(norm_mix_g[layer]),
        'win': jnp.concatenate([w_in[layer], lowrank], axis=1).astype(BF16),
        'wdx': (mx_cols * lowrank).astype(BF16),
        'mu_rkv': rwkv_mu_rkv[layer],
        'w0a0': jnp.concatenate([row(rwkv_w0[layer]), row(rwkv_a0[layer])], axis=1),
        'w2a2': w2a2.astype(BF16),
        'g2': rwkv_g2[layer].astype(BF16),
        'k_k': row(rwkv_k_k[layer]),
        'k_a': row(rwkv_k_a[layer]),
        'seg': (head[:, None] == head[None, :]).astype(BF16),
        'r_k': row(rwkv_r_k[layer]),
        'ln_w': row(rwkv_ln_w[layer]),
        'ln_b': row(rwkv_ln_b[layer]),
        'wout': w_out[layer].astype(BF16),
        'gffn': row(norm_ffn_g[layer]),
        'rw_cat': jnp.concatenate([rw_hi, (rw - rw_hi.astype(F32)).astype(BF16)], axis=1),
        'rb': rb,
        'w1': expert_w1[layer],
        'w3': expert_w3[layer],
        'w2': expert_w2[layer],
        'gfin': row(norm_final_g),
    }


def _prompt_mix(x, p):
    b, s, d = x.shape
    c = p['c_rwkv']
    x2 = x.reshape(b * s, d)
    tm = PROJ_TILE
    flag = jnp.zeros((b * s, 1), F32)
    keep = min(max(w for w, _ in DILATED_CONFIGS), s)
    assert keep % tm == 0
    r, lw, k, v, kk, bb, g, qa, ka, va, xl, kt, vt = _proj_call(x2, flag, p, s // tm, tm, keep_tiles=keep // tm)
    s0 = jnp.zeros((b, c // HEAD_DIM, HEAD_DIM, HEAD_DIM), F32)
    yr, s_new = _rwkv_call((r, lw, k, v, kk, bb, g), s0, p, b, s, RWKV_CHUNK, 8)
    ya = _attn_prompt_call(qa, ka, va, b, s)
    shift = xl.reshape(b, s // tm, 8, d)[:, -1, 7, :]
    to_cache = lambda z: jnp.transpose(z.reshape(b, c // HEAD_DIM, HEAD_DIM, keep), (0, 3, 1, 2))
    k_keep, v_keep = to_cache(kt), to_cache(vt)
    return (x2, yr, ya), (s_new, shift, k_keep, v_keep)


def _sample_mix(x, shift0, s0, k_buf, v_buf, p, after):
    b, t, d = x.shape
    c = p['c_rwkv']
    n_heads = c // HEAD_DIM
    t_pad = 8
    xc = jnp.concatenate([shift0[:, None, :], x, jnp.zeros((b, t_pad - 1 - t, d), x.dtype)], axis=1)
    flag = jnp.zeros((b, t_pad, 1), F32).at[:, 0].set(1.0)
    outs = _proj_call(xc.reshape(b * t_pad, d), flag.reshape(b * t_pad, 1), p, 1, b * t_pad, xl_rows=b * t_pad)
    xl = outs[10]
    live = (jnp.arange(t_pad) < t)[None, :, None]
    shifted = [jnp.where(live, jnp.roll(o.reshape(b, t_pad, c), -1, axis=1), 0.0) for o in outs[:10]]
    r, lw, k, v, kk, bb, g, qa, ka, va = shifted
    flat = lambda z: z.reshape(b * t_pad, c)
    yr, s_new = _rwkv_call(tuple(flat(z) for z in (r, lw, k, v, kk, bb, g)), s0, p, b, t_pad, t_pad, 8)
    qa_held, _ = lax.optimization_barrier((qa, after))
    ya = _attn_sample_call(qa_held, ka, va, k_buf, v_buf, t)
    x_pad = jnp.concatenate([x, jnp.zeros((b, t_pad - t, d), x.dtype)], axis=1).reshape(b * t_pad, d)
    shift = xl.reshape(b, t_pad, d)[:, t]
    return ((x_pad, yr, flat(ya)),
            (s_new, shift, ka[:, :t].reshape(b, t, n_heads, HEAD_DIM), va[:, :t].reshape(b, t, n_heads, HEAD_DIM)))


def kernel(x_prompt, x_sample, state_rwkv, state_shift, cache_att_k, cache_att_v, norm_mix_g, w_in, rwkv_mu_rkv, rwkv_mu_wag, rwkv_w0, rwkv_w1, rwkv_w2, rwkv_a0, rwkv_a1, rwkv_a2, rwkv_g1, rwkv_g2, rwkv_k_k, rwkv_k_a, rwkv_r_k, rwkv_ln_w, rwkv_ln_b, w_out, norm_ffn_g, router_group_w, router_group_b, router_expert_w, router_expert_b, expert_w1, expert_w3, expert_w2, norm_final_g):
    assert w_in.shape[0] == 1, "single-layer trunk"
    p = _prep_params(0, norm_mix_g, w_in, rwkv_mu_rkv, rwkv_mu_wag, rwkv_w0, rwkv_w1, rwkv_w2, rwkv_a0, rwkv_a1,
                     rwkv_a2, rwkv_g1, rwkv_g2, rwkv_k_k, rwkv_k_a, rwkv_r_k, rwkv_ln_w, rwkv_ln_b, w_out,
                     norm_ffn_g, router_group_w, router_group_b, router_expert_w, router_expert_b, expert_w1,
                     expert_w3, expert_w2, norm_final_g)
    moe_p, (rw_p, sh_p, kc_p, vc_p) = _prompt_mix(x_prompt, p)
    ctx_p = _moe_dispatch(*moe_p, p, PROJ_TILE, EXPERT_TILE, True)
    yb_p = _expert_call(ctx_p['visits'], ctx_p['xs'], p, EXPERT_TILE, 2 * ctx_p['h'].shape[0])
    y12_p = _gather_rows(yb_p, ctx_p['dest'].reshape(-1))
    moe_s, (rw_s, sh_s, kc_s, vc_s) = _sample_mix(x_sample, state_shift[0], state_rwkv[0], cache_att_k[0],
                                                   cache_att_v[0], p, after=yb_p)
    tm_s = moe_s[0].shape[0] // 2
    ctx_s = _moe_dispatch(*moe_s, p, tm_s, SAMPLE_EXPERT_TILE, False)
    yb_s = _expert_call(ctx_s['visits'], ctx_s['xs'], p, SAMPLE_EXPERT_TILE, 2 * ctx_s['h'].shape[0])
    h_p = ctx_p['h']
    y_p = _final_call(h_p, ctx_p['route'], y12_p.reshape(2, *h_p.shape), p['gfin'], PROJ_TILE).reshape(x_prompt.shape)
    y_s = _moe_combine(ctx_s, yb_s, p, tm_s)
    y_s = y_s.reshape(x_sample.shape[0], -1, x_sample.shape[2])[:, :x_sample.shape[1]]
    return (y_p, y_s, rw_p[None], sh_p[None], kc_p[None], vc_p[None], rw_s[None], sh_s[None], kc_s[None], vc_s[None])
```

```python
import functools
import math

import jax
import jax.numpy as jnp
from jax import lax
from jax.experimental import pallas as pl
from jax.experimental.pallas import tpu as pltpu
from jax.experimental.pallas import tpu_sc as plsc

F32 = jnp.float32
BF16 = jnp.bfloat16

HEAD_DIM = 64
GN_EPS = 64e-5
NORM_EPS = 1e-6
DILATED_CONFIGS = ((128, 1), (512, 4), (2048, 16))
N_GROUPS = 4
EXPERTS_PER_GROUP = 8
N_EXPERTS = N_GROUPS * EXPERTS_PER_GROUP
NEG_INF = -1e30

V7X_VMEM_LIMIT = 56 * 1024 * 1024
LANES = 128

PROJ_TILE = 512
RWKV_CHUNK = 64
ATT_BAND = 128
EXPERT_TILE = 512
ATT_UNROLL = {1: 8, 4: 8, 16: 4}
SC_GATHER_BYTES = 128 * 1024
SC_INDEX_LIMIT = 128
SLOT_MULTIPLE = 8192
SAMPLE_EXPERT_TILE = 128

NN = (((1,), (0,)), ((), ()))
NT = (((1,), (1,)), ((), ()))
TN = (((0,), (0,)), ((), ()))


def _dot(a, b):
    return jnp.dot(a, b, preferred_element_type=F32)


def _dot_split(a, b_bf16):
    hi = a.astype(BF16)
    lo = (a - hi.astype(F32)).astype(BF16)
    return _dot(hi, b_bf16) + _dot(lo, b_bf16)


def _mm(a, b, dims):
    return lax.dot_general(a.astype(BF16), b.astype(BF16), dims, preferred_element_type=F32)


def _round_up(x, k):
    return -(-x // k) * k


def _sigmoid(z):
    return 1.0 / (1.0 + jnp.exp(-z))


def _proj_kernel(x_ref, flag_ref, gmix_ref, win_ref, wdx_ref, mu_ref, w0a0_ref, w2a2_ref, g2_ref,
                 kk_ref, ka_ref, seg_ref,
                 r_o, lw_o, k_o, v_o, kkn_o, b_o, g_o, qa_o, kat_o, vat_o, xl_o, *rest,
                 tiles_per_seq, c_rwkv, keep_tiles):
    xn_carry, pj_carry = rest[-2:]
    i = pl.program_id(0)

    @pl.when(i % tiles_per_seq == 0)
    def _():
        xn_carry[...] = jnp.zeros_like(xn_carry)
        pj_carry[...] = jnp.zeros_like(pj_carry)

    c = c_rwkv
    x = x_ref[...]
    tm = x.shape[0]
    ms = jnp.mean(x * x, axis=-1, keepdims=True)
    xn = (x * lax.rsqrt(ms + NORM_EPS)) * gmix_ref[...]
    xn = jnp.where(flag_ref[...] > 0.0, x, xn)
    row = lax.broadcasted_iota(jnp.int32, (tm, 1), 0)
    xn_prev = jnp.where(row == 0, xn_carry[7:8, :], pltpu.roll(xn, 1, axis=0))
    dx = xn_prev - xn

    proj = _dot(xn.astype(BF16), win_ref[...])
    cur = proj[:, :3 * c]
    prev = jnp.where(row == 0, pj_carry[7:8, :], pltpu.roll(cur, 1, axis=0))
    xn_carry[...] = xn[tm - 8:, :]
    pj_carry[...] = cur[tm - 8:, :]
    xl_rows = xl_o.shape[1]
    xl_o[0] = xn[tm - xl_rows:, :]

    mu = mu_ref[...]
    r = cur[:, :c] + mu[0:1] * (prev[:, :c] - cur[:, :c])
    k = cur[:, c:2 * c] + mu[1:2] * (prev[:, c:2 * c] - cur[:, c:2 * c])
    v = cur[:, 2 * c:3 * c] + mu[2:3] * (prev[:, 2 * c:3 * c] - cur[:, 2 * c:3 * c])

    lr = proj[:, 6 * c:] + _dot(dx.astype(BF16), wdx_ref[...])
    lane = lax.broadcasted_iota(jnp.int32, (1, LANES), 1)
    wa_in = jnp.where(lane < 64, jnp.tanh(lr[:, :LANES]), lr[:, :LANES])
    wa = _dot(wa_in.astype(BF16), w2a2_ref[...]) + w0a0_ref[...]
    z = -wa[:, :c]
    softplus = jnp.maximum(z, 0.0) + jnp.log1p(jnp.exp(-jnp.abs(z)))
    lw = -jnp.exp(-softplus - 0.5)
    a = _sigmoid(wa[:, c:])
    g = _dot(_sigmoid(lr[:, LANES:]).astype(BF16), g2_ref[...])

    kk = k * kk_ref[...]
    sq = kk * kk
    gw = seg_ref.shape[0]
    ss = jnp.concatenate([_dot_split(sq[:, j:j + gw], seg_ref[...]) for j in range(0, c, gw)], axis=1)
    kk = kk * lax.rsqrt(jnp.maximum(ss, 1e-24))

    r_o[...] = r
    lw_o[...] = lw
    k_o[...] = k * (1.0 + (a - 1.0) * ka_ref[...])
    v_o[...] = v
    kkn_o[...] = kk
    b_o[...] = kk * a
    g_o[...] = g
    qa_o[...] = proj[:, 3 * c:4 * c]
    kat_o[...] = proj[:, 4 * c:5 * c]
    vat_o[...] = proj[:, 5 * c:6 * c]
    if keep_tiles:
        kt_o, vt_o = rest[:2]

        @pl.when(i % tiles_per_seq >= tiles_per_seq - keep_tiles)
        def _():
            kt_o[...] = proj[:, 4 * c:5 * c].T
            vt_o[...] = proj[:, 5 * c:6 * c].T


def _proj_call(x2, flag, p, tiles_per_seq, tm, xl_rows=8, keep_tiles=0):
    t, d = x2.shape
    c = p['c_rwkv']
    n_tiles = t // tm
    n_seq = n_tiles // tiles_per_seq
    first = tiles_per_seq - keep_tiles
    kept = pl.BlockSpec((None, c, tm), lambda i: (i // tiles_per_seq, 0, jnp.maximum(i % tiles_per_seq - first, 0)))
    kept_specs = [kept, kept] if keep_tiles else []
    kept_shapes = [jax.ShapeDtypeStruct((n_seq, c, keep_tiles * tm), F32)] * 2 if keep_tiles else []
    full = lambda a: pl.BlockSpec(a.shape, lambda i: (0,) * a.ndim, pipeline_mode=pl.Buffered(1))
    tok = lambda w: pl.BlockSpec((tm, w), lambda i: (i, 0))
    weights = [p['gmix'], p['win'], p['wdx'], p['mu_rkv'], p['w0a0'], p['w2a2'], p['g2'], p['k_k'], p['k_a'],
               p['seg'][:GROUP_LANES, :GROUP_LANES]]
    outs = pl.pallas_call(
        functools.partial(_proj_kernel, tiles_per_seq=tiles_per_seq, c_rwkv=c, keep_tiles=keep_tiles),
        grid=(n_tiles,),
        in_specs=[tok(d), tok(1)] + [full(w) for w in weights],
        out_specs=[tok(c)] * 10 + [pl.BlockSpec((1, xl_rows, d), lambda i: (i, 0, 0))] + kept_specs,
        out_shape=([jax.ShapeDtypeStruct((t, c), F32)] * 10 + [jax.ShapeDtypeStruct((n_tiles, xl_rows, d), F32)]
                   + kept_shapes),
        scratch_shapes=[pltpu.VMEM((8, d), F32), pltpu.VMEM((8, 3 * c), F32)],
        compiler_params=pltpu.CompilerParams(dimension_semantics=("arbitrary",), vmem_limit_bytes=V7X_VMEM_LIMIT),
        name="proj",
    )(x2, flag, *weights)
    return outs


GROUP_LANES = 256
GROUP_HEADS = GROUP_LANES // HEAD_DIM


def _rwkv_kernel(r_ref, lw_ref, k_ref, v_ref, kk_ref, b_ref, g_ref, s0_ref, rk_ref, lnw_ref, lnb_ref, seg_ref,
                 y_ref, sout_ref, s_scr):
    ci = pl.program_id(1)
    nb, L, c = r_ref.shape
    gw, gh, hd = GROUP_LANES, GROUP_HEADS, HEAD_DIM
    n_groups = c // gw

    lane_head = lax.broadcasted_iota(jnp.int32, (1, gw), 1) // hd
    head_masks = [lane_head == j for j in range(gh)]
    bd_state = (lax.broadcasted_iota(jnp.int32, (gw, gw), 0) // hd) == (lax.broadcasted_iota(jnp.int32, (gw, gw), 1) // hd)
    bd_time = (lax.broadcasted_iota(jnp.int32, (gh * L, gh * L), 0) // L) == (lax.broadcasted_iota(jnp.int32, (gh * L, gh * L), 1) // L)
    t_row = lax.broadcasted_iota(jnp.int32, (L, gh * L), 0)
    t_col = lax.broadcasted_iota(jnp.int32, (L, gh * L), 1) % L
    strict4 = t_row > t_col
    incl4 = t_row >= t_col
    incl = lax.broadcasted_iota(jnp.int32, (L, L), 0) >= lax.broadcasted_iota(jnp.int32, (L, L), 1)

    def stack(x):
        return jnp.concatenate([jnp.where(m, x, jnp.zeros_like(x)) for m in head_masks], axis=0)

    def block_diag(n):
        tiled = jnp.concatenate([n] * gh, axis=0)
        return jnp.where(bd_time, tiled, jnp.zeros_like(tiled))

    @pl.when(ci == 0)
    def _():
        for bi in range(nb):
            for gi in range(n_groups):
                s_in = s0_ref[bi, gi * gh:(gi + 1) * gh].reshape(gw, hd)
                s_scr[bi, gi] = jnp.where(bd_state, jnp.concatenate([s_in] * gh, axis=1), 0.0)

    n_apply = max(1, int(math.log2(L)))
    seg = seg_ref[...]
    pre = []
    for bi in range(nb):
        lw = lw_ref[bi]
        lw_hi = lw.astype(BF16)
        lw_r = lw - lw_hi.astype(F32)
        lw_mid = lw_r.astype(BF16)
        lw_lo = (lw_r - lw_mid.astype(F32)).astype(BF16)
        cs3 = _dot(incl.astype(BF16), jnp.concatenate([lw_hi, lw_mid, lw_lo], axis=1))
        cs = cs3[:, :c] + cs3[:, c:2 * c] + cs3[:, 2 * c:]
        cp = cs - lw
        cm = cs[L // 2 - 1:L // 2, :]
        c_last = cs[L - 1:L, :]
        r, k, v, kk, b = r_ref[bi], k_ref[bi], v_ref[bi], kk_ref[bi], b_ref[bi]
        e_dn = jnp.exp(cm - cs)
        e_l = jnp.exp(c_last - cs)
        pre.append(dict(v=v, rt=r * jnp.exp(cs - cm), kkt=kk * jnp.exp(cp - cm), bt=b * e_dn, kt=k * e_dn,
                        kg=kk * jnp.exp(cp), rg=r * jnp.exp(cs), bh=b * e_l, kh=k * e_l, g_last=jnp.exp(c_last),
                        rkk=r * k * rk_ref[...]))

    chains = [(bi, gi) for bi in range(nb) for gi in range(n_groups)]
    col = lambda bi, gi, name: pre[bi][name][:, gi * gw:(gi + 1) * gw]
    each = lambda fn: [fn(i, bi, gi) for i, (bi, gi) in enumerate(chains)]

    vg = each(lambda i, bi, gi: col(bi, gi, 'v'))
    v_st = each(lambda i, bi, gi: stack(vg[i].astype(BF16)))
    a_all = each(lambda i, bi, gi: _mm(
        jnp.concatenate([col(bi, gi, 'kkt'), col(bi, gi, 'rt')], axis=0),
        jnp.concatenate([stack(col(bi, gi, 'bt').astype(BF16)), stack(col(bi, gi, 'kt').astype(BF16))], axis=0),
        NT))
    p_ak = each(lambda i, bi, gi: jnp.where(strict4, a_all[i][:L, gh * L:], 0.0))
    p_rb = each(lambda i, bi, gi: jnp.where(incl4, a_all[i][L:, :gh * L], 0.0))
    p_rk = each(lambda i, bi, gi: jnp.where(incl4, a_all[i][L:, gh * L:], 0.0))
    eye4 = (t_row == t_col).astype(F32)
    nm = each(lambda i, bi, gi: -jnp.where(strict4, a_all[i][:L, :gh * L], 0.0))
    t_inv = [eye4 + n for n in nm]
    for it in range(n_apply - 1):
        lhs = nm if it == 0 else [jnp.concatenate([n, t], axis=0) for n, t in zip(nm, t_inv)]
        both = each(lambda i, bi, gi: _mm(lhs[i], block_diag(nm[i].astype(BF16)), NN))
        if it > 0:
            t_inv = [t + bo[L:] for t, bo in zip(t_inv, both)]
        nm = [bo[:L] for bo in both]
    t_inv = each(lambda i, bi, gi: t_inv[i] + _mm(t_inv[i], block_diag(nm[i].astype(BF16)), NN))
    av = each(lambda i, bi, gi: _mm(p_ak[i], v_st[i], NN))
    x = each(lambda i, bi, gi: _mm(
        t_inv[i], jnp.concatenate([stack(col(bi, gi, 'kg').astype(BF16)), stack(av[i].astype(BF16))], axis=1),
        NN))
    w_m = [xi[:, :gw] for xi in x]
    u0 = [-xi[:, gw:] for xi in x]
    rbw = each(lambda i, bi, gi: _mm(
        p_rb[i], jnp.concatenate([stack(w_m[i].astype(BF16)), stack(u0[i].astype(BF16))], axis=1),
        NN))
    rkv = each(lambda i, bi, gi: _mm(p_rk[i], v_st[i], NN))
    s_old = each(lambda i, bi, gi: s_scr[bi, gi])
    ws = each(lambda i, bi, gi: _mm(
        jnp.concatenate([w_m[i], col(bi, gi, 'rg') - rbw[i][:, :gw]], axis=0), s_old[i], NT))
    u = each(lambda i, bi, gi: u0[i] - ws[i][:L])
    y = each(lambda i, bi, gi: ws[i][L:] + rbw[i][:, gw:] + rkv[i])
    upd = each(lambda i, bi, gi: _mm(
        jnp.concatenate([u[i], vg[i]], axis=0),
        jnp.concatenate([col(bi, gi, 'bh'), col(bi, gi, 'kh')], axis=0), TN))
    for i, (bi, gi) in enumerate(chains):
        s_scr[bi, gi] = s_old[i] * col(bi, gi, 'g_last') + jnp.where(bd_state, upd[i], 0.0)

    inv = 1.0 / hd
    n_ch = len(chains)
    sums = _dot_split(jnp.concatenate(y + each(lambda i, bi, gi: col(bi, gi, 'rkk')), axis=0), seg)
    mean = [sums[i * L:(i + 1) * L] * inv for i in range(n_ch)]
    bonus = [sums[(n_ch + i) * L:(n_ch + i + 1) * L] * vg[i] for i in range(n_ch)]
    yc = [y[i] - mean[i] for i in range(n_ch)]
    sq = _dot_split(jnp.concatenate([z * z for z in yc], axis=0), seg)
    var = [sq[i * L:(i + 1) * L] * inv for i in range(n_ch)]
    for i, (bi, gi) in enumerate(chains):
        sl = slice(gi * gw, (gi + 1) * gw)
        yn = yc[i] * lax.rsqrt(var[i] + GN_EPS) * lnw_ref[:, sl] + lnb_ref[:, sl]
        y_ref[bi, :, sl] = (yn + bonus[i]) * g_ref[bi, :, sl]

    @pl.when(ci == pl.num_programs(1) - 1)
    def _():
        for bi in range(nb):
            for gi in range(n_groups):
                bd = s_scr[bi, gi]
                folded = bd[:, 0:hd]
                for j in range(1, gh):
                    folded = folded + bd[:, j * hd:(j + 1) * hd]
                sout_ref[bi, gi * gh:(gi + 1) * gh] = folded.reshape(gh, hd, hd)


def _rwkv_call(vecs, s0, p, n_seq, seq_len, chunk, nb):
    c = p['c_rwkv']
    n_heads = c // HEAD_DIM
    n_chunks = seq_len // chunk
    assert n_seq % nb == 0 and seq_len % chunk == 0
    vecs = [z.reshape(n_seq, seq_len, c) for z in vecs]
    tok = pl.BlockSpec((nb, chunk, c), lambda bi, ci: (bi, ci, 0))
    st = pl.BlockSpec((nb, n_heads, HEAD_DIM, HEAD_DIM), lambda bi, ci: (bi, 0, 0, 0))
    rowvec = pl.BlockSpec((1, c), lambda bi, ci: (0, 0))
    seg = p['seg'][:GROUP_LANES, :GROUP_LANES]
    y, s_out = pl.pallas_call(
        _rwkv_kernel,
        grid=(n_seq // nb, n_chunks),
        in_specs=[tok] * 7 + [st, rowvec, rowvec, rowvec, pl.BlockSpec(seg.shape, lambda bi, ci: (0, 0))],
        out_specs=[tok, st],
        out_shape=[jax.ShapeDtypeStruct((n_seq, seq_len, c), F32),
                   jax.ShapeDtypeStruct((n_seq, n_heads, HEAD_DIM, HEAD_DIM), F32)],
        scratch_shapes=[pltpu.VMEM((nb, c // GROUP_LANES, GROUP_LANES, GROUP_LANES), F32)],
        compiler_params=pltpu.CompilerParams(dimension_semantics=("arbitrary", "arbitrary"),
                                             vmem_limit_bytes=V7X_VMEM_LIMIT),
        name="rwkv",
    )(*vecs, s0, p['r_k'], p['ln_w'], p['ln_b'], seg)
    return y.reshape(n_seq * seq_len, c), s_out


def _attn_prompt_kernel(q_ref, k_ref, v_ref, o_ref, m_scr, l_scr, acc_scr):
    s_len = q_ref.shape[0]
    band = ATT_BAND
    n_blk = s_len // band

    lane = lax.broadcasted_iota(jnp.int32, (1, LANES), 1)
    head0 = lane < HEAD_DIM
    qi = lax.broadcasted_iota(jnp.int32, (band, 2 * band), 0)
    kj = lax.broadcasted_iota(jnp.int32, (band, 2 * band), 1)
    in_band = (kj >= qi) & (kj <= qi + band)
    in_band2 = jnp.concatenate([in_band, in_band], axis=0)
    kj2 = jnp.concatenate([kj, kj], axis=0)
    scale = HEAD_DIM ** -0.5
    ones = jnp.ones((2 * band, LANES), BF16)

    for ci, (window, dil) in enumerate(DILATED_CONFIGS):
        assert window // dil == band
        per_res = n_blk // dil

        unroll = ATT_UNROLL[dil]
        run = min(unroll, per_res)

        def body(it, carry, ci=ci, dil=dil, per_res=per_res, run=run, unroll=unroll):
            span = band * dil
            tile = lambda start: pl.ds(start, band, stride=dil) if dil > 1 else pl.ds(start, band)
            blocks = []
            tiles = []
            first_dyn = []
            for r in range(unroll // run):
                i0 = it * unroll + r * run
                blk0 = i0 % per_res
                start0 = i0 // per_res + blk0 * span
                base = len(tiles)
                if per_res > run:
                    tiles.append(tile(jnp.maximum(start0 - span, 0)))
                    first_dyn.append(blk0 == 0)
                else:
                    tiles.append(None)
                    first_dyn.append(None)
                for t in range(run):
                    tiles.append(tile(start0 + t * span))
                    blocks.append((tiles[-1], base + t, r if t == 0 else None))
            kt = [None if w is None else k_ref[w, :].astype(BF16) for w in tiles]
            vt = [None if w is None else v_ref[w, :].astype(BF16) for w in tiles]
            prev = lambda ts, i: ts[i + 1] if ts[i] is None else ts[i]
            q = [q_ref[rows, :] * scale for rows, _, _ in blocks]

            def mask_of(first):
                if first is None:
                    return in_band2
                if first_dyn[first] is None:
                    return in_band2 & (kj2 >= band)
                return in_band2 & (kj2 >= jnp.where(first_dyn[first], band, 0))

            s = [jnp.where(mask_of(first), lax.dot_general(
                jnp.concatenate([jnp.where(head0, qj, 0.0), jnp.where(head0, 0.0, qj)], axis=0).astype(BF16),
                jnp.concatenate([prev(kt, i), kt[i + 1]], axis=0), NT, preferred_element_type=F32), NEG_INF)
                 for qj, (_, i, first) in zip(q, blocks)]
            m = [jnp.max(z, axis=-1, keepdims=True) for z in s]
            p = [jnp.exp(z - mx).astype(BF16) for z, mx in zip(s, m)]
            o = [_dot(pj, jnp.concatenate([jnp.concatenate([prev(vt, i), vt[i + 1]], axis=0), ones], axis=1))
                 for pj, (_, i, _) in zip(p, blocks)]
            for j, (rows, _, _) in enumerate(blocks):
                m_scr[ci, rows, :] = jnp.where(head0, m[j][:band], m[j][band:])
                acc_scr[ci, rows, :] = jnp.where(head0, o[j][:band, :LANES], o[j][band:, :LANES])
                l_scr[ci, rows, :] = jnp.where(head0, o[j][:band, LANES:], o[j][band:, LANES:])
            return carry

        lax.fori_loop(0, n_blk // unroll, body, 0)

    rows_per = 256

    def merge(i, carry):
        rows = pl.ds(pl.multiple_of(i * rows_per, rows_per), rows_per)
        ms = [m_scr[ci, rows, :] for ci in range(len(DILATED_CONFIGS))]
        m_all = functools.reduce(jnp.maximum, ms)
        num = jnp.zeros((rows_per, LANES), F32)
        den = jnp.zeros((rows_per, LANES), F32)
        for ci, m_c in enumerate(ms):
            w_c = jnp.exp(m_c - m_all)
            num = num + w_c * acc_scr[ci, rows, :]
            den = den + w_c * l_scr[ci, rows, :]
        o_ref[rows, :] = num / den
        return carry

    lax.fori_loop(0, s_len // rows_per, merge, 0)


def _attn_prompt_call(q, k, v, n_seq, seq_len):
    c = q.shape[1]
    n_pairs = c // LANES
    blk = pl.BlockSpec((seq_len, LANES), lambda bi, hi: (bi, hi))
    return pl.pallas_call(
        _attn_prompt_kernel,
        grid=(n_seq, n_pairs),
        in_specs=[blk, blk, blk],
        out_specs=blk,
        out_shape=jax.ShapeDtypeStruct((n_seq * seq_len, c), F32),
        scratch_shapes=[pltpu.VMEM((len(DILATED_CONFIGS), seq_len, LANES), F32)] * 3,
        compiler_params=pltpu.CompilerParams(dimension_semantics=("arbitrary", "arbitrary"),
                                             vmem_limit_bytes=V7X_VMEM_LIMIT),
        name="attn_prompt",
    )(q, k, v)


def _attn_sample_kernel(q_ref, kn_ref, vn_ref, kc_ref, vc_ref, o_ref, *, n_new):
    hd = HEAD_DIM
    _, n_heads, _, n_buf = kc_ref.shape
    t_pad = kn_ref.shape[1]
    c = q_ref.shape[2]
    q = q_ref[0] * (hd ** -0.5)
    lane_head = lax.broadcasted_iota(jnp.int32, (1, c), 1) // hd
    qs = jnp.concatenate([jnp.where(lane_head == h, q, 0.0) for h in range(n_heads)], axis=0).astype(BF16)
    n_rows = n_heads * t_pad
    t_idx = lax.broadcasted_iota(jnp.int32, (n_rows, 1), 0) % t_pad

    def multiplicity(dist):
        mult = jnp.zeros(dist.shape, F32)
        for window, dil in DILATED_CONFIGS:
            hit = (dist >= 0) & (dist <= window) & (dist % dil == 0)
            mult = mult + jnp.where(hit, 1.0, 0.0)
        return mult

    jc = lax.broadcasted_iota(jnp.int32, (1, n_buf), 1)
    mult_c = multiplicity(n_buf + t_idx - jc)
    jn = lax.broadcasted_iota(jnp.int32, (1, t_pad), 1)
    mult_n = jnp.where(jn < n_new, multiplicity(t_idx - jn), 0.0)

    sc = jnp.concatenate(
        [_dot(qs[h * t_pad:(h + 1) * t_pad, h * hd:(h + 1) * hd], kc_ref[0, h].astype(BF16))
         for h in range(n_heads)], axis=0)
    sn = lax.dot_general(qs, kn_ref[0].astype(BF16), NT, preferred_element_type=F32)
    sc = jnp.where(mult_c > 0.0, sc, NEG_INF)
    sn = jnp.where(mult_n > 0.0, sn, NEG_INF)
    m = jnp.maximum(jnp.max(sc, axis=-1, keepdims=True), jnp.max(sn, axis=-1, keepdims=True))
    pc = (mult_c * jnp.exp(sc - m)).astype(BF16)
    pn = mult_n * jnp.exp(sn - m)
    inv_l = 1.0 / (jnp.sum(pc.astype(F32), axis=-1, keepdims=True) + jnp.sum(pn, axis=-1, keepdims=True))
    o_new = _dot(pn.astype(BF16), vn_ref[0].astype(BF16)) * inv_l
    out = jnp.zeros((t_pad, c), F32)
    for h in range(n_heads):
        out = out + jnp.where(lane_head == h, o_new[h * t_pad:(h + 1) * t_pad, :], 0.0)
    o_buf = [lax.dot_general(pc[h * t_pad:(h + 1) * t_pad, :], vc_ref[0, h].astype(BF16), NT,
                             preferred_element_type=F32) * inv_l[h * t_pad:(h + 1) * t_pad, :]
             for h in range(n_heads)]
    o_ref[0] = out + jnp.concatenate(o_buf, axis=1)


def _attn_sample_call(q, kn, vn, k_buf, v_buf, n_new):
    b, t_pad, c = q.shape
    _, n_buf, n_heads, hd = k_buf.shape
    k_t = jnp.transpose(k_buf, (0, 2, 3, 1))
    v_t = jnp.transpose(v_buf, (0, 2, 3, 1))
    new = pl.BlockSpec((1, t_pad, c), lambda bi: (bi, 0, 0))
    buf = pl.BlockSpec((1, n_heads, hd, n_buf), lambda bi: (bi, 0, 0, 0))
    return pl.pallas_call(
        functools.partial(_attn_sample_kernel, n_new=n_new),
        grid=(b,),
        in_specs=[new, new, new, buf, buf],
        out_specs=new,
        out_shape=jax.ShapeDtypeStruct((b, t_pad, c), F32),
        compiler_params=pltpu.CompilerParams(dimension_semantics=("arbitrary",), vmem_limit_bytes=V7X_VMEM_LIMIT),
        name="attn_sample",
    )(q, kn, vn, k_t, v_t)


def _route_rows(logits, seen):
    lane = lax.broadcasted_iota(jnp.int32, logits.shape, 1)
    lane_f = lane.astype(F32)
    first = lambda hit: jnp.min(jnp.where(hit, lane_f, float(LANES)), axis=-1, keepdims=True)
    is_g = lane < N_GROUPS
    lg = jnp.where(is_g, logits, NEG_INF)
    g_max = jnp.max(lg, axis=-1, keepdims=True)
    g_idx = first(lg == g_max)
    g_w = 1.0 / jnp.sum(jnp.where(is_g, jnp.exp(lg - g_max), 0.0), axis=-1, keepdims=True)
    lo = N_GROUPS + EXPERTS_PER_GROUP * g_idx
    le = jnp.where((lane_f >= lo) & (lane_f < lo + EXPERTS_PER_GROUP), logits, NEG_INF)
    e1 = jnp.max(le, axis=-1, keepdims=True)
    i1 = first(le == e1)
    le2 = jnp.where(lane_f == i1, NEG_INF, le)
    e2 = jnp.max(le2, axis=-1, keepdims=True)
    i2 = first(le2 == e2)
    ex = jnp.exp(e2 - e1)
    gate1 = g_w / (1.0 + ex)
    gate2 = g_w * ex / (1.0 + ex)
    tm = logits.shape[0]
    pick1 = lane_f == i1
    pick2 = lane_f == i2
    picks = jnp.where(pick1 | pick2, 1.0, 0.0)
    earlier = (lax.broadcasted_iota(jnp.int32, (tm, tm), 0) > lax.broadcasted_iota(jnp.int32, (tm, tm), 1))
    before = seen + _dot(earlier.astype(BF16), picks.astype(BF16))
    rank1 = jnp.sum(jnp.where(pick1, before, 0.0), axis=-1, keepdims=True)
    rank2 = jnp.sum(jnp.where(pick2, before, 0.0), axis=-1, keepdims=True)
    out = jnp.where(lane == 0, gate1, jnp.where(lane == 1, gate2, 0.0))
    out = jnp.where(lane == 2, i1 - N_GROUPS, jnp.where(lane == 3, i2 - N_GROUPS, out))
    out = jnp.where(lane == 4, rank1, jnp.where(lane == 5, rank2, out))
    return out, jnp.sum(picks, axis=0, keepdims=True)


def _post_kernel(x_ref, yr_ref, ya_ref, wo_ref, gffn_ref, rw_cat_ref, rb_ref,
                 h_o, hna_o, hnb_o, lg_o, cnt_o, seen_scr):
    @pl.when(pl.program_id(0) == 0)
    def _():
        seen_scr[...] = jnp.zeros_like(seen_scr)

    c = yr_ref.shape[1]
    h = (x_ref[...] + _dot(yr_ref[...].astype(BF16), wo_ref[:c, :]) + _dot(ya_ref[...].astype(BF16), wo_ref[c:, :]))
    ms = jnp.mean(h * h, axis=-1, keepdims=True)
    hn = (h * lax.rsqrt(ms + NORM_EPS)) * gffn_ref[...]
    h_o[...] = h
    bits = pltpu.bitcast(hn.astype(BF16).astype(F32), jnp.uint32)
    half = hn.shape[1] // 2
    packed = (bits[:, :half] >> 16) | (bits[:, half:] & jnp.uint32(0xFFFF0000))
    hna_o[...] = packed[:, :half // 2]
    hnb_o[...] = packed[:, half // 2:]
    hi = hn.astype(BF16)
    lo = (hn - hi.astype(F32)).astype(BF16)
    both = _dot(hi, rw_cat_ref[...])
    logits = (both[:, :LANES] + both[:, LANES:] + _dot(lo, rw_cat_ref[:, :LANES])) + rb_ref[...]
    route, picked = _route_rows(logits, seen_scr[...])
    lg_o[...] = route
    seen_scr[...] = seen_scr[...] + picked
    cnt_o[...] = seen_scr[...]


def _post_call(x2, yr, ya, p, tm):
    t, d = x2.shape
    c = yr.shape[1]
    full = lambda a: pl.BlockSpec(a.shape, lambda i: (0,) * a.ndim)
    tok = lambda w: pl.BlockSpec((tm, w), lambda i: (i, 0))
    weights = [p['wout'], p['gffn'], p['rw_cat'], p['rb']]
    return pl.pallas_call(
        _post_kernel,
        grid=(t // tm,),
        in_specs=[tok(d), tok(c), tok(c)] + [full(w) for w in weights],
        out_specs=[tok(d), tok(d // 4), tok(d // 4), tok(LANES), pl.BlockSpec((1, LANES), lambda i: (0, 0))],
        out_shape=[jax.ShapeDtypeStruct((t, d), F32), jax.ShapeDtypeStruct((t, d // 4), jnp.uint32),
                   jax.ShapeDtypeStruct((t, d // 4), jnp.uint32),
                   jax.ShapeDtypeStruct((t, LANES), F32), jax.ShapeDtypeStruct((1, LANES), F32)],
        scratch_shapes=[pltpu.VMEM((1, LANES), F32)],
        compiler_params=pltpu.CompilerParams(dimension_semantics=("arbitrary",), vmem_limit_bytes=V7X_VMEM_LIMIT),
        name="post",
    )(x2, yr, ya, *weights)


def _expert_kernel(ve_ref, vb_ref, lo_ref, hi_ref, nv_ref, xa_ref, xb_ref, w1_ref, w3_ref, w2_ref, y_ref, w1_s, w3_s, w2_s):
    i = pl.program_id(0)
    live = i < nv_ref[0]
    prev = jnp.maximum(i - 1, 0)

    @pl.when(live & ((i == 0) | (ve_ref[i] != ve_ref[prev])))
    def _():
        w1_s[...] = w1_ref[...].astype(BF16)
        w3_s[...] = w3_ref[...].astype(BF16)
        w2_s[...] = w2_ref[...].astype(BF16)

    @pl.when(live)
    def _():
        packed = jnp.concatenate([xa_ref[...], xb_ref[...]], axis=1)
        half = packed.shape[1]
        lo = pltpu.bitcast(packed << 16, F32).astype(BF16)
        hi = pltpu.bitcast(packed & jnp.uint32(0xFFFF0000), F32).astype(BF16)
        h1 = _dot(lo, w1_s[:half, :]) + _dot(hi, w1_s[half:, :])
        h3 = _dot(lo, w3_s[:half, :]) + _dot(hi, w3_s[half:, :])
        act = (h1 * _sigmoid(h1)) * h3
        y = _dot(act.astype(BF16), w2_s[...])
        row = lax.broadcasted_iota(jnp.int32, (y.shape[0], 1), 0)
        mine = (row >= lo_ref[i]) & (row < hi_ref[i])
        revisit = (i > 0) & (vb_ref[i] == vb_ref[prev])

        @pl.when(revisit)
        def _():
            y_ref[...] = jnp.where(mine, y, y_ref[...])

        @pl.when(jnp.logical_not(revisit))
        def _():
            y_ref[...] = jnp.where(mine, y, 0.0)


def _expert_call(visits, xs, p, bm, n_rows):
    vis_exp, vis_blk, vis_lo, vis_hi, n_vis = visits
    _, d, de = p['w1'].shape
    grid_spec = pltpu.PrefetchScalarGridSpec(
        num_scalar_prefetch=5,
        grid=(vis_exp.shape[0],),
        in_specs=[pl.BlockSpec((bm, d // 4), lambda i, ve, vb, lo, hi, nv: (vb[i], 0)),
                  pl.BlockSpec((bm, d // 4), lambda i, ve, vb, lo, hi, nv: (vb[i], 0)),
                  pl.BlockSpec((None, d, de), lambda i, ve, vb, lo, hi, nv: (ve[i], 0, 0)),
                  pl.BlockSpec((None, d, de), lambda i, ve, vb, lo, hi, nv: (ve[i], 0, 0)),
                  pl.BlockSpec((None, de, d), lambda i, ve, vb, lo, hi, nv: (ve[i], 0, 0))],
        out_specs=pl.BlockSpec((bm, d), lambda i, ve, vb, lo, hi, nv: (vb[i], 0)),
        scratch_shapes=[pltpu.VMEM((d, de), BF16), pltpu.VMEM((d, de), BF16), pltpu.VMEM((de, d), BF16)],
    )
    return pl.pallas_call(
        _expert_kernel,
        grid_spec=grid_spec,
        out_shape=jax.ShapeDtypeStruct((n_rows, d), F32),
        compiler_params=pltpu.CompilerParams(dimension_semantics=("arbitrary",), vmem_limit_bytes=V7X_VMEM_LIMIT),
        name="experts",
    )(vis_exp, vis_blk, vis_lo, vis_hi, n_vis, *xs, p['w1'], p['w3'], p['w2'])


def _final_kernel(h_ref, route_ref, y1_ref, y2_ref, gfin_ref, o_ref):
    route = route_ref[...]
    h = h_ref[...] + (route[:, 0:1] * y1_ref[...] + route[:, 1:2] * y2_ref[...])
    ms = jnp.mean(h * h, axis=-1, keepdims=True)
    o_ref[...] = (h * lax.rsqrt(ms + NORM_EPS)) * gfin_ref[...]


def _final_call(h, route, y12, gfin, tm):
    t, d = h.shape
    tok = pl.BlockSpec((tm, d), lambda i: (i, 0))
    routed = lambda a: pl.BlockSpec((None, tm, d), lambda i: (a, i, 0))
    return pl.pallas_call(
        _final_kernel,
        grid=(t // tm,),
        in_specs=[tok, pl.BlockSpec((tm, LANES), lambda i: (i, 0)), routed(0), routed(1),
                  pl.BlockSpec((1, d), lambda i: (0, 0))],
        out_specs=tok,
        out_shape=jax.ShapeDtypeStruct((t, d), F32),
        compiler_params=pltpu.CompilerParams(dimension_semantics=("arbitrary",), vmem_limit_bytes=V7X_VMEM_LIMIT),
        name="final",
    )(h, route, y12, y12, gfin)


def _gather_rows(table, idx):
    info = plsc.get_sparse_core_info()
    nc, ns = info.num_cores, info.num_subcores
    b, d = idx.shape[0], table.shape[1]
    chunk = min(SC_INDEX_LIMIT, SC_GATHER_BYTES // (d * table.dtype.itemsize))
    assert b % (nc * ns * chunk * 2) == 0, "rows must split evenly into chunk pairs per subcore"
    per_w = b // (nc * ns)
    n_chunks = per_w // chunk
    mesh = plsc.VectorSubcoreMesh(core_axis_name="c", subcore_axis_name="s")

    @functools.partial(
        pl.kernel, mesh=mesh, out_type=jax.ShapeDtypeStruct((b, d), table.dtype),
        scratch_types=[pltpu.VMEM((per_w,), jnp.int32), pltpu.VMEM((2, chunk, d), table.dtype),
                       pltpu.SemaphoreType.DMA((2,)), pltpu.SemaphoreType.DMA((2,))])
    def gather(table_hbm, idx_hbm, out_hbm, idx_v, rows_v, fetch_sem, put_sem):
        base = (lax.axis_index("s") * nc + lax.axis_index("c")) * per_w
        pltpu.sync_copy(idx_hbm.at[pl.ds(base, per_w)], idx_v)

        def fetch(c, slot):
            off = pl.multiple_of(c * chunk, chunk)
            return pltpu.make_async_copy(table_hbm.at[idx_v.at[pl.ds(off, chunk)]], rows_v.at[slot],
                                         fetch_sem.at[slot])

        def put(c, slot):
            off = pl.multiple_of(c * chunk, chunk)
            return pltpu.make_async_copy(rows_v.at[slot], out_hbm.at[pl.ds(base + off, chunk)], put_sem.at[slot])

        fetch(0, 0).start()

        @pl.loop(0, n_chunks, step=2)
        def _(c):
            @pl.when(c > 0)
            def _():
                put(c - 1, 1).wait()
            fetch(c + 1, 1).start()
            fetch(c, 0).wait()
            put(c, 0).start()
            fetch(c + 1, 1).wait()
            put(c, 0).wait()

            @pl.when(c + 2 < n_chunks)
            def _():
                fetch(c + 2, 0).start()
            put(c + 1, 1).start()

        put(n_chunks - 1, 1).wait()

    return gather(table, idx)


def _scatter_rows(rows, dest, n_out):
    n, d = rows.shape
    window = SC_INDEX_LIMIT
    assert n % window == 0 and 2 * window * d * rows.dtype.itemsize <= 2 * SC_GATHER_BYTES
    mesh = plsc.VectorSubcoreMesh(core_axis_name="c", subcore_axis_name="s")

    @functools.partial(pl.kernel, mesh=mesh, out_type=jax.ShapeDtypeStruct((n_out, d), rows.dtype))
    def scatter(rows_hbm, idx0_hbm, idx1_hbm, out_hbm):
        def body(rows_v, idx0_v, idx1_v):
            pltpu.sync_copy(rows_v, out_hbm.at[idx0_v.at[0]])
            pltpu.sync_copy(rows_v, out_hbm.at[idx1_v.at[0]])

        idx_spec = pl.BlockSpec((1, window), lambda i: (0, i))
        pltpu.emit_pipeline(
            body, grid=(n // window,),
            in_specs=[pl.BlockSpec((window, d), lambda i: (i, 0)), idx_spec, idx_spec],
            out_specs=[], core_axis_name=("c", "s"), dimension_semantics=(pltpu.PARALLEL,),
        )(rows_hbm, idx0_hbm, idx1_hbm)

    return scatter(rows, dest[0:1], dest[1:2])


def _route(route, counts, bm, by_scatter):
    n = route.shape[0]
    eid = route[:, 2:4].astype(jnp.int32)
    rank = route[:, 4:6].astype(jnp.int32)
    m = 2 * n
    assert m % bm == 0
    experts = jnp.arange(N_EXPERTS, dtype=jnp.int32)
    counts = counts[0, N_GROUPS:N_GROUPS + N_EXPERTS].astype(jnp.int32)
    ends = jnp.cumsum(counts)
    starts = ends - counts
    lookup = lambda tbl, e: jnp.sum(jnp.where(e[..., None] == experts, tbl, 0), axis=-1)
    dest = lookup(starts, eid.T) + rank.T
    row_tok = None
    if not by_scatter:
        bits = max(1, (m - 1).bit_length())
        order = lax.sort(eid.reshape(-1) * (1 << bits) + jnp.arange(m, dtype=jnp.int32)) & ((1 << bits) - 1)
        pad = _round_up(m, SLOT_MULTIPLE) - m
        row_tok = jnp.concatenate([order // 2, jnp.arange(pad, dtype=jnp.int32) % n])
    blk_lo = starts // bm
    n_touch = jnp.where(counts > 0, (ends - 1) // bm - blk_lo + 1, 0)
    v_end = jnp.cumsum(n_touch)
    v = jnp.arange(m // bm + N_EXPERTS, dtype=jnp.int32)
    vis_exp = jnp.minimum(jnp.sum((v_end[None, :] <= v[:, None]).astype(jnp.int32), axis=1), N_EXPERTS - 1)
    vis_blk = jnp.minimum(lookup(blk_lo, vis_exp) + v - lookup(v_end - n_touch, vis_exp), m // bm - 1)
    vis_lo = jnp.clip(lookup(starts, vis_exp) - vis_blk * bm, 0, bm)
    vis_hi = jnp.clip(lookup(ends, vis_exp) - vis_blk * bm, 0, bm)
    n_vis = v_end[-1:].astype(jnp.int32)
    return row_tok, dest, (vis_exp.astype(jnp.int32), vis_blk.astype(jnp.int32), vis_lo.astype(jnp.int32),
                           vis_hi.astype(jnp.int32), n_vis)


def _moe_dispatch(x2, yr, ya, p, tm, bm, by_scatter):
    h, hn_a, hn_b, route, counts = _post_call(x2, yr, ya, p, tm)
    row_tok, dest, visits = _route(route, counts, bm, by_scatter)
    move = (lambda z: _scatter_rows(z, dest, 2 * z.shape[0])) if by_scatter else (lambda z: _gather_rows(z, row_tok))
    return dict(h=h, route=route, dest=dest, visits=visits, xs=(move(hn_a), move(hn_b)))


def _moe_combine(ctx, yb, p, tm):
    h = ctx['h']
    y12 = _gather_rows(yb, ctx['dest'].reshape(-1)).reshape(2, h.shape[0], h.shape[1])
    return _final_call(h, ctx['route'], y12, p['gfin'], tm)


def _prep_params(layer, norm_mix_g, w_in, rwkv_mu_rkv, rwkv_mu_wag, rwkv_w0, rwkv_w1, rwkv_w2, rwkv_a0, rwkv_a1,
                 rwkv_a2, rwkv_g1, rwkv_g2, rwkv_k_k, rwkv_k_a, rwkv_r_k, rwkv_ln_w, rwkv_ln_b, w_out, norm_ffn_g,
                 router_group_w, router_group_b, router_expert_w, router_expert_b, expert_w1, expert_w3, expert_w2,
                 norm_final_g):
    d = w_in.shape[1]
    c = rwkv_w0.shape[1]
    row = lambda a: a.reshape(1, -1).astype(F32)
    lowrank = jnp.concatenate([rwkv_w1[layer], rwkv_a1[layer], rwkv_g1[layer]], axis=1)
    mx = rwkv_mu_wag[layer]
    r_w = rwkv_w1.shape[2]
    r_a = rwkv_a1.shape[2]
    r_g = rwkv_g1.shape[2]
    assert r_w + r_a == LANES and r_g == LANES
    mx_cols = jnp.concatenate([jnp.broadcast_to(mx[0][:, None], (d, r_w)), jnp.broadcast_to(mx[1][:, None], (d, r_a)),
                               jnp.broadcast_to(mx[2][:, None], (d, r_g))], axis=1)
    w2a2 = jnp.zeros((LANES, 2 * c), F32)
    w2a2 = w2a2.at[:r_w, :c].set(rwkv_w2[layer]).at[r_w:, c:].set(rwkv_a2[layer])
    head = jnp.arange(c) // HEAD_DIM
    rw = jnp.zeros((d, LANES), F32)
    rw = rw.at[:, :N_GROUPS].set(router_group_w[layer]).at[:, N_GROUPS:N_GROUPS + N_EXPERTS].set(router_expert_w[layer])
    rw_hi = rw.astype(BF16)
    rb = jnp.zeros((1, LANES), F32)
    rb = rb.at[0, :N_GROUPS].set(router_group_b[layer]).at[0, N_GROUPS:N_GROUPS + N_EXPERTS].set(router_expert_b[layer])
    return {
        'c_rwkv': c,
        'gmix': row(norm_mix_g[layer]),
        'win': jnp.concatenate([w_in[layer], lowrank], axis=1).astype(BF16),
        'wdx': (mx_cols * lowrank).astype(BF16),
        'mu_rkv': rwkv_mu_rkv[layer],
        'w0a0': jnp.concatenate([row(rwkv_w0[layer]), row(rwkv_a0[layer])], axis=1),
        'w2a2': w2a2.astype(BF16),
        'g2': rwkv_g2[layer].astype(BF16),
        'k_k': row(rwkv_k_k[layer]),
        'k_a': row(rwkv_k_a[layer]),
        'seg': (head[:, None] == head[None, :]).astype(BF16),
        'r_k': row(rwkv_r_k[layer]),
        'ln_w': row(rwkv_ln_w[layer]),
        'ln_b': row(rwkv_ln_b[layer]),
        'wout': w_out[layer].astype(BF16),
        'gffn': row(norm_ffn_g[layer]),
        'rw_cat': jnp.concatenate([rw_hi, (rw - rw_hi.astype(F32)).astype(BF16)], axis=1),
        'rb': rb,
        'w1': expert_w1[layer],
        'w3': expert_w3[layer],
        'w2': expert_w2[layer],
        'gfin': row(norm_final_g),
    }


def _prompt_mix(x, p):
    b, s, d = x.shape
    c = p['c_rwkv']
    x2 = x.reshape(b * s, d)
    tm = PROJ_TILE
    flag = jnp.zeros((b * s, 1), F32)
    keep = min(max(w for w, _ in DILATED_CONFIGS), s)
    assert keep % tm == 0
    r, lw, k, v, kk, bb, g, qa, ka, va, xl, kt, vt = _proj_call(x2, flag, p, s // tm, tm, keep_tiles=keep // tm)
    s0 = jnp.zeros((b, c // HEAD_DIM, HEAD_DIM, HEAD_DIM), F32)
    yr, s_new = _rwkv_call((r, lw, k, v, kk, bb, g), s0, p, b, s, RWKV_CHUNK, 8)
    ya = _attn_prompt_call(qa, ka, va, b, s)
    shift = xl.reshape(b, s // tm, 8, d)[:, -1, 7, :]
    to_cache = lambda z: jnp.transpose(z.reshape(b, c // HEAD_DIM, HEAD_DIM, keep), (0, 3, 1, 2))
    k_keep, v_keep = to_cache(kt), to_cache(vt)
    return (x2, yr, ya), (s_new, shift, k_keep, v_keep)


def _sample_mix(x, shift0, s0, k_buf, v_buf, p, after):
    b, t, d = x.shape
    c = p['c_rwkv']
    n_heads = c // HEAD_DIM
    t_pad = 8
    xc = jnp.concatenate([shift0[:, None, :], x, jnp.zeros((b, t_pad - 1 - t, d), x.dtype)], axis=1)
    flag = jnp.zeros((b, t_pad, 1), F32).at[:, 0].set(1.0)
    outs = _proj_call(xc.reshape(b * t_pad, d), flag.reshape(b * t_pad, 1), p, 1, b * t_pad, xl_rows=b * t_pad)
    xl = outs[10]
    live = (jnp.arange(t_pad) < t)[None, :, None]
    shifted = [jnp.where(live, jnp.roll(o.reshape(b, t_pad, c), -1, axis=1), 0.0) for o in outs[:10]]
    r, lw, k, v, kk, bb, g, qa, ka, va = shifted
    flat = lambda z: z.reshape(b * t_pad, c)
    yr, s_new = _rwkv_call(tuple(flat(z) for z in (r, lw, k, v, kk, bb, g)), s0, p, b, t_pad, t_pad, 8)
    qa_held, _ = lax.optimization_barrier((qa, after))
    ya = _attn_sample_call(qa_held, ka, va, k_buf, v_buf, t)
    x_pad = jnp.concatenate([x, jnp.zeros((b, t_pad - t, d), x.dtype)], axis=1).reshape(b * t_pad, d)
    shift = xl.reshape(b, t_pad, d)[:, t]
    return ((x_pad, yr, flat(ya)),
            (s_new, shift, ka[:, :t].reshape(b, t, n_heads, HEAD_DIM), va[:, :t].reshape(b, t, n_heads, HEAD_DIM)))


def kernel(x_prompt, x_sample, state_rwkv, state_shift, cache_att_k, cache_att_v, norm_mix_g, w_in, rwkv_mu_rkv, rwkv_mu_wag, rwkv_w0, rwkv_w1, rwkv_w2, rwkv_a0, rwkv_a1, rwkv_a2, rwkv_g1, rwkv_g2, rwkv_k_k, rwkv_k_a, rwkv_r_k, rwkv_ln_w, rwkv_ln_b, w_out, norm_ffn_g, router_group_w, router_group_b, router_expert_w, router_expert_b, expert_w1, expert_w3, expert_w2, norm_final_g):
    assert w_in.shape[0] == 1, "single-layer trunk"
    p = _prep_params(0, norm_mix_g, w_in, rwkv_mu_rkv, rwkv_mu_wag, rwkv_w0, rwkv_w1, rwkv_w2, rwkv_a0, rwkv_a1,
                     rwkv_a2, rwkv_g1, rwkv_g2, rwkv_k_k, rwkv_k_a, rwkv_r_k, rwkv_ln_w, rwkv_ln_b, w_out,
                     norm_ffn_g, router_group_w, router_group_b, router_expert_w, router_expert_b, expert_w1,
                     expert_w3, expert_w2, norm_final_g)
    moe_p, (rw_p, sh_p, kc_p, vc_p) = _prompt_mix(x_prompt, p)
    ctx_p = _moe_dispatch(*moe_p, p, PROJ_TILE, EXPERT_TILE, True)
    yb_p = _expert_call(ctx_p['visits'], ctx_p['xs'], p, EXPERT_TILE, 2 * ctx_p['h'].shape[0])
    y12_p = _gather_rows(yb_p, ctx_p['dest'].reshape(-1))
    moe_s, (rw_s, sh_s, kc_s, vc_s) = _sample_mix(x_sample, state_shift[0], state_rwkv[0], cache_att_k[0],
                                                   cache_att_v[0], p, after=yb_p)
    tm_s = moe_s[0].shape[0] // 2
    ctx_s = _moe_dispatch(*moe_s, p, tm_s, SAMPLE_EXPERT_TILE, False)
    yb_s = _expert_call(ctx_s['visits'], ctx_s['xs'], p, SAMPLE_EXPERT_TILE, 2 * ctx_s['h'].shape[0])
    h_p = ctx_p['h']
    y_p = _final_call(h_p, ctx_p['route'], y12_p.reshape(2, *h_p.shape), p['gfin'], PROJ_TILE).reshape(x_prompt.shape)
    y_s = _moe_combine(ctx_s, yb_s, p, tm_s)
    y_s = y_s.reshape(x_sample.shape[0], -1, x_sample.shape[2])[:, :x_sample.shape[1]]
    return (y_p, y_s, rw_p[None], sh_p[None], kc_p[None], vc_p[None], rw_s[None], sh_s[None], kc_s[None], vc_s[None])
```
